```python
import math
import jax
import jax.numpy as jnp
from jax import lax
import numpy as np

D_MODEL = 1024
BATCH = 16
SEQ = 256
DEPTH = 2
DEC_BATCH = 2
DEC_SEQ = 4096
PAST_LEN = 256

GRID_W = 64
HEAD_DIM = 64
ROPE_HALF = HEAD_DIM // 2
ROPE_AXIS_PAIRS = HEAD_DIM // 4
ROPE_THETA = 10000.0
BLOCK = 128
A_HEADS = 8
A_KV_HEADS = 2
A_GROUP = A_HEADS // A_KV_HEADS
WINDOW = 128
B_HEADS = 8
NA_KH = 8
NA_KW = 16
C_HEADS = 4
C_VDIM = 2 * HEAD_DIM
BRANCH_W = 512
N_BRANCH = 3
N_EXPERTS = 32
TOP_K = 4
D_FF = D_MODEL
SWIGLU_LIMIT = 7.0
SWIGLU_ALPHA = 1.702
MOE_BLOCK = 128
N_MOD = 6
EPS = 1e-6
NEG_INF = -1e30
PROJ_WIDTHS = (A_HEADS * HEAD_DIM, A_KV_HEADS * HEAD_DIM, A_KV_HEADS * HEAD_DIM,
               B_HEADS * HEAD_DIM, B_HEADS * HEAD_DIM, B_HEADS * HEAD_DIM,
               C_HEADS * 2 * HEAD_DIM, C_HEADS * 2 * HEAD_DIM, C_HEADS * C_VDIM,
               D_MODEL, D_MODEL, D_MODEL)
D_PROJ = sum(PROJ_WIDTHS)

kernel_name = 'hybrid_dit_window_natten_diffattn_moe_step'


def rms_norm(x, g):
    xf = x.astype(jnp.float32)
    y = xf * lax.rsqrt(jnp.mean(xf * xf, axis=-1, keepdims=True) + EPS)
    return (y * g.astype(jnp.float32)).astype(x.dtype)


def modulation(cond, w_ada, b_ada):
    m = jax.nn.silu(cond) @ w_ada + b_ada
    return jnp.split(m[:, None, :], N_MOD, axis=-1)


def modulate(x, g, shift, scale):
    return rms_norm(x, g) * (1.0 + scale) + shift


def axial_rope(n_tokens):
    t = jnp.arange(n_tokens)
    row = (t // GRID_W).astype(jnp.float32)
    col = (t % GRID_W).astype(jnp.float32)
    inv = ROPE_THETA ** (-jnp.arange(ROPE_AXIS_PAIRS, dtype=jnp.float32) / ROPE_AXIS_PAIRS)
    ang = jnp.concatenate([row[:, None] * inv, col[:, None] * inv], axis=-1)
    return jnp.cos(ang), jnp.sin(ang)


def apply_rope(x, cos, sin):
    shp = (cos.shape[0],) + (1,) * (x.ndim - 3) + (cos.shape[-1],)
    c, s = cos.reshape(shp), sin.reshape(shp)
    xf = x.astype(jnp.float32)
    x1, x2 = xf[..., :ROPE_HALF], xf[..., ROPE_HALF:]
    return jnp.concatenate([x1 * c - x2 * s, x2 * c + x1 * s], axis=-1).astype(x.dtype)


def project(h, w_in):
    bsz, n = h.shape[:2]
    cuts = np.cumsum(PROJ_WIDTHS)[:-1].tolist()
    qa, ka, va, qb, kb, vb, qc, kc, vc, ga, gb, gc = jnp.split(h @ w_in, cuts, axis=-1)
    qa = qa.reshape(bsz, n, A_KV_HEADS, A_GROUP, HEAD_DIM)
    ka = ka.reshape(bsz, n, A_KV_HEADS, HEAD_DIM)
    va = va.reshape(bsz, n, A_KV_HEADS, HEAD_DIM)
    qb = qb.reshape(bsz, n, B_HEADS, HEAD_DIM)
    kb = kb.reshape(bsz, n, B_HEADS, HEAD_DIM)
    vb = vb.reshape(bsz, n, B_HEADS, HEAD_DIM)
    qc = qc.reshape(bsz, n, C_HEADS, 2, HEAD_DIM)
    kc = kc.reshape(bsz, n, C_HEADS, 2, HEAD_DIM)
    vc = vc.reshape(bsz, n, C_HEADS, C_VDIM)
    return qa, ka, va, qb, kb, vb, qc, kc, vc, ga, gb, gc


def _query_blocks(q):
    bsz, n = q.shape[:2]
    return jnp.moveaxis(q.reshape((bsz, n // BLOCK, BLOCK) + q.shape[2:]), 1, 0)


def _unblock(o):
    nb, bsz = o.shape[:2]
    return jnp.moveaxis(o, 0, 1).reshape((bsz, nb * o.shape[2]) + o.shape[3:])


def dense_attention(q, k, v, sink):
    scale = HEAD_DIM ** -0.5

    def attend(qblk):
        s = jnp.einsum('bqhgd,bkhd->bhgqk', qblk, k).astype(jnp.float32) * scale
        if sink is not None:
            col = jnp.broadcast_to(sink.astype(jnp.float32)[None, :, :, None, None], s.shape[:-1] + (1,))
            p = jax.nn.softmax(jnp.concatenate([s, col], axis=-1), axis=-1)[..., :-1]
        else:
            p = jax.nn.softmax(s, axis=-1)
        return jnp.einsum('bhgqk,bkhd->bqhgd', p.astype(v.dtype), v)

    o = _unblock(lax.map(attend, _query_blocks(q)))
    return o.reshape(o.shape[:2] + (-1,))


def windowed_sink_attention(q, k, v, k_ctx, v_ctx, sink):
    bsz, n = q.shape[:2]
    nb = n // BLOCK
    scale = HEAD_DIM ** -0.5
    qb = q.reshape(bsz, nb, BLOCK, A_KV_HEADS, A_GROUP, HEAD_DIM)

    def band(t):
        tp = jnp.pad(t, ((0, 0), (BLOCK, BLOCK), (0, 0), (0, 0))).reshape(bsz, nb + 2, BLOCK, A_KV_HEADS, HEAD_DIM)
        return jnp.concatenate([tp[:, :-2], tp[:, 1:-1], tp[:, 2:]], axis=2)

    kb, vb = band(k), band(v)
    qpos = jnp.arange(n).reshape(nb, BLOCK)
    kpos = (jnp.arange(nb) * BLOCK - BLOCK)[:, None] + jnp.arange(3 * BLOCK)[None, :]
    kp = kpos[:, None, :]
    valid = (kp >= 0) & (kp < n) & (jnp.abs(kp - qpos[:, :, None]) <= WINDOW)
    s_loc = jnp.einsum('bnqhgd,bnkhd->bnhgqk', qb, kb).astype(jnp.float32) * scale
    s_loc = jnp.where(valid[None, :, None, None], s_loc, NEG_INF)
    s_ctx = jnp.einsum('bnqhgd,bkhd->bnhgqk', qb, k_ctx).astype(jnp.float32) * scale
    sink_col = jnp.broadcast_to(sink.astype(jnp.float32)[None, None, :, :, None, None], s_loc.shape[:-1] + (1,))
    p = jax.nn.softmax(jnp.concatenate([s_loc, s_ctx, sink_col], axis=-1), axis=-1).astype(v.dtype)
    nk = 3 * BLOCK
    o = (jnp.einsum('bnhgqk,bnkhd->bnqhgd', p[..., :nk], vb)
         + jnp.einsum('bnhgqk,bkhd->bnqhgd', p[..., nk:-1], v_ctx))
    return o.reshape(bsz, n, A_HEADS * HEAD_DIM)


def neighborhood_attention(q, k, v, k_ctx, v_ctx, rpb):
    bsz, n = q.shape[:2]
    rows = n // GRID_W
    kh = min(NA_KH, rows)
    scale = HEAD_DIM ** -0.5
    r = jnp.arange(rows)
    row_start = jnp.clip(r - kh // 2, 0, rows - kh)
    row_idx = row_start[:, None] + jnp.arange(kh)[None, :]
    col = jnp.arange(GRID_W)
    col_start = jnp.clip(col - NA_KW // 2, 0, GRID_W - NA_KW)
    col_ok = (col[None, :] >= col_start[:, None]) & (col[None, :] < col_start[:, None] + NA_KW)
    qg = q.reshape(bsz, rows, GRID_W, B_HEADS, HEAD_DIM)
    kg = k.reshape(bsz, rows, GRID_W, B_HEADS, HEAD_DIM)[:, row_idx]
    vg = v.reshape(bsz, rows, GRID_W, B_HEADS, HEAD_DIM)[:, row_idx]
    dr = row_idx - r[:, None] + (NA_KH - 1)
    dc = jnp.clip(col[None, :] - col[:, None] + (NA_KW - 1), 0, 2 * NA_KW - 2)
    bias = rpb.astype(jnp.float32)[:, dr][..., dc]
    bias = jnp.transpose(bias, (1, 0, 3, 2, 4))
    s_loc = jnp.einsum('brqhd,brjkhd->brhqjk', qg, kg).astype(jnp.float32) * scale + bias[None]
    s_loc = jnp.where(col_ok[:, None, :], s_loc, NEG_INF)
    nk = kh * GRID_W
    s_loc = s_loc.reshape(bsz, rows, B_HEADS, GRID_W, nk)
    s_ctx = jnp.einsum('brqhd,bkhd->brhqk', qg, k_ctx).astype(jnp.float32) * scale
    p = jax.nn.softmax(jnp.concatenate([s_loc, s_ctx], axis=-1), axis=-1).astype(v.dtype)
    o = (jnp.einsum('brhqj,brjhd->brqhd', p[..., :nk], vg.reshape(bsz, rows, nk, B_HEADS, HEAD_DIM))
         + jnp.einsum('brhqk,bkhd->brqhd', p[..., nk:], v_ctx))
    return o.reshape(bsz, n, B_HEADS * HEAD_DIM)


def diff_lambda(lq, lambda_init):
    lf = lq.astype(jnp.float32)
    return jnp.exp(jnp.sum(lf[0] * lf[1])) - jnp.exp(jnp.sum(lf[2] * lf[3])) + lambda_init


def diff_attention(q, k, v, lam, lambda_init, subln_g):
    bsz, n = q.shape[:2]
    scale = HEAD_DIM ** -0.5

    def attend(qblk):
        s = jnp.einsum('bqhcd,bkhcd->bhcqk', qblk, k).astype(jnp.float32) * scale
        p = jax.nn.softmax(s, axis=-1)
        a = p[:, :, 0] - lam * p[:, :, 1]
        return jnp.einsum('bhqk,bkhe->bqhe', a.astype(v.dtype), v)

    o = _unblock(lax.map(attend, _query_blocks(q)))
    o = rms_norm(o, subln_g) * (1.0 - lambda_init)
    return o.reshape(bsz, n, C_HEADS * C_VDIM)


def merge_branches(oa, ob, oc, ga, gb, gc, w_branch, w_out):
    y = (jax.nn.sigmoid(ga) * (oa @ w_branch[0])
         + jax.nn.sigmoid(gb) * (ob @ w_branch[1])
         + jax.nn.sigmoid(gc) * (oc @ w_branch[2]))
    return y @ w_out


def moe_ffn(h, w_router, b_router, w_gu, b_gu, w_down, b_down):
    bsz, n_tok, d = h.shape
    n = bsz * n_tok
    x = h.reshape(n, d)
    logits = (x @ w_router + b_router).astype(jnp.float32)
    top_val, top_idx = lax.top_k(logits, TOP_K)
    gates = jax.nn.softmax(top_val, axis=-1)
    e_flat = top_idx.reshape(-1)
    tok_flat = jnp.repeat(jnp.arange(n, dtype=jnp.int32), TOP_K)
    g_flat = gates.reshape(-1)
    order = jnp.argsort(e_flat)
    e_s, tok_s, g_s = e_flat[order], tok_flat[order], g_flat[order]
    counts = jnp.bincount(e_flat, length=N_EXPERTS)
    starts = jnp.cumsum(counts) - counts
    padded = (counts + MOE_BLOCK - 1) // MOE_BLOCK * MOE_BLOCK
    pad_end = jnp.cumsum(padded)
    pad_start = pad_end - padded
    dest = pad_start[e_s] + jnp.arange(n * TOP_K) - starts[e_s]
    n_blocks = (n * TOP_K + N_EXPERTS * (MOE_BLOCK - 1) + MOE_BLOCK - 1) // MOE_BLOCK
    m = n_blocks * MOE_BLOCK
    row_tok = jnp.full((m,), n, jnp.int32).at[dest].set(tok_s)
    row_g = jnp.zeros((m,), jnp.float32).at[dest].set(g_s)
    block_e = jnp.minimum(jnp.searchsorted(pad_end, jnp.arange(n_blocks) * MOE_BLOCK, side='right'), N_EXPERTS - 1)
    x_pad = jnp.concatenate([x, jnp.zeros((1, d), x.dtype)], axis=0)
    xb = x_pad[row_tok].reshape(n_blocks, MOE_BLOCK, d)

    def expert_block(args):
        xblk, e = args
        gu = xblk @ w_gu[e] + b_gu[e]
        gate = jnp.minimum(gu[:, :D_FF], SWIGLU_LIMIT)
        lin = jnp.clip(gu[:, D_FF:], -SWIGLU_LIMIT, SWIGLU_LIMIT)
        act = gate * jax.nn.sigmoid(SWIGLU_ALPHA * gate) * (lin + 1.0)
        return act @ w_down[e] + b_down[e]

    yb = lax.map(expert_block, (xb, block_e)).reshape(m, d)
    y = yb * row_g[:, None].astype(yb.dtype)
    out = jax.ops.segment_sum(y, row_tok, num_segments=n + 1)[:n]
    return out.reshape(bsz, n_tok, d)


def setup_inputs(seed: int = 0) -> dict:
    key = jax.random.key(seed)
    ks = jax.random.split(key, 32)

    def nrm(k, shape, s):
        return s * jax.random.normal(k, shape, jnp.float32)

    return {
        'x_prompt': nrm(ks[0], (BATCH, SEQ, D_MODEL), 1.0),
        'x_sample': nrm(ks[1], (DEC_BATCH, DEC_SEQ, D_MODEL), 1.0),
        'cache_k_a': nrm(ks[2], (DEC_BATCH, DEPTH, PAST_LEN, A_KV_HEADS, HEAD_DIM), 1.0),
        'cache_v_a': nrm(ks[3], (DEC_BATCH, DEPTH, PAST_LEN, A_KV_HEADS, HEAD_DIM), 1.0),
        'cache_k_b': nrm(ks[4], (DEC_BATCH, DEPTH, PAST_LEN, B_HEADS, HEAD_DIM), 1.0),
        'cache_v_b': nrm(ks[5], (DEC_BATCH, DEPTH, PAST_LEN, B_HEADS, HEAD_DIM), 1.0),
        'cache_k_c': nrm(ks[6], (DEC_BATCH, DEPTH, PAST_LEN, C_HEADS, 2 * HEAD_DIM), 1.0),
        'cache_v_c': nrm(ks[7], (DEC_BATCH, DEPTH, PAST_LEN, C_HEADS, C_VDIM), 1.0),
        'c': nrm(ks[8], (DEC_BATCH, D_MODEL), 1.0),
        'c_ctx': nrm(ks[9], (D_MODEL,), 1.0),
        'w_ada': nrm(ks[10], (DEPTH, D_MODEL, N_MOD * D_MODEL), 0.5 * D_MODEL ** -0.5),
        'b_ada': nrm(ks[11], (DEPTH, N_MOD * D_MODEL), 0.02),
        'norm_attn': 1.0 + nrm(ks[12], (DEPTH, D_MODEL), 0.02),
        'norm_ffn': 1.0 + nrm(ks[13], (DEPTH, D_MODEL), 0.02),
        'w_in': nrm(ks[14], (DEPTH, D_MODEL, D_PROJ), D_MODEL ** -0.5),
        'sinks': nrm(ks[15], (DEPTH, A_HEADS), 0.5),
        'rpb': nrm(ks[16], (DEPTH, B_HEADS, 2 * NA_KH - 1, 2 * NA_KW - 1), 0.1),
        'lambda_qk': nrm(ks[17], (DEPTH, 4, HEAD_DIM), 0.1),
        'subln': 1.0 + nrm(ks[18], (DEPTH, C_VDIM), 0.02),
        'w_branch': nrm(ks[19], (DEPTH, N_BRANCH, BRANCH_W, D_MODEL), BRANCH_W ** -0.5),
        'w_out': nrm(ks[20], (DEPTH, D_MODEL, D_MODEL), D_MODEL ** -0.5),
        'w_router': nrm(ks[21], (DEPTH, D_MODEL, N_EXPERTS), D_MODEL ** -0.5),
        'b_router': nrm(ks[22], (DEPTH, N_EXPERTS), 0.01),
        'w_gu': nrm(ks[23], (DEPTH, N_EXPERTS, D_MODEL, 2 * D_FF), D_MODEL ** -0.5),
        'b_gu': nrm(ks[24], (DEPTH, N_EXPERTS, 2 * D_FF), 0.01),
        'w_down': nrm(ks[25], (DEPTH, N_EXPERTS, D_FF, D_MODEL), D_FF ** -0.5),
        'b_down': nrm(ks[26], (DEPTH, N_EXPERTS, D_MODEL), 0.01),
        'norm_final': 1.0 + nrm(ks[27], (D_MODEL,), 0.02),
    }


def reference(x_prompt, x_sample, cache_k_a, cache_v_a, cache_k_b, cache_v_b, cache_k_c, cache_v_c,
              c, c_ctx, w_ada, b_ada, norm_attn, norm_ffn, w_in, sinks, rpb, lambda_qk, subln,
              w_branch, w_out, w_router, b_router, w_gu, b_gu, w_down, b_down, norm_final):
    bsz_d, n_lat = x_sample.shape[:2]
    past = cache_k_c.shape[2]
    cos, sin = axial_rope(n_lat)
    xp, xs = x_prompt, x_sample
    ka_l, va_l, kb_l, vb_l, kc_l, vc_l = [], [], [], [], [], []
    for l in range(DEPTH):
        lambda_init = 0.8 - 0.6 * math.exp(-0.3 * l)
        lam = diff_lambda(lambda_qk[l], lambda_init)
        sink_l = sinks[l].reshape(A_KV_HEADS, A_GROUP)
        mc = modulation(c_ctx[None, :], w_ada[l], b_ada[l])
        ml = modulation(c, w_ada[l], b_ada[l])

        h = modulate(xp, norm_attn[l], mc[0], mc[1])
        qa, ka, va, qb, kb, vb, qc, kc, vc, ga, gb, gc = project(h, w_in[l])
        oa = dense_attention(qa, ka, va, sink_l)
        ob = dense_attention(qb[:, :, :, None], kb, vb, None)
        oc = diff_attention(qc, kc, vc, lam, lambda_init, subln[l])
        xp = xp + mc[2] * merge_branches(oa, ob, oc, ga, gb, gc, w_branch[l], w_out[l])
        xp = xp + mc[5] * moe_ffn(modulate(xp, norm_ffn[l], mc[3], mc[4]),
                                  w_router[l], b_router[l], w_gu[l], b_gu[l], w_down[l], b_down[l])
        ka_l.append(ka)
        va_l.append(va)
        kb_l.append(kb)
        vb_l.append(vb)
        kc_l.append(kc.reshape(kc.shape[0], kc.shape[1], C_HEADS, 2 * HEAD_DIM))
        vc_l.append(vc)

        h = modulate(xs, norm_attn[l], ml[0], ml[1])
        qa, ka, va, qb, kb, vb, qc, kc, vc, ga, gb, gc = project(h, w_in[l])
        qa, ka = apply_rope(qa, cos, sin), apply_rope(ka, cos, sin)
        qc, kc = apply_rope(qc, cos, sin), apply_rope(kc, cos, sin)
        oa = windowed_sink_attention(qa, ka, va, cache_k_a[:, l], cache_v_a[:, l], sink_l)
        ob = neighborhood_attention(qb, kb, vb, cache_k_b[:, l], cache_v_b[:, l], rpb[l])
        kc_all = jnp.concatenate([kc, cache_k_c[:, l].reshape(bsz_d, past, C_HEADS, 2, HEAD_DIM)], axis=1)
        vc_all = jnp.concatenate([vc, cache_v_c[:, l]], axis=1)
        oc = diff_attention(qc, kc_all, vc_all, lam, lambda_init, subln[l])
        xs = xs + ml[2] * merge_branches(oa, ob, oc, ga, gb, gc, w_branch[l], w_out[l])
        xs = xs + ml[5] * moe_ffn(modulate(xs, norm_ffn[l], ml[3], ml[4]),
                                  w_router[l], b_router[l], w_gu[l], b_gu[l], w_down[l], b_down[l])

    y_prompt = rms_norm(xp, norm_final)
    y_sample = rms_norm(xs, norm_final)
    new_k_a = jnp.stack(ka_l, axis=1)
    new_v_a = jnp.stack(va_l, axis=1)
    new_k_b = jnp.stack(kb_l, axis=1)
    new_v_b = jnp.stack(vb_l, axis=1)
    new_k_c = jnp.stack(kc_l, axis=1)
    new_v_c = jnp.stack(vc_l, axis=1)
    return (y_prompt, y_sample, new_k_a, new_v_a, new_k_b, new_v_b, new_k_c, new_v_c)
```

```python
import functools
import math

import jax
import jax.numpy as jnp
import numpy as np
from jax import lax
from jax.experimental import pallas as pl
from jax.experimental.pallas import tpu as pltpu

F32 = jnp.float32
BF16 = jnp.bfloat16

D_MODEL = 1024
DEPTH = 2
GRID_W = 64
HEAD_DIM = 64
ROPE_HALF = HEAD_DIM // 2
ROPE_AXIS_PAIRS = HEAD_DIM // 4
ROPE_THETA = 10000.0
A_HEADS = 8
A_KV_HEADS = 2
WINDOW = 128
B_HEADS = 8
NA_KH = 8
NA_KW = 16
C_HEADS = 4
C_VDIM = 2 * HEAD_DIM
BRANCH_W = 512
N_EXPERTS = 32
TOP_K = 4
D_FF = D_MODEL
SWIGLU_LIMIT = 7.0
SWIGLU_ALPHA = 1.702
N_MOD = 6
EPS = 1e-6
NEG_INF = -1e30
ATTN_SCALE = HEAD_DIM ** -0.5

LANES = 128
VMEM_LIMIT = 56 * 1024 * 1024

A_COLS = 1024
B_COLS = 1536
C_COLS = 1536
G_COLS = 3 * D_MODEL
P_COLS = A_COLS + B_COLS + C_COLS + G_COLS

TM = 256
TQ = 256
NA_ROWS_Q = TQ // GRID_W
NA_ROWS_K = NA_ROWS_Q + NA_KH
NA_KEYS = NA_ROWS_K * GRID_W
WIN_KEYS = TQ + 2 * WINDOW
C_CHUNK = 512
MOE_TM = 256


def _cparams(sem):
    return pltpu.CompilerParams(dimension_semantics=sem, vmem_limit_bytes=VMEM_LIMIT)


def _head_masks():
    lane = lax.broadcasted_iota(jnp.int32, (1, LANES), 1)
    lo = jnp.where(lane < HEAD_DIM, 1.0, 0.0).astype(BF16)
    hi = jnp.where(lane >= HEAD_DIM, 1.0, 0.0).astype(BF16)
    return lo, hi


def _split_heads(q):
    lo, hi = _head_masks()
    return jnp.concatenate([q * lo, q * hi], axis=0)


def _merge_heads(pv, t):
    lane = lax.broadcasted_iota(jnp.int32, (t, LANES), 1)
    return jnp.where(lane < HEAD_DIM, pv[:t], pv[t:])


def _dot_nt(a, b):
    return lax.dot_general(a, b, (((1,), (1,)), ((), ())), preferred_element_type=F32)


def _dot(a, b):
    return jnp.dot(a, b, preferred_element_type=F32)


def _ada_kernel(c_ref, w_ref, b_ref, o_ref):
    c = c_ref[...]
    s = c * (1.0 / (1.0 + jnp.exp(-c)))
    o_ref[0] = jnp.dot(s, w_ref[0], preferred_element_type=F32,
                       precision=lax.Precision.HIGHEST) + b_ref[0]


def _modulation(cond, w_ada, b_ada):
    tn = 1536
    n = N_MOD * D_MODEL
    return pl.pallas_call(
        _ada_kernel,
        grid=(DEPTH, n // tn),
        in_specs=[
            pl.BlockSpec((8, D_MODEL), lambda l, j: (0, 0)),
            pl.BlockSpec((1, D_MODEL, tn), lambda l, j: (l, 0, j)),
            pl.BlockSpec((1, 1, tn), lambda l, j: (l, 0, j)),
        ],
        out_specs=pl.BlockSpec((1, 8, tn), lambda l, j: (l, 0, j)),
        out_shape=jax.ShapeDtypeStruct((DEPTH, 8, n), F32),
        compiler_params=_cparams(("arbitrary", "arbitrary")),
        name="modulation",
    )(cond, w_ada, b_ada.reshape(DEPTH, 1, n))


def _rms_modulate(x, g, shift, scale):
    ms = jnp.mean(x * x, axis=-1, keepdims=True)
    return (x * lax.rsqrt(ms + EPS) * g) * (1.0 + scale) + shift


def _rope_cols(v, cos, sin_signed):
    t, w = v.shape
    lane = lax.broadcasted_iota(jnp.int32, (t, LANES), 1)
    first_half = (lane & (HEAD_DIM - 1)) < ROPE_HALF
    cols = []
    for c in range(w // LANES):
        xc = v[:, c * LANES:(c + 1) * LANES]
        partner = jnp.where(first_half,
                            pltpu.roll(xc, LANES - ROPE_HALF, axis=1),
                            pltpu.roll(xc, ROPE_HALF, axis=1))
        cols.append(xc * cos + partner * sin_signed)
    return jnp.concatenate(cols, axis=1) if len(cols) > 1 else cols[0]


def _proj_kernel(x_ref, mod_ref, g_ref, w_ref, cos_ref, sin_ref, *out_refs, rope, emit_kv):
    oa_ref, ob_ref, oc_ref, og_ref = out_refs[:4]
    h = _rms_modulate(x_ref[...], g_ref[...], mod_ref[0, 0:1, :], mod_ref[0, 1:2, :]).astype(BF16)
    if rope:
        cos = cos_ref[...]
        sin = sin_ref[...]

    def seg(start, width):
        return _dot(h, w_ref[:, start:start + width])

    def maybe_rope(v):
        return _rope_cols(v, cos, sin) if rope else v

    oa_ref[:, 0:512] = (maybe_rope(seg(0, 512)) * ATTN_SCALE).astype(BF16)
    ka = maybe_rope(seg(512, 256))
    va = seg(768, 256)
    oa_ref[:, 512:768] = ka.astype(BF16)
    oa_ref[:, 768:1024] = va.astype(BF16)
    ob_ref[:, 0:512] = (seg(A_COLS, 512) * ATTN_SCALE).astype(BF16)
    kb = seg(A_COLS + 512, 512)
    vb = seg(A_COLS + 1024, 512)
    ob_ref[:, 512:1024] = kb.astype(BF16)
    ob_ref[:, 1024:1536] = vb.astype(BF16)
    c0 = A_COLS + B_COLS
    oc_ref[:, 0:512] = (maybe_rope(seg(c0, 512)) * ATTN_SCALE).astype(BF16)
    kc = maybe_rope(seg(c0 + 512, 512))
    vc = seg(c0 + 1024, 512)
    oc_ref[:, 512:1024] = kc.astype(BF16)
    oc_ref[:, 1024:1536] = vc.astype(BF16)
    g0 = c0 + C_COLS
    for j in range(G_COLS // 512):
        gv = seg(g0 + j * 512, 512)
        og_ref[:, j * 512:(j + 1) * 512] = (1.0 / (1.0 + jnp.exp(-gv))).astype(BF16)
    if emit_kv:
        kva_ref, kvb_ref, kvc_ref = out_refs[4:]
        kva_ref[:, 0:256] = ka
        kva_ref[:, 256:512] = va
        kvb_ref[:, 0:512] = kb
        kvb_ref[:, 512:1024] = vb
        kvc_ref[:, 0:512] = kc
        kvc_ref[:, 512:1024] = vc


def _project(x, mod, mod_row0, tiles_per_row, g, w_ext, cos, sin, *, rope, emit_kv):
    n = x.shape[0]
    nt = n // TM
    pos_tiles = cos.shape[0] // TM
    out_shape = [jax.ShapeDtypeStruct((n, A_COLS), BF16), jax.ShapeDtypeStruct((n, B_COLS), BF16),
                 jax.ShapeDtypeStruct((n, C_COLS), BF16), jax.ShapeDtypeStruct((n, G_COLS), BF16)]
    out_specs = [pl.BlockSpec((TM, A_COLS), lambda i: (i, 0)), pl.BlockSpec((TM, B_COLS), lambda i: (i, 0)),
                 pl.BlockSpec((TM, C_COLS), lambda i: (i, 0)), pl.BlockSpec((TM, G_COLS), lambda i: (i, 0))]
    if emit_kv:
        out_shape += [jax.ShapeDtypeStruct((n, 512), F32), jax.ShapeDtypeStruct((n, 1024), F32),
                      jax.ShapeDtypeStruct((n, 1024), F32)]
        out_specs += [pl.BlockSpec((TM, 512), lambda i: (i, 0)), pl.BlockSpec((TM, 1024), lambda i: (i, 0)),
                      pl.BlockSpec((TM, 1024), lambda i: (i, 0))]
    return pl.pallas_call(
        functools.partial(_proj_kernel, rope=rope, emit_kv=emit_kv),
        grid=(nt,),
        in_specs=[
            pl.BlockSpec((TM, D_MODEL), lambda i: (i, 0)),
            pl.BlockSpec((1, N_MOD, D_MODEL), lambda i: (mod_row0 + i // tiles_per_row, 0, 0)),
            pl.BlockSpec((1, D_MODEL), lambda i: (0, 0)),
            pl.BlockSpec((D_MODEL, P_COLS), lambda i: (0, 0)),
            pl.BlockSpec((TM, LANES), lambda i: (i % pos_tiles, 0)),
            pl.BlockSpec((TM, LANES), lambda i: (i % pos_tiles, 0)),
        ],
        out_specs=out_specs,
        out_shape=out_shape,
        compiler_params=_cparams(("arbitrary",)),
        name="project_rope" if rope else "project",
    )(x, mod, g, w_ext, cos, sin)


def _softmax_pv(scores, values, sink=None):
    m = scores[0].max(axis=-1, keepdims=True)
    for s in scores[1:]:
        m = jnp.maximum(m, s.max(axis=-1, keepdims=True))
    if sink is not None:
        m = jnp.maximum(m, sink)
    l = None
    acc = None
    for s, v in zip(scores, values):
        e = jnp.exp(s - m)
        ls = e.sum(axis=-1, keepdims=True)
        pv = _dot(e.astype(BF16), v)
        l = ls if l is None else l + ls
        acc = pv if acc is None else acc + pv
    if sink is not None:
        l = l + jnp.exp(sink - m)
    return acc * (1.0 / l)


def _sink_column(sink_ref, first_head, rows_per_head, n_heads):
    row = lax.broadcasted_iota(jnp.int32, (rows_per_head * n_heads, 1), 0)
    col = jnp.full((rows_per_head * n_heads, 1), sink_ref[first_head], F32)
    for j in range(1, n_heads):
        col = jnp.where(row >= j * rows_per_head, sink_ref[first_head + j], col)
    return col


def _diff_lambda(lq_ref, lambda_init):
    lf = lq_ref[...]
    a = jnp.sum(lf[0:1] * lf[1:2], axis=-1, keepdims=True)
    b = jnp.sum(lf[2:3] * lf[3:4], axis=-1, keepdims=True)
    return jnp.exp(a) - jnp.exp(b) + lambda_init


def _subln(o, g, lambda_init):
    ms = jnp.mean(o * o, axis=-1, keepdims=True)
    return (o * lax.rsqrt(ms + EPS) * g) * (1.0 - lambda_init)


def _ctx_attn_kernel(sink_ref, lq_ref, sg_ref, a_ref, b_ref, c_ref, oa_ref, ob_ref, oc_ref, *, lambda_init):
    t = a_ref.shape[0]
    for g in range(A_KV_HEADS):
        q = a_ref[:, g * 256:(g + 1) * 256]
        lhs = jnp.concatenate([_split_heads(q[:, :LANES]), _split_heads(q[:, LANES:])], axis=0)
        k = a_ref[:, 512 + g * LANES:512 + (g + 1) * LANES]
        v = a_ref[:, 768 + g * LANES:768 + (g + 1) * LANES]
        sink = _sink_column(sink_ref, 4 * g, t, 4)
        o = _softmax_pv([_dot_nt(lhs, k)], [v], sink)
        oa_ref[:, g * 256:g * 256 + LANES] = _merge_heads(o[:2 * t], t).astype(BF16)
        oa_ref[:, g * 256 + LANES:(g + 1) * 256] = _merge_heads(o[2 * t:], t).astype(BF16)
    for c in range(B_HEADS // 2):
        lhs = _split_heads(b_ref[:, c * LANES:(c + 1) * LANES])
        k = b_ref[:, 512 + c * LANES:512 + (c + 1) * LANES]
        v = b_ref[:, 1024 + c * LANES:1024 + (c + 1) * LANES]
        o = _softmax_pv([_dot_nt(lhs, k)], [v])
        ob_ref[:, c * LANES:(c + 1) * LANES] = _merge_heads(o, t).astype(BF16)
    lam = _diff_lambda(lq_ref, lambda_init)
    for h in range(C_HEADS):
        lhs = _split_heads(c_ref[:, h * LANES:(h + 1) * LANES])
        k = c_ref[:, 512 + h * LANES:512 + (h + 1) * LANES]
        v = c_ref[:, 1024 + h * LANES:1024 + (h + 1) * LANES]
        s = _dot_nt(lhs, k)
        m = s.max(axis=-1, keepdims=True)
        e = jnp.exp(s - m)
        p = e * (1.0 / e.sum(axis=-1, keepdims=True))
        a = (p[:t] - lam * p[t:]).astype(BF16)
        oc_ref[:, h * LANES:(h + 1) * LANES] = _subln(_dot(a, v), sg_ref[...], lambda_init).astype(BF16)


def _ctx_attention(pa, pb, pc, sink, lq, sg, seq, lambda_init):
    n = pa.shape[0]
    smem = pl.BlockSpec(memory_space=pltpu.SMEM)
    return pl.pallas_call(
        functools.partial(_ctx_attn_kernel, lambda_init=lambda_init),
        grid=(n // seq,),
        in_specs=[
            smem,
            pl.BlockSpec((4, HEAD_DIM), lambda b: (0, 0)),
            pl.BlockSpec((1, C_VDIM), lambda b: (0, 0)),
            pl.BlockSpec((seq, A_COLS), lambda b: (b, 0)),
            pl.BlockSpec((seq, B_COLS), lambda b: (b, 0)),
            pl.BlockSpec((seq, C_COLS), lambda b: (b, 0)),
        ],
        out_specs=[pl.BlockSpec((seq, BRANCH_W), lambda b: (b, 0))] * 3,
        out_shape=[jax.ShapeDtypeStruct((n, BRANCH_W), BF16)] * 3,
        compiler_params=_cparams(("arbitrary",)),
        name="context_attention",
    )(sink, lq, sg, pa, pb, pc)


def _win_attn_kernel(sink_ref, q_ref, k_ref, v_ref, kc_ref, vc_ref, o_ref, *, n_lat):
    g = pl.program_id(1)
    qi = pl.program_id(2)
    q = q_ref[...]
    lhs = jnp.concatenate([_split_heads(q[:, :LANES]), _split_heads(q[:, LANES:])], axis=0)
    ws = pl.multiple_of(jnp.clip(qi * TQ - WINDOW, 0, n_lat - WIN_KEYS), WINDOW)
    kwin = k_ref[pl.ds(ws, WIN_KEYS), :]
    vwin = v_ref[pl.ds(ws, WIN_KEYS), :]
    s_loc = _dot_nt(lhs, kwin)
    qpos = qi * TQ + (lax.broadcasted_iota(jnp.int32, s_loc.shape, 0) & (TQ - 1))
    kpos = ws + lax.broadcasted_iota(jnp.int32, s_loc.shape, 1)
    s_loc = jnp.where(jnp.abs(kpos - qpos) <= WINDOW, s_loc, NEG_INF)
    s_ctx = _dot_nt(lhs, kc_ref[0])
    sink = _sink_column(sink_ref, 4 * g, TQ, 4)
    o = _softmax_pv([s_loc, s_ctx], [vwin, vc_ref[0]], sink)
    o_ref[:, :LANES] = _merge_heads(o[:2 * TQ], TQ).astype(BF16)
    o_ref[:, LANES:] = _merge_heads(o[2 * TQ:], TQ).astype(BF16)


def _win_attention(pa, kctx, vctx, sink, n_batch, n_lat):
    nq = n_lat // TQ
    past = kctx.shape[1]
    return pl.pallas_call(
        functools.partial(_win_attn_kernel, n_lat=n_lat),
        grid=(n_batch, A_KV_HEADS, nq),
        in_specs=[
            pl.BlockSpec(memory_space=pltpu.SMEM),
            pl.BlockSpec((TQ, 256), lambda b, g, i: (b * nq + i, g)),
            pl.BlockSpec((n_lat, LANES), lambda b, g, i: (b, 4 + g)),
            pl.BlockSpec((n_lat, LANES), lambda b, g, i: (b, 6 + g)),
            pl.BlockSpec((1, past, LANES), lambda b, g, i: (b, 0, g)),
            pl.BlockSpec((1, past, LANES), lambda b, g, i: (b, 0, g)),
        ],
        out_specs=pl.BlockSpec((TQ, 256), lambda b, g, i: (b * nq + i, g)),
        out_shape=jax.ShapeDtypeStruct((n_batch * n_lat, BRANCH_W), BF16),
        compiler_params=_cparams(("arbitrary", "arbitrary", "arbitrary")),
        name="window_attention",
    )(sink, pa, pa, pa, kctx, vctx)


def _na_window_start(qi, rows):
    return jnp.clip(qi * NA_ROWS_Q - NA_KH // 2, 0, rows - NA_ROWS_K)


def _na_attn_kernel(q_ref, k_ref, v_ref, kc_ref, vc_ref, bias_ref, o_ref, *, rows):
    qi = pl.program_id(2)
    ws = pl.multiple_of(_na_window_start(qi, rows) * GRID_W, GRID_W)
    lhs = _split_heads(q_ref[...])
    s_loc = _dot_nt(lhs, k_ref[pl.ds(ws, NA_KEYS), :]) + bias_ref[0].reshape(2 * TQ, NA_KEYS)
    s_ctx = _dot_nt(lhs, kc_ref[0])
    o = _softmax_pv([s_loc, s_ctx], [v_ref[pl.ds(ws, NA_KEYS), :], vc_ref[0]])
    o_ref[...] = _merge_heads(o, TQ).astype(BF16)


def _na_bias_table(rpb, rows):
    nq = rows // NA_ROWS_Q
    tables = []
    for qi in (0, 1, nq - 1):
        r0 = qi * NA_ROWS_Q
        ws = int(np.clip(r0 - NA_KH // 2, 0, rows - NA_ROWS_K))
        r = r0 + np.arange(NA_ROWS_Q)[:, None, None, None]
        qc = np.arange(GRID_W)[None, :, None, None]
        kr = ws + np.arange(NA_ROWS_K)[None, None, :, None]
        kc = np.arange(GRID_W)[None, None, None, :]
        lo = np.clip(r - NA_KH // 2, 0, rows - NA_KH)
        cs = np.clip(qc - NA_KW // 2, 0, GRID_W - NA_KW)
        ok = (kr >= lo) & (kr < lo + NA_KH) & (kc >= cs) & (kc < cs + NA_KW)
        dr = np.clip(kr - r + NA_KH - 1, 0, 2 * NA_KH - 2)
        dc = np.clip(kc - qc + NA_KW - 1, 0, 2 * NA_KW - 2)
        shape = (NA_ROWS_Q, GRID_W, NA_ROWS_K, GRID_W)
        flat = (np.broadcast_to(dr, shape) * (2 * NA_KW - 1) + np.broadcast_to(dc, shape)).reshape(-1)
        vals = rpb.reshape(B_HEADS, -1)[:, flat].reshape((B_HEADS,) + shape)
        vals = jnp.where(jnp.asarray(np.broadcast_to(ok, shape))[None], vals, NEG_INF)
        tables.append(vals.reshape(B_HEADS, TQ, NA_KEYS))
    return jnp.stack(tables, axis=0)


def _na_attention(pb, kctx, vctx, bias, n_batch, n_lat):
    nq = n_lat // TQ
    rows = n_lat // GRID_W
    past = kctx.shape[1]

    def bias_idx(b, c, i):
        return (jnp.where(i == 0, 0, jnp.where(i == nq - 1, 2, 1)), c, 0, 0)

    return pl.pallas_call(
        functools.partial(_na_attn_kernel, rows=rows),
        grid=(n_batch, B_HEADS // 2, nq),
        in_specs=[
            pl.BlockSpec((TQ, LANES), lambda b, c, i: (b * nq + i, c)),
            pl.BlockSpec((n_lat, LANES), lambda b, c, i: (b, 4 + c)),
            pl.BlockSpec((n_lat, LANES), lambda b, c, i: (b, 8 + c)),
            pl.BlockSpec((1, past, LANES), lambda b, c, i: (b, 0, c)),
            pl.BlockSpec((1, past, LANES), lambda b, c, i: (b, 0, c)),
            pl.BlockSpec((1, 2, TQ, NA_KEYS), bias_idx),
        ],
        out_specs=pl.BlockSpec((TQ, LANES), lambda b, c, i: (b * nq + i, c)),
        out_shape=jax.ShapeDtypeStruct((n_batch * n_lat, BRANCH_W), BF16),
        compiler_params=_cparams(("arbitrary", "arbitrary", "arbitrary")),
        name="neighborhood_attention",
    )(pb, pb, pb, kctx, vctx, bias)


def _diff_attn_kernel(lq_ref, sg_ref, q_ref, k_ref, v_ref, kc_ref, vc_ref, o_ref, *, n_lat, lambda_init):
    lhs = _split_heads(q_ref[...])

    def step(s, v, carry):
        m, l, acc = carry
        m_new = jnp.maximum(m, s.max(axis=-1, keepdims=True))
        alpha = jnp.exp(m - m_new)
        e = jnp.exp(s - m_new)
        l = alpha * l + e.sum(axis=-1, keepdims=True)
        acc = alpha * acc + _dot(e.astype(BF16), v)
        return m_new, l, acc

    def body(j, carry):
        off = pl.multiple_of(j * C_CHUNK, C_CHUNK)
        return step(_dot_nt(lhs, k_ref[pl.ds(off, C_CHUNK), :]), v_ref[pl.ds(off, C_CHUNK), :], carry)

    init = (jnp.full((2 * TQ, 1), NEG_INF, F32), jnp.zeros((2 * TQ, 1), F32), jnp.zeros((2 * TQ, LANES), F32))
    carry = lax.fori_loop(0, n_lat // C_CHUNK, body, init)
    m, l, acc = step(_dot_nt(lhs, kc_ref[0]), vc_ref[0], carry)
    o = acc * (1.0 / l)
    lam = _diff_lambda(lq_ref, lambda_init)
    o_ref[...] = _subln(o[:TQ] - lam * o[TQ:], sg_ref[...], lambda_init).astype(BF16)


def _diff_attention(pc, kctx, vctx, lq, sg, n_batch, n_lat, lambda_init):
    nq = n_lat // TQ
    past = kctx.shape[1]
    return pl.pallas_call(
        functools.partial(_diff_attn_kernel, n_lat=n_lat, lambda_init=lambda_init),
        grid=(n_batch, C_HEADS, nq),
        in_specs=[
            pl.BlockSpec((4, HEAD_DIM), lambda b, h, i: (0, 0)),
            pl.BlockSpec((1, C_VDIM), lambda b, h, i: (0, 0)),
            pl.BlockSpec((TQ, LANES), lambda b, h, i: (b * nq + i, h)),
            pl.BlockSpec((n_lat, LANES), lambda b, h, i: (b, 4 + h)),
            pl.BlockSpec((n_lat, LANES), lambda b, h, i: (b, 8 + h)),
            pl.BlockSpec((1, past, LANES), lambda b, h, i: (b, 0, h)),
            pl.BlockSpec((1, past, LANES), lambda b, h, i: (b, 0, h)),
        ],
        out_specs=pl.BlockSpec((TQ, LANES), lambda b, h, i: (b * nq + i, h)),
        out_shape=jax.ShapeDtypeStruct((n_batch * n_lat, BRANCH_W), BF16),
        compiler_params=_cparams(("arbitrary", "arbitrary", "arbitrary")),
        name="differential_attention",
    )(lq, sg, pc, pc, pc, kctx, vctx)


def _merge_kernel(x_ref, mod_ref, g_ref, oa_ref, ob_ref, oc_ref, sg_ref, wb_ref, wo_ref, wr_ref, br_ref,
                  xo_ref, h_ref, idx_ref, gate_ref):
    y = None
    for j, o_ref in enumerate((oa_ref, ob_ref, oc_ref)):
        t = sg_ref[:, j * D_MODEL:(j + 1) * D_MODEL].astype(F32) * _dot(o_ref[...], wb_ref[j])
        y = t if y is None else y + t
    x = x_ref[...] + mod_ref[0, 2:3, :] * _dot(y.astype(BF16), wo_ref[...])
    xo_ref[...] = x
    h = _rms_modulate(x, g_ref[...], mod_ref[0, 3:4, :], mod_ref[0, 4:5, :])
    h_ref[...] = h.astype(BF16)
    logits = jnp.dot(h, wr_ref[...], preferred_element_type=F32, precision=lax.Precision.HIGHEST) + br_ref[...]
    tm = logits.shape[0]
    lane_e = lax.broadcasted_iota(jnp.int32, (tm, N_EXPERTS), 1)
    lane_o = lax.broadcasted_iota(jnp.int32, (tm, LANES), 1)
    idx_out = jnp.zeros((tm, LANES), jnp.int32)
    val_out = jnp.zeros((tm, LANES), F32)
    top = None
    denom = None
    for k in range(TOP_K):
        mx = logits.max(axis=-1, keepdims=True)
        sel = jnp.min(jnp.where(logits == mx, lane_e, N_EXPERTS), axis=-1, keepdims=True)
        logits = jnp.where(lane_e == sel, -jnp.inf, logits)
        if top is None:
            top = mx
        e = jnp.exp(mx - top)
        denom = e if denom is None else denom + e
        idx_out = jnp.where(lane_o == k, sel, idx_out)
        val_out = jnp.where(lane_o == k, e, val_out)
    idx_ref[...] = idx_out
    gate_ref[...] = val_out * (1.0 / denom)


def _merge_route(x, mod, mod_row0, tiles_per_row, g_ffn, oa, ob, oc, sg, wb, wo, wr, br):
    n = x.shape[0]
    row = lambda i: (i, 0)
    fixed2 = lambda i: (0, 0)
    return pl.pallas_call(
        _merge_kernel,
        grid=(n // TM,),
        in_specs=[
            pl.BlockSpec((TM, D_MODEL), row),
            pl.BlockSpec((1, N_MOD, D_MODEL), lambda i: (mod_row0 + i // tiles_per_row, 0, 0)),
            pl.BlockSpec((1, D_MODEL), fixed2),
            pl.BlockSpec((TM, BRANCH_W), row),
            pl.BlockSpec((TM, BRANCH_W), row),
            pl.BlockSpec((TM, BRANCH_W), row),
            pl.BlockSpec((TM, G_COLS), row),
            pl.BlockSpec((3, BRANCH_W, D_MODEL), lambda i: (0, 0, 0)),
            pl.BlockSpec((D_MODEL, D_MODEL), fixed2),
            pl.BlockSpec((D_MODEL, N_EXPERTS), fixed2),
            pl.BlockSpec((1, N_EXPERTS), fixed2),
        ],
        out_specs=[pl.BlockSpec((TM, D_MODEL), row), pl.BlockSpec((TM, D_MODEL), row),
                   pl.BlockSpec((TM, LANES), row), pl.BlockSpec((TM, LANES), row)],
        out_shape=[jax.ShapeDtypeStruct((n, D_MODEL), F32), jax.ShapeDtypeStruct((n, D_MODEL), BF16),
                   jax.ShapeDtypeStruct((n, LANES), jnp.int32), jax.ShapeDtypeStruct((n, LANES), F32)],
        compiler_params=_cparams(("arbitrary",)),
        name="merge_route",
    )(x, mod, g_ffn, oa, ob, oc, sg, wb, wo, wr, br)


def _moe_kernel(be_ref, first_ref, valid_ref, x_ref, g_ref, wgu_ref, bgu_ref, wd_ref, bd_ref, o_ref,
                wgu_s, wd_s):
    i = pl.program_id(0)

    @pl.when(first_ref[i] == 1)
    def _():
        wgu_s[...] = wgu_ref[0].astype(BF16)
        wd_s[...] = wd_ref[0].astype(BF16)

    @pl.when(valid_ref[i] == 1)
    def _():
        gu = _dot(x_ref[...], wgu_s[...]) + bgu_ref[0]
        gate = jnp.minimum(gu[:, :D_FF], SWIGLU_LIMIT)
        lin = jnp.clip(gu[:, D_FF:], -SWIGLU_LIMIT, SWIGLU_LIMIT)
        act = gate * (1.0 / (1.0 + jnp.exp(-SWIGLU_ALPHA * gate))) * (lin + 1.0)
        y = _dot(act.astype(BF16), wd_s[...]) + bd_ref[0]
        o_ref[...] = y * g_ref[...]

    @pl.when(valid_ref[i] == 0)
    def _():
        o_ref[...] = jnp.zeros_like(o_ref)


def _moe_experts(xb, row_g, block_e, block_first, block_valid, w_gu, b_gu, w_down, b_down):
    m = xb.shape[0]
    nb = m // MOE_TM
    grid_spec = pltpu.PrefetchScalarGridSpec(
        num_scalar_prefetch=3,
        grid=(nb,),
        in_specs=[
            pl.BlockSpec((MOE_TM, D_MODEL), lambda i, be, bf, bv: (i, 0)),
            pl.BlockSpec((MOE_TM, 1), lambda i, be, bf, bv: (i, 0)),
            pl.BlockSpec((1, D_MODEL, 2 * D_FF), lambda i, be, bf, bv: (be[i], 0, 0)),
            pl.BlockSpec((1, 1, 2 * D_FF), lambda i, be, bf, bv: (be[i], 0, 0)),
            pl.BlockSpec((1, D_FF, D_MODEL), lambda i, be, bf, bv: (be[i], 0, 0)),
            pl.BlockSpec((1, 1, D_MODEL), lambda i, be, bf, bv: (be[i], 0, 0)),
        ],
        out_specs=pl.BlockSpec((MOE_TM, D_MODEL), lambda i, be, bf, bv: (i, 0)),
        scratch_shapes=[pltpu.VMEM((D_MODEL, 2 * D_FF), BF16), pltpu.VMEM((D_FF, D_MODEL), BF16)],
    )
    return pl.pallas_call(
        _moe_kernel,
        grid_spec=grid_spec,
        out_shape=jax.ShapeDtypeStruct((m, D_MODEL), F32),
        compiler_params=_cparams(("arbitrary",)),
        name="moe_experts",
    )(block_e, block_first, block_valid, xb, row_g, w_gu, b_gu.reshape(N_EXPERTS, 1, 2 * D_FF),
      w_down, b_down.reshape(N_EXPERTS, 1, D_MODEL))


def _moe_ffn(h, idx, gates, w_gu, b_gu, w_down, b_down):
    n = h.shape[0]
    na = n * TOP_K
    nb = (na + N_EXPERTS * (MOE_TM - 1) + MOE_TM - 1) // MOE_TM
    m = nb * MOE_TM
    e_flat = idx.reshape(-1)
    order = jnp.argsort(e_flat).astype(jnp.int32)
    e_s = e_flat[order]
    counts = jnp.zeros((N_EXPERTS,), jnp.int32).at[e_flat].add(1)
    starts = jnp.cumsum(counts) - counts
    padded = (counts + MOE_TM - 1) // MOE_TM * MOE_TM
    pad_end = jnp.cumsum(padded)
    pad_start = pad_end - padded
    dest_s = pad_start[e_s] + jnp.arange(na, dtype=jnp.int32) - starts[e_s]
    row_tok = jnp.zeros((m,), jnp.int32).at[dest_s].set(order // TOP_K)
    row_g = jnp.zeros((m,), F32).at[dest_s].set(gates.reshape(-1)[order])
    dest = jnp.zeros((na,), jnp.int32).at[order].set(dest_s)
    blk = jnp.arange(nb, dtype=jnp.int32) * MOE_TM
    block_e = jnp.minimum(jnp.searchsorted(pad_end, blk, side='right'), N_EXPERTS - 1).astype(jnp.int32)
    block_valid = (blk < pad_end[-1]).astype(jnp.int32)
    prev = jnp.concatenate([jnp.full((1,), -1, jnp.int32), block_e[:-1]])
    block_first = (block_e != prev).astype(jnp.int32)
    yb = _moe_experts(h[row_tok], row_g[:, None], block_e, block_first, block_valid, w_gu, b_gu, w_down, b_down)
    return yb[dest.reshape(n, TOP_K)].sum(axis=1)


def _final_kernel(x_ref, g_ref, o_ref):
    x = x_ref[...]
    ms = jnp.mean(x * x, axis=-1, keepdims=True)
    o_ref[...] = x * lax.rsqrt(ms + EPS) * g_ref[...]


def _final_norm(x, g):
    n = x.shape[0]
    return pl.pallas_call(
        _final_kernel,
        grid=(n // TM,),
        in_specs=[pl.BlockSpec((TM, D_MODEL), lambda i: (i, 0)), pl.BlockSpec((1, D_MODEL), lambda i: (0, 0))],
        out_specs=pl.BlockSpec((TM, D_MODEL), lambda i: (i, 0)),
        out_shape=jax.ShapeDtypeStruct((n, D_MODEL), F32),
        compiler_params=_cparams(("arbitrary",)),
        name="final_norm",
    )(x, g)


def _rope_tables(n_tokens):
    t = np.arange(n_tokens)
    row = (t // GRID_W).astype(np.float32)
    col = (t % GRID_W).astype(np.float32)
    inv = jnp.asarray(ROPE_THETA, F32) ** (-jnp.arange(ROPE_AXIS_PAIRS, dtype=F32) / ROPE_AXIS_PAIRS)
    ang = jnp.concatenate([jnp.asarray(row)[:, None] * inv, jnp.asarray(col)[:, None] * inv], axis=-1)
    cos, sin = jnp.cos(ang), jnp.sin(ang)
    cos = jnp.tile(cos, (1, LANES // ROPE_HALF))
    sin = jnp.tile(jnp.concatenate([-sin, sin], axis=-1), (1, LANES // HEAD_DIM))
    return cos, sin


def _extended_w_in(w):
    qa, ka, va, rest = w[:, :512], w[:, 512:640], w[:, 640:768], w[:, 768:]

    def dup(t):
        t = t.reshape(D_MODEL, A_KV_HEADS, 1, HEAD_DIM)
        return jnp.broadcast_to(t, (D_MODEL, A_KV_HEADS, 2, HEAD_DIM)).reshape(D_MODEL, 4 * HEAD_DIM)

    return jnp.concatenate([qa, dup(ka), dup(va), rest], axis=-1).astype(BF16)


def _dup_kv_heads(t):
    b, p = t.shape[:2]
    t = jnp.broadcast_to(t[:, :, :, None, :], (b, p, A_KV_HEADS, 2, HEAD_DIM))
    return t.reshape(b, p, 4 * HEAD_DIM).astype(BF16)


def kernel(x_prompt, x_sample, cache_k_a, cache_v_a, cache_k_b, cache_v_b, cache_k_c, cache_v_c, c, c_ctx,
           w_ada, b_ada, norm_attn, norm_ffn, w_in, sinks, rpb, lambda_qk, subln, w_branch, w_out,
           w_router, b_router, w_gu, b_gu, w_down, b_down, norm_final):
    bsz, seq = x_prompt.shape[:2]
    bsz_d, n_lat = x_sample.shape[:2]
    past = cache_k_a.shape[2]
    n_p = bsz * seq
    n_s = bsz_d * n_lat
    assert seq % TM == 0 and n_lat % TQ == 0 and n_lat % C_CHUNK == 0 and bsz_d + 1 <= 8

    cond = jnp.zeros((8, D_MODEL), F32).at[0].set(c_ctx).at[1:1 + bsz_d].set(c)
    mod_all = _modulation(cond, w_ada, b_ada).reshape(DEPTH, 8, N_MOD, D_MODEL)
    cos, sin = _rope_tables(n_lat)

    xp = x_prompt.reshape(n_p, D_MODEL)
    xs = x_sample.reshape(n_s, D_MODEL)
    kv_out = []
    for l in range(DEPTH):
        lambda_init = 0.8 - 0.6 * math.exp(-0.3 * l)
        mod = mod_all[l]
        w_ext = _extended_w_in(w_in[l])
        g_attn = norm_attn[l][None, :]
        g_ffn = norm_ffn[l][None, :]
        sg = subln[l][None, :]
        wb = w_branch[l].astype(BF16)
        wo = w_out[l].astype(BF16)
        br = b_router[l][None, :]

        pa, pb, pc, pg, kva, kvb, kvc = _project(xp, mod, 0, n_p // TM, g_attn, w_ext, cos, sin,
                                                 rope=False, emit_kv=True)
        oa, ob, oc = _ctx_attention(pa, pb, pc, sinks[l], lambda_qk[l], sg, seq, lambda_init)
        xp_mid, hp, idx_p, gate_p = _merge_route(xp, mod, 0, n_p // TM, g_ffn, oa, ob, oc, pg, wb, wo,
                                                 w_router[l], br)
        kv_out.append((kva, kvb, kvc))

        pa, pb, pc, pg = _project(xs, mod, 1, n_lat // TM, g_attn, w_ext, cos, sin, rope=True, emit_kv=False)
        oa = _win_attention(pa, _dup_kv_heads(cache_k_a[:, l]), _dup_kv_heads(cache_v_a[:, l]), sinks[l],
                            bsz_d, n_lat)
        ob = _na_attention(pb, cache_k_b[:, l].reshape(bsz_d, past, -1).astype(BF16),
                           cache_v_b[:, l].reshape(bsz_d, past, -1).astype(BF16),
                           _na_bias_table(rpb[l], n_lat // GRID_W), bsz_d, n_lat)
        oc = _diff_attention(pc, cache_k_c[:, l].reshape(bsz_d, past, -1).astype(BF16),
                             cache_v_c[:, l].reshape(bsz_d, past, -1).astype(BF16),
                             lambda_qk[l], sg, bsz_d, n_lat, lambda_init)
        xs_mid, hs, idx_s, gate_s = _merge_route(xs, mod, 1, n_lat // TM, g_ffn, oa, ob, oc, pg, wb, wo,
                                                 w_router[l], br)

        ffn = _moe_ffn(jnp.concatenate([hp, hs], axis=0),
                       jnp.concatenate([idx_p[:, :TOP_K], idx_s[:, :TOP_K]], axis=0),
                       jnp.concatenate([gate_p[:, :TOP_K], gate_s[:, :TOP_K]], axis=0),
                       w_gu[l], b_gu[l], w_down[l], b_down[l])
        xp = xp_mid + mod[0, 5][None, :] * ffn[:n_p]
        gate_s_rows = jnp.repeat(mod[1:1 + bsz_d, 5], n_lat, axis=0)
        xs = xs_mid + gate_s_rows * ffn[n_p:]

    y_prompt = _final_norm(xp, norm_final[None, :]).reshape(bsz, seq, D_MODEL)
    y_sample = _final_norm(xs, norm_final[None, :]).reshape(bsz_d, n_lat, D_MODEL)

    def stack(pick):
        return jnp.stack([pick(kv) for kv in kv_out], axis=1)

    def heads_a(t):
        return t.reshape(bsz, seq, A_KV_HEADS, 2, HEAD_DIM)[:, :, :, 0]

    new_k_a = stack(lambda kv: heads_a(kv[0][:, 0:256]))
    new_v_a = stack(lambda kv: heads_a(kv[0][:, 256:512]))
    new_k_b = stack(lambda kv: kv[1][:, 0:512].reshape(bsz, seq, B_HEADS, HEAD_DIM))
    new_v_b = stack(lambda kv: kv[1][:, 512:1024].reshape(bsz, seq, B_HEADS, HEAD_DIM))
    new_k_c = stack(lambda kv: kv[2][:, 0:512].reshape(bsz, seq, C_HEADS, 2 * HEAD_DIM))
    new_v_c = stack(lambda kv: kv[2][:, 512:1024].reshape(bsz, seq, C_HEADS, C_VDIM))
    return (y_prompt, y_sample, new_k_a, new_v_a, new_k_b, new_v_b, new_k_c, new_v_c)
```

```python
import functools
import math

import jax
import jax.numpy as jnp
import numpy as np
from jax import lax
from jax.experimental import pallas as pl
from jax.experimental.pallas import tpu as pltpu

F32 = jnp.float32
BF16 = jnp.bfloat16

D_MODEL = 1024
DEPTH = 2
GRID_W = 64
HEAD_DIM = 64
ROPE_HALF = HEAD_DIM // 2
ROPE_AXIS_PAIRS = HEAD_DIM // 4
ROPE_THETA = 10000.0
A_HEADS = 8
A_KV_HEADS = 2
WINDOW = 128
B_HEADS = 8
NA_KH = 8
NA_KW = 16
C_HEADS = 4
C_VDIM = 2 * HEAD_DIM
BRANCH_W = 512
N_EXPERTS = 32
TOP_K = 4
D_FF = D_MODEL
SWIGLU_LIMIT = 7.0
SWIGLU_ALPHA = 1.702
N_MOD = 6
EPS = 1e-6
NEG_INF = -1e30
ATTN_SCALE = HEAD_DIM ** -0.5

LANES = 128
VMEM_LIMIT = 56 * 1024 * 1024

A_COLS = 1024
B_COLS = 1536
C_COLS = 1536
G_COLS = 3 * D_MODEL
P_COLS = A_COLS + B_COLS + C_COLS + G_COLS

TM = 256
TQ = 256
NA_ROWS_Q = TQ // GRID_W
NA_ROWS_K = NA_ROWS_Q + NA_KH
NA_KEYS = NA_ROWS_K * GRID_W
WIN_KEYS = TQ + 2 * WINDOW
C_CHUNK = 512
MOE_TM = 256


def _cparams(sem):
    return pltpu.CompilerParams(dimension_semantics=sem, vmem_limit_bytes=VMEM_LIMIT)


def _head_masks():
    lane = lax.broadcasted_iota(jnp.int32, (1, LANES), 1)
    lo = jnp.where(lane < HEAD_DIM, 1.0, 0.0).astype(BF16)
    hi = jnp.where(lane >= HEAD_DIM, 1.0, 0.0).astype(BF16)
    return lo, hi


def _split_heads(q):
    lo, hi = _head_masks()
    return jnp.concatenate([q * lo, q * hi], axis=0)


def _merge_heads(pv, t):
    lane = lax.broadcasted_iota(jnp.int32, (t, LANES), 1)
    return jnp.where(lane < HEAD_DIM, pv[:t], pv[t:])


def _dot_nt(a, b):
    return lax.dot_general(a, b, (((1,), (1,)), ((), ())), preferred_element_type=F32)


def _dot(a, b):
    return jnp.dot(a, b, preferred_element_type=F32)


def _ada_kernel(c_ref, w_ref, b_ref, o_ref):
    c = c_ref[...]
    s = c * (1.0 / (1.0 + jnp.exp(-c)))
    o_ref[0] = jnp.dot(s, w_ref[0], preferred_element_type=F32,
                       precision=lax.Precision.HIGHEST) + b_ref[0]


def _modulation(cond, w_ada, b_ada):
    tn = 1536
    n = N_MOD * D_MODEL
    return pl.pallas_call(
        _ada_kernel,
        grid=(DEPTH, n // tn),
        in_specs=[
            pl.BlockSpec((8, D_MODEL), lambda l, j: (0, 0)),
            pl.BlockSpec((1, D_MODEL, tn), lambda l, j: (l, 0, j)),
            pl.BlockSpec((1, 1, tn), lambda l, j: (l, 0, j)),
        ],
        out_specs=pl.BlockSpec((1, 8, tn), lambda l, j: (l, 0, j)),
        out_shape=jax.ShapeDtypeStruct((DEPTH, 8, n), F32),
        compiler_params=_cparams(("arbitrary", "arbitrary")),
        name="modulation",
    )(cond, w_ada, b_ada.reshape(DEPTH, 1, n))


def _rms_modulate(x, g, shift, scale):
    ms = jnp.mean(x * x, axis=-1, keepdims=True)
    return (x * lax.rsqrt(ms + EPS) * g) * (1.0 + scale) + shift


def _rope_cols(v, cos, sin_signed):
    t, w = v.shape
    lane = lax.broadcasted_iota(jnp.int32, (t, LANES), 1)
    first_half = (lane & (HEAD_DIM - 1)) < ROPE_HALF
    cols = []
    for c in range(w // LANES):
        xc = v[:, c * LANES:(c + 1) * LANES]
        partner = jnp.where(first_half,
                            pltpu.roll(xc, LANES - ROPE_HALF, axis=1),
                            pltpu.roll(xc, ROPE_HALF, axis=1))
        cols.append(xc * cos + partner * sin_signed)
    return jnp.concatenate(cols, axis=1) if len(cols) > 1 else cols[0]


def _proj_kernel(x_ref, mod_ref, g_ref, w_ref, cos_ref, sin_ref, *out_refs, rope, emit_kv):
    oa_ref, ob_ref, oc_ref, og_ref = out_refs[:4]
    h = _rms_modulate(x_ref[...], g_ref[...], mod_ref[0, 0:1, :], mod_ref[0, 1:2, :]).astype(BF16)
    if rope:
        cos = cos_ref[...]
        sin = sin_ref[...]

    def seg(start, width):
        return _dot(h, w_ref[:, start:start + width])

    def maybe_rope(v):
        return _rope_cols(v, cos, sin) if rope else v

    oa_ref[:, 0:512] = (maybe_rope(seg(0, 512)) * ATTN_SCALE).astype(BF16)
    ka = maybe_rope(seg(512, 256))
    va = seg(768, 256)
    oa_ref[:, 512:768] = ka.astype(BF16)
    oa_ref[:, 768:1024] = va.astype(BF16)
    ob_ref[:, 0:512] = (seg(A_COLS, 512) * ATTN_SCALE).astype(BF16)
    kb = seg(A_COLS + 512, 512)
    vb = seg(A_COLS + 1024, 512)
    ob_ref[:, 512:1024] = kb.astype(BF16)
    ob_ref[:, 1024:1536] = vb.astype(BF16)
    c0 = A_COLS + B_COLS
    oc_ref[:, 0:512] = (maybe_rope(seg(c0, 512)) * ATTN_SCALE).astype(BF16)
    kc = maybe_rope(seg(c0 + 512, 512))
    vc = seg(c0 + 1024, 512)
    oc_ref[:, 512:1024] = kc.astype(BF16)
    oc_ref[:, 1024:1536] = vc.astype(BF16)
    g0 = c0 + C_COLS
    for j in range(G_COLS // 512):
        gv = seg(g0 + j * 512, 512)
        og_ref[:, j * 512:(j + 1) * 512] = (1.0 / (1.0 + jnp.exp(-gv))).astype(BF16)
    if emit_kv:
        kva_ref, kvb_ref, kvc_ref = out_refs[4:]
        kva_ref[:, 0:256] = ka
        kva_ref[:, 256:512] = va
        kvb_ref[:, 0:512] = kb
        kvb_ref[:, 512:1024] = vb
        kvc_ref[:, 0:512] = kc
        kvc_ref[:, 512:1024] = vc


def _project(x, mod, mod_row0, tiles_per_row, g, w_ext, cos, sin, *, rope, emit_kv):
    n = x.shape[0]
    nt = n // TM
    pos_tiles = cos.shape[0] // TM
    out_shape = [jax.ShapeDtypeStruct((n, A_COLS), BF16), jax.ShapeDtypeStruct((n, B_COLS), BF16),
                 jax.ShapeDtypeStruct((n, C_COLS), BF16), jax.ShapeDtypeStruct((n, G_COLS), BF16)]
    out_specs = [pl.BlockSpec((TM, A_COLS), lambda i: (i, 0)), pl.BlockSpec((TM, B_COLS), lambda i: (i, 0)),
                 pl.BlockSpec((TM, C_COLS), lambda i: (i, 0)), pl.BlockSpec((TM, G_COLS), lambda i: (i, 0))]
    if emit_kv:
        out_shape += [jax.ShapeDtypeStruct((n, 512), F32), jax.ShapeDtypeStruct((n, 1024), F32),
                      jax.ShapeDtypeStruct((n, 1024), F32)]
        out_specs += [pl.BlockSpec((TM, 512), lambda i: (i, 0)), pl.BlockSpec((TM, 1024), lambda i: (i, 0)),
                      pl.BlockSpec((TM, 1024), lambda i: (i, 0))]
    return pl.pallas_call(
        functools.partial(_proj_kernel, rope=rope, emit_kv=emit_kv),
        grid=(nt,),
        in_specs=[
            pl.BlockSpec((TM, D_MODEL), lambda i: (i, 0)),
            pl.BlockSpec((1, N_MOD, D_MODEL), lambda i: (mod_row0 + i // tiles_per_row, 0, 0)),
            pl.BlockSpec((1, D_MODEL), lambda i: (0, 0)),
            pl.BlockSpec((D_MODEL, P_COLS), lambda i: (0, 0)),
            pl.BlockSpec((TM, LANES), lambda i: (i % pos_tiles, 0)),
            pl.BlockSpec((TM, LANES), lambda i: (i % pos_tiles, 0)),
        ],
        out_specs=out_specs,
        out_shape=out_shape,
        compiler_params=_cparams(("arbitrary",)),
        name="project_rope" if rope else "project",
    )(x, mod, g, w_ext, cos, sin)


def _softmax_pv(scores, values, sink=None):
    m = scores[0].max(axis=-1, keepdims=True)
    for s in scores[1:]:
        m = jnp.maximum(m, s.max(axis=-1, keepdims=True))
    if sink is not None:
        m = jnp.maximum(m, sink)
    l = None
    acc = None
    for s, v in zip(scores, values):
        e = jnp.exp(s - m)
        ls = e.sum(axis=-1, keepdims=True)
        pv = _dot(e.astype(BF16), v)
        l = ls if l is None else l + ls
        acc = pv if acc is None else acc + pv
    if sink is not None:
        l = l + jnp.exp(sink - m)
    return acc * (1.0 / l)


def _sink_column(sink_ref, first_head, rows_per_head, n_heads):
    row = lax.broadcasted_iota(jnp.int32, (rows_per_head * n_heads, 1), 0)
    col = jnp.full((rows_per_head * n_heads, 1), sink_ref[first_head], F32)
    for j in range(1, n_heads):
        col = jnp.where(row >= j * rows_per_head, sink_ref[first_head + j], col)
    return col


def _diff_lambda(lq_ref, lambda_init):
    lf = lq_ref[...]
    a = jnp.sum(lf[0:1] * lf[1:2], axis=-1, keepdims=True)
    b = jnp.sum(lf[2:3] * lf[3:4], axis=-1, keepdims=True)
    return jnp.exp(a) - jnp.exp(b) + lambda_init


def _subln(o, g, lambda_init):
    ms = jnp.mean(o * o, axis=-1, keepdims=True)
    return (o * lax.rsqrt(ms + EPS) * g) * (1.0 - lambda_init)


def _ctx_attn_kernel(sink_ref, lq_ref, sg_ref, a_ref, b_ref, c_ref, oa_ref, ob_ref, oc_ref, *, lambda_init):
    t = a_ref.shape[0]
    for g in range(A_KV_HEADS):
        q = a_ref[:, g * 256:(g + 1) * 256]
        lhs = jnp.concatenate([_split_heads(q[:, :LANES]), _split_heads(q[:, LANES:])], axis=0)
        k = a_ref[:, 512 + g * LANES:512 + (g + 1) * LANES]
        v = a_ref[:, 768 + g * LANES:768 + (g + 1) * LANES]
        sink = _sink_column(sink_ref, 4 * g, t, 4)
        o = _softmax_pv([_dot_nt(lhs, k)], [v], sink)
        oa_ref[:, g * 256:g * 256 + LANES] = _merge_heads(o[:2 * t], t).astype(BF16)
        oa_ref[:, g * 256 + LANES:(g + 1) * 256] = _merge_heads(o[2 * t:], t).astype(BF16)
    for c in range(B_HEADS // 2):
        lhs = _split_heads(b_ref[:, c * LANES:(c + 1) * LANES])
        k = b_ref[:, 512 + c * LANES:512 + (c + 1) * LANES]
        v = b_ref[:, 1024 + c * LANES:1024 + (c + 1) * LANES]
        o = _softmax_pv([_dot_nt(lhs, k)], [v])
        ob_ref[:, c * LANES:(c + 1) * LANES] = _merge_heads(o, t).astype(BF16)
    lam = _diff_lambda(lq_ref, lambda_init)
    for h in range(C_HEADS):
        lhs = _split_heads(c_ref[:, h * LANES:(h + 1) * LANES])
        k = c_ref[:, 512 + h * LANES:512 + (h + 1) * LANES]
        v = c_ref[:, 1024 + h * LANES:1024 + (h + 1) * LANES]
        s = _dot_nt(lhs, k)
        m = s.max(axis=-1, keepdims=True)
        e = jnp.exp(s - m)
        p = e * (1.0 / e.sum(axis=-1, keepdims=True))
        a = (p[:t] - lam * p[t:]).astype(BF16)
        oc_ref[:, h * LANES:(h + 1) * LANES] = _subln(_dot(a, v), sg_ref[...], lambda_init).astype(BF16)


def _ctx_attention(pa, pb, pc, sink, lq, sg, seq, lambda_init):
    n = pa.shape[0]
    smem = pl.BlockSpec(memory_space=pltpu.SMEM)
    return pl.pallas_call(
        functools.partial(_ctx_attn_kernel, lambda_init=lambda_init),
        grid=(n // seq,),
        in_specs=[
            smem,
            pl.BlockSpec((4, HEAD_DIM), lambda b: (0, 0)),
            pl.BlockSpec((1, C_VDIM), lambda b: (0, 0)),
            pl.BlockSpec((seq, A_COLS), lambda b: (b, 0)),
            pl.BlockSpec((seq, B_COLS), lambda b: (b, 0)),
            pl.BlockSpec((seq, C_COLS), lambda b: (b, 0)),
        ],
        out_specs=[pl.BlockSpec((seq, BRANCH_W), lambda b: (b, 0))] * 3,
        out_shape=[jax.ShapeDtypeStruct((n, BRANCH_W), BF16)] * 3,
        compiler_params=_cparams(("arbitrary",)),
        name="context_attention",
    )(sink, lq, sg, pa, pb, pc)


def _win_attn_kernel(sink_ref, q_ref, k_ref, v_ref, kc_ref, vc_ref, o_ref, *, n_lat):
    g = pl.program_id(1)
    qi = pl.program_id(2)
    q = q_ref[...]
    lhs = jnp.concatenate([_split_heads(q[:, :LANES]), _split_heads(q[:, LANES:])], axis=0)
    ws = pl.multiple_of(jnp.clip(qi * TQ - WINDOW, 0, n_lat - WIN_KEYS), WINDOW)
    kwin = k_ref[pl.ds(ws, WIN_KEYS), :]
    vwin = v_ref[pl.ds(ws, WIN_KEYS), :]
    s_loc = _dot_nt(lhs, kwin)
    qpos = qi * TQ + (lax.broadcasted_iota(jnp.int32, s_loc.shape, 0) & (TQ - 1))
    kpos = ws + lax.broadcasted_iota(jnp.int32, s_loc.shape, 1)
    s_loc = jnp.where(jnp.abs(kpos - qpos) <= WINDOW, s_loc, NEG_INF)
    s_ctx = _dot_nt(lhs, kc_ref[0])
    sink = _sink_column(sink_ref, 4 * g, TQ, 4)
    o = _softmax_pv([s_loc, s_ctx], [vwin, vc_ref[0]], sink)
    o_ref[:, :LANES] = _merge_heads(o[:2 * TQ], TQ).astype(BF16)
    o_ref[:, LANES:] = _merge_heads(o[2 * TQ:], TQ).astype(BF16)


def _win_attention(pa, kctx, vctx, sink, n_batch, n_lat):
    nq = n_lat // TQ
    past = kctx.shape[1]
    return pl.pallas_call(
        functools.partial(_win_attn_kernel, n_lat=n_lat),
        grid=(n_batch, A_KV_HEADS, nq),
        in_specs=[
            pl.BlockSpec(memory_space=pltpu.SMEM),
            pl.BlockSpec((TQ, 256), lambda b, g, i: (b * nq + i, g)),
            pl.BlockSpec((n_lat, LANES), lambda b, g, i: (b, 4 + g)),
            pl.BlockSpec((n_lat, LANES), lambda b, g, i: (b, 6 + g)),
            pl.BlockSpec((1, past, LANES), lambda b, g, i: (b, 0, g)),
            pl.BlockSpec((1, past, LANES), lambda b, g, i: (b, 0, g)),
        ],
        out_specs=pl.BlockSpec((TQ, 256), lambda b, g, i: (b * nq + i, g)),
        out_shape=jax.ShapeDtypeStruct((n_batch * n_lat, BRANCH_W), BF16),
        compiler_params=_cparams(("arbitrary", "arbitrary", "arbitrary")),
        name="window_attention",
    )(sink, pa, pa, pa, kctx, vctx)


def _na_window_start(qi, rows):
    return jnp.clip(qi * NA_ROWS_Q - NA_KH // 2, 0, rows - NA_ROWS_K)


def _na_attn_kernel(q_ref, k_ref, v_ref, kc_ref, vc_ref, bias_ref, o_ref, *, rows):
    qi = pl.program_id(2)
    ws = pl.multiple_of(_na_window_start(qi, rows) * GRID_W, GRID_W)
    lhs = _split_heads(q_ref[...])
    s_loc = _dot_nt(lhs, k_ref[pl.ds(ws, NA_KEYS), :]) + bias_ref[0].reshape(2 * TQ, NA_KEYS)
    s_ctx = _dot_nt(lhs, kc_ref[0])
    o = _softmax_pv([s_loc, s_ctx], [v_ref[pl.ds(ws, NA_KEYS), :], vc_ref[0]])
    o_ref[...] = _merge_heads(o, TQ).astype(BF16)


def _na_bias_kernel(rpb_ref, o_ref, *, layer, rows):
    h = pl.program_id(0)
    n_dr = 2 * NA_KH - 1
    n_dc = 2 * NA_KW - 1
    base = (layer * B_HEADS + h) * n_dr * n_dc
    lane = lax.broadcasted_iota(jnp.int32, (GRID_W, LANES), 1)
    qc = lax.broadcasted_iota(jnp.int32, (GRID_W, LANES), 0)
    kc = lane & (GRID_W - 1)
    second = lane >= GRID_W
    cs = jnp.clip(qc - NA_KW // 2, 0, GRID_W - NA_KW)
    col_ok = (kc >= cs) & (kc < cs + NA_KW)
    dc_map = kc - qc + (NA_KW - 1)
    cache = {}

    def pair_tile(dr0, ok0, ok1):
        key = (dr0 if ok0 else None, dr0 + 1 if ok1 else None)
        if key not in cache:
            if not (ok0 or ok1):
                cache[key] = jnp.full((GRID_W, LANES), NEG_INF, F32)
            else:
                acc = jnp.zeros((GRID_W, LANES), F32)
                for dc in range(n_dc):
                    v0 = rpb_ref[base + dr0 * n_dc + dc] if ok0 else 0.0
                    v1 = rpb_ref[base + (dr0 + 1) * n_dc + dc] if ok1 else 0.0
                    acc = jnp.where(dc_map == dc, jnp.where(second, v1, v0), acc)
                if ok0 and ok1:
                    ok = col_ok
                elif ok0:
                    ok = col_ok & jnp.logical_not(second)
                else:
                    ok = col_ok & second
                cache[key] = jnp.where(ok, acc, NEG_INF)
        return cache[key]

    nq = rows // NA_ROWS_Q
    for p, qi in enumerate((0, 1, nq - 1)):
        r0 = qi * NA_ROWS_Q
        ws = min(max(r0 - NA_KH // 2, 0), rows - NA_ROWS_K)
        for i in range(NA_ROWS_Q):
            r = r0 + i
            lo = min(max(r - NA_KH // 2, 0), rows - NA_KH)
            for jp in range(NA_ROWS_K // 2):
                kr = ws + 2 * jp
                ok0 = lo <= kr < lo + NA_KH
                ok1 = lo <= kr + 1 < lo + NA_KH
                o_ref[p, 0, i * GRID_W:(i + 1) * GRID_W, jp * LANES:(jp + 1) * LANES] = pair_tile(
                    kr - r + NA_KH - 1, ok0, ok1)


def _na_bias_table(rpb_flat, layer, rows):
    return pl.pallas_call(
        functools.partial(_na_bias_kernel, layer=layer, rows=rows),
        grid=(B_HEADS,),
        in_specs=[pl.BlockSpec(memory_space=pltpu.SMEM)],
        out_specs=pl.BlockSpec((3, 1, TQ, NA_KEYS), lambda h: (0, h, 0, 0)),
        out_shape=jax.ShapeDtypeStruct((3, B_HEADS, TQ, NA_KEYS), F32),
        compiler_params=_cparams(("arbitrary",)),
        name="neighborhood_bias",
    )(rpb_flat)


def _na_attention(pb, kctx, vctx, bias, n_batch, n_lat):
    nq = n_lat // TQ
    rows = n_lat // GRID_W
    past = kctx.shape[1]

    def bias_idx(b, c, i):
        return (jnp.where(i == 0, 0, jnp.where(i == nq - 1, 2, 1)), c, 0, 0)

    return pl.pallas_call(
        functools.partial(_na_attn_kernel, rows=rows),
        grid=(n_batch, B_HEADS // 2, nq),
        in_specs=[
            pl.BlockSpec((TQ, LANES), lambda b, c, i: (b * nq + i, c)),
            pl.BlockSpec((n_lat, LANES), lambda b, c, i: (b, 4 + c)),
            pl.BlockSpec((n_lat, LANES), lambda b, c, i: (b, 8 + c)),
            pl.BlockSpec((1, past, LANES), lambda b, c, i: (b, 0, c)),
            pl.BlockSpec((1, past, LANES), lambda b, c, i: (b, 0, c)),
            pl.BlockSpec((1, 2, TQ, NA_KEYS), bias_idx),
        ],
        out_specs=pl.BlockSpec((TQ, LANES), lambda b, c, i: (b * nq + i, c)),
        out_shape=jax.ShapeDtypeStruct((n_batch * n_lat, BRANCH_W), BF16),
        compiler_params=_cparams(("arbitrary", "arbitrary", "arbitrary")),
        name="neighborhood_attention",
    )(pb, pb, pb, kctx, vctx, bias)


def _diff_attn_kernel(lq_ref, sg_ref, q_ref, k_ref, v_ref, kc_ref, vc_ref, o_ref, *, n_lat, lambda_init):
    lhs = _split_heads(q_ref[...])

    def step(s, v, carry):
        m, l, acc = carry
        m_new = jnp.maximum(m, s.max(axis=-1, keepdims=True))
        alpha = jnp.exp(m - m_new)
        e = jnp.exp(s - m_new)
        l = alpha * l + e.sum(axis=-1, keepdims=True)
        acc = alpha * acc + _dot(e.astype(BF16), v)
        return m_new, l, acc

    def body(j, carry):
        off = pl.multiple_of(j * C_CHUNK, C_CHUNK)
        return step(_dot_nt(lhs, k_ref[pl.ds(off, C_CHUNK), :]), v_ref[pl.ds(off, C_CHUNK), :], carry)

    init = (jnp.full((2 * TQ, 1), NEG_INF, F32), jnp.zeros((2 * TQ, 1), F32), jnp.zeros((2 * TQ, LANES), F32))
    carry = lax.fori_loop(0, n_lat // C_CHUNK, body, init)
    m, l, acc = step(_dot_nt(lhs, kc_ref[0]), vc_ref[0], carry)
    o = acc * (1.0 / l)
    lam = _diff_lambda(lq_ref, lambda_init)
    o_ref[...] = _subln(o[:TQ] - lam * o[TQ:], sg_ref[...], lambda_init).astype(BF16)


def _diff_attention(pc, kctx, vctx, lq, sg, n_batch, n_lat, lambda_init):
    nq = n_lat // TQ
    past = kctx.shape[1]
    return pl.pallas_call(
        functools.partial(_diff_attn_kernel, n_lat=n_lat, lambda_init=lambda_init),
        grid=(n_batch, C_HEADS, nq),
        in_specs=[
            pl.BlockSpec((4, HEAD_DIM), lambda b, h, i: (0, 0)),
            pl.BlockSpec((1, C_VDIM), lambda b, h, i: (0, 0)),
            pl.BlockSpec((TQ, LANES), lambda b, h, i: (b * nq + i, h)),
            pl.BlockSpec((n_lat, LANES), lambda b, h, i: (b, 4 + h)),
            pl.BlockSpec((n_lat, LANES), lambda b, h, i: (b, 8 + h)),
            pl.BlockSpec((1, past, LANES), lambda b, h, i: (b, 0, h)),
            pl.BlockSpec((1, past, LANES), lambda b, h, i: (b, 0, h)),
        ],
        out_specs=pl.BlockSpec((TQ, LANES), lambda b, h, i: (b * nq + i, h)),
        out_shape=jax.ShapeDtypeStruct((n_batch * n_lat, BRANCH_W), BF16),
        compiler_params=_cparams(("arbitrary", "arbitrary", "arbitrary")),
        name="differential_attention",
    )(lq, sg, pc, pc, pc, kctx, vctx)


def _merge_kernel(x_ref, mod_ref, g_ref, oa_ref, ob_ref, oc_ref, sg_ref, wb_ref, wo_ref, wr_ref, br_ref,
                  xo_ref, h_ref, idx_ref, gate_ref):
    y = None
    for j, o_ref in enumerate((oa_ref, ob_ref, oc_ref)):
        t = sg_ref[:, j * D_MODEL:(j + 1) * D_MODEL].astype(F32) * _dot(o_ref[...], wb_ref[j])
        y = t if y is None else y + t
    x = x_ref[...] + mod_ref[0, 2:3, :] * _dot(y.astype(BF16), wo_ref[...])
    xo_ref[...] = x
    h = _rms_modulate(x, g_ref[...], mod_ref[0, 3:4, :], mod_ref[0, 4:5, :])
    h_ref[...] = h.astype(BF16)
    logits = jnp.dot(h, wr_ref[...], preferred_element_type=F32, precision=lax.Precision.HIGHEST) + br_ref[...]
    tm = logits.shape[0]
    lane_e = lax.broadcasted_iota(jnp.int32, (tm, N_EXPERTS), 1)
    lane_o = lax.broadcasted_iota(jnp.int32, (tm, LANES), 1)
    idx_out = jnp.zeros((tm, LANES), jnp.int32)
    val_out = jnp.zeros((tm, LANES), F32)
    top = None
    denom = None
    for k in range(TOP_K):
        mx = logits.max(axis=-1, keepdims=True)
        sel = jnp.min(jnp.where(logits == mx, lane_e, N_EXPERTS), axis=-1, keepdims=True)
        logits = jnp.where(lane_e == sel, -jnp.inf, logits)
        if top is None:
            top = mx
        e = jnp.exp(mx - top)
        denom = e if denom is None else denom + e
        idx_out = jnp.where(lane_o == k, sel, idx_out)
        val_out = jnp.where(lane_o == k, e, val_out)
    idx_ref[...] = idx_out
    gate_ref[...] = val_out * (1.0 / denom)


def _merge_route(x, mod, mod_row0, tiles_per_row, g_ffn, oa, ob, oc, sg, wb, wo, wr, br):
    n = x.shape[0]
    row = lambda i: (i, 0)
    fixed2 = lambda i: (0, 0)
    return pl.pallas_call(
        _merge_kernel,
        grid=(n // TM,),
        in_specs=[
            pl.BlockSpec((TM, D_MODEL), row),
            pl.BlockSpec((1, N_MOD, D_MODEL), lambda i: (mod_row0 + i // tiles_per_row, 0, 0)),
            pl.BlockSpec((1, D_MODEL), fixed2),
            pl.BlockSpec((TM, BRANCH_W), row),
            pl.BlockSpec((TM, BRANCH_W), row),
            pl.BlockSpec((TM, BRANCH_W), row),
            pl.BlockSpec((TM, G_COLS), row),
            pl.BlockSpec((3, BRANCH_W, D_MODEL), lambda i: (0, 0, 0)),
            pl.BlockSpec((D_MODEL, D_MODEL), fixed2),
            pl.BlockSpec((D_MODEL, N_EXPERTS), fixed2),
            pl.BlockSpec((1, N_EXPERTS), fixed2),
        ],
        out_specs=[pl.BlockSpec((TM, D_MODEL), row), pl.BlockSpec((TM, D_MODEL), row),
                   pl.BlockSpec((TM, LANES), row), pl.BlockSpec((TM, LANES), row)],
        out_shape=[jax.ShapeDtypeStruct((n, D_MODEL), F32), jax.ShapeDtypeStruct((n, D_MODEL), BF16),
                   jax.ShapeDtypeStruct((n, LANES), jnp.int32), jax.ShapeDtypeStruct((n, LANES), F32)],
        compiler_params=_cparams(("arbitrary",)),
        name="merge_route",
    )(x, mod, g_ffn, oa, ob, oc, sg, wb, wo, wr, br)


def _moe_kernel(be_ref, first_ref, valid_ref, x_ref, g_ref, wgu_ref, bgu_ref, wd_ref, bd_ref, o_ref,
                wgu_s, wd_s):
    i = pl.program_id(0)

    @pl.when(first_ref[i] == 1)
    def _():
        wgu_s[...] = wgu_ref[0, 0].astype(BF16)
        wd_s[...] = wd_ref[0, 0].astype(BF16)

    @pl.when(valid_ref[i] == 1)
    def _():
        gu = _dot(x_ref[...], wgu_s[...]) + bgu_ref[0, 0]
        gate = jnp.minimum(gu[:, :D_FF], SWIGLU_LIMIT)
        lin = jnp.clip(gu[:, D_FF:], -SWIGLU_LIMIT, SWIGLU_LIMIT)
        act = gate * (1.0 / (1.0 + jnp.exp(-SWIGLU_ALPHA * gate))) * (lin + 1.0)
        y = _dot(act.astype(BF16), wd_s[...]) + bd_ref[0, 0]
        o_ref[...] = y * g_ref[...]

    @pl.when(valid_ref[i] == 0)
    def _():
        o_ref[...] = jnp.zeros_like(o_ref)


def _moe_experts(xb, row_g, block_e, block_first, block_valid, layer, w_gu, b_gu, w_down, b_down):
    m = xb.shape[0]
    nb = m // MOE_TM
    grid_spec = pltpu.PrefetchScalarGridSpec(
        num_scalar_prefetch=3,
        grid=(nb,),
        in_specs=[
            pl.BlockSpec((MOE_TM, D_MODEL), lambda i, be, bf, bv: (i, 0)),
            pl.BlockSpec((MOE_TM, 1), lambda i, be, bf, bv: (i, 0)),
            pl.BlockSpec((1, 1, D_MODEL, 2 * D_FF), lambda i, be, bf, bv: (layer, be[i], 0, 0)),
            pl.BlockSpec((1, 1, 1, 2 * D_FF), lambda i, be, bf, bv: (layer, be[i], 0, 0)),
            pl.BlockSpec((1, 1, D_FF, D_MODEL), lambda i, be, bf, bv: (layer, be[i], 0, 0)),
            pl.BlockSpec((1, 1, 1, D_MODEL), lambda i, be, bf, bv: (layer, be[i], 0, 0)),
        ],
        out_specs=pl.BlockSpec((MOE_TM, D_MODEL), lambda i, be, bf, bv: (i, 0)),
        scratch_shapes=[pltpu.VMEM((D_MODEL, 2 * D_FF), BF16), pltpu.VMEM((D_FF, D_MODEL), BF16)],
    )
    return pl.pallas_call(
        _moe_kernel,
        grid_spec=grid_spec,
        out_shape=jax.ShapeDtypeStruct((m, D_MODEL), F32),
        compiler_params=_cparams(("arbitrary",)),
        name="moe_experts",
    )(block_e, block_first, block_valid, xb, row_g, w_gu, b_gu.reshape(DEPTH, N_EXPERTS, 1, 2 * D_FF),
      w_down, b_down.reshape(DEPTH, N_EXPERTS, 1, D_MODEL))


def _moe_ffn(h, idx, gates, layer, w_gu, b_gu, w_down, b_down):
    n = h.shape[0]
    na = n * TOP_K
    nb = (na + N_EXPERTS * (MOE_TM - 1) + MOE_TM - 1) // MOE_TM
    m = nb * MOE_TM
    e_flat = idx.reshape(-1)
    order = jnp.argsort(e_flat).astype(jnp.int32)
    e_s = e_flat[order]
    counts = jnp.zeros((N_EXPERTS,), jnp.int32).at[e_flat].add(1)
    starts = jnp.cumsum(counts) - counts
    padded = (counts + MOE_TM - 1) // MOE_TM * MOE_TM
    pad_end = jnp.cumsum(padded)
    pad_start = pad_end - padded
    dest_s = pad_start[e_s] + jnp.arange(na, dtype=jnp.int32) - starts[e_s]
    row_tok = jnp.zeros((m,), jnp.int32).at[dest_s].set(order // TOP_K)
    row_g = jnp.zeros((m,), F32).at[dest_s].set(gates.reshape(-1)[order])
    dest = jnp.zeros((na,), jnp.int32).at[order].set(dest_s)
    blk = jnp.arange(nb, dtype=jnp.int32) * MOE_TM
    block_e = jnp.minimum(jnp.searchsorted(pad_end, blk, side='right'), N_EXPERTS - 1).astype(jnp.int32)
    block_valid = (blk < pad_end[-1]).astype(jnp.int32)
    prev = jnp.concatenate([jnp.full((1,), -1, jnp.int32), block_e[:-1]])
    block_first = (block_e != prev).astype(jnp.int32)
    yb = _moe_experts(h[row_tok], row_g[:, None], block_e, block_first, block_valid, layer,
                      w_gu, b_gu, w_down, b_down)
    return yb[dest.reshape(n, TOP_K)].sum(axis=1)


def _final_kernel(x_ref, g_ref, o_ref):
    x = x_ref[...]
    ms = jnp.mean(x * x, axis=-1, keepdims=True)
    o_ref[...] = x * lax.rsqrt(ms + EPS) * g_ref[...]


def _final_norm(x, g):
    n = x.shape[0]
    return pl.pallas_call(
        _final_kernel,
        grid=(n // TM,),
        in_specs=[pl.BlockSpec((TM, D_MODEL), lambda i: (i, 0)), pl.BlockSpec((1, D_MODEL), lambda i: (0, 0))],
        out_specs=pl.BlockSpec((TM, D_MODEL), lambda i: (i, 0)),
        out_shape=jax.ShapeDtypeStruct((n, D_MODEL), F32),
        compiler_params=_cparams(("arbitrary",)),
        name="final_norm",
    )(x, g)


def _rope_tables(n_tokens):
    t = np.arange(n_tokens)
    row = (t // GRID_W).astype(np.float32)
    col = (t % GRID_W).astype(np.float32)
    inv = jnp.asarray(ROPE_THETA, F32) ** (-jnp.arange(ROPE_AXIS_PAIRS, dtype=F32) / ROPE_AXIS_PAIRS)
    ang = jnp.concatenate([jnp.asarray(row)[:, None] * inv, jnp.asarray(col)[:, None] * inv], axis=-1)
    cos, sin = jnp.cos(ang), jnp.sin(ang)
    cos = jnp.tile(cos, (1, LANES // ROPE_HALF))
    sin = jnp.tile(jnp.concatenate([-sin, sin], axis=-1), (1, LANES // HEAD_DIM))
    return cos, sin


def _extended_w_in(w):
    qa, ka, va, rest = w[:, :512], w[:, 512:640], w[:, 640:768], w[:, 768:]

    def dup(t):
        t = t.reshape(D_MODEL, A_KV_HEADS, 1, HEAD_DIM)
        return jnp.broadcast_to(t, (D_MODEL, A_KV_HEADS, 2, HEAD_DIM)).reshape(D_MODEL, 4 * HEAD_DIM)

    return jnp.concatenate([qa, dup(ka), dup(va), rest], axis=-1).astype(BF16)


def _dup_kv_heads(t):
    b, p = t.shape[:2]
    t = jnp.broadcast_to(t[:, :, :, None, :], (b, p, A_KV_HEADS, 2, HEAD_DIM))
    return t.reshape(b, p, 4 * HEAD_DIM).astype(BF16)


def kernel(x_prompt, x_sample, cache_k_a, cache_v_a, cache_k_b, cache_v_b, cache_k_c, cache_v_c, c, c_ctx,
           w_ada, b_ada, norm_attn, norm_ffn, w_in, sinks, rpb, lambda_qk, subln, w_branch, w_out,
           w_router, b_router, w_gu, b_gu, w_down, b_down, norm_final):
    bsz, seq = x_prompt.shape[:2]
    bsz_d, n_lat = x_sample.shape[:2]
    past = cache_k_a.shape[2]
    n_p = bsz * seq
    n_s = bsz_d * n_lat
    assert seq % TM == 0 and n_lat % TQ == 0 and n_lat % C_CHUNK == 0 and bsz_d + 1 <= 8

    cond = jnp.zeros((8, D_MODEL), F32).at[0].set(c_ctx).at[1:1 + bsz_d].set(c)
    mod_all = _modulation(cond, w_ada, b_ada).reshape(DEPTH, 8, N_MOD, D_MODEL)
    cos, sin = _rope_tables(n_lat)

    xp = x_prompt.reshape(n_p, D_MODEL)
    xs = x_sample.reshape(n_s, D_MODEL)
    kv_out = []
    for l in range(DEPTH):
        lambda_init = 0.8 - 0.6 * math.exp(-0.3 * l)
        mod = mod_all[l]
        w_ext = _extended_w_in(w_in[l])
        g_attn = norm_attn[l][None, :]
        g_ffn = norm_ffn[l][None, :]
        sg = subln[l][None, :]
        wb = w_branch[l].astype(BF16)
        wo = w_out[l].astype(BF16)
        br = b_router[l][None, :]

        pa, pb, pc, pg, kva, kvb, kvc = _project(xp, mod, 0, n_p // TM, g_attn, w_ext, cos, sin,
                                                 rope=False, emit_kv=True)
        oa, ob, oc = _ctx_attention(pa, pb, pc, sinks[l], lambda_qk[l], sg, seq, lambda_init)
        xp_mid, hp, idx_p, gate_p = _merge_route(xp, mod, 0, n_p // TM, g_ffn, oa, ob, oc, pg, wb, wo,
                                                 w_router[l], br)
        kv_out.append((kva, kvb, kvc))

        pa, pb, pc, pg = _project(xs, mod, 1, n_lat // TM, g_attn, w_ext, cos, sin, rope=True, emit_kv=False)
        oa = _win_attention(pa, _dup_kv_heads(cache_k_a[:, l]), _dup_kv_heads(cache_v_a[:, l]), sinks[l],
                            bsz_d, n_lat)
        ob = _na_attention(pb, cache_k_b[:, l].reshape(bsz_d, past, -1).astype(BF16),
                           cache_v_b[:, l].reshape(bsz_d, past, -1).astype(BF16),
                           _na_bias_table(rpb.reshape(-1), l, n_lat // GRID_W), bsz_d, n_lat)
        oc = _diff_attention(pc, cache_k_c[:, l].reshape(bsz_d, past, -1).astype(BF16),
                             cache_v_c[:, l].reshape(bsz_d, past, -1).astype(BF16),
                             lambda_qk[l], sg, bsz_d, n_lat, lambda_init)
        xs_mid, hs, idx_s, gate_s = _merge_route(xs, mod, 1, n_lat // TM, g_ffn, oa, ob, oc, pg, wb, wo,
                                                 w_router[l], br)

        ffn = _moe_ffn(jnp.concatenate([hp, hs], axis=0),
                       jnp.concatenate([idx_p[:, :TOP_K], idx_s[:, :TOP_K]], axis=0),
                       jnp.concatenate([gate_p[:, :TOP_K], gate_s[:, :TOP_K]], axis=0),
                       l, w_gu, b_gu, w_down, b_down)
        xp = xp_mid + mod[0, 5][None, :] * ffn[:n_p]
        gate_s_rows = jnp.repeat(mod[1:1 + bsz_d, 5], n_lat, axis=0)
        xs = xs_mid + gate_s_rows * ffn[n_p:]

    y_prompt = _final_norm(xp, norm_final[None, :]).reshape(bsz, seq, D_MODEL)
    y_sample = _final_norm(xs, norm_final[None, :]).reshape(bsz_d, n_lat, D_MODEL)

    def stack(pick):
        return jnp.stack([pick(kv) for kv in kv_out], axis=1)

    def heads_a(t):
        return t.reshape(bsz, seq, A_KV_HEADS, 2, HEAD_DIM)[:, :, :, 0]

    new_k_a = stack(lambda kv: heads_a(kv[0][:, 0:256]))
    new_v_a = stack(lambda kv: heads_a(kv[0][:, 256:512]))
    new_k_b = stack(lambda kv: kv[1][:, 0:512].reshape(bsz, seq, B_HEADS, HEAD_DIM))
    new_v_b = stack(lambda kv: kv[1][:, 512:1024].reshape(bsz, seq, B_HEADS, HEAD_DIM))
    new_k_c = stack(lambda kv: kv[2][:, 0:512].reshape(bsz, seq, C_HEADS, 2 * HEAD_DIM))
    new_v_c = stack(lambda kv: kv[2][:, 512:1024].reshape(bsz, seq, C_HEADS, C_VDIM))
    return (y_prompt, y_sample, new_k_a, new_v_a, new_k_b, new_v_b, new_k_c, new_v_c)
```

```python
import functools
import math

import jax
import jax.numpy as jnp
import numpy as np
from jax import lax
from jax.experimental import pallas as pl
from jax.experimental.pallas import tpu as pltpu

F32 = jnp.float32
BF16 = jnp.bfloat16

D_MODEL = 1024
DEPTH = 2
GRID_W = 64
HEAD_DIM = 64
ROPE_HALF = HEAD_DIM // 2
ROPE_AXIS_PAIRS = HEAD_DIM // 4
ROPE_THETA = 10000.0
A_HEADS = 8
A_KV_HEADS = 2
WINDOW = 128
B_HEADS = 8
NA_KH = 8
NA_KW = 16
C_HEADS = 4
C_VDIM = 2 * HEAD_DIM
BRANCH_W = 512
N_EXPERTS = 32
TOP_K = 4
D_FF = D_MODEL
SWIGLU_LIMIT = 7.0
SWIGLU_ALPHA = 1.702
N_MOD = 6
EPS = 1e-6
NEG_INF = -1e30
ATTN_SCALE = HEAD_DIM ** -0.5

LANES = 128
VMEM_LIMIT = 56 * 1024 * 1024

A_COLS = 1024
B_COLS = 1536
C_COLS = 1536
G_COLS = 3 * D_MODEL
P_COLS = A_COLS + B_COLS + C_COLS + G_COLS

TM = 256
TQ = 256
NA_ROWS_Q = TQ // GRID_W
NA_ROWS_K = NA_ROWS_Q + NA_KH
NA_KEYS = NA_ROWS_K * GRID_W
WIN_KEYS = TQ + 2 * WINDOW
C_CHUNK = 512
MOE_TM = 256


def _cparams(sem):
    return pltpu.CompilerParams(dimension_semantics=sem, vmem_limit_bytes=VMEM_LIMIT)


def _head_masks():
    lane = lax.broadcasted_iota(jnp.int32, (1, LANES), 1)
    lo = jnp.where(lane < HEAD_DIM, 1.0, 0.0).astype(BF16)
    hi = jnp.where(lane >= HEAD_DIM, 1.0, 0.0).astype(BF16)
    return lo, hi


def _split_heads(q):
    lo, hi = _head_masks()
    return jnp.concatenate([q * lo, q * hi], axis=0)


def _merge_heads(pv, t):
    lane = lax.broadcasted_iota(jnp.int32, (t, LANES), 1)
    return jnp.where(lane < HEAD_DIM, pv[:t], pv[t:])


def _dot_nt(a, b):
    return lax.dot_general(a, b, (((1,), (1,)), ((), ())), preferred_element_type=F32)


def _dot(a, b):
    return jnp.dot(a, b, preferred_element_type=F32)


def _ada_kernel(c_ref, w_ref, b_ref, o_ref):
    c = c_ref[...]
    s = c * (1.0 / (1.0 + jnp.exp(-c)))
    o_ref[0] = jnp.dot(s, w_ref[0], preferred_element_type=F32,
                       precision=lax.Precision.HIGHEST) + b_ref[0]


def _modulation(cond, w_ada, b_ada):
    tn = 1536
    n = N_MOD * D_MODEL
    return pl.pallas_call(
        _ada_kernel,
        grid=(DEPTH, n // tn),
        in_specs=[
            pl.BlockSpec((8, D_MODEL), lambda l, j: (0, 0)),
            pl.BlockSpec((1, D_MODEL, tn), lambda l, j: (l, 0, j)),
            pl.BlockSpec((1, 1, tn), lambda l, j: (l, 0, j)),
        ],
        out_specs=pl.BlockSpec((1, 8, tn), lambda l, j: (l, 0, j)),
        out_shape=jax.ShapeDtypeStruct((DEPTH, 8, n), F32),
        compiler_params=_cparams(("arbitrary", "arbitrary")),
        name="modulation",
    )(cond, w_ada, b_ada.reshape(DEPTH, 1, n))


def _rms_modulate(x, g, shift, scale):
    ms = jnp.mean(x * x, axis=-1, keepdims=True)
    return (x * lax.rsqrt(ms + EPS) * g) * (1.0 + scale) + shift


def _rope_cols(v, cos, sin_signed):
    t, w = v.shape
    lane = lax.broadcasted_iota(jnp.int32, (t, LANES), 1)
    first_half = (lane & (HEAD_DIM - 1)) < ROPE_HALF
    cols = []
    for c in range(w // LANES):
        xc = v[:, c * LANES:(c + 1) * LANES]
        partner = jnp.where(first_half,
                            pltpu.roll(xc, LANES - ROPE_HALF, axis=1),
                            pltpu.roll(xc, ROPE_HALF, axis=1))
        cols.append(xc * cos + partner * sin_signed)
    return jnp.concatenate(cols, axis=1) if len(cols) > 1 else cols[0]


def _proj_kernel(x_ref, mod_ref, g_ref, w_ref, cos_ref, sin_ref, *out_refs, rope, emit_kv):
    oa_ref, ob_ref, oc_ref, og_ref = out_refs[:4]
    h = _rms_modulate(x_ref[...], g_ref[...], mod_ref[0, 0:1, :], mod_ref[0, 1:2, :]).astype(BF16)
    if rope:
        cos = cos_ref[...]
        sin = sin_ref[...]

    def seg(start, width):
        return _dot(h, w_ref[:, start:start + width])

    def maybe_rope(v):
        return _rope_cols(v, cos, sin) if rope else v

    oa_ref[:, 0:512] = (maybe_rope(seg(0, 512)) * ATTN_SCALE).astype(BF16)
    ka = maybe_rope(seg(512, 256))
    va = seg(768, 256)
    oa_ref[:, 512:768] = ka.astype(BF16)
    oa_ref[:, 768:1024] = va.astype(BF16)
    ob_ref[:, 0:512] = (seg(A_COLS, 512) * ATTN_SCALE).astype(BF16)
    kb = seg(A_COLS + 512, 512)
    vb = seg(A_COLS + 1024, 512)
    ob_ref[:, 512:1024] = kb.astype(BF16)
    ob_ref[:, 1024:1536] = vb.astype(BF16)
    c0 = A_COLS + B_COLS
    oc_ref[:, 0:512] = (maybe_rope(seg(c0, 512)) * ATTN_SCALE).astype(BF16)
    kc = maybe_rope(seg(c0 + 512, 512))
    vc = seg(c0 + 1024, 512)
    oc_ref[:, 512:1024] = kc.astype(BF16)
    oc_ref[:, 1024:1536] = vc.astype(BF16)
    g0 = c0 + C_COLS
    for j in range(G_COLS // 512):
        gv = seg(g0 + j * 512, 512)
        og_ref[:, j * 512:(j + 1) * 512] = (1.0 / (1.0 + jnp.exp(-gv))).astype(BF16)
    if emit_kv:
        kva_ref, kvb_ref, kvc_ref = out_refs[4:]
        kva_ref[:, 0:256] = ka
        kva_ref[:, 256:512] = va
        kvb_ref[:, 0:512] = kb
        kvb_ref[:, 512:1024] = vb
        kvc_ref[:, 0:512] = kc
        kvc_ref[:, 512:1024] = vc


def _project(x, mod, mod_row0, tiles_per_row, g, w_ext, cos, sin, *, rope, emit_kv):
    n = x.shape[0]
    nt = n // TM
    pos_tiles = cos.shape[0] // TM
    out_shape = [jax.ShapeDtypeStruct((n, A_COLS), BF16), jax.ShapeDtypeStruct((n, B_COLS), BF16),
                 jax.ShapeDtypeStruct((n, C_COLS), BF16), jax.ShapeDtypeStruct((n, G_COLS), BF16)]
    out_specs = [pl.BlockSpec((TM, A_COLS), lambda i: (i, 0)), pl.BlockSpec((TM, B_COLS), lambda i: (i, 0)),
                 pl.BlockSpec((TM, C_COLS), lambda i: (i, 0)), pl.BlockSpec((TM, G_COLS), lambda i: (i, 0))]
    if emit_kv:
        out_shape += [jax.ShapeDtypeStruct((n, 512), F32), jax.ShapeDtypeStruct((n, 1024), F32),
                      jax.ShapeDtypeStruct((n, 1024), F32)]
        out_specs += [pl.BlockSpec((TM, 512), lambda i: (i, 0)), pl.BlockSpec((TM, 1024), lambda i: (i, 0)),
                      pl.BlockSpec((TM, 1024), lambda i: (i, 0))]
    return pl.pallas_call(
        functools.partial(_proj_kernel, rope=rope, emit_kv=emit_kv),
        grid=(nt,),
        in_specs=[
            pl.BlockSpec((TM, D_MODEL), lambda i: (i, 0)),
            pl.BlockSpec((1, N_MOD, D_MODEL), lambda i: (mod_row0 + i // tiles_per_row, 0, 0)),
            pl.BlockSpec((1, D_MODEL), lambda i: (0, 0)),
            pl.BlockSpec((D_MODEL, P_COLS), lambda i: (0, 0)),
            pl.BlockSpec((TM, LANES), lambda i: (i % pos_tiles, 0)),
            pl.BlockSpec((TM, LANES), lambda i: (i % pos_tiles, 0)),
        ],
        out_specs=out_specs,
        out_shape=out_shape,
        compiler_params=_cparams(("arbitrary",)),
        name="project_rope" if rope else "project",
    )(x, mod, g, w_ext, cos, sin)


def _softmax_pv(scores, values, sink=None):
    m = scores[0].max(axis=-1, keepdims=True)
    for s in scores[1:]:
        m = jnp.maximum(m, s.max(axis=-1, keepdims=True))
    if sink is not None:
        m = jnp.maximum(m, sink)
    l = None
    acc = None
    for s, v in zip(scores, values):
        e = jnp.exp(s - m)
        ls = e.sum(axis=-1, keepdims=True)
        pv = _dot(e.astype(BF16), v)
        l = ls if l is None else l + ls
        acc = pv if acc is None else acc + pv
    if sink is not None:
        l = l + jnp.exp(sink - m)
    return acc * (1.0 / l)


def _sink_column(sink_ref, first_head, rows_per_head, n_heads):
    row = lax.broadcasted_iota(jnp.int32, (rows_per_head * n_heads, 1), 0)
    col = jnp.full((rows_per_head * n_heads, 1), sink_ref[first_head], F32)
    for j in range(1, n_heads):
        col = jnp.where(row >= j * rows_per_head, sink_ref[first_head + j], col)
    return col


def _diff_lambda(lq_ref, lambda_init):
    lf = lq_ref[...]
    a = jnp.sum(lf[0:1] * lf[1:2], axis=-1, keepdims=True)
    b = jnp.sum(lf[2:3] * lf[3:4], axis=-1, keepdims=True)
    return jnp.exp(a) - jnp.exp(b) + lambda_init


def _subln(o, g, lambda_init):
    ms = jnp.mean(o * o, axis=-1, keepdims=True)
    return (o * lax.rsqrt(ms + EPS) * g) * (1.0 - lambda_init)


def _ctx_attn_kernel(sink_ref, lq_ref, sg_ref, a_ref, b_ref, c_ref, oa_ref, ob_ref, oc_ref, *, lambda_init):
    t = a_ref.shape[0]
    for g in range(A_KV_HEADS):
        q = a_ref[:, g * 256:(g + 1) * 256]
        lhs = jnp.concatenate([_split_heads(q[:, :LANES]), _split_heads(q[:, LANES:])], axis=0)
        k = a_ref[:, 512 + g * LANES:512 + (g + 1) * LANES]
        v = a_ref[:, 768 + g * LANES:768 + (g + 1) * LANES]
        sink = _sink_column(sink_ref, 4 * g, t, 4)
        o = _softmax_pv([_dot_nt(lhs, k)], [v], sink)
        oa_ref[:, g * 256:g * 256 + LANES] = _merge_heads(o[:2 * t], t).astype(BF16)
        oa_ref[:, g * 256 + LANES:(g + 1) * 256] = _merge_heads(o[2 * t:], t).astype(BF16)
    for c in range(B_HEADS // 2):
        lhs = _split_heads(b_ref[:, c * LANES:(c + 1) * LANES])
        k = b_ref[:, 512 + c * LANES:512 + (c + 1) * LANES]
        v = b_ref[:, 1024 + c * LANES:1024 + (c + 1) * LANES]
        o = _softmax_pv([_dot_nt(lhs, k)], [v])
        ob_ref[:, c * LANES:(c + 1) * LANES] = _merge_heads(o, t).astype(BF16)
    lam = _diff_lambda(lq_ref, lambda_init)
    for h in range(C_HEADS):
        lhs = _split_heads(c_ref[:, h * LANES:(h + 1) * LANES])
        k = c_ref[:, 512 + h * LANES:512 + (h + 1) * LANES]
        v = c_ref[:, 1024 + h * LANES:1024 + (h + 1) * LANES]
        s = _dot_nt(lhs, k)
        m = s.max(axis=-1, keepdims=True)
        e = jnp.exp(s - m)
        p = e * (1.0 / e.sum(axis=-1, keepdims=True))
        a = (p[:t] - lam * p[t:]).astype(BF16)
        oc_ref[:, h * LANES:(h + 1) * LANES] = _subln(_dot(a, v), sg_ref[...], lambda_init).astype(BF16)


def _ctx_attention(pa, pb, pc, sink, lq, sg, seq, lambda_init):
    n = pa.shape[0]
    smem = pl.BlockSpec(memory_space=pltpu.SMEM)
    return pl.pallas_call(
        functools.partial(_ctx_attn_kernel, lambda_init=lambda_init),
        grid=(n // seq,),
        in_specs=[
            smem,
            pl.BlockSpec((4, HEAD_DIM), lambda b: (0, 0)),
            pl.BlockSpec((1, C_VDIM), lambda b: (0, 0)),
            pl.BlockSpec((seq, A_COLS), lambda b: (b, 0)),
            pl.BlockSpec((seq, B_COLS), lambda b: (b, 0)),
            pl.BlockSpec((seq, C_COLS), lambda b: (b, 0)),
        ],
        out_specs=[pl.BlockSpec((seq, BRANCH_W), lambda b: (b, 0))] * 3,
        out_shape=[jax.ShapeDtypeStruct((n, BRANCH_W), BF16)] * 3,
        compiler_params=_cparams(("arbitrary",)),
        name="context_attention",
    )(sink, lq, sg, pa, pb, pc)


def _win_attn_kernel(sink_ref, q_ref, k_ref, v_ref, kc_ref, vc_ref, o_ref, *, n_lat):
    g = pl.program_id(1)
    qi = pl.program_id(2)
    q = q_ref[...]
    lhs = jnp.concatenate([_split_heads(q[:, :LANES]), _split_heads(q[:, LANES:])], axis=0)
    ws = pl.multiple_of(jnp.clip(qi * TQ - WINDOW, 0, n_lat - WIN_KEYS), WINDOW)
    kwin = k_ref[pl.ds(ws, WIN_KEYS), :]
    vwin = v_ref[pl.ds(ws, WIN_KEYS), :]
    s_loc = _dot_nt(lhs, kwin)
    qpos = qi * TQ + (lax.broadcasted_iota(jnp.int32, s_loc.shape, 0) & (TQ - 1))
    kpos = ws + lax.broadcasted_iota(jnp.int32, s_loc.shape, 1)
    s_loc = jnp.where(jnp.abs(kpos - qpos) <= WINDOW, s_loc, NEG_INF)
    s_ctx = _dot_nt(lhs, kc_ref[0])
    sink = _sink_column(sink_ref, 4 * g, TQ, 4)
    o = _softmax_pv([s_loc, s_ctx], [vwin, vc_ref[0]], sink)
    o_ref[:, :LANES] = _merge_heads(o[:2 * TQ], TQ).astype(BF16)
    o_ref[:, LANES:] = _merge_heads(o[2 * TQ:], TQ).astype(BF16)


def _win_attention(pa, kctx, vctx, sink, n_batch, n_lat):
    nq = n_lat // TQ
    past = kctx.shape[1]
    return pl.pallas_call(
        functools.partial(_win_attn_kernel, n_lat=n_lat),
        grid=(n_batch, A_KV_HEADS, nq),
        in_specs=[
            pl.BlockSpec(memory_space=pltpu.SMEM),
            pl.BlockSpec((TQ, 256), lambda b, g, i: (b * nq + i, g)),
            pl.BlockSpec((n_lat, LANES), lambda b, g, i: (b, 4 + g)),
            pl.BlockSpec((n_lat, LANES), lambda b, g, i: (b, 6 + g)),
            pl.BlockSpec((1, past, LANES), lambda b, g, i: (b, 0, g)),
            pl.BlockSpec((1, past, LANES), lambda b, g, i: (b, 0, g)),
        ],
        out_specs=pl.BlockSpec((TQ, 256), lambda b, g, i: (b * nq + i, g)),
        out_shape=jax.ShapeDtypeStruct((n_batch * n_lat, BRANCH_W), BF16),
        compiler_params=_cparams(("arbitrary", "arbitrary", "arbitrary")),
        name="window_attention",
    )(sink, pa, pa, pa, kctx, vctx)


def _na_window_start(qi, rows):
    return jnp.clip(qi * NA_ROWS_Q - NA_KH // 2, 0, rows - NA_ROWS_K)


def _na_attn_kernel(q_ref, k_ref, v_ref, kc_ref, vc_ref, bias_ref, o_ref, *, rows):
    qi = pl.program_id(2)
    ws = pl.multiple_of(_na_window_start(qi, rows) * GRID_W, GRID_W)
    lhs = _split_heads(q_ref[...])
    s_loc = _dot_nt(lhs, k_ref[pl.ds(ws, NA_KEYS), :]) + bias_ref[0].reshape(2 * TQ, NA_KEYS)
    s_ctx = _dot_nt(lhs, kc_ref[0])
    o = _softmax_pv([s_loc, s_ctx], [v_ref[pl.ds(ws, NA_KEYS), :], vc_ref[0]])
    o_ref[...] = _merge_heads(o, TQ).astype(BF16)


def _na_bias_kernel(rpb_ref, o_ref, *, layer, rows):
    h = pl.program_id(0)
    n_dr = 2 * NA_KH - 1
    n_dc = 2 * NA_KW - 1
    base = (layer * B_HEADS + h) * n_dr * n_dc
    lane = lax.broadcasted_iota(jnp.int32, (GRID_W, LANES), 1)
    qc = lax.broadcasted_iota(jnp.int32, (GRID_W, LANES), 0)
    kc = lane & (GRID_W - 1)
    second = lane >= GRID_W
    cs = jnp.clip(qc - NA_KW // 2, 0, GRID_W - NA_KW)
    col_ok = (kc >= cs) & (kc < cs + NA_KW)
    dc_map = kc - qc + (NA_KW - 1)
    cache = {}

    def pair_tile(dr0, ok0, ok1):
        key = (dr0 if ok0 else None, dr0 + 1 if ok1 else None)
        if key not in cache:
            if not (ok0 or ok1):
                cache[key] = jnp.full((GRID_W, LANES), NEG_INF, F32)
            else:
                acc = jnp.zeros((GRID_W, LANES), F32)
                for dc in range(n_dc):
                    v0 = rpb_ref[base + dr0 * n_dc + dc] if ok0 else 0.0
                    v1 = rpb_ref[base + (dr0 + 1) * n_dc + dc] if ok1 else 0.0
                    acc = jnp.where(dc_map == dc, jnp.where(second, v1, v0), acc)
                if ok0 and ok1:
                    ok = col_ok
                elif ok0:
                    ok = col_ok & jnp.logical_not(second)
                else:
                    ok = col_ok & second
                cache[key] = jnp.where(ok, acc, NEG_INF)
        return cache[key]

    nq = rows // NA_ROWS_Q
    for p, qi in enumerate((0, 1, nq - 1)):
        r0 = qi * NA_ROWS_Q
        ws = min(max(r0 - NA_KH // 2, 0), rows - NA_ROWS_K)
        for i in range(NA_ROWS_Q):
            r = r0 + i
            lo = min(max(r - NA_KH // 2, 0), rows - NA_KH)
            for jp in range(NA_ROWS_K // 2):
                kr = ws + 2 * jp
                ok0 = lo <= kr < lo + NA_KH
                ok1 = lo <= kr + 1 < lo + NA_KH
                o_ref[p, 0, i * GRID_W:(i + 1) * GRID_W, jp * LANES:(jp + 1) * LANES] = pair_tile(
                    kr - r + NA_KH - 1, ok0, ok1)


def _na_bias_table(rpb_flat, layer, rows):
    return pl.pallas_call(
        functools.partial(_na_bias_kernel, layer=layer, rows=rows),
        grid=(B_HEADS,),
        in_specs=[pl.BlockSpec(memory_space=pltpu.SMEM)],
        out_specs=pl.BlockSpec((3, 1, TQ, NA_KEYS), lambda h: (0, h, 0, 0)),
        out_shape=jax.ShapeDtypeStruct((3, B_HEADS, TQ, NA_KEYS), F32),
        compiler_params=_cparams(("arbitrary",)),
        name="neighborhood_bias",
    )(rpb_flat)


def _na_attention(pb, kctx, vctx, bias, n_batch, n_lat):
    nq = n_lat // TQ
    rows = n_lat // GRID_W
    past = kctx.shape[1]

    def bias_idx(b, c, i):
        return (jnp.where(i == 0, 0, jnp.where(i == nq - 1, 2, 1)), c, 0, 0)

    return pl.pallas_call(
        functools.partial(_na_attn_kernel, rows=rows),
        grid=(n_batch, B_HEADS // 2, nq),
        in_specs=[
            pl.BlockSpec((TQ, LANES), lambda b, c, i: (b * nq + i, c)),
            pl.BlockSpec((n_lat, LANES), lambda b, c, i: (b, 4 + c)),
            pl.BlockSpec((n_lat, LANES), lambda b, c, i: (b, 8 + c)),
            pl.BlockSpec((1, past, LANES), lambda b, c, i: (b, 0, c)),
            pl.BlockSpec((1, past, LANES), lambda b, c, i: (b, 0, c)),
            pl.BlockSpec((1, 2, TQ, NA_KEYS), bias_idx),
        ],
        out_specs=pl.BlockSpec((TQ, LANES), lambda b, c, i: (b * nq + i, c)),
        out_shape=jax.ShapeDtypeStruct((n_batch * n_lat, BRANCH_W), BF16),
        compiler_params=_cparams(("arbitrary", "arbitrary", "arbitrary")),
        name="neighborhood_attention",
    )(pb, pb, pb, kctx, vctx, bias)


def _diff_attn_kernel(lq_ref, sg_ref, q_ref, k_ref, v_ref, kc_ref, vc_ref, o_ref, *, n_lat, lambda_init):
    lhs = _split_heads(q_ref[...])

    def step(s, v, carry):
        m, l, acc = carry
        m_new = jnp.maximum(m, s.max(axis=-1, keepdims=True))
        alpha = jnp.exp(m - m_new)
        e = jnp.exp(s - m_new)
        l = alpha * l + e.sum(axis=-1, keepdims=True)
        acc = alpha * acc + _dot(e.astype(BF16), v)
        return m_new, l, acc

    def body(j, carry):
        off = pl.multiple_of(j * C_CHUNK, C_CHUNK)
        return step(_dot_nt(lhs, k_ref[pl.ds(off, C_CHUNK), :]), v_ref[pl.ds(off, C_CHUNK), :], carry)

    init = (jnp.full((2 * TQ, 1), NEG_INF, F32), jnp.zeros((2 * TQ, 1), F32), jnp.zeros((2 * TQ, LANES), F32))
    carry = lax.fori_loop(0, n_lat // C_CHUNK, body, init)
    m, l, acc = step(_dot_nt(lhs, kc_ref[0]), vc_ref[0], carry)
    o = acc * (1.0 / l)
    lam = _diff_lambda(lq_ref, lambda_init)
    o_ref[...] = _subln(o[:TQ] - lam * o[TQ:], sg_ref[...], lambda_init).astype(BF16)


def _diff_attention(pc, kctx, vctx, lq, sg, n_batch, n_lat, lambda_init):
    nq = n_lat // TQ
    past = kctx.shape[1]
    return pl.pallas_call(
        functools.partial(_diff_attn_kernel, n_lat=n_lat, lambda_init=lambda_init),
        grid=(n_batch, C_HEADS, nq),
        in_specs=[
            pl.BlockSpec((4, HEAD_DIM), lambda b, h, i: (0, 0)),
            pl.BlockSpec((1, C_VDIM), lambda b, h, i: (0, 0)),
            pl.BlockSpec((TQ, LANES), lambda b, h, i: (b * nq + i, h)),
            pl.BlockSpec((n_lat, LANES), lambda b, h, i: (b, 4 + h)),
            pl.BlockSpec((n_lat, LANES), lambda b, h, i: (b, 8 + h)),
            pl.BlockSpec((1, past, LANES), lambda b, h, i: (b, 0, h)),
            pl.BlockSpec((1, past, LANES), lambda b, h, i: (b, 0, h)),
        ],
        out_specs=pl.BlockSpec((TQ, LANES), lambda b, h, i: (b * nq + i, h)),
        out_shape=jax.ShapeDtypeStruct((n_batch * n_lat, BRANCH_W), BF16),
        compiler_params=_cparams(("arbitrary", "arbitrary", "arbitrary")),
        name="differential_attention",
    )(lq, sg, pc, pc, pc, kctx, vctx)


def _merge_kernel(x_ref, mod_ref, g_ref, oa_ref, ob_ref, oc_ref, sg_ref, wb_ref, wo_ref, wr_ref, br_ref,
                  xo_ref, h_ref, idx_ref, gate_ref):
    y = None
    for j, o_ref in enumerate((oa_ref, ob_ref, oc_ref)):
        t = sg_ref[:, j * D_MODEL:(j + 1) * D_MODEL].astype(F32) * _dot(o_ref[...], wb_ref[j])
        y = t if y is None else y + t
    x = x_ref[...] + mod_ref[0, 2:3, :] * _dot(y.astype(BF16), wo_ref[...])
    xo_ref[...] = x
    h = _rms_modulate(x, g_ref[...], mod_ref[0, 3:4, :], mod_ref[0, 4:5, :])
    h_ref[...] = h
    logits = jnp.dot(h, wr_ref[...], preferred_element_type=F32, precision=lax.Precision.HIGHEST) + br_ref[...]
    tm = logits.shape[0]
    lane_e = lax.broadcasted_iota(jnp.int32, (tm, N_EXPERTS), 1)
    lane_o = lax.broadcasted_iota(jnp.int32, (tm, LANES), 1)
    idx_out = jnp.zeros((tm, LANES), jnp.int32)
    val_out = jnp.zeros((tm, LANES), F32)
    top = None
    denom = None
    for k in range(TOP_K):
        mx = logits.max(axis=-1, keepdims=True)
        sel = jnp.min(jnp.where(logits == mx, lane_e, N_EXPERTS), axis=-1, keepdims=True)
        logits = jnp.where(lane_e == sel, -jnp.inf, logits)
        if top is None:
            top = mx
        e = jnp.exp(mx - top)
        denom = e if denom is None else denom + e
        idx_out = jnp.where(lane_o == k, sel, idx_out)
        val_out = jnp.where(lane_o == k, e, val_out)
    idx_ref[...] = idx_out
    gate_ref[...] = val_out * (1.0 / denom)


def _merge_route(x, mod, mod_row0, tiles_per_row, g_ffn, oa, ob, oc, sg, wb, wo, wr, br):
    n = x.shape[0]
    row = lambda i: (i, 0)
    fixed2 = lambda i: (0, 0)
    return pl.pallas_call(
        _merge_kernel,
        grid=(n // TM,),
        in_specs=[
            pl.BlockSpec((TM, D_MODEL), row),
            pl.BlockSpec((1, N_MOD, D_MODEL), lambda i: (mod_row0 + i // tiles_per_row, 0, 0)),
            pl.BlockSpec((1, D_MODEL), fixed2),
            pl.BlockSpec((TM, BRANCH_W), row),
            pl.BlockSpec((TM, BRANCH_W), row),
            pl.BlockSpec((TM, BRANCH_W), row),
            pl.BlockSpec((TM, G_COLS), row),
            pl.BlockSpec((3, BRANCH_W, D_MODEL), lambda i: (0, 0, 0)),
            pl.BlockSpec((D_MODEL, D_MODEL), fixed2),
            pl.BlockSpec((D_MODEL, N_EXPERTS), fixed2),
            pl.BlockSpec((1, N_EXPERTS), fixed2),
        ],
        out_specs=[pl.BlockSpec((TM, D_MODEL), row), pl.BlockSpec((TM, D_MODEL), row),
                   pl.BlockSpec((TM, LANES), row), pl.BlockSpec((TM, LANES), row)],
        out_shape=[jax.ShapeDtypeStruct((n, D_MODEL), F32), jax.ShapeDtypeStruct((n, D_MODEL), F32),
                   jax.ShapeDtypeStruct((n, LANES), jnp.int32), jax.ShapeDtypeStruct((n, LANES), F32)],
        compiler_params=_cparams(("arbitrary",)),
        name="merge_route",
    )(x, mod, g_ffn, oa, ob, oc, sg, wb, wo, wr, br)


def _moe_kernel(be_ref, first_ref, valid_ref, x_ref, wgu_ref, bgu_ref, wd_ref, bd_ref, o_ref, wgu_s, wd_s):
    i = pl.program_id(0)

    @pl.when(first_ref[i] == 1)
    def _():
        wgu_s[...] = wgu_ref[0, 0].astype(BF16)
        wd_s[...] = wd_ref[0, 0].astype(BF16)

    @pl.when(valid_ref[i] == 1)
    def _():
        gu = _dot(x_ref[...].astype(BF16), wgu_s[...]) + bgu_ref[0, 0]
        gate = jnp.minimum(gu[:, :D_FF], SWIGLU_LIMIT)
        lin = jnp.clip(gu[:, D_FF:], -SWIGLU_LIMIT, SWIGLU_LIMIT)
        act = gate * (1.0 / (1.0 + jnp.exp(-SWIGLU_ALPHA * gate))) * (lin + 1.0)
        o_ref[...] = _dot(act.astype(BF16), wd_s[...]) + bd_ref[0, 0]

    @pl.when(valid_ref[i] == 0)
    def _():
        o_ref[...] = jnp.zeros_like(o_ref)


def _moe_experts(xb, block_e, block_first, block_valid, layer, w_gu, b_gu, w_down, b_down):
    m = xb.shape[0]
    nb = m // MOE_TM
    grid_spec = pltpu.PrefetchScalarGridSpec(
        num_scalar_prefetch=3,
        grid=(nb,),
        in_specs=[
            pl.BlockSpec((MOE_TM, D_MODEL), lambda i, be, bf, bv: (i, 0)),
            pl.BlockSpec((1, 1, D_MODEL, 2 * D_FF), lambda i, be, bf, bv: (layer, be[i], 0, 0)),
            pl.BlockSpec((1, 1, 1, 2 * D_FF), lambda i, be, bf, bv: (layer, be[i], 0, 0)),
            pl.BlockSpec((1, 1, D_FF, D_MODEL), lambda i, be, bf, bv: (layer, be[i], 0, 0)),
            pl.BlockSpec((1, 1, 1, D_MODEL), lambda i, be, bf, bv: (layer, be[i], 0, 0)),
        ],
        out_specs=pl.BlockSpec((MOE_TM, D_MODEL), lambda i, be, bf, bv: (i, 0)),
        scratch_shapes=[pltpu.VMEM((D_MODEL, 2 * D_FF), BF16), pltpu.VMEM((D_FF, D_MODEL), BF16)],
    )
    return pl.pallas_call(
        _moe_kernel,
        grid_spec=grid_spec,
        out_shape=jax.ShapeDtypeStruct((m, D_MODEL), F32),
        compiler_params=_cparams(("arbitrary",)),
        name="moe_experts",
    )(block_e, block_first, block_valid, xb, w_gu, b_gu.reshape(DEPTH, N_EXPERTS, 1, 2 * D_FF),
      w_down, b_down.reshape(DEPTH, N_EXPERTS, 1, D_MODEL))


def _moe_block_tables(counts, nb):
    padded = (counts + MOE_TM - 1) // MOE_TM * MOE_TM
    pad_end = jnp.cumsum(padded).astype(jnp.int32)
    blk = jnp.arange(nb, dtype=jnp.int32) * MOE_TM
    block_e = jnp.minimum(jnp.sum((blk[:, None] >= pad_end[None, :]).astype(jnp.int32), axis=1), N_EXPERTS - 1)
    block_valid = (blk < pad_end[-1]).astype(jnp.int32)
    prev = jnp.concatenate([jnp.full((1,), -1, jnp.int32), block_e[:-1]])
    block_first = (block_e != prev).astype(jnp.int32)
    return pad_end, block_e, block_first, block_valid


def _route_kernel(idx_ref, dest_ref, cnt_ref, run_s, start_s):
    p = pl.program_id(0)
    i = pl.program_id(1)
    tm = idx_ref.shape[0]
    lane = lax.broadcasted_iota(jnp.int32, (tm, LANES), 1)
    idx = idx_ref[...]
    sel = [idx[:, k:k + 1] for k in range(TOP_K)]
    onehot = jnp.zeros((tm, LANES), F32)
    for k in range(TOP_K):
        onehot = onehot + jnp.where(lane == sel[k], 1.0, 0.0)
    tile_cnt = jnp.sum(onehot, axis=0, keepdims=True)

    @pl.when((p == 0) & (i == 0))
    def _():
        run_s[...] = jnp.zeros_like(run_s)
        cnt_ref[...] = jnp.zeros_like(cnt_ref)

    @pl.when(p == 0)
    def _():
        run_s[...] += tile_cnt

    @pl.when((p == 1) & (i == 0))
    def _():
        cnt = run_s[...]
        padded = ((cnt.astype(jnp.int32) + (MOE_TM - 1)) // MOE_TM * MOE_TM).astype(F32)
        r = lax.broadcasted_iota(jnp.int32, (LANES, LANES), 0)
        c = lax.broadcasted_iota(jnp.int32, (LANES, LANES), 1)
        before = jnp.where(r < c, 1.0, 0.0)
        start = jnp.dot(jnp.broadcast_to(padded, (8, LANES)), before, preferred_element_type=F32,
                        precision=lax.Precision.HIGHEST)
        start_s[...] = start[0:1]
        cnt_ref[...] = jnp.broadcast_to(cnt, cnt_ref.shape)
        run_s[...] = jnp.zeros_like(run_s)

    @pl.when(p == 1)
    def _():
        r = lax.broadcasted_iota(jnp.int32, (tm, tm), 0)
        c = lax.broadcasted_iota(jnp.int32, (tm, tm), 1)
        earlier = jnp.where(c < r, 1.0, 0.0).astype(BF16)
        pos = _dot(earlier, onehot.astype(BF16)) + run_s[...] + start_s[...]
        out = jnp.zeros((tm, LANES), F32)
        for k in range(TOP_K):
            d = jnp.sum(jnp.where(lane == sel[k], pos, 0.0), axis=-1, keepdims=True)
            out = jnp.where(lane == k, d, out)
        dest_ref[...] = out.astype(jnp.int32)
        run_s[...] += tile_cnt


def _route(idx_slab):
    n = idx_slab.shape[0]
    return pl.pallas_call(
        _route_kernel,
        grid=(2, n // TM),
        in_specs=[pl.BlockSpec((TM, LANES), lambda p, i: (i, 0))],
        out_specs=[pl.BlockSpec((TM, LANES), lambda p, i: (i * p, 0)),
                   pl.BlockSpec((8, LANES), lambda p, i: (0, 0))],
        out_shape=[jax.ShapeDtypeStruct((n, LANES), jnp.int32), jax.ShapeDtypeStruct((8, LANES), F32)],
        scratch_shapes=[pltpu.VMEM((1, LANES), F32), pltpu.VMEM((1, LANES), F32)],
        compiler_params=_cparams(("arbitrary", "arbitrary")),
        name="route",
    )(idx_slab)


def _row_copy(src_ref, src_row, dst_ref, dst_row, sem):
    return pltpu.make_async_copy(src_ref.at[pl.ds(src_row, 1)], dst_ref.at[pl.ds(dst_row, 1)], sem)


def _dispatch_kernel(pe_ref, cnt_ref, dest_ref, hp_ref, hs_ref, xb_ref, zero_s, sem, *, tiles_p, first_tail_block):
    i = pl.program_id(0)
    tm = hp_ref.shape[0]
    nb = xb_ref.shape[0] // MOE_TM

    def zero_block(row0):
        return pltpu.make_async_copy(zero_s, xb_ref.at[pl.ds(pl.multiple_of(row0, MOE_TM), MOE_TM)], sem)

    @pl.when(i == 0)
    def _():
        zero_s[...] = jnp.zeros_like(zero_s)
        for start in (True, False):
            for e in range(N_EXPERTS):
                @pl.when(cnt_ref[e] > 0)
                def _():
                    cp = zero_block(pe_ref[e] - MOE_TM)
                    cp.start() if start else cp.wait()
            for b in range(first_tail_block, nb):
                @pl.when(b * MOE_TM >= pe_ref[N_EXPERTS - 1])
                def _():
                    cp = zero_block(b * MOE_TM)
                    cp.start() if start else cp.wait()

    def scatter_rows(h_ref):
        def issue(t, carry):
            for k in range(TOP_K):
                _row_copy(h_ref, t, xb_ref, dest_ref[0, 0, t * TOP_K + k], sem).start()
            return carry

        def drain(t, carry):
            for k in range(TOP_K):
                _row_copy(h_ref, 0, xb_ref, 0, sem).wait()
            return carry

        lax.fori_loop(0, tm, issue, 0)
        lax.fori_loop(0, tm, drain, 0)

    @pl.when(i < tiles_p)
    def _():
        scatter_rows(hp_ref)

    @pl.when(i >= tiles_p)
    def _():
        scatter_rows(hs_ref)


def _dispatch(pad_end, counts, dest, hp, hs, m):
    tiles_p = hp.shape[0] // TM
    tiles_s = hs.shape[0] // TM
    grid_spec = pltpu.PrefetchScalarGridSpec(
        num_scalar_prefetch=2,
        grid=(tiles_p + tiles_s,),
        in_specs=[
            pl.BlockSpec((1, 1, TM * TOP_K), lambda i, pe, cn: (i, 0, 0), memory_space=pltpu.SMEM),
            pl.BlockSpec((TM, D_MODEL), lambda i, pe, cn: (jnp.minimum(i, tiles_p - 1), 0)),
            pl.BlockSpec((TM, D_MODEL), lambda i, pe, cn: (jnp.maximum(i - tiles_p, 0), 0)),
        ],
        out_specs=pl.BlockSpec(memory_space=pl.ANY),
        scratch_shapes=[pltpu.VMEM((MOE_TM, D_MODEL), F32), pltpu.SemaphoreType.DMA(())],
    )
    return pl.pallas_call(
        functools.partial(_dispatch_kernel, tiles_p=tiles_p, first_tail_block=m // MOE_TM - N_EXPERTS),
        grid_spec=grid_spec,
        out_shape=jax.ShapeDtypeStruct((m, D_MODEL), F32),
        compiler_params=_cparams(("arbitrary",)),
        name="dispatch",
    )(pad_end, counts, dest, hp, hs)


def _combine_kernel(dest_ref, x_ref, mod_ref, gate_ref, gfin_ref, yb_ref, o_ref, buf, sem, *, final):
    tm = x_ref.shape[0]

    def issue(t, carry):
        for k in range(TOP_K):
            _row_copy(yb_ref, dest_ref[0, 0, t * TOP_K + k], buf.at[k], t, sem).start()
        return carry

    def drain(t, carry):
        for k in range(TOP_K):
            _row_copy(yb_ref, 0, buf.at[k], 0, sem).wait()
        return carry

    lax.fori_loop(0, tm, issue, 0)
    lax.fori_loop(0, tm, drain, 0)
    g = gate_ref[...]
    acc = g[:, 0:1] * buf[0]
    for k in range(1, TOP_K):
        acc = acc + g[:, k:k + 1] * buf[k]
    x = x_ref[...] + mod_ref[0, 5:6, :] * acc
    if final:
        ms = jnp.mean(x * x, axis=-1, keepdims=True)
        x = x * lax.rsqrt(ms + EPS) * gfin_ref[...]
    o_ref[...] = x


def _combine(dest, x, mod, mod_row0, tiles_per_row, gates, g_final, yb, *, final):
    n = x.shape[0]
    return pl.pallas_call(
        functools.partial(_combine_kernel, final=final),
        grid=(n // TM,),
        in_specs=[
            pl.BlockSpec((1, 1, TM * TOP_K), lambda i: (i, 0, 0), memory_space=pltpu.SMEM),
            pl.BlockSpec((TM, D_MODEL), lambda i: (i, 0)),
            pl.BlockSpec((1, N_MOD, D_MODEL), lambda i: (mod_row0 + i // tiles_per_row, 0, 0)),
            pl.BlockSpec((TM, LANES), lambda i: (i, 0)),
            pl.BlockSpec((1, D_MODEL), lambda i: (0, 0)),
            pl.BlockSpec(memory_space=pl.ANY),
        ],
        out_specs=pl.BlockSpec((TM, D_MODEL), lambda i: (i, 0)),
        out_shape=jax.ShapeDtypeStruct((n, D_MODEL), F32),
        scratch_shapes=[pltpu.VMEM((TOP_K, TM, D_MODEL), F32), pltpu.SemaphoreType.DMA(())],
        compiler_params=_cparams(("arbitrary",)),
        name="combine_final" if final else "combine",
    )(dest, x, mod, gates, g_final, yb)


def _rope_tables(n_tokens):
    t = np.arange(n_tokens)
    row = (t // GRID_W).astype(np.float32)
    col = (t % GRID_W).astype(np.float32)
    inv = jnp.asarray(ROPE_THETA, F32) ** (-jnp.arange(ROPE_AXIS_PAIRS, dtype=F32) / ROPE_AXIS_PAIRS)
    ang = jnp.concatenate([jnp.asarray(row)[:, None] * inv, jnp.asarray(col)[:, None] * inv], axis=-1)
    cos, sin = jnp.cos(ang), jnp.sin(ang)
    cos = jnp.tile(cos, (1, LANES // ROPE_HALF))
    sin = jnp.tile(jnp.concatenate([-sin, sin], axis=-1), (1, LANES // HEAD_DIM))
    return cos, sin


def _extended_w_in(w):
    qa, ka, va, rest = w[:, :512], w[:, 512:640], w[:, 640:768], w[:, 768:]

    def dup(t):
        t = t.reshape(D_MODEL, A_KV_HEADS, 1, HEAD_DIM)
        return jnp.broadcast_to(t, (D_MODEL, A_KV_HEADS, 2, HEAD_DIM)).reshape(D_MODEL, 4 * HEAD_DIM)

    return jnp.concatenate([qa, dup(ka), dup(va), rest], axis=-1).astype(BF16)


def _dup_kv_heads(t):
    b, p = t.shape[:2]
    t = jnp.broadcast_to(t[:, :, :, None, :], (b, p, A_KV_HEADS, 2, HEAD_DIM))
    return t.reshape(b, p, 4 * HEAD_DIM).astype(BF16)


def kernel(x_prompt, x_sample, cache_k_a, cache_v_a, cache_k_b, cache_v_b, cache_k_c, cache_v_c, c, c_ctx,
           w_ada, b_ada, norm_attn, norm_ffn, w_in, sinks, rpb, lambda_qk, subln, w_branch, w_out,
           w_router, b_router, w_gu, b_gu, w_down, b_down, norm_final):
    bsz, seq = x_prompt.shape[:2]
    bsz_d, n_lat = x_sample.shape[:2]
    past = cache_k_a.shape[2]
    n_p = bsz * seq
    n_s = bsz_d * n_lat
    assert seq % TM == 0 and n_lat % TQ == 0 and n_lat % C_CHUNK == 0 and bsz_d + 1 <= 8

    cond = jnp.zeros((8, D_MODEL), F32).at[0].set(c_ctx).at[1:1 + bsz_d].set(c)
    mod_all = _modulation(cond, w_ada, b_ada).reshape(DEPTH, 8, N_MOD, D_MODEL)
    cos, sin = _rope_tables(n_lat)
    g_final = norm_final[None, :]
    n_assign = (n_p + n_s) * TOP_K
    m_rows = (n_assign + N_EXPERTS * (MOE_TM - 1) + MOE_TM - 1) // MOE_TM * MOE_TM

    xp = x_prompt.reshape(n_p, D_MODEL)
    xs = x_sample.reshape(n_s, D_MODEL)
    kv_out = []
    for l in range(DEPTH):
        lambda_init = 0.8 - 0.6 * math.exp(-0.3 * l)
        mod = mod_all[l]
        w_ext = _extended_w_in(w_in[l])
        g_attn = norm_attn[l][None, :]
        g_ffn = norm_ffn[l][None, :]
        sg = subln[l][None, :]
        wb = w_branch[l].astype(BF16)
        wo = w_out[l].astype(BF16)
        br = b_router[l][None, :]

        pa, pb, pc, pg, kva, kvb, kvc = _project(xp, mod, 0, n_p // TM, g_attn, w_ext, cos, sin,
                                                 rope=False, emit_kv=True)
        oa, ob, oc = _ctx_attention(pa, pb, pc, sinks[l], lambda_qk[l], sg, seq, lambda_init)
        xp_mid, hp, idx_p, gate_p = _merge_route(xp, mod, 0, n_p // TM, g_ffn, oa, ob, oc, pg, wb, wo,
                                                 w_router[l], br)
        kv_out.append((kva, kvb, kvc))

        pa, pb, pc, pg = _project(xs, mod, 1, n_lat // TM, g_attn, w_ext, cos, sin, rope=True, emit_kv=False)
        oa = _win_attention(pa, _dup_kv_heads(cache_k_a[:, l]), _dup_kv_heads(cache_v_a[:, l]), sinks[l],
                            bsz_d, n_lat)
        ob = _na_attention(pb, cache_k_b[:, l].reshape(bsz_d, past, -1).astype(BF16),
                           cache_v_b[:, l].reshape(bsz_d, past, -1).astype(BF16),
                           _na_bias_table(rpb.reshape(-1), l, n_lat // GRID_W), bsz_d, n_lat)
        oc = _diff_attention(pc, cache_k_c[:, l].reshape(bsz_d, past, -1).astype(BF16),
                             cache_v_c[:, l].reshape(bsz_d, past, -1).astype(BF16),
                             lambda_qk[l], sg, bsz_d, n_lat, lambda_init)
        xs_mid, hs, idx_s, gate_s = _merge_route(xs, mod, 1, n_lat // TM, g_ffn, oa, ob, oc, pg, wb, wo,
                                                 w_router[l], br)

        last = l == DEPTH - 1
        dest_slab, cnt = _route(jnp.concatenate([idx_p, idx_s], axis=0))
        counts = cnt[0, :N_EXPERTS].astype(jnp.int32)
        pad_end, block_e, block_first, block_valid = _moe_block_tables(counts, m_rows // MOE_TM)
        dest = dest_slab[:, :TOP_K].reshape(-1, 1, TM * TOP_K)
        dest_p, dest_s = dest[:n_p // TM], dest[n_p // TM:]
        xb = _dispatch(pad_end, counts, dest, hp, hs, m_rows)
        yb = _moe_experts(xb, block_e, block_first, block_valid, l, w_gu, b_gu, w_down, b_down)
        xp = _combine(dest_p, xp_mid, mod, 0, n_p // TM, gate_p, g_final, yb, final=last)
        xs = _combine(dest_s, xs_mid, mod, 1, n_lat // TM, gate_s, g_final, yb, final=last)

    y_prompt = xp.reshape(bsz, seq, D_MODEL)
    y_sample = xs.reshape(bsz_d, n_lat, D_MODEL)

    def stack(pick):
        return jnp.stack([pick(kv) for kv in kv_out], axis=1)

    def heads_a(t):
        return t.reshape(bsz, seq, A_KV_HEADS, 2, HEAD_DIM)[:, :, :, 0]

    new_k_a = stack(lambda kv: heads_a(kv[0][:, 0:256]))
    new_v_a = stack(lambda kv: heads_a(kv[0][:, 256:512]))
    new_k_b = stack(lambda kv: kv[1][:, 0:512].reshape(bsz, seq, B_HEADS, HEAD_DIM))
    new_v_b = stack(lambda kv: kv[1][:, 512:1024].reshape(bsz, seq, B_HEADS, HEAD_DIM))
    new_k_c = stack(lambda kv: kv[2][:, 0:512].reshape(bsz, seq, C_HEADS, 2 * HEAD_DIM))
    new_v_c = stack(lambda kv: kv[2][:, 512:1024].reshape(bsz, seq, C_HEADS, C_VDIM))
    return (y_prompt, y_sample, new_k_a, new_v_a, new_k_b, new_v_b, new_k_c, new_v_c)
```

```python
import functools
import math

import jax
import jax.numpy as jnp
import numpy as np
from jax import lax
from jax.experimental import pallas as pl
from jax.experimental.pallas import tpu as pltpu

F32 = jnp.float32
BF16 = jnp.bfloat16

D_MODEL = 1024
DEPTH = 2
GRID_W = 64
HEAD_DIM = 64
ROPE_HALF = HEAD_DIM // 2
ROPE_AXIS_PAIRS = HEAD_DIM // 4
ROPE_THETA = 10000.0
A_HEADS = 8
A_KV_HEADS = 2
WINDOW = 128
B_HEADS = 8
NA_KH = 8
NA_KW = 16
C_HEADS = 4
C_VDIM = 2 * HEAD_DIM
BRANCH_W = 512
N_EXPERTS = 32
TOP_K = 4
D_FF = D_MODEL
SWIGLU_LIMIT = 7.0
SWIGLU_ALPHA = 1.702
N_MOD = 6
EPS = 1e-6
NEG_INF = -1e30
ATTN_SCALE = HEAD_DIM ** -0.5
LOG2E = 1.4426950408889634
C_QSCALE = ATTN_SCALE * LOG2E

LANES = 128
VMEM_LIMIT = 56 * 1024 * 1024

A_COLS = 1024
B_COLS = 1536
C_COLS = 1536
G_COLS = 3 * D_MODEL
P_COLS = A_COLS + B_COLS + C_COLS + G_COLS

TM = 256
TQ = 256
NA_ROWS_Q = TQ // GRID_W
NA_ROWS_K = NA_ROWS_Q + NA_KH
NA_KEYS = NA_ROWS_K * GRID_W
WIN_KEYS = TQ + 2 * WINDOW
C_CHUNK = 512
MOE_TM = 512


def _cparams(sem):
    return pltpu.CompilerParams(dimension_semantics=sem, vmem_limit_bytes=VMEM_LIMIT)


def _head_masks():
    lane = lax.broadcasted_iota(jnp.int32, (1, LANES), 1)
    lo = jnp.where(lane < HEAD_DIM, 1.0, 0.0).astype(BF16)
    hi = jnp.where(lane >= HEAD_DIM, 1.0, 0.0).astype(BF16)
    return lo, hi


def _split_heads(q):
    lo, hi = _head_masks()
    return jnp.concatenate([q * lo, q * hi], axis=0)


def _merge_heads(pv, t):
    lane = lax.broadcasted_iota(jnp.int32, (t, LANES), 1)
    return jnp.where(lane < HEAD_DIM, pv[:t], pv[t:])


def _dot_nt(a, b):
    return lax.dot_general(a, b, (((1,), (1,)), ((), ())), preferred_element_type=F32)


def _dot(a, b):
    return jnp.dot(a, b, preferred_element_type=F32)


def _ada_kernel(c_ref, w_ref, b_ref, o_ref):
    c = c_ref[...]
    s = c * (1.0 / (1.0 + jnp.exp(-c)))
    o_ref[0] = jnp.dot(s, w_ref[0], preferred_element_type=F32,
                       precision=lax.Precision.HIGHEST) + b_ref[0]


def _modulation(cond, w_ada, b_ada):
    tn = 1536
    n = N_MOD * D_MODEL
    return pl.pallas_call(
        _ada_kernel,
        grid=(DEPTH, n // tn),
        in_specs=[
            pl.BlockSpec((8, D_MODEL), lambda l, j: (0, 0)),
            pl.BlockSpec((1, D_MODEL, tn), lambda l, j: (l, 0, j)),
            pl.BlockSpec((1, 1, tn), lambda l, j: (l, 0, j)),
        ],
        out_specs=pl.BlockSpec((1, 8, tn), lambda l, j: (l, 0, j)),
        out_shape=jax.ShapeDtypeStruct((DEPTH, 8, n), F32),
        compiler_params=_cparams(("arbitrary", "arbitrary")),
        name="modulation",
    )(cond, w_ada, b_ada.reshape(DEPTH, 1, n))


def _rms_modulate(x, g, shift, scale):
    ms = jnp.mean(x * x, axis=-1, keepdims=True)
    return (x * lax.rsqrt(ms + EPS) * g) * (1.0 + scale) + shift


def _rope_cols(v, cos, sin_signed):
    t, w = v.shape
    lane = lax.broadcasted_iota(jnp.int32, (t, LANES), 1)
    first_half = (lane & (HEAD_DIM - 1)) < ROPE_HALF
    cols = []
    for c in range(w // LANES):
        xc = v[:, c * LANES:(c + 1) * LANES]
        partner = jnp.where(first_half,
                            pltpu.roll(xc, LANES - ROPE_HALF, axis=1),
                            pltpu.roll(xc, ROPE_HALF, axis=1))
        cols.append(xc * cos + partner * sin_signed)
    return jnp.concatenate(cols, axis=1) if len(cols) > 1 else cols[0]


def _proj_kernel(x_ref, mod_ref, g_ref, w_ref, cos_ref, sin_ref, *out_refs, rope, emit_kv):
    oa_ref, ob_ref, oc_ref, og_ref = out_refs[:4]
    h = _rms_modulate(x_ref[...], g_ref[...], mod_ref[0, 0:1, :], mod_ref[0, 1:2, :]).astype(BF16)
    if rope:
        cos = cos_ref[...]
        sin = sin_ref[...]

    def seg(start, width):
        return _dot(h, w_ref[:, start:start + width])

    def maybe_rope(v):
        return _rope_cols(v, cos, sin) if rope else v

    oa_ref[:, 0:512] = (maybe_rope(seg(0, 512)) * ATTN_SCALE).astype(BF16)
    ka = maybe_rope(seg(512, 256))
    va = seg(768, 256)
    oa_ref[:, 512:768] = ka.astype(BF16)
    oa_ref[:, 768:1024] = va.astype(BF16)
    ob_ref[:, 0:512] = (seg(A_COLS, 512) * ATTN_SCALE).astype(BF16)
    kb = seg(A_COLS + 512, 512)
    vb = seg(A_COLS + 1024, 512)
    ob_ref[:, 512:1024] = kb.astype(BF16)
    ob_ref[:, 1024:1536] = vb.astype(BF16)
    c0 = A_COLS + B_COLS
    oc_ref[:, 0:512] = (maybe_rope(seg(c0, 512)) * C_QSCALE).astype(BF16)
    kc = maybe_rope(seg(c0 + 512, 512))
    vc = seg(c0 + 1024, 512)
    oc_ref[:, 512:1024] = kc.astype(BF16)
    oc_ref[:, 1024:1536] = vc.astype(BF16)
    g0 = c0 + C_COLS
    for j in range(G_COLS // 512):
        gv = seg(g0 + j * 512, 512)
        og_ref[:, j * 512:(j + 1) * 512] = (1.0 / (1.0 + jnp.exp(-gv))).astype(BF16)
    if emit_kv:
        kva_ref, kvb_ref, kvc_ref = out_refs[4:]
        kva_ref[:, 0:256] = ka
        kva_ref[:, 256:512] = va
        kvb_ref[:, 0:512] = kb
        kvb_ref[:, 512:1024] = vb
        kvc_ref[:, 0:512] = kc
        kvc_ref[:, 512:1024] = vc


def _project(x, mod, mod_row0, tiles_per_row, g, w_ext, cos, sin, *, rope, emit_kv):
    n = x.shape[0]
    nt = n // TM
    pos_tiles = cos.shape[0] // TM
    out_shape = [jax.ShapeDtypeStruct((n, A_COLS), BF16), jax.ShapeDtypeStruct((n, B_COLS), BF16),
                 jax.ShapeDtypeStruct((n, C_COLS), BF16), jax.ShapeDtypeStruct((n, G_COLS), BF16)]
    out_specs = [pl.BlockSpec((TM, A_COLS), lambda i: (i, 0)), pl.BlockSpec((TM, B_COLS), lambda i: (i, 0)),
                 pl.BlockSpec((TM, C_COLS), lambda i: (i, 0)), pl.BlockSpec((TM, G_COLS), lambda i: (i, 0))]
    if emit_kv:
        out_shape += [jax.ShapeDtypeStruct((n, 512), F32), jax.ShapeDtypeStruct((n, 1024), F32),
                      jax.ShapeDtypeStruct((n, 1024), F32)]
        out_specs += [pl.BlockSpec((TM, 512), lambda i: (i, 0)), pl.BlockSpec((TM, 1024), lambda i: (i, 0)),
                      pl.BlockSpec((TM, 1024), lambda i: (i, 0))]
    return pl.pallas_call(
        functools.partial(_proj_kernel, rope=rope, emit_kv=emit_kv),
        grid=(nt,),
        in_specs=[
            pl.BlockSpec((TM, D_MODEL), lambda i: (i, 0)),
            pl.BlockSpec((1, N_MOD, D_MODEL), lambda i: (mod_row0 + i // tiles_per_row, 0, 0)),
            pl.BlockSpec((1, D_MODEL), lambda i: (0, 0)),
            pl.BlockSpec((D_MODEL, P_COLS), lambda i: (0, 0)),
            pl.BlockSpec((TM, LANES), lambda i: (i % pos_tiles, 0)),
            pl.BlockSpec((TM, LANES), lambda i: (i % pos_tiles, 0)),
        ],
        out_specs=out_specs,
        out_shape=out_shape,
        compiler_params=_cparams(("arbitrary",)),
        name="project_rope" if rope else "project",
    )(x, mod, g, w_ext, cos, sin)


def _softmax_pv(scores, values, sink=None):
    m = scores[0].max(axis=-1, keepdims=True)
    for s in scores[1:]:
        m = jnp.maximum(m, s.max(axis=-1, keepdims=True))
    if sink is not None:
        m = jnp.maximum(m, sink)
    l = None
    acc = None
    for s, v in zip(scores, values):
        e = jnp.exp(s - m)
        ls = e.sum(axis=-1, keepdims=True)
        pv = _dot(e.astype(BF16), v)
        l = ls if l is None else l + ls
        acc = pv if acc is None else acc + pv
    if sink is not None:
        l = l + jnp.exp(sink - m)
    return acc * (1.0 / l)


def _sink_column(sink_ref, first_head, rows_per_head, n_heads):
    row = lax.broadcasted_iota(jnp.int32, (rows_per_head * n_heads, 1), 0)
    col = jnp.full((rows_per_head * n_heads, 1), sink_ref[first_head], F32)
    for j in range(1, n_heads):
        col = jnp.where(row >= j * rows_per_head, sink_ref[first_head + j], col)
    return col


def _diff_lambda(lq_ref, lambda_init):
    lf = lq_ref[...]
    a = jnp.sum(lf[0:1] * lf[1:2], axis=-1, keepdims=True)
    b = jnp.sum(lf[2:3] * lf[3:4], axis=-1, keepdims=True)
    return jnp.exp(a) - jnp.exp(b) + lambda_init


def _subln(o, g, lambda_init):
    ms = jnp.mean(o * o, axis=-1, keepdims=True)
    return (o * lax.rsqrt(ms + EPS) * g) * (1.0 - lambda_init)


def _ctx_attn_kernel(sink_ref, lq_ref, sg_ref, a_ref, b_ref, c_ref, oa_ref, ob_ref, oc_ref, *, lambda_init):
    t = a_ref.shape[0]
    for g in range(A_KV_HEADS):
        q = a_ref[:, g * 256:(g + 1) * 256]
        lhs = jnp.concatenate([_split_heads(q[:, :LANES]), _split_heads(q[:, LANES:])], axis=0)
        k = a_ref[:, 512 + g * LANES:512 + (g + 1) * LANES]
        v = a_ref[:, 768 + g * LANES:768 + (g + 1) * LANES]
        sink = _sink_column(sink_ref, 4 * g, t, 4)
        o = _softmax_pv([_dot_nt(lhs, k)], [v], sink)
        oa_ref[:, g * 256:g * 256 + LANES] = _merge_heads(o[:2 * t], t).astype(BF16)
        oa_ref[:, g * 256 + LANES:(g + 1) * 256] = _merge_heads(o[2 * t:], t).astype(BF16)
    for c in range(B_HEADS // 2):
        lhs = _split_heads(b_ref[:, c * LANES:(c + 1) * LANES])
        k = b_ref[:, 512 + c * LANES:512 + (c + 1) * LANES]
        v = b_ref[:, 1024 + c * LANES:1024 + (c + 1) * LANES]
        o = _softmax_pv([_dot_nt(lhs, k)], [v])
        ob_ref[:, c * LANES:(c + 1) * LANES] = _merge_heads(o, t).astype(BF16)
    lam = _diff_lambda(lq_ref, lambda_init)
    for h in range(C_HEADS):
        lhs = _split_heads(c_ref[:, h * LANES:(h + 1) * LANES])
        k = c_ref[:, 512 + h * LANES:512 + (h + 1) * LANES]
        v = c_ref[:, 1024 + h * LANES:1024 + (h + 1) * LANES]
        s = _dot_nt(lhs, k)
        m = s.max(axis=-1, keepdims=True)
        e = jnp.exp2(s - m)
        p = e * (1.0 / e.sum(axis=-1, keepdims=True))
        a = (p[:t] - lam * p[t:]).astype(BF16)
        oc_ref[:, h * LANES:(h + 1) * LANES] = _subln(_dot(a, v), sg_ref[...], lambda_init).astype(BF16)


def _ctx_attention(pa, pb, pc, sink, lq, sg, seq, lambda_init):
    n = pa.shape[0]
    smem = pl.BlockSpec(memory_space=pltpu.SMEM)
    return pl.pallas_call(
        functools.partial(_ctx_attn_kernel, lambda_init=lambda_init),
        grid=(n // seq,),
        in_specs=[
            smem,
            pl.BlockSpec((4, HEAD_DIM), lambda b: (0, 0)),
            pl.BlockSpec((1, C_VDIM), lambda b: (0, 0)),
            pl.BlockSpec((seq, A_COLS), lambda b: (b, 0)),
            pl.BlockSpec((seq, B_COLS), lambda b: (b, 0)),
            pl.BlockSpec((seq, C_COLS), lambda b: (b, 0)),
        ],
        out_specs=[pl.BlockSpec((seq, BRANCH_W), lambda b: (b, 0))] * 3,
        out_shape=[jax.ShapeDtypeStruct((n, BRANCH_W), BF16)] * 3,
        compiler_params=_cparams(("arbitrary",)),
        name="context_attention",
    )(sink, lq, sg, pa, pb, pc)


def _win_attn_kernel(sink_ref, q_ref, k_ref, v_ref, kc_ref, vc_ref, o_ref, *, n_lat):
    g = pl.program_id(1)
    qi = pl.program_id(2)
    q = q_ref[...]
    lhs = jnp.concatenate([_split_heads(q[:, :LANES]), _split_heads(q[:, LANES:])], axis=0)
    ws = pl.multiple_of(jnp.clip(qi * TQ - WINDOW, 0, n_lat - WIN_KEYS), WINDOW)
    kwin = k_ref[pl.ds(ws, WIN_KEYS), :]
    vwin = v_ref[pl.ds(ws, WIN_KEYS), :]
    s_loc = _dot_nt(lhs, kwin)
    qpos = qi * TQ + (lax.broadcasted_iota(jnp.int32, s_loc.shape, 0) & (TQ - 1))
    kpos = ws + lax.broadcasted_iota(jnp.int32, s_loc.shape, 1)
    s_loc = jnp.where(jnp.abs(kpos - qpos) <= WINDOW, s_loc, NEG_INF)
    s_ctx = _dot_nt(lhs, kc_ref[0])
    sink = _sink_column(sink_ref, 4 * g, TQ, 4)
    o = _softmax_pv([s_loc, s_ctx], [vwin, vc_ref[0]], sink)
    o_ref[:, :LANES] = _merge_heads(o[:2 * TQ], TQ).astype(BF16)
    o_ref[:, LANES:] = _merge_heads(o[2 * TQ:], TQ).astype(BF16)


def _win_attention(pa, kctx, vctx, sink, n_batch, n_lat):
    nq = n_lat // TQ
    past = kctx.shape[1]
    return pl.pallas_call(
        functools.partial(_win_attn_kernel, n_lat=n_lat),
        grid=(n_batch, A_KV_HEADS, nq),
        in_specs=[
            pl.BlockSpec(memory_space=pltpu.SMEM),
            pl.BlockSpec((TQ, 256), lambda b, g, i: (b * nq + i, g)),
            pl.BlockSpec((n_lat, LANES), lambda b, g, i: (b, 4 + g)),
            pl.BlockSpec((n_lat, LANES), lambda b, g, i: (b, 6 + g)),
            pl.BlockSpec((1, past, LANES), lambda b, g, i: (b, 0, g)),
            pl.BlockSpec((1, past, LANES), lambda b, g, i: (b, 0, g)),
        ],
        out_specs=pl.BlockSpec((TQ, 256), lambda b, g, i: (b * nq + i, g)),
        out_shape=jax.ShapeDtypeStruct((n_batch * n_lat, BRANCH_W), BF16),
        compiler_params=_cparams(("arbitrary", "arbitrary", "arbitrary")),
        name="window_attention",
    )(sink, pa, pa, pa, kctx, vctx)


def _na_window_start(qi, rows):
    return jnp.clip(qi * NA_ROWS_Q - NA_KH // 2, 0, rows - NA_ROWS_K)


def _na_attn_kernel(q_ref, k_ref, v_ref, kc_ref, vc_ref, bias_ref, o_ref, *, rows):
    qi = pl.program_id(2)
    ws = pl.multiple_of(_na_window_start(qi, rows) * GRID_W, GRID_W)
    lhs = _split_heads(q_ref[...])
    s_loc = _dot_nt(lhs, k_ref[pl.ds(ws, NA_KEYS), :]) + bias_ref[0].reshape(2 * TQ, NA_KEYS)
    s_ctx = _dot_nt(lhs, kc_ref[0])
    o = _softmax_pv([s_loc, s_ctx], [v_ref[pl.ds(ws, NA_KEYS), :], vc_ref[0]])
    o_ref[...] = _merge_heads(o, TQ).astype(BF16)


def _na_bias_kernel(rpb_ref, o_ref, *, layer, rows):
    h = pl.program_id(0)
    n_dr = 2 * NA_KH - 1
    n_dc = 2 * NA_KW - 1
    base = (layer * B_HEADS + h) * n_dr * n_dc
    lane = lax.broadcasted_iota(jnp.int32, (GRID_W, LANES), 1)
    qc = lax.broadcasted_iota(jnp.int32, (GRID_W, LANES), 0)
    kc = lane & (GRID_W - 1)
    second = lane >= GRID_W
    cs = jnp.clip(qc - NA_KW // 2, 0, GRID_W - NA_KW)
    col_ok = (kc >= cs) & (kc < cs + NA_KW)
    dc_map = kc - qc + (NA_KW - 1)
    cache = {}

    def pair_tile(dr0, ok0, ok1):
        key = (dr0 if ok0 else None, dr0 + 1 if ok1 else None)
        if key not in cache:
            if not (ok0 or ok1):
                cache[key] = jnp.full((GRID_W, LANES), NEG_INF, F32)
            else:
                acc = jnp.zeros((GRID_W, LANES), F32)
                for dc in range(n_dc):
                    v0 = rpb_ref[base + dr0 * n_dc + dc] if ok0 else 0.0
                    v1 = rpb_ref[base + (dr0 + 1) * n_dc + dc] if ok1 else 0.0
                    acc = jnp.where(dc_map == dc, jnp.where(second, v1, v0), acc)
                if ok0 and ok1:
                    ok = col_ok
                elif ok0:
                    ok = col_ok & jnp.logical_not(second)
                else:
                    ok = col_ok & second
                cache[key] = jnp.where(ok, acc, NEG_INF)
        return cache[key]

    nq = rows // NA_ROWS_Q
    for p, qi in enumerate((0, 1, nq - 1)):
        r0 = qi * NA_ROWS_Q
        ws = min(max(r0 - NA_KH // 2, 0), rows - NA_ROWS_K)
        for i in range(NA_ROWS_Q):
            r = r0 + i
            lo = min(max(r - NA_KH // 2, 0), rows - NA_KH)
            for jp in range(NA_ROWS_K // 2):
                kr = ws + 2 * jp
                ok0 = lo <= kr < lo + NA_KH
                ok1 = lo <= kr + 1 < lo + NA_KH
                o_ref[p, 0, i * GRID_W:(i + 1) * GRID_W, jp * LANES:(jp + 1) * LANES] = pair_tile(
                    kr - r + NA_KH - 1, ok0, ok1)


def _na_bias_table(rpb_flat, layer, rows):
    return pl.pallas_call(
        functools.partial(_na_bias_kernel, layer=layer, rows=rows),
        grid=(B_HEADS,),
        in_specs=[pl.BlockSpec(memory_space=pltpu.SMEM)],
        out_specs=pl.BlockSpec((3, 1, TQ, NA_KEYS), lambda h: (0, h, 0, 0)),
        out_shape=jax.ShapeDtypeStruct((3, B_HEADS, TQ, NA_KEYS), F32),
        compiler_params=_cparams(("arbitrary",)),
        name="neighborhood_bias",
    )(rpb_flat)


def _na_attention(pb, kctx, vctx, bias, n_batch, n_lat):
    nq = n_lat // TQ
    rows = n_lat // GRID_W
    past = kctx.shape[1]

    def bias_idx(b, c, i):
        return (jnp.where(i == 0, 0, jnp.where(i == nq - 1, 2, 1)), c, 0, 0)

    return pl.pallas_call(
        functools.partial(_na_attn_kernel, rows=rows),
        grid=(n_batch, B_HEADS // 2, nq),
        in_specs=[
            pl.BlockSpec((TQ, LANES), lambda b, c, i: (b * nq + i, c)),
            pl.BlockSpec((n_lat, LANES), lambda b, c, i: (b, 4 + c)),
            pl.BlockSpec((n_lat, LANES), lambda b, c, i: (b, 8 + c)),
            pl.BlockSpec((1, past, LANES), lambda b, c, i: (b, 0, c)),
            pl.BlockSpec((1, past, LANES), lambda b, c, i: (b, 0, c)),
            pl.BlockSpec((1, 2, TQ, NA_KEYS), bias_idx),
        ],
        out_specs=pl.BlockSpec((TQ, LANES), lambda b, c, i: (b * nq + i, c)),
        out_shape=jax.ShapeDtypeStruct((n_batch * n_lat, BRANCH_W), BF16),
        compiler_params=_cparams(("arbitrary", "arbitrary", "arbitrary")),
        name="neighborhood_attention",
    )(pb, pb, pb, kctx, vctx, bias)


def _diff_attn_kernel(lq_ref, sg_ref, q_ref, k_ref, v_ref, kc_ref, vc_ref, o_ref, vt_s, vtc_s, acc_s, *,
                      n_lat, lambda_init):
    n_chunks = n_lat // C_CHUNK

    @pl.when(pl.program_id(2) == 0)
    def _():
        for j in range(n_chunks):
            vt_s[j] = v_ref[j * C_CHUNK:(j + 1) * C_CHUNK, :].astype(F32).T.astype(BF16)
        vtc_s[...] = vc_ref[0].astype(F32).T.astype(BF16)

    lhs = _split_heads(q_ref[...])
    acc_s[...] = jnp.zeros_like(acc_s)

    def step(k, vt, m, l):
        st = _dot_nt(k, lhs)
        m_new = jnp.maximum(m, st.max(axis=0, keepdims=True))
        alpha = jnp.exp2(m - m_new)
        e = jnp.exp2(st - m_new)
        acc_s[...] = alpha * acc_s[...] + _dot(vt, e.astype(BF16))
        return m_new, alpha * l + e.sum(axis=0, keepdims=True)

    def body(j, carry):
        off = pl.multiple_of(j * C_CHUNK, C_CHUNK)
        return step(k_ref[pl.ds(off, C_CHUNK), :], vt_s[j], *carry)

    init = (jnp.full((1, 2 * TQ), NEG_INF, F32), jnp.zeros((1, 2 * TQ), F32))
    m, l = lax.fori_loop(0, n_chunks, body, init)
    m, l = step(kc_ref[0], vtc_s[...], m, l)
    o = acc_s[...] * (1.0 / l)
    d = o[:, :TQ] - _diff_lambda(lq_ref, lambda_init) * o[:, TQ:]
    ms = jnp.mean(d * d, axis=0, keepdims=True)
    y = (d * lax.rsqrt(ms + EPS)).T
    o_ref[...] = (y * sg_ref[...] * (1.0 - lambda_init)).astype(BF16)


def _diff_attention(pc, kctx, vctx, lq, sg, n_batch, n_lat, lambda_init):
    nq = n_lat // TQ
    past = kctx.shape[1]
    return pl.pallas_call(
        functools.partial(_diff_attn_kernel, n_lat=n_lat, lambda_init=lambda_init),
        grid=(n_batch, C_HEADS, nq),
        in_specs=[
            pl.BlockSpec((4, HEAD_DIM), lambda b, h, i: (0, 0)),
            pl.BlockSpec((1, C_VDIM), lambda b, h, i: (0, 0)),
            pl.BlockSpec((TQ, LANES), lambda b, h, i: (b * nq + i, h)),
            pl.BlockSpec((n_lat, LANES), lambda b, h, i: (b, 4 + h)),
            pl.BlockSpec((n_lat, LANES), lambda b, h, i: (b, 8 + h)),
            pl.BlockSpec((1, past, LANES), lambda b, h, i: (b, 0, h)),
            pl.BlockSpec((1, past, LANES), lambda b, h, i: (b, 0, h)),
        ],
        out_specs=pl.BlockSpec((TQ, LANES), lambda b, h, i: (b * nq + i, h)),
        out_shape=jax.ShapeDtypeStruct((n_batch * n_lat, BRANCH_W), BF16),
        scratch_shapes=[pltpu.VMEM((n_lat // C_CHUNK, C_VDIM, C_CHUNK), BF16), pltpu.VMEM((C_VDIM, past), BF16),
                        pltpu.VMEM((C_VDIM, 2 * TQ), F32)],
        compiler_params=_cparams(("arbitrary", "arbitrary", "arbitrary")),
        name="differential_attention",
    )(lq, sg, pc, pc, pc, kctx, vctx)


def _merge_kernel(x_ref, mod_ref, g_ref, oa_ref, ob_ref, oc_ref, sg_ref, wb_ref, wo_ref, wr_ref, br_ref,
                  xo_ref, h_ref, idx_ref, gate_ref):
    y = None
    for j, o_ref in enumerate((oa_ref, ob_ref, oc_ref)):
        t = sg_ref[:, j * D_MODEL:(j + 1) * D_MODEL].astype(F32) * _dot(o_ref[...], wb_ref[j])
        y = t if y is None else y + t
    x = x_ref[...] + mod_ref[0, 2:3, :] * _dot(y.astype(BF16), wo_ref[...])
    xo_ref[...] = x
    h = _rms_modulate(x, g_ref[...], mod_ref[0, 3:4, :], mod_ref[0, 4:5, :])
    h_ref[...] = h
    wr = wr_ref[...]
    h_hi = h.astype(BF16)
    h_lo = (h - h_hi.astype(F32)).astype(BF16)
    w_hi = wr.astype(BF16)
    w_lo = (wr - w_hi.astype(F32)).astype(BF16)
    logits = _dot(h_hi, w_hi) + (_dot(h_hi, w_lo) + _dot(h_lo, w_hi)) + br_ref[...]
    tm = logits.shape[0]
    lane_e = lax.broadcasted_iota(jnp.int32, (tm, N_EXPERTS), 1)
    lane_o = lax.broadcasted_iota(jnp.int32, (tm, LANES), 1)
    idx_out = jnp.zeros((tm, LANES), jnp.int32)
    val_out = jnp.zeros((tm, LANES), F32)
    top = None
    denom = None
    for k in range(TOP_K):
        mx = logits.max(axis=-1, keepdims=True)
        sel = jnp.min(jnp.where(logits == mx, lane_e, N_EXPERTS), axis=-1, keepdims=True)
        logits = jnp.where(lane_e == sel, -jnp.inf, logits)
        if top is None:
            top = mx
        e = jnp.exp(mx - top)
        denom = e if denom is None else denom + e
        idx_out = jnp.where(lane_o == k, sel, idx_out)
        val_out = jnp.where(lane_o == k, e, val_out)
    idx_ref[...] = idx_out
    gate_ref[...] = val_out * (1.0 / denom)


def _merge_route(x, mod, mod_row0, tiles_per_row, g_ffn, oa, ob, oc, sg, wb, wo, wr, br):
    n = x.shape[0]
    row = lambda i: (i, 0)
    fixed2 = lambda i: (0, 0)
    return pl.pallas_call(
        _merge_kernel,
        grid=(n // TM,),
        in_specs=[
            pl.BlockSpec((TM, D_MODEL), row),
            pl.BlockSpec((1, N_MOD, D_MODEL), lambda i: (mod_row0 + i // tiles_per_row, 0, 0)),
            pl.BlockSpec((1, D_MODEL), fixed2),
            pl.BlockSpec((TM, BRANCH_W), row),
            pl.BlockSpec((TM, BRANCH_W), row),
            pl.BlockSpec((TM, BRANCH_W), row),
            pl.BlockSpec((TM, G_COLS), row),
            pl.BlockSpec((3, BRANCH_W, D_MODEL), lambda i: (0, 0, 0)),
            pl.BlockSpec((D_MODEL, D_MODEL), fixed2),
            pl.BlockSpec((D_MODEL, N_EXPERTS), fixed2),
            pl.BlockSpec((1, N_EXPERTS), fixed2),
        ],
        out_specs=[pl.BlockSpec((TM, D_MODEL), row), pl.BlockSpec((TM, D_MODEL), row),
                   pl.BlockSpec((TM, LANES), row), pl.BlockSpec((TM, LANES), row)],
        out_shape=[jax.ShapeDtypeStruct((n, D_MODEL), F32), jax.ShapeDtypeStruct((n, D_MODEL), F32),
                   jax.ShapeDtypeStruct((n, LANES), jnp.int32), jax.ShapeDtypeStruct((n, LANES), F32)],
        compiler_params=_cparams(("arbitrary",)),
        name="merge_route",
    )(x, mod, g_ffn, oa, ob, oc, sg, wb, wo, wr, br)


def _moe_kernel(be_ref, first_ref, valid_ref, x_ref, wgu_ref, bgu_ref, wd_ref, bd_ref, o_ref, wgu_s, wd_s):
    i = pl.program_id(0)

    @pl.when(first_ref[i] == 1)
    def _():
        wgu_s[...] = wgu_ref[0, 0].astype(BF16)
        wd_s[...] = wd_ref[0, 0].astype(BF16)

    @pl.when(valid_ref[i] == 1)
    def _():
        gu = _dot(x_ref[...].astype(BF16), wgu_s[...]) + bgu_ref[0, 0]
        gate = jnp.minimum(gu[:, :D_FF], SWIGLU_LIMIT)
        lin = jnp.clip(gu[:, D_FF:], -SWIGLU_LIMIT, SWIGLU_LIMIT)
        act = gate * (1.0 / (1.0 + jnp.exp(-SWIGLU_ALPHA * gate))) * (lin + 1.0)
        o_ref[...] = _dot(act.astype(BF16), wd_s[...]) + bd_ref[0, 0]

    @pl.when(valid_ref[i] == 0)
    def _():
        o_ref[...] = jnp.zeros_like(o_ref)


def _moe_experts(xb, block_e, block_first, block_valid, layer, w_gu, b_gu, w_down, b_down):
    m = xb.shape[0]
    nb = m // MOE_TM
    grid_spec = pltpu.PrefetchScalarGridSpec(
        num_scalar_prefetch=3,
        grid=(nb,),
        in_specs=[
            pl.BlockSpec((MOE_TM, D_MODEL), lambda i, be, bf, bv: (i, 0)),
            pl.BlockSpec((1, 1, D_MODEL, 2 * D_FF), lambda i, be, bf, bv: (layer, be[i], 0, 0)),
            pl.BlockSpec((1, 1, 1, 2 * D_FF), lambda i, be, bf, bv: (layer, be[i], 0, 0)),
            pl.BlockSpec((1, 1, D_FF, D_MODEL), lambda i, be, bf, bv: (layer, be[i], 0, 0)),
            pl.BlockSpec((1, 1, 1, D_MODEL), lambda i, be, bf, bv: (layer, be[i], 0, 0)),
        ],
        out_specs=pl.BlockSpec((MOE_TM, D_MODEL), lambda i, be, bf, bv: (i, 0)),
        scratch_shapes=[pltpu.VMEM((D_MODEL, 2 * D_FF), BF16), pltpu.VMEM((D_FF, D_MODEL), BF16)],
    )
    return pl.pallas_call(
        _moe_kernel,
        grid_spec=grid_spec,
        out_shape=jax.ShapeDtypeStruct((m, D_MODEL), F32),
        compiler_params=_cparams(("arbitrary",)),
        name="moe_experts",
    )(block_e, block_first, block_valid, xb, w_gu, b_gu.reshape(DEPTH, N_EXPERTS, 1, 2 * D_FF),
      w_down, b_down.reshape(DEPTH, N_EXPERTS, 1, D_MODEL))


def _moe_block_tables(counts, nb):
    padded = (counts + MOE_TM - 1) // MOE_TM * MOE_TM
    pad_end = jnp.cumsum(padded).astype(jnp.int32)
    blk = jnp.arange(nb, dtype=jnp.int32) * MOE_TM
    block_e = jnp.minimum(jnp.sum((blk[:, None] >= pad_end[None, :]).astype(jnp.int32), axis=1), N_EXPERTS - 1)
    block_valid = (blk < pad_end[-1]).astype(jnp.int32)
    prev = jnp.concatenate([jnp.full((1,), -1, jnp.int32), block_e[:-1]])
    block_first = (block_e != prev).astype(jnp.int32)
    return pad_end, block_e, block_first, block_valid


def _route_kernel(idx_ref, dest_ref, cnt_ref, run_s, start_s):
    p = pl.program_id(0)
    i = pl.program_id(1)
    tm = idx_ref.shape[0]
    lane = lax.broadcasted_iota(jnp.int32, (tm, LANES), 1)
    idx = idx_ref[...]
    sel = [idx[:, k:k + 1] for k in range(TOP_K)]
    onehot = jnp.zeros((tm, LANES), F32)
    for k in range(TOP_K):
        onehot = onehot + jnp.where(lane == sel[k], 1.0, 0.0)
    tile_cnt = jnp.sum(onehot, axis=0, keepdims=True)

    @pl.when((p == 0) & (i == 0))
    def _():
        run_s[...] = jnp.zeros_like(run_s)
        cnt_ref[...] = jnp.zeros_like(cnt_ref)

    @pl.when(p == 0)
    def _():
        run_s[...] += tile_cnt

    @pl.when((p == 1) & (i == 0))
    def _():
        cnt = run_s[...]
        padded = ((cnt.astype(jnp.int32) + (MOE_TM - 1)) // MOE_TM * MOE_TM).astype(F32)
        r = lax.broadcasted_iota(jnp.int32, (LANES, LANES), 0)
        c = lax.broadcasted_iota(jnp.int32, (LANES, LANES), 1)
        before = jnp.where(r < c, 1.0, 0.0)
        start = jnp.dot(jnp.broadcast_to(padded, (8, LANES)), before, preferred_element_type=F32,
                        precision=lax.Precision.HIGHEST)
        start_s[...] = start[0:1]
        cnt_ref[...] = jnp.broadcast_to(cnt, cnt_ref.shape)
        run_s[...] = jnp.zeros_like(run_s)

    @pl.when(p == 1)
    def _():
        r = lax.broadcasted_iota(jnp.int32, (tm, tm), 0)
        c = lax.broadcasted_iota(jnp.int32, (tm, tm), 1)
        earlier = jnp.where(c < r, 1.0, 0.0).astype(BF16)
        pos = _dot(earlier, onehot.astype(BF16)) + run_s[...] + start_s[...]
        out = jnp.zeros((tm, LANES), F32)
        for k in range(TOP_K):
            d = jnp.sum(jnp.where(lane == sel[k], pos, 0.0), axis=-1, keepdims=True)
            out = jnp.where(lane == k, d, out)
        dest_ref[...] = out.astype(jnp.int32)
        run_s[...] += tile_cnt


def _route(idx_slab):
    n = idx_slab.shape[0]
    return pl.pallas_call(
        _route_kernel,
        grid=(2, n // TM),
        in_specs=[pl.BlockSpec((TM, LANES), lambda p, i: (i, 0))],
        out_specs=[pl.BlockSpec((TM, LANES), lambda p, i: (i * p, 0)),
                   pl.BlockSpec((8, LANES), lambda p, i: (0, 0))],
        out_shape=[jax.ShapeDtypeStruct((n, LANES), jnp.int32), jax.ShapeDtypeStruct((8, LANES), F32)],
        scratch_shapes=[pltpu.VMEM((1, LANES), F32), pltpu.VMEM((1, LANES), F32)],
        compiler_params=_cparams(("arbitrary", "arbitrary")),
        name="route",
    )(idx_slab)


def _row_copy(src_ref, src_row, dst_ref, dst_row, sem):
    return pltpu.make_async_copy(src_ref.at[pl.ds(src_row, 1)], dst_ref.at[pl.ds(dst_row, 1)], sem)


def _dispatch_kernel(pe_ref, cnt_ref, dest_ref, hp_ref, hs_ref, xb_ref, zero_s, sem, *, tiles_p, first_tail_block):
    i = pl.program_id(0)
    tm = hp_ref.shape[0]
    nb = xb_ref.shape[0] // MOE_TM

    def zero_block(row0):
        return pltpu.make_async_copy(zero_s, xb_ref.at[pl.ds(pl.multiple_of(row0, MOE_TM), MOE_TM)], sem)

    @pl.when(i == 0)
    def _():
        zero_s[...] = jnp.zeros_like(zero_s)
        for start in (True, False):
            for e in range(N_EXPERTS):
                @pl.when(cnt_ref[e] > 0)
                def _():
                    cp = zero_block(pe_ref[e] - MOE_TM)
                    cp.start() if start else cp.wait()
            for b in range(first_tail_block, nb):
                @pl.when(b * MOE_TM >= pe_ref[N_EXPERTS - 1])
                def _():
                    cp = zero_block(b * MOE_TM)
                    cp.start() if start else cp.wait()

    def scatter_rows(h_ref):
        def issue(t, carry):
            for k in range(TOP_K):
                _row_copy(h_ref, t, xb_ref, dest_ref[0, 0, t * TOP_K + k], sem).start()
            return carry

        def drain(t, carry):
            for k in range(TOP_K):
                _row_copy(h_ref, 0, xb_ref, 0, sem).wait()
            return carry

        lax.fori_loop(0, tm, issue, 0)
        lax.fori_loop(0, tm, drain, 0)

    @pl.when(i < tiles_p)
    def _():
        scatter_rows(hp_ref)

    @pl.when(i >= tiles_p)
    def _():
        scatter_rows(hs_ref)


def _dispatch(pad_end, counts, dest, hp, hs, m):
    tiles_p = hp.shape[0] // TM
    tiles_s = hs.shape[0] // TM
    grid_spec = pltpu.PrefetchScalarGridSpec(
        num_scalar_prefetch=2,
        grid=(tiles_p + tiles_s,),
        in_specs=[
            pl.BlockSpec((1, 1, TM * TOP_K), lambda i, pe, cn: (i, 0, 0), memory_space=pltpu.SMEM),
            pl.BlockSpec((TM, D_MODEL), lambda i, pe, cn: (jnp.minimum(i, tiles_p - 1), 0)),
            pl.BlockSpec((TM, D_MODEL), lambda i, pe, cn: (jnp.maximum(i - tiles_p, 0), 0)),
        ],
        out_specs=pl.BlockSpec(memory_space=pl.ANY),
        scratch_shapes=[pltpu.VMEM((MOE_TM, D_MODEL), F32), pltpu.SemaphoreType.DMA(())],
    )
    return pl.pallas_call(
        functools.partial(_dispatch_kernel, tiles_p=tiles_p, first_tail_block=m // MOE_TM - N_EXPERTS),
        grid_spec=grid_spec,
        out_shape=jax.ShapeDtypeStruct((m, D_MODEL), F32),
        compiler_params=_cparams(("arbitrary",)),
        name="dispatch",
    )(pad_end, counts, dest, hp, hs)


def _combine_kernel(dest_ref, x_ref, mod_ref, gate_ref, gfin_ref, yb_ref, o_ref, buf, sem, *, final):
    tm = x_ref.shape[0]

    def issue(t, carry):
        for k in range(TOP_K):
            _row_copy(yb_ref, dest_ref[0, 0, t * TOP_K + k], buf.at[k], t, sem).start()
        return carry

    def drain(t, carry):
        for k in range(TOP_K):
            _row_copy(yb_ref, 0, buf.at[k], 0, sem).wait()
        return carry

    lax.fori_loop(0, tm, issue, 0)
    lax.fori_loop(0, tm, drain, 0)
    g = gate_ref[...]
    acc = g[:, 0:1] * buf[0]
    for k in range(1, TOP_K):
        acc = acc + g[:, k:k + 1] * buf[k]
    x = x_ref[...] + mod_ref[0, 5:6, :] * acc
    if final:
        ms = jnp.mean(x * x, axis=-1, keepdims=True)
        x = x * lax.rsqrt(ms + EPS) * gfin_ref[...]
    o_ref[...] = x


def _combine(dest, x, mod, mod_row0, tiles_per_row, gates, g_final, yb, *, final):
    n = x.shape[0]
    return pl.pallas_call(
        functools.partial(_combine_kernel, final=final),
        grid=(n // TM,),
        in_specs=[
            pl.BlockSpec((1, 1, TM * TOP_K), lambda i: (i, 0, 0), memory_space=pltpu.SMEM),
            pl.BlockSpec((TM, D_MODEL), lambda i: (i, 0)),
            pl.BlockSpec((1, N_MOD, D_MODEL), lambda i: (mod_row0 + i // tiles_per_row, 0, 0)),
            pl.BlockSpec((TM, LANES), lambda i: (i, 0)),
            pl.BlockSpec((1, D_MODEL), lambda i: (0, 0)),
            pl.BlockSpec(memory_space=pl.ANY),
        ],
        out_specs=pl.BlockSpec((TM, D_MODEL), lambda i: (i, 0)),
        out_shape=jax.ShapeDtypeStruct((n, D_MODEL), F32),
        scratch_shapes=[pltpu.VMEM((TOP_K, TM, D_MODEL), F32), pltpu.SemaphoreType.DMA(())],
        compiler_params=_cparams(("arbitrary",)),
        name="combine_final" if final else "combine",
    )(dest, x, mod, gates, g_final, yb)


def _rope_tables(n_tokens):
    t = np.arange(n_tokens)
    row = (t // GRID_W).astype(np.float32)
    col = (t % GRID_W).astype(np.float32)
    inv = jnp.asarray(ROPE_THETA, F32) ** (-jnp.arange(ROPE_AXIS_PAIRS, dtype=F32) / ROPE_AXIS_PAIRS)
    ang = jnp.concatenate([jnp.asarray(row)[:, None] * inv, jnp.asarray(col)[:, None] * inv], axis=-1)
    cos, sin = jnp.cos(ang), jnp.sin(ang)
    cos = jnp.tile(cos, (1, LANES // ROPE_HALF))
    sin = jnp.tile(jnp.concatenate([-sin, sin], axis=-1), (1, LANES // HEAD_DIM))
    return cos, sin


def _extended_w_in(w):
    qa, ka, va, rest = w[:, :512], w[:, 512:640], w[:, 640:768], w[:, 768:]

    def dup(t):
        t = t.reshape(D_MODEL, A_KV_HEADS, 1, HEAD_DIM)
        return jnp.broadcast_to(t, (D_MODEL, A_KV_HEADS, 2, HEAD_DIM)).reshape(D_MODEL, 4 * HEAD_DIM)

    return jnp.concatenate([qa, dup(ka), dup(va), rest], axis=-1).astype(BF16)


def _dup_kv_heads(t):
    b, p = t.shape[:2]
    t = jnp.broadcast_to(t[:, :, :, None, :], (b, p, A_KV_HEADS, 2, HEAD_DIM))
    return t.reshape(b, p, 4 * HEAD_DIM).astype(BF16)


def kernel(x_prompt, x_sample, cache_k_a, cache_v_a, cache_k_b, cache_v_b, cache_k_c, cache_v_c, c, c_ctx,
           w_ada, b_ada, norm_attn, norm_ffn, w_in, sinks, rpb, lambda_qk, subln, w_branch, w_out,
           w_router, b_router, w_gu, b_gu, w_down, b_down, norm_final):
    bsz, seq = x_prompt.shape[:2]
    bsz_d, n_lat = x_sample.shape[:2]
    past = cache_k_a.shape[2]
    n_p = bsz * seq
    n_s = bsz_d * n_lat
    assert seq % TM == 0 and n_lat % TQ == 0 and n_lat % C_CHUNK == 0 and bsz_d + 1 <= 8

    cond = jnp.zeros((8, D_MODEL), F32).at[0].set(c_ctx).at[1:1 + bsz_d].set(c)
    mod_all = _modulation(cond, w_ada, b_ada).reshape(DEPTH, 8, N_MOD, D_MODEL)
    cos, sin = _rope_tables(n_lat)
    g_final = norm_final[None, :]
    n_assign = (n_p + n_s) * TOP_K
    m_rows = (n_assign + N_EXPERTS * (MOE_TM - 1) + MOE_TM - 1) // MOE_TM * MOE_TM

    xp = x_prompt.reshape(n_p, D_MODEL)
    xs = x_sample.reshape(n_s, D_MODEL)
    kv_out = []
    for l in range(DEPTH):
        lambda_init = 0.8 - 0.6 * math.exp(-0.3 * l)
        mod = mod_all[l]
        w_ext = _extended_w_in(w_in[l])
        g_attn = norm_attn[l][None, :]
        g_ffn = norm_ffn[l][None, :]
        sg = subln[l][None, :]
        wb = w_branch[l].astype(BF16)
        wo = w_out[l].astype(BF16)
        br = b_router[l][None, :]

        pa, pb, pc, pg, kva, kvb, kvc = _project(xp, mod, 0, n_p // TM, g_attn, w_ext, cos, sin,
                                                 rope=False, emit_kv=True)
        oa, ob, oc = _ctx_attention(pa, pb, pc, sinks[l], lambda_qk[l], sg, seq, lambda_init)
        xp_mid, hp, idx_p, gate_p = _merge_route(xp, mod, 0, n_p // TM, g_ffn, oa, ob, oc, pg, wb, wo,
                                                 w_router[l], br)
        kv_out.append((kva, kvb, kvc))

        pa, pb, pc, pg = _project(xs, mod, 1, n_lat // TM, g_attn, w_ext, cos, sin, rope=True, emit_kv=False)
        oa = _win_attention(pa, _dup_kv_heads(cache_k_a[:, l]), _dup_kv_heads(cache_v_a[:, l]), sinks[l],
                            bsz_d, n_lat)
        ob = _na_attention(pb, cache_k_b[:, l].reshape(bsz_d, past, -1).astype(BF16),
                           cache_v_b[:, l].reshape(bsz_d, past, -1).astype(BF16),
                           _na_bias_table(rpb.reshape(-1), l, n_lat // GRID_W), bsz_d, n_lat)
        oc = _diff_attention(pc, cache_k_c[:, l].reshape(bsz_d, past, -1).astype(BF16),
                             cache_v_c[:, l].reshape(bsz_d, past, -1).astype(BF16),
                             lambda_qk[l], sg, bsz_d, n_lat, lambda_init)
        xs_mid, hs, idx_s, gate_s = _merge_route(xs, mod, 1, n_lat // TM, g_ffn, oa, ob, oc, pg, wb, wo,
                                                 w_router[l], br)

        last = l == DEPTH - 1
        dest_slab, cnt = _route(jnp.concatenate([idx_p, idx_s], axis=0))
        counts = cnt[0, :N_EXPERTS].astype(jnp.int32)
        pad_end, block_e, block_first, block_valid = _moe_block_tables(counts, m_rows // MOE_TM)
        dest = dest_slab[:, :TOP_K].reshape(-1, 1, TM * TOP_K)
        dest_p, dest_s = dest[:n_p // TM], dest[n_p // TM:]
        xb = _dispatch(pad_end, counts, dest, hp, hs, m_rows)
        yb = _moe_experts(xb, block_e, block_first, block_valid, l, w_gu, b_gu, w_down, b_down)
        xp = _combine(dest_p, xp_mid, mod, 0, n_p // TM, gate_p, g_final, yb, final=last)
        xs = _combine(dest_s, xs_mid, mod, 1, n_lat // TM, gate_s, g_final, yb, final=last)

    y_prompt = xp.reshape(bsz, seq, D_MODEL)
    y_sample = xs.reshape(bsz_d, n_lat, D_MODEL)

    def stack(pick):
        return jnp.stack([pick(kv) for kv in kv_out], axis=1)

    def heads_a(t):
        return t.reshape(bsz, seq, A_KV_HEADS, 2, HEAD_DIM)[:, :, :, 0]

    new_k_a = stack(lambda kv: heads_a(kv[0][:, 0:256]))
    new_v_a = stack(lambda kv: heads_a(kv[0][:, 256:512]))
    new_k_b = stack(lambda kv: kv[1][:, 0:512].reshape(bsz, seq, B_HEADS, HEAD_DIM))
    new_v_b = stack(lambda kv: kv[1][:, 512:1024].reshape(bsz, seq, B_HEADS, HEAD_DIM))
    new_k_c = stack(lambda kv: kv[2][:, 0:512].reshape(bsz, seq, C_HEADS, 2 * HEAD_DIM))
    new_v_c = stack(lambda kv: kv[2][:, 512:1024].reshape(bsz, seq, C_HEADS, C_VDIM))
    return (y_prompt, y_sample, new_k_a, new_v_a, new_k_b, new_v_b, new_k_c, new_v_c)
```

```python
import functools
import math

import jax
import jax.numpy as jnp
import numpy as np
from jax import lax
from jax.experimental import pallas as pl
from jax.experimental.pallas import tpu as pltpu

F32 = jnp.float32
BF16 = jnp.bfloat16

D_MODEL = 1024
DEPTH = 2
GRID_W = 64
HEAD_DIM = 64
ROPE_HALF = HEAD_DIM // 2
ROPE_AXIS_PAIRS = HEAD_DIM // 4
ROPE_THETA = 10000.0
A_HEADS = 8
A_KV_HEADS = 2
WINDOW = 128
B_HEADS = 8
NA_KH = 8
NA_KW = 16
C_HEADS = 4
C_VDIM = 2 * HEAD_DIM
BRANCH_W = 512
N_EXPERTS = 32
TOP_K = 4
D_FF = D_MODEL
SWIGLU_LIMIT = 7.0
SWIGLU_ALPHA = 1.702
N_MOD = 6
EPS = 1e-6
NEG_INF = -1e30
ATTN_SCALE = HEAD_DIM ** -0.5
LOG2E = 1.4426950408889634
C_QSCALE = ATTN_SCALE * LOG2E

LANES = 128
VMEM_LIMIT = 56 * 1024 * 1024

A_COLS = 1024
B_COLS = 1536
C_COLS = 1536
G_COLS = 3 * D_MODEL
P_COLS = A_COLS + B_COLS + C_COLS + G_COLS

TM = 256
TQ = 256
NA_ROWS_Q = TQ // GRID_W
NA_ROWS_K = NA_ROWS_Q + NA_KH
NA_KEYS = NA_ROWS_K * GRID_W
WIN_KEYS = TQ + 2 * WINDOW
NA_COLS_PER_STEP = 2
C_CHUNK = 512
MOE_TM = 512


def _cparams(sem):
    return pltpu.CompilerParams(dimension_semantics=sem, vmem_limit_bytes=VMEM_LIMIT)


def _head_masks():
    lane = lax.broadcasted_iota(jnp.int32, (1, LANES), 1)
    lo = jnp.where(lane < HEAD_DIM, 1.0, 0.0).astype(BF16)
    hi = jnp.where(lane >= HEAD_DIM, 1.0, 0.0).astype(BF16)
    return lo, hi


def _split_heads(q):
    lo, hi = _head_masks()
    return jnp.concatenate([q * lo, q * hi], axis=0)


def _merge_heads(pv, t):
    lane = lax.broadcasted_iota(jnp.int32, (t, LANES), 1)
    return jnp.where(lane < HEAD_DIM, pv[:t], pv[t:])


def _dot_nt(a, b):
    return lax.dot_general(a, b, (((1,), (1,)), ((), ())), preferred_element_type=F32)


def _dot(a, b):
    return jnp.dot(a, b, preferred_element_type=F32)


def _ada_kernel(c_ref, w_ref, b_ref, o_ref):
    c = c_ref[...]
    s = c * (1.0 / (1.0 + jnp.exp(-c)))
    o_ref[0] = jnp.dot(s, w_ref[0], preferred_element_type=F32,
                       precision=lax.Precision.HIGHEST) + b_ref[0]


def _modulation(cond, w_ada, b_ada):
    tn = 1536
    n = N_MOD * D_MODEL
    return pl.pallas_call(
        _ada_kernel,
        grid=(DEPTH, n // tn),
        in_specs=[
            pl.BlockSpec((8, D_MODEL), lambda l, j: (0, 0)),
            pl.BlockSpec((1, D_MODEL, tn), lambda l, j: (l, 0, j)),
            pl.BlockSpec((1, 1, tn), lambda l, j: (l, 0, j)),
        ],
        out_specs=pl.BlockSpec((1, 8, tn), lambda l, j: (l, 0, j)),
        out_shape=jax.ShapeDtypeStruct((DEPTH, 8, n), F32),
        compiler_params=_cparams(("arbitrary", "arbitrary")),
        name="modulation",
    )(cond, w_ada, b_ada.reshape(DEPTH, 1, n))


def _rms_modulate(x, g, shift, scale):
    ms = jnp.mean(x * x, axis=-1, keepdims=True)
    return (x * lax.rsqrt(ms + EPS) * g) * (1.0 + scale) + shift


def _rope_cols(v, cos, sin_signed):
    t, w = v.shape
    lane = lax.broadcasted_iota(jnp.int32, (t, LANES), 1)
    first_half = (lane & (HEAD_DIM - 1)) < ROPE_HALF
    cols = []
    for c in range(w // LANES):
        xc = v[:, c * LANES:(c + 1) * LANES]
        partner = jnp.where(first_half,
                            pltpu.roll(xc, LANES - ROPE_HALF, axis=1),
                            pltpu.roll(xc, ROPE_HALF, axis=1))
        cols.append(xc * cos + partner * sin_signed)
    return jnp.concatenate(cols, axis=1) if len(cols) > 1 else cols[0]


def _proj_kernel(x_ref, mod_ref, g_ref, w_ref, cos_ref, sin_ref, *out_refs, rope, emit_kv):
    oa_ref, ob_ref, oc_ref, og_ref = out_refs[:4]
    h = _rms_modulate(x_ref[...], g_ref[...], mod_ref[0, 0:1, :], mod_ref[0, 1:2, :]).astype(BF16)
    if rope:
        cos = cos_ref[...]
        sin = sin_ref[...]

    def seg(start, width):
        return _dot(h, w_ref[:, start:start + width])

    def maybe_rope(v):
        return _rope_cols(v, cos, sin) if rope else v

    oa_ref[:, 0:512] = (maybe_rope(seg(0, 512)) * ATTN_SCALE).astype(BF16)
    ka = maybe_rope(seg(512, 256))
    va = seg(768, 256)
    oa_ref[:, 512:768] = ka.astype(BF16)
    oa_ref[:, 768:1024] = va.astype(BF16)
    ob_ref[:, 0:512] = (seg(A_COLS, 512) * ATTN_SCALE).astype(BF16)
    kb = seg(A_COLS + 512, 512)
    vb = seg(A_COLS + 1024, 512)
    ob_ref[:, 512:1024] = kb.astype(BF16)
    ob_ref[:, 1024:1536] = vb.astype(BF16)
    c0 = A_COLS + B_COLS
    oc_ref[:, 0:512] = (maybe_rope(seg(c0, 512)) * C_QSCALE).astype(BF16)
    kc = maybe_rope(seg(c0 + 512, 512))
    vc = seg(c0 + 1024, 512)
    oc_ref[:, 512:1024] = kc.astype(BF16)
    oc_ref[:, 1024:1536] = vc.astype(BF16)
    g0 = c0 + C_COLS
    for j in range(G_COLS // 512):
        gv = seg(g0 + j * 512, 512)
        og_ref[:, j * 512:(j + 1) * 512] = (1.0 / (1.0 + jnp.exp(-gv))).astype(BF16)
    if emit_kv:
        kva_ref, kvb_ref, kvc_ref = out_refs[4:]
        kva_ref[:, 0:256] = ka
        kva_ref[:, 256:512] = va
        kvb_ref[:, 0:512] = kb
        kvb_ref[:, 512:1024] = vb
        kvc_ref[:, 0:512] = kc
        kvc_ref[:, 512:1024] = vc


def _project(x, mod, mod_row0, tiles_per_row, g, w_ext, cos, sin, *, rope, emit_kv):
    n = x.shape[0]
    nt = n // TM
    pos_tiles = cos.shape[0] // TM
    out_shape = [jax.ShapeDtypeStruct((n, A_COLS), BF16), jax.ShapeDtypeStruct((n, B_COLS), BF16),
                 jax.ShapeDtypeStruct((n, C_COLS), BF16), jax.ShapeDtypeStruct((n, G_COLS), BF16)]
    out_specs = [pl.BlockSpec((TM, A_COLS), lambda i: (i, 0)), pl.BlockSpec((TM, B_COLS), lambda i: (i, 0)),
                 pl.BlockSpec((TM, C_COLS), lambda i: (i, 0)), pl.BlockSpec((TM, G_COLS), lambda i: (i, 0))]
    if emit_kv:
        out_shape += [jax.ShapeDtypeStruct((n, 512), F32), jax.ShapeDtypeStruct((n, 1024), F32),
                      jax.ShapeDtypeStruct((n, 1024), F32)]
        out_specs += [pl.BlockSpec((TM, 512), lambda i: (i, 0)), pl.BlockSpec((TM, 1024), lambda i: (i, 0)),
                      pl.BlockSpec((TM, 1024), lambda i: (i, 0))]
    return pl.pallas_call(
        functools.partial(_proj_kernel, rope=rope, emit_kv=emit_kv),
        grid=(nt,),
        in_specs=[
            pl.BlockSpec((TM, D_MODEL), lambda i: (i, 0)),
            pl.BlockSpec((1, N_MOD, D_MODEL), lambda i: (mod_row0 + i // tiles_per_row, 0, 0)),
            pl.BlockSpec((1, D_MODEL), lambda i: (0, 0)),
            pl.BlockSpec((D_MODEL, P_COLS), lambda i: (0, 0)),
            pl.BlockSpec((TM, LANES), lambda i: (i % pos_tiles, 0)),
            pl.BlockSpec((TM, LANES), lambda i: (i % pos_tiles, 0)),
        ],
        out_specs=out_specs,
        out_shape=out_shape,
        compiler_params=_cparams(("arbitrary",)),
        name="project_rope" if rope else "project",
    )(x, mod, g, w_ext, cos, sin)


def _softmax_pv(scores, values, sink=None):
    m = scores[0].max(axis=-1, keepdims=True)
    for s in scores[1:]:
        m = jnp.maximum(m, s.max(axis=-1, keepdims=True))
    if sink is not None:
        m = jnp.maximum(m, sink)
    l = None
    acc = None
    for s, v in zip(scores, values):
        e = jnp.exp(s - m)
        ls = e.sum(axis=-1, keepdims=True)
        pv = _dot(e.astype(BF16), v)
        l = ls if l is None else l + ls
        acc = pv if acc is None else acc + pv
    if sink is not None:
        l = l + jnp.exp(sink - m)
    return acc * (1.0 / l)


def _sink_column(sink_ref, first_head, rows_per_head, n_heads):
    row = lax.broadcasted_iota(jnp.int32, (rows_per_head * n_heads, 1), 0)
    col = jnp.full((rows_per_head * n_heads, 1), sink_ref[first_head], F32)
    for j in range(1, n_heads):
        col = jnp.where(row >= j * rows_per_head, sink_ref[first_head + j], col)
    return col


def _diff_lambda(lq_ref, lambda_init):
    lf = lq_ref[...]
    a = jnp.sum(lf[0:1] * lf[1:2], axis=-1, keepdims=True)
    b = jnp.sum(lf[2:3] * lf[3:4], axis=-1, keepdims=True)
    return jnp.exp(a) - jnp.exp(b) + lambda_init


def _subln(o, g, lambda_init):
    ms = jnp.mean(o * o, axis=-1, keepdims=True)
    return (o * lax.rsqrt(ms + EPS) * g) * (1.0 - lambda_init)


def _ctx_attn_kernel(sink_ref, lq_ref, sg_ref, a_ref, b_ref, c_ref, oa_ref, ob_ref, oc_ref, *, lambda_init):
    t = a_ref.shape[0]
    for g in range(A_KV_HEADS):
        q = a_ref[:, g * 256:(g + 1) * 256]
        lhs = jnp.concatenate([_split_heads(q[:, :LANES]), _split_heads(q[:, LANES:])], axis=0)
        k = a_ref[:, 512 + g * LANES:512 + (g + 1) * LANES]
        v = a_ref[:, 768 + g * LANES:768 + (g + 1) * LANES]
        sink = _sink_column(sink_ref, 4 * g, t, 4)
        o = _softmax_pv([_dot_nt(lhs, k)], [v], sink)
        oa_ref[:, g * 256:g * 256 + LANES] = _merge_heads(o[:2 * t], t).astype(BF16)
        oa_ref[:, g * 256 + LANES:(g + 1) * 256] = _merge_heads(o[2 * t:], t).astype(BF16)
    for c in range(B_HEADS // 2):
        lhs = _split_heads(b_ref[:, c * LANES:(c + 1) * LANES])
        k = b_ref[:, 512 + c * LANES:512 + (c + 1) * LANES]
        v = b_ref[:, 1024 + c * LANES:1024 + (c + 1) * LANES]
        o = _softmax_pv([_dot_nt(lhs, k)], [v])
        ob_ref[:, c * LANES:(c + 1) * LANES] = _merge_heads(o, t).astype(BF16)
    lam = _diff_lambda(lq_ref, lambda_init)
    for h in range(C_HEADS):
        lhs = _split_heads(c_ref[:, h * LANES:(h + 1) * LANES])
        k = c_ref[:, 512 + h * LANES:512 + (h + 1) * LANES]
        v = c_ref[:, 1024 + h * LANES:1024 + (h + 1) * LANES]
        s = _dot_nt(lhs, k)
        m = s.max(axis=-1, keepdims=True)
        e = jnp.exp2(s - m)
        p = e * (1.0 / e.sum(axis=-1, keepdims=True))
        a = (p[:t] - lam * p[t:]).astype(BF16)
        oc_ref[:, h * LANES:(h + 1) * LANES] = _subln(_dot(a, v), sg_ref[...], lambda_init).astype(BF16)


def _ctx_attention(pa, pb, pc, sink, lq, sg, seq, lambda_init):
    n = pa.shape[0]
    smem = pl.BlockSpec(memory_space=pltpu.SMEM)
    return pl.pallas_call(
        functools.partial(_ctx_attn_kernel, lambda_init=lambda_init),
        grid=(n // seq,),
        in_specs=[
            smem,
            pl.BlockSpec((4, HEAD_DIM), lambda b: (0, 0)),
            pl.BlockSpec((1, C_VDIM), lambda b: (0, 0)),
            pl.BlockSpec((seq, A_COLS), lambda b: (b, 0)),
            pl.BlockSpec((seq, B_COLS), lambda b: (b, 0)),
            pl.BlockSpec((seq, C_COLS), lambda b: (b, 0)),
        ],
        out_specs=[pl.BlockSpec((seq, BRANCH_W), lambda b: (b, 0))] * 3,
        out_shape=[jax.ShapeDtypeStruct((n, BRANCH_W), BF16)] * 3,
        compiler_params=_cparams(("arbitrary",)),
        name="context_attention",
    )(sink, lq, sg, pa, pb, pc)


def _win_attn_kernel(sink_ref, q_ref, k_ref, v_ref, kc_ref, vc_ref, o_ref, *, n_lat):
    qi = pl.program_id(1)
    ws = pl.multiple_of(jnp.clip(qi * TQ - WINDOW, 0, n_lat - WIN_KEYS), WINDOW)
    qpos = qi * TQ + lax.broadcasted_iota(jnp.int32, (TQ, WIN_KEYS), 0)
    kpos = ws + lax.broadcasted_iota(jnp.int32, (TQ, WIN_KEYS), 1)
    band = jnp.where(jnp.abs(kpos - qpos) <= WINDOW, 0.0, NEG_INF)
    band = jnp.concatenate([band] * 4, axis=0)
    for g in range(A_KV_HEADS):
        q = q_ref[:, g * 256:(g + 1) * 256]
        lhs = jnp.concatenate([_split_heads(q[:, :LANES]), _split_heads(q[:, LANES:])], axis=0)
        cols = slice(g * LANES, (g + 1) * LANES)
        s_loc = _dot_nt(lhs, k_ref[pl.ds(ws, WIN_KEYS), cols]) + band
        s_ctx = _dot_nt(lhs, kc_ref[0, :, cols])
        sink = _sink_column(sink_ref, 4 * g, TQ, 4)
        o = _softmax_pv([s_loc, s_ctx], [v_ref[pl.ds(ws, WIN_KEYS), cols], vc_ref[0, :, cols]], sink)
        o_ref[:, g * 256:g * 256 + LANES] = _merge_heads(o[:2 * TQ], TQ).astype(BF16)
        o_ref[:, g * 256 + LANES:(g + 1) * 256] = _merge_heads(o[2 * TQ:], TQ).astype(BF16)


def _win_attention(pa, kctx, vctx, sink, n_batch, n_lat):
    nq = n_lat // TQ
    past = kctx.shape[1]
    return pl.pallas_call(
        functools.partial(_win_attn_kernel, n_lat=n_lat),
        grid=(n_batch, nq),
        in_specs=[
            pl.BlockSpec(memory_space=pltpu.SMEM),
            pl.BlockSpec((TQ, 512), lambda b, i: (b * nq + i, 0)),
            pl.BlockSpec((n_lat, 256), lambda b, i: (b, 2)),
            pl.BlockSpec((n_lat, 256), lambda b, i: (b, 3)),
            pl.BlockSpec((1, past, 256), lambda b, i: (b, 0, 0)),
            pl.BlockSpec((1, past, 256), lambda b, i: (b, 0, 0)),
        ],
        out_specs=pl.BlockSpec((TQ, BRANCH_W), lambda b, i: (b * nq + i, 0)),
        out_shape=jax.ShapeDtypeStruct((n_batch * n_lat, BRANCH_W), BF16),
        compiler_params=_cparams(("arbitrary", "arbitrary")),
        name="window_attention",
    )(sink, pa, pa, pa, kctx, vctx)


def _na_window_start(qi, rows):
    return jnp.clip(qi * NA_ROWS_Q - NA_KH // 2, 0, rows - NA_ROWS_K)


def _na_attn_kernel(q_ref, k_ref, v_ref, kc_ref, vc_ref, bias_ref, o_ref, *, rows):
    qi = pl.program_id(2)
    ws = pl.multiple_of(_na_window_start(qi, rows) * GRID_W, GRID_W)
    for c in range(NA_COLS_PER_STEP):
        cols = slice(c * LANES, (c + 1) * LANES)
        lhs = _split_heads(q_ref[:, cols])
        s_loc = _dot_nt(lhs, k_ref[pl.ds(ws, NA_KEYS), cols]) + bias_ref[0, 2 * c:2 * c + 2].reshape(2 * TQ, NA_KEYS)
        s_ctx = _dot_nt(lhs, kc_ref[0, :, cols])
        o = _softmax_pv([s_loc, s_ctx], [v_ref[pl.ds(ws, NA_KEYS), cols], vc_ref[0, :, cols]])
        o_ref[:, cols] = _merge_heads(o, TQ).astype(BF16)


def _na_bias_kernel(rpb_ref, o_ref, *, layer, rows):
    h = pl.program_id(0)
    n_dr = 2 * NA_KH - 1
    n_dc = 2 * NA_KW - 1
    base = (layer * B_HEADS + h) * n_dr * n_dc
    lane = lax.broadcasted_iota(jnp.int32, (GRID_W, LANES), 1)
    qc = lax.broadcasted_iota(jnp.int32, (GRID_W, LANES), 0)
    kc = lane & (GRID_W - 1)
    second = lane >= GRID_W
    cs = jnp.clip(qc - NA_KW // 2, 0, GRID_W - NA_KW)
    col_ok = (kc >= cs) & (kc < cs + NA_KW)
    dc_map = kc - qc + (NA_KW - 1)
    cache = {}

    def pair_tile(dr0, ok0, ok1):
        key = (dr0 if ok0 else None, dr0 + 1 if ok1 else None)
        if key not in cache:
            if not (ok0 or ok1):
                cache[key] = jnp.full((GRID_W, LANES), NEG_INF, F32)
            else:
                acc = jnp.zeros((GRID_W, LANES), F32)
                for dc in range(n_dc):
                    v0 = rpb_ref[base + dr0 * n_dc + dc] if ok0 else 0.0
                    v1 = rpb_ref[base + (dr0 + 1) * n_dc + dc] if ok1 else 0.0
                    acc = jnp.where(dc_map == dc, jnp.where(second, v1, v0), acc)
                if ok0 and ok1:
                    ok = col_ok
                elif ok0:
                    ok = col_ok & jnp.logical_not(second)
                else:
                    ok = col_ok & second
                cache[key] = jnp.where(ok, acc, NEG_INF)
        return cache[key]

    nq = rows // NA_ROWS_Q
    for p, qi in enumerate((0, 1, nq - 1)):
        r0 = qi * NA_ROWS_Q
        ws = min(max(r0 - NA_KH // 2, 0), rows - NA_ROWS_K)
        for i in range(NA_ROWS_Q):
            r = r0 + i
            lo = min(max(r - NA_KH // 2, 0), rows - NA_KH)
            for jp in range(NA_ROWS_K // 2):
                kr = ws + 2 * jp
                ok0 = lo <= kr < lo + NA_KH
                ok1 = lo <= kr + 1 < lo + NA_KH
                o_ref[p, 0, i * GRID_W:(i + 1) * GRID_W, jp * LANES:(jp + 1) * LANES] = pair_tile(
                    kr - r + NA_KH - 1, ok0, ok1)


def _na_bias_table(rpb_flat, layer, rows):
    return pl.pallas_call(
        functools.partial(_na_bias_kernel, layer=layer, rows=rows),
        grid=(B_HEADS,),
        in_specs=[pl.BlockSpec(memory_space=pltpu.SMEM)],
        out_specs=pl.BlockSpec((3, 1, TQ, NA_KEYS), lambda h: (0, h, 0, 0)),
        out_shape=jax.ShapeDtypeStruct((3, B_HEADS, TQ, NA_KEYS), F32),
        compiler_params=_cparams(("arbitrary",)),
        name="neighborhood_bias",
    )(rpb_flat)


def _na_attention(pb, kctx, vctx, bias, n_batch, n_lat):
    nq = n_lat // TQ
    rows = n_lat // GRID_W
    past = kctx.shape[1]
    w = NA_COLS_PER_STEP * LANES
    n_cb = BRANCH_W // w

    def bias_idx(b, c, i):
        return (jnp.where(i == 0, 0, jnp.where(i == nq - 1, 2, 1)), c, 0, 0)

    return pl.pallas_call(
        functools.partial(_na_attn_kernel, rows=rows),
        grid=(n_batch, n_cb, nq),
        in_specs=[
            pl.BlockSpec((TQ, w), lambda b, c, i: (b * nq + i, c)),
            pl.BlockSpec((n_lat, w), lambda b, c, i: (b, n_cb + c)),
            pl.BlockSpec((n_lat, w), lambda b, c, i: (b, 2 * n_cb + c)),
            pl.BlockSpec((1, past, w), lambda b, c, i: (b, 0, c)),
            pl.BlockSpec((1, past, w), lambda b, c, i: (b, 0, c)),
            pl.BlockSpec((1, 2 * NA_COLS_PER_STEP, TQ, NA_KEYS), bias_idx),
        ],
        out_specs=pl.BlockSpec((TQ, w), lambda b, c, i: (b * nq + i, c)),
        out_shape=jax.ShapeDtypeStruct((n_batch * n_lat, BRANCH_W), BF16),
        compiler_params=_cparams(("arbitrary", "arbitrary", "arbitrary")),
        name="neighborhood_attention",
    )(pb, pb, pb, kctx, vctx, bias)


def _diff_attn_kernel(lq_ref, sg_ref, q_ref, k_ref, v_ref, kc_ref, vc_ref, o_ref, vt_s, vtc_s, acc_s, *,
                      n_lat, lambda_init):
    n_chunks = n_lat // C_CHUNK

    @pl.when(pl.program_id(2) == 0)
    def _():
        for j in range(n_chunks):
            vt_s[j] = v_ref[j * C_CHUNK:(j + 1) * C_CHUNK, :].astype(F32).T.astype(BF16)
        vtc_s[...] = vc_ref[0].astype(F32).T.astype(BF16)

    lhs = _split_heads(q_ref[...])
    acc_s[...] = jnp.zeros_like(acc_s)

    def step(k, vt, m, l):
        st = _dot_nt(k, lhs)
        m_new = jnp.maximum(m, st.max(axis=0, keepdims=True))
        alpha = jnp.exp2(m - m_new)
        e = jnp.exp2(st - m_new)
        acc_s[...] = alpha * acc_s[...] + _dot(vt, e.astype(BF16))
        return m_new, alpha * l + e.sum(axis=0, keepdims=True)

    m, l = jnp.full((1, 2 * TQ), NEG_INF, F32), jnp.zeros((1, 2 * TQ), F32)
    for j in range(n_chunks):
        m, l = step(k_ref[j * C_CHUNK:(j + 1) * C_CHUNK, :], vt_s[j], m, l)
    m, l = step(kc_ref[0], vtc_s[...], m, l)
    o = acc_s[...] * (1.0 / l)
    d = o[:, :TQ] - _diff_lambda(lq_ref, lambda_init) * o[:, TQ:]
    ms = jnp.mean(d * d, axis=0, keepdims=True)
    y = (d * lax.rsqrt(ms + EPS)).T
    o_ref[...] = (y * sg_ref[...] * (1.0 - lambda_init)).astype(BF16)


def _diff_attention(pc, kctx, vctx, lq, sg, n_batch, n_lat, lambda_init):
    nq = n_lat // TQ
    past = kctx.shape[1]
    return pl.pallas_call(
        functools.partial(_diff_attn_kernel, n_lat=n_lat, lambda_init=lambda_init),
        grid=(n_batch, C_HEADS, nq),
        in_specs=[
            pl.BlockSpec((4, HEAD_DIM), lambda b, h, i: (0, 0)),
            pl.BlockSpec((1, C_VDIM), lambda b, h, i: (0, 0)),
            pl.BlockSpec((TQ, LANES), lambda b, h, i: (b * nq + i, h)),
            pl.BlockSpec((n_lat, LANES), lambda b, h, i: (b, 4 + h)),
            pl.BlockSpec((n_lat, LANES), lambda b, h, i: (b, 8 + h)),
            pl.BlockSpec((1, past, LANES), lambda b, h, i: (b, 0, h)),
            pl.BlockSpec((1, past, LANES), lambda b, h, i: (b, 0, h)),
        ],
        out_specs=pl.BlockSpec((TQ, LANES), lambda b, h, i: (b * nq + i, h)),
        out_shape=jax.ShapeDtypeStruct((n_batch * n_lat, BRANCH_W), BF16),
        scratch_shapes=[pltpu.VMEM((n_lat // C_CHUNK, C_VDIM, C_CHUNK), BF16), pltpu.VMEM((C_VDIM, past), BF16),
                        pltpu.VMEM((C_VDIM, 2 * TQ), F32)],
        compiler_params=_cparams(("arbitrary", "arbitrary", "arbitrary")),
        name="differential_attention",
    )(lq, sg, pc, pc, pc, kctx, vctx)


def _merge_kernel(x_ref, mod_ref, g_ref, oa_ref, ob_ref, oc_ref, sg_ref, wb_ref, wo_ref, wr_ref, br_ref,
                  xo_ref, h_ref, idx_ref, gate_ref):
    y = None
    for j, o_ref in enumerate((oa_ref, ob_ref, oc_ref)):
        t = sg_ref[:, j * D_MODEL:(j + 1) * D_MODEL].astype(F32) * _dot(o_ref[...], wb_ref[j])
        y = t if y is None else y + t
    x = x_ref[...] + mod_ref[0, 2:3, :] * _dot(y.astype(BF16), wo_ref[...])
    xo_ref[...] = x
    h = _rms_modulate(x, g_ref[...], mod_ref[0, 3:4, :], mod_ref[0, 4:5, :])
    h_ref[...] = h
    wr = wr_ref[...]
    h_hi = h.astype(BF16)
    h_lo = (h - h_hi.astype(F32)).astype(BF16)
    w_hi = wr.astype(BF16)
    w_lo = (wr - w_hi.astype(F32)).astype(BF16)
    logits = _dot(h_hi, w_hi) + (_dot(h_hi, w_lo) + _dot(h_lo, w_hi)) + br_ref[...]
    tm = logits.shape[0]
    lane_e = lax.broadcasted_iota(jnp.int32, (tm, N_EXPERTS), 1)
    lane_o = lax.broadcasted_iota(jnp.int32, (tm, LANES), 1)
    idx_out = jnp.zeros((tm, LANES), jnp.int32)
    val_out = jnp.zeros((tm, LANES), F32)
    top = None
    denom = None
    for k in range(TOP_K):
        mx = logits.max(axis=-1, keepdims=True)
        sel = jnp.min(jnp.where(logits == mx, lane_e, N_EXPERTS), axis=-1, keepdims=True)
        logits = jnp.where(lane_e == sel, -jnp.inf, logits)
        if top is None:
            top = mx
        e = jnp.exp(mx - top)
        denom = e if denom is None else denom + e
        idx_out = jnp.where(lane_o == k, sel, idx_out)
        val_out = jnp.where(lane_o == k, e, val_out)
    idx_ref[...] = idx_out
    gate_ref[...] = val_out * (1.0 / denom)


def _merge_route(x, mod, mod_row0, tiles_per_row, g_ffn, oa, ob, oc, sg, wb, wo, wr, br):
    n = x.shape[0]
    row = lambda i: (i, 0)
    fixed2 = lambda i: (0, 0)
    return pl.pallas_call(
        _merge_kernel,
        grid=(n // TM,),
        in_specs=[
            pl.BlockSpec((TM, D_MODEL), row),
            pl.BlockSpec((1, N_MOD, D_MODEL), lambda i: (mod_row0 + i // tiles_per_row, 0, 0)),
            pl.BlockSpec((1, D_MODEL), fixed2),
            pl.BlockSpec((TM, BRANCH_W), row),
            pl.BlockSpec((TM, BRANCH_W), row),
            pl.BlockSpec((TM, BRANCH_W), row),
            pl.BlockSpec((TM, G_COLS), row),
            pl.BlockSpec((3, BRANCH_W, D_MODEL), lambda i: (0, 0, 0)),
            pl.BlockSpec((D_MODEL, D_MODEL), fixed2),
            pl.BlockSpec((D_MODEL, N_EXPERTS), fixed2),
            pl.BlockSpec((1, N_EXPERTS), fixed2),
        ],
        out_specs=[pl.BlockSpec((TM, D_MODEL), row), pl.BlockSpec((TM, D_MODEL), row),
                   pl.BlockSpec((TM, LANES), row), pl.BlockSpec((TM, LANES), row)],
        out_shape=[jax.ShapeDtypeStruct((n, D_MODEL), F32), jax.ShapeDtypeStruct((n, D_MODEL), F32),
                   jax.ShapeDtypeStruct((n, LANES), jnp.int32), jax.ShapeDtypeStruct((n, LANES), F32)],
        compiler_params=_cparams(("arbitrary",)),
        name="merge_route",
    )(x, mod, g_ffn, oa, ob, oc, sg, wb, wo, wr, br)


def _moe_kernel(be_ref, first_ref, valid_ref, x_ref, wgu_ref, bgu_ref, wd_ref, bd_ref, o_ref, wgu_s, wd_s):
    i = pl.program_id(0)

    @pl.when(first_ref[i] == 1)
    def _():
        wgu_s[...] = wgu_ref[0, 0].astype(BF16)
        wd_s[...] = wd_ref[0, 0].astype(BF16)

    @pl.when(valid_ref[i] == 1)
    def _():
        gu = _dot(x_ref[...].astype(BF16), wgu_s[...]) + bgu_ref[0, 0]
        gate = jnp.minimum(gu[:, :D_FF], SWIGLU_LIMIT)
        lin = jnp.clip(gu[:, D_FF:], -SWIGLU_LIMIT, SWIGLU_LIMIT)
        act = gate * (1.0 / (1.0 + jnp.exp(-SWIGLU_ALPHA * gate))) * (lin + 1.0)
        o_ref[...] = _dot(act.astype(BF16), wd_s[...]) + bd_ref[0, 0]

    @pl.when(valid_ref[i] == 0)
    def _():
        o_ref[...] = jnp.zeros_like(o_ref)


def _moe_experts(xb, block_e, block_first, block_valid, layer, w_gu, b_gu, w_down, b_down):
    m = xb.shape[0]
    nb = m // MOE_TM
    grid_spec = pltpu.PrefetchScalarGridSpec(
        num_scalar_prefetch=3,
        grid=(nb,),
        in_specs=[
            pl.BlockSpec((MOE_TM, D_MODEL), lambda i, be, bf, bv: (i, 0)),
            pl.BlockSpec((1, 1, D_MODEL, 2 * D_FF), lambda i, be, bf, bv: (layer, be[i], 0, 0)),
            pl.BlockSpec((1, 1, 1, 2 * D_FF), lambda i, be, bf, bv: (layer, be[i], 0, 0)),
            pl.BlockSpec((1, 1, D_FF, D_MODEL), lambda i, be, bf, bv: (layer, be[i], 0, 0)),
            pl.BlockSpec((1, 1, 1, D_MODEL), lambda i, be, bf, bv: (layer, be[i], 0, 0)),
        ],
        out_specs=pl.BlockSpec((MOE_TM, D_MODEL), lambda i, be, bf, bv: (i, 0)),
        scratch_shapes=[pltpu.VMEM((D_MODEL, 2 * D_FF), BF16), pltpu.VMEM((D_FF, D_MODEL), BF16)],
    )
    return pl.pallas_call(
        _moe_kernel,
        grid_spec=grid_spec,
        out_shape=jax.ShapeDtypeStruct((m, D_MODEL), F32),
        compiler_params=_cparams(("arbitrary",)),
        name="moe_experts",
    )(block_e, block_first, block_valid, xb, w_gu, b_gu.reshape(DEPTH, N_EXPERTS, 1, 2 * D_FF),
      w_down, b_down.reshape(DEPTH, N_EXPERTS, 1, D_MODEL))


def _moe_block_tables(counts, nb):
    padded = (counts + MOE_TM - 1) // MOE_TM * MOE_TM
    pad_end = jnp.cumsum(padded).astype(jnp.int32)
    blk = jnp.arange(nb, dtype=jnp.int32) * MOE_TM
    block_e = jnp.minimum(jnp.sum((blk[:, None] >= pad_end[None, :]).astype(jnp.int32), axis=1), N_EXPERTS - 1)
    block_valid = (blk < pad_end[-1]).astype(jnp.int32)
    prev = jnp.concatenate([jnp.full((1,), -1, jnp.int32), block_e[:-1]])
    block_first = (block_e != prev).astype(jnp.int32)
    return pad_end, block_e, block_first, block_valid


def _route_kernel(idx_ref, dest_ref, cnt_ref, run_s, start_s):
    p = pl.program_id(0)
    i = pl.program_id(1)
    tm = idx_ref.shape[0]
    lane = lax.broadcasted_iota(jnp.int32, (tm, LANES), 1)
    idx = idx_ref[...]
    sel = [idx[:, k:k + 1] for k in range(TOP_K)]
    onehot = jnp.zeros((tm, LANES), F32)
    for k in range(TOP_K):
        onehot = onehot + jnp.where(lane == sel[k], 1.0, 0.0)
    tile_cnt = jnp.sum(onehot, axis=0, keepdims=True)

    @pl.when((p == 0) & (i == 0))
    def _():
        run_s[...] = jnp.zeros_like(run_s)
        cnt_ref[...] = jnp.zeros_like(cnt_ref)

    @pl.when(p == 0)
    def _():
        run_s[...] += tile_cnt

    @pl.when((p == 1) & (i == 0))
    def _():
        cnt = run_s[...]
        padded = ((cnt.astype(jnp.int32) + (MOE_TM - 1)) // MOE_TM * MOE_TM).astype(F32)
        r = lax.broadcasted_iota(jnp.int32, (LANES, LANES), 0)
        c = lax.broadcasted_iota(jnp.int32, (LANES, LANES), 1)
        before = jnp.where(r < c, 1.0, 0.0)
        start = jnp.dot(jnp.broadcast_to(padded, (8, LANES)), before, preferred_element_type=F32,
                        precision=lax.Precision.HIGHEST)
        start_s[...] = start[0:1]
        cnt_ref[...] = jnp.broadcast_to(cnt, cnt_ref.shape)
        run_s[...] = jnp.zeros_like(run_s)

    @pl.when(p == 1)
    def _():
        r = lax.broadcasted_iota(jnp.int32, (tm, tm), 0)
        c = lax.broadcasted_iota(jnp.int32, (tm, tm), 1)
        earlier = jnp.where(c < r, 1.0, 0.0).astype(BF16)
        pos = _dot(earlier, onehot.astype(BF16)) + run_s[...] + start_s[...]
        out = jnp.zeros((tm, LANES), F32)
        for k in range(TOP_K):
            d = jnp.sum(jnp.where(lane == sel[k], pos, 0.0), axis=-1, keepdims=True)
            out = jnp.where(lane == k, d, out)
        dest_ref[...] = out.astype(jnp.int32)
        run_s[...] += tile_cnt


def _route(idx_slab):
    n = idx_slab.shape[0]
    return pl.pallas_call(
        _route_kernel,
        grid=(2, n // TM),
        in_specs=[pl.BlockSpec((TM, LANES), lambda p, i: (i, 0))],
        out_specs=[pl.BlockSpec((TM, LANES), lambda p, i: (i * p, 0)),
                   pl.BlockSpec((8, LANES), lambda p, i: (0, 0))],
        out_shape=[jax.ShapeDtypeStruct((n, LANES), jnp.int32), jax.ShapeDtypeStruct((8, LANES), F32)],
        scratch_shapes=[pltpu.VMEM((1, LANES), F32), pltpu.VMEM((1, LANES), F32)],
        compiler_params=_cparams(("arbitrary", "arbitrary")),
        name="route",
    )(idx_slab)


def _row_copy(src_ref, src_row, dst_ref, dst_row, sem):
    return pltpu.make_async_copy(src_ref.at[pl.ds(src_row, 1)], dst_ref.at[pl.ds(dst_row, 1)], sem)


def _dispatch_kernel(pe_ref, cnt_ref, dest_ref, hp_ref, hs_ref, xb_ref, zero_s, sem, *, tiles_p, first_tail_block):
    i = pl.program_id(0)
    tm = hp_ref.shape[0]
    nb = xb_ref.shape[0] // MOE_TM

    def zero_block(row0):
        return pltpu.make_async_copy(zero_s, xb_ref.at[pl.ds(pl.multiple_of(row0, MOE_TM), MOE_TM)], sem)

    @pl.when(i == 0)
    def _():
        zero_s[...] = jnp.zeros_like(zero_s)
        for start in (True, False):
            for e in range(N_EXPERTS):
                @pl.when(cnt_ref[e] > 0)
                def _():
                    cp = zero_block(pe_ref[e] - MOE_TM)
                    cp.start() if start else cp.wait()
            for b in range(first_tail_block, nb):
                @pl.when(b * MOE_TM >= pe_ref[N_EXPERTS - 1])
                def _():
                    cp = zero_block(b * MOE_TM)
                    cp.start() if start else cp.wait()

    def scatter_rows(h_ref):
        def issue(t, carry):
            for k in range(TOP_K):
                _row_copy(h_ref, t, xb_ref, dest_ref[0, 0, t * TOP_K + k], sem).start()
            return carry

        def drain(t, carry):
            for k in range(TOP_K):
                _row_copy(h_ref, 0, xb_ref, 0, sem).wait()
            return carry

        lax.fori_loop(0, tm, issue, 0)
        lax.fori_loop(0, tm, drain, 0)

    @pl.when(i < tiles_p)
    def _():
        scatter_rows(hp_ref)

    @pl.when(i >= tiles_p)
    def _():
        scatter_rows(hs_ref)


def _dispatch(pad_end, counts, dest, hp, hs, m):
    tiles_p = hp.shape[0] // TM
    tiles_s = hs.shape[0] // TM
    grid_spec = pltpu.PrefetchScalarGridSpec(
        num_scalar_prefetch=2,
        grid=(tiles_p + tiles_s,),
        in_specs=[
            pl.BlockSpec((1, 1, TM * TOP_K), lambda i, pe, cn: (i, 0, 0), memory_space=pltpu.SMEM),
            pl.BlockSpec((TM, D_MODEL), lambda i, pe, cn: (jnp.minimum(i, tiles_p - 1), 0)),
            pl.BlockSpec((TM, D_MODEL), lambda i, pe, cn: (jnp.maximum(i - tiles_p, 0), 0)),
        ],
        out_specs=pl.BlockSpec(memory_space=pl.ANY),
        scratch_shapes=[pltpu.VMEM((MOE_TM, D_MODEL), F32), pltpu.SemaphoreType.DMA(())],
    )
    return pl.pallas_call(
        functools.partial(_dispatch_kernel, tiles_p=tiles_p, first_tail_block=m // MOE_TM - N_EXPERTS),
        grid_spec=grid_spec,
        out_shape=jax.ShapeDtypeStruct((m, D_MODEL), F32),
        compiler_params=_cparams(("arbitrary",)),
        name="dispatch",
    )(pad_end, counts, dest, hp, hs)


def _combine_kernel(dest_ref, x_ref, mod_ref, gate_ref, gfin_ref, yb_ref, o_ref, buf, sem, *, final):
    tm = x_ref.shape[0]

    def issue(t, carry):
        for k in range(TOP_K):
            _row_copy(yb_ref, dest_ref[0, 0, t * TOP_K + k], buf.at[k], t, sem).start()
        return carry

    def drain(t, carry):
        for k in range(TOP_K):
            _row_copy(yb_ref, 0, buf.at[k], 0, sem).wait()
        return carry

    lax.fori_loop(0, tm, issue, 0)
    lax.fori_loop(0, tm, drain, 0)
    g = gate_ref[...]
    acc = g[:, 0:1] * buf[0]
    for k in range(1, TOP_K):
        acc = acc + g[:, k:k + 1] * buf[k]
    x = x_ref[...] + mod_ref[0, 5:6, :] * acc
    if final:
        ms = jnp.mean(x * x, axis=-1, keepdims=True)
        x = x * lax.rsqrt(ms + EPS) * gfin_ref[...]
    o_ref[...] = x


def _combine(dest, x, mod, mod_row0, tiles_per_row, gates, g_final, yb, *, final):
    n = x.shape[0]
    return pl.pallas_call(
        functools.partial(_combine_kernel, final=final),
        grid=(n // TM,),
        in_specs=[
            pl.BlockSpec((1, 1, TM * TOP_K), lambda i: (i, 0, 0), memory_space=pltpu.SMEM),
            pl.BlockSpec((TM, D_MODEL), lambda i: (i, 0)),
            pl.BlockSpec((1, N_MOD, D_MODEL), lambda i: (mod_row0 + i // tiles_per_row, 0, 0)),
            pl.BlockSpec((TM, LANES), lambda i: (i, 0)),
            pl.BlockSpec((1, D_MODEL), lambda i: (0, 0)),
            pl.BlockSpec(memory_space=pl.ANY),
        ],
        out_specs=pl.BlockSpec((TM, D_MODEL), lambda i: (i, 0)),
        out_shape=jax.ShapeDtypeStruct((n, D_MODEL), F32),
        scratch_shapes=[pltpu.VMEM((TOP_K, TM, D_MODEL), F32), pltpu.SemaphoreType.DMA(())],
        compiler_params=_cparams(("arbitrary",)),
        name="combine_final" if final else "combine",
    )(dest, x, mod, gates, g_final, yb)


def _rope_tables(n_tokens):
    t = np.arange(n_tokens)
    row = (t // GRID_W).astype(np.float32)
    col = (t % GRID_W).astype(np.float32)
    inv = jnp.asarray(ROPE_THETA, F32) ** (-jnp.arange(ROPE_AXIS_PAIRS, dtype=F32) / ROPE_AXIS_PAIRS)
    ang = jnp.concatenate([jnp.asarray(row)[:, None] * inv, jnp.asarray(col)[:, None] * inv], axis=-1)
    cos, sin = jnp.cos(ang), jnp.sin(ang)
    cos = jnp.tile(cos, (1, LANES // ROPE_HALF))
    sin = jnp.tile(jnp.concatenate([-sin, sin], axis=-1), (1, LANES // HEAD_DIM))
    return cos, sin


def _extended_w_in(w):
    qa, ka, va, rest = w[:, :512], w[:, 512:640], w[:, 640:768], w[:, 768:]

    def dup(t):
        t = t.reshape(D_MODEL, A_KV_HEADS, 1, HEAD_DIM)
        return jnp.broadcast_to(t, (D_MODEL, A_KV_HEADS, 2, HEAD_DIM)).reshape(D_MODEL, 4 * HEAD_DIM)

    return jnp.concatenate([qa, dup(ka), dup(va), rest], axis=-1).astype(BF16)


def _dup_kv_heads(t):
    b, p = t.shape[:2]
    t = jnp.broadcast_to(t[:, :, :, None, :], (b, p, A_KV_HEADS, 2, HEAD_DIM))
    return t.reshape(b, p, 4 * HEAD_DIM).astype(BF16)


def kernel(x_prompt, x_sample, cache_k_a, cache_v_a, cache_k_b, cache_v_b, cache_k_c, cache_v_c, c, c_ctx,
           w_ada, b_ada, norm_attn, norm_ffn, w_in, sinks, rpb, lambda_qk, subln, w_branch, w_out,
           w_router, b_router, w_gu, b_gu, w_down, b_down, norm_final):
    bsz, seq = x_prompt.shape[:2]
    bsz_d, n_lat = x_sample.shape[:2]
    past = cache_k_a.shape[2]
    n_p = bsz * seq
    n_s = bsz_d * n_lat
    assert seq % TM == 0 and n_lat % TQ == 0 and n_lat % C_CHUNK == 0 and bsz_d + 1 <= 8

    cond = jnp.zeros((8, D_MODEL), F32).at[0].set(c_ctx).at[1:1 + bsz_d].set(c)
    mod_all = _modulation(cond, w_ada, b_ada).reshape(DEPTH, 8, N_MOD, D_MODEL)
    cos, sin = _rope_tables(n_lat)
    g_final = norm_final[None, :]
    n_assign = (n_p + n_s) * TOP_K
    m_rows = (n_assign + N_EXPERTS * (MOE_TM - 1) + MOE_TM - 1) // MOE_TM * MOE_TM

    xp = x_prompt.reshape(n_p, D_MODEL)
    xs = x_sample.reshape(n_s, D_MODEL)
    kv_out = []
    for l in range(DEPTH):
        lambda_init = 0.8 - 0.6 * math.exp(-0.3 * l)
        mod = mod_all[l]
        w_ext = _extended_w_in(w_in[l])
        g_attn = norm_attn[l][None, :]
        g_ffn = norm_ffn[l][None, :]
        sg = subln[l][None, :]
        wb = w_branch[l].astype(BF16)
        wo = w_out[l].astype(BF16)
        br = b_router[l][None, :]

        pa, pb, pc, pg, kva, kvb, kvc = _project(xp, mod, 0, n_p // TM, g_attn, w_ext, cos, sin,
                                                 rope=False, emit_kv=True)
        oa, ob, oc = _ctx_attention(pa, pb, pc, sinks[l], lambda_qk[l], sg, seq, lambda_init)
        xp_mid, hp, idx_p, gate_p = _merge_route(xp, mod, 0, n_p // TM, g_ffn, oa, ob, oc, pg, wb, wo,
                                                 w_router[l], br)
        kv_out.append((kva, kvb, kvc))

        pa, pb, pc, pg = _project(xs, mod, 1, n_lat // TM, g_attn, w_ext, cos, sin, rope=True, emit_kv=False)
        oa = _win_attention(pa, _dup_kv_heads(cache_k_a[:, l]), _dup_kv_heads(cache_v_a[:, l]), sinks[l],
                            bsz_d, n_lat)
        ob = _na_attention(pb, cache_k_b[:, l].reshape(bsz_d, past, -1).astype(BF16),
                           cache_v_b[:, l].reshape(bsz_d, past, -1).astype(BF16),
                           _na_bias_table(rpb.reshape(-1), l, n_lat // GRID_W), bsz_d, n_lat)
        oc = _diff_attention(pc, cache_k_c[:, l].reshape(bsz_d, past, -1).astype(BF16),
                             cache_v_c[:, l].reshape(bsz_d, past, -1).astype(BF16),
                             lambda_qk[l], sg, bsz_d, n_lat, lambda_init)
        xs_mid, hs, idx_s, gate_s = _merge_route(xs, mod, 1, n_lat // TM, g_ffn, oa, ob, oc, pg, wb, wo,
                                                 w_router[l], br)

        last = l == DEPTH - 1
        dest_slab, cnt = _route(jnp.concatenate([idx_p, idx_s], axis=0))
        counts = cnt[0, :N_EXPERTS].astype(jnp.int32)
        pad_end, block_e, block_first, block_valid = _moe_block_tables(counts, m_rows // MOE_TM)
        dest = dest_slab[:, :TOP_K].reshape(-1, 1, TM * TOP_K)
        dest_p, dest_s = dest[:n_p // TM], dest[n_p // TM:]
        xb = _dispatch(pad_end, counts, dest, hp, hs, m_rows)
        yb = _moe_experts(xb, block_e, block_first, block_valid, l, w_gu, b_gu, w_down, b_down)
        xp = _combine(dest_p, xp_mid, mod, 0, n_p // TM, gate_p, g_final, yb, final=last)
        xs = _combine(dest_s, xs_mid, mod, 1, n_lat // TM, gate_s, g_final, yb, final=last)

    y_prompt = xp.reshape(bsz, seq, D_MODEL)
    y_sample = xs.reshape(bsz_d, n_lat, D_MODEL)

    def stack(pick):
        return jnp.stack([pick(kv) for kv in kv_out], axis=1)

    def heads_a(t):
        return t.reshape(bsz, seq, A_KV_HEADS, 2, HEAD_DIM)[:, :, :, 0]

    new_k_a = stack(lambda kv: heads_a(kv[0][:, 0:256]))
    new_v_a = stack(lambda kv: heads_a(kv[0][:, 256:512]))
    new_k_b = stack(lambda kv: kv[1][:, 0:512].reshape(bsz, seq, B_HEADS, HEAD_DIM))
    new_v_b = stack(lambda kv: kv[1][:, 512:1024].reshape(bsz, seq, B_HEADS, HEAD_DIM))
    new_k_c = stack(lambda kv: kv[2][:, 0:512].reshape(bsz, seq, C_HEADS, 2 * HEAD_DIM))
    new_v_c = stack(lambda kv: kv[2][:, 512:1024].reshape(bsz, seq, C_HEADS, C_VDIM))
    return (y_prompt, y_sample, new_k_a, new_v_a, new_k_b, new_v_b, new_k_c, new_v_c)
```

```python
import functools
import math

import jax
import jax.numpy as jnp
import numpy as np
from jax import lax
from jax.experimental import pallas as pl
from jax.experimental.pallas import tpu as pltpu

F32 = jnp.float32
BF16 = jnp.bfloat16

D_MODEL = 1024
DEPTH = 2
GRID_W = 64
HEAD_DIM = 64
ROPE_HALF = HEAD_DIM // 2
ROPE_AXIS_PAIRS = HEAD_DIM // 4
ROPE_THETA = 10000.0
A_HEADS = 8
A_KV_HEADS = 2
WINDOW = 128
B_HEADS = 8
NA_KH = 8
NA_KW = 16
C_HEADS = 4
C_VDIM = 2 * HEAD_DIM
BRANCH_W = 512
N_EXPERTS = 32
TOP_K = 4
D_FF = D_MODEL
SWIGLU_LIMIT = 7.0
SWIGLU_ALPHA = 1.702
N_MOD = 6
EPS = 1e-6
NEG_INF = -1e30
ATTN_SCALE = HEAD_DIM ** -0.5
LOG2E = 1.4426950408889634
C_QSCALE = ATTN_SCALE * LOG2E

LANES = 128
VMEM_LIMIT = 56 * 1024 * 1024

A_COLS = 1024
B_COLS = 1536
C_COLS = 1536
G_COLS = 3 * D_MODEL
P_COLS = A_COLS + B_COLS + C_COLS + G_COLS

TM = 256
TQ = 256
NA_ROWS_Q = TQ // GRID_W
NA_ROWS_K = NA_ROWS_Q + NA_KH
NA_KEYS = NA_ROWS_K * GRID_W
WIN_KEYS = TQ + 2 * WINDOW
NA_COLS_PER_STEP = 2
C_CHUNK = 512
MOE_TM = 512
ROUTE_TM = 1024


def _cparams(sem):
    return pltpu.CompilerParams(dimension_semantics=sem, vmem_limit_bytes=VMEM_LIMIT)


def _head_masks():
    lane = lax.broadcasted_iota(jnp.int32, (1, LANES), 1)
    lo = jnp.where(lane < HEAD_DIM, 1.0, 0.0).astype(BF16)
    hi = jnp.where(lane >= HEAD_DIM, 1.0, 0.0).astype(BF16)
    return lo, hi


def _split_heads(q):
    lo, hi = _head_masks()
    return jnp.concatenate([q * lo, q * hi], axis=0)


def _merge_heads(pv, t):
    lane = lax.broadcasted_iota(jnp.int32, (t, LANES), 1)
    return jnp.where(lane < HEAD_DIM, pv[:t], pv[t:])


def _dot_nt(a, b):
    return lax.dot_general(a, b, (((1,), (1,)), ((), ())), preferred_element_type=F32)


def _dot(a, b):
    return jnp.dot(a, b, preferred_element_type=F32)


def _ada_kernel(c_ref, w_ref, b_ref, o_ref):
    c = c_ref[...]
    s = c * (1.0 / (1.0 + jnp.exp(-c)))
    o_ref[0] = jnp.dot(s, w_ref[0], preferred_element_type=F32,
                       precision=lax.Precision.HIGHEST) + b_ref[0]


def _modulation(cond, w_ada, b_ada):
    tn = 1536
    n = N_MOD * D_MODEL
    return pl.pallas_call(
        _ada_kernel,
        grid=(DEPTH, n // tn),
        in_specs=[
            pl.BlockSpec((8, D_MODEL), lambda l, j: (0, 0)),
            pl.BlockSpec((1, D_MODEL, tn), lambda l, j: (l, 0, j)),
            pl.BlockSpec((1, 1, tn), lambda l, j: (l, 0, j)),
        ],
        out_specs=pl.BlockSpec((1, 8, tn), lambda l, j: (l, 0, j)),
        out_shape=jax.ShapeDtypeStruct((DEPTH, 8, n), F32),
        compiler_params=_cparams(("arbitrary", "arbitrary")),
        name="modulation",
    )(cond, w_ada, b_ada.reshape(DEPTH, 1, n))


def _rms_modulate(x, g, shift, scale):
    ms = jnp.mean(x * x, axis=-1, keepdims=True)
    return (x * lax.rsqrt(ms + EPS) * g) * (1.0 + scale) + shift


def _rope_cols(v, cos, sin_signed):
    t, w = v.shape
    lane = lax.broadcasted_iota(jnp.int32, (t, LANES), 1)
    first_half = (lane & (HEAD_DIM - 1)) < ROPE_HALF
    cols = []
    for c in range(w // LANES):
        xc = v[:, c * LANES:(c + 1) * LANES]
        partner = jnp.where(first_half,
                            pltpu.roll(xc, LANES - ROPE_HALF, axis=1),
                            pltpu.roll(xc, ROPE_HALF, axis=1))
        cols.append(xc * cos + partner * sin_signed)
    return jnp.concatenate(cols, axis=1) if len(cols) > 1 else cols[0]


def _proj_kernel(x_ref, mod_ref, g_ref, w_ref, cos_ref, sin_ref, *out_refs, rope, emit_kv):
    oa_ref, ob_ref, oc_ref, og_ref = out_refs[:4]
    h = _rms_modulate(x_ref[...], g_ref[...], mod_ref[0, 0:1, :], mod_ref[0, 1:2, :]).astype(BF16)
    if rope:
        cos = cos_ref[...]
        sin = sin_ref[...]

    def seg(start, width):
        return _dot(h, w_ref[:, start:start + width])

    def maybe_rope(v):
        return _rope_cols(v, cos, sin) if rope else v

    oa_ref[:, 0:512] = (maybe_rope(seg(0, 512)) * ATTN_SCALE).astype(BF16)
    ka = maybe_rope(seg(512, 256))
    va = seg(768, 256)
    oa_ref[:, 512:768] = ka.astype(BF16)
    oa_ref[:, 768:1024] = va.astype(BF16)
    ob_ref[:, 0:512] = (seg(A_COLS, 512) * ATTN_SCALE).astype(BF16)
    kb = seg(A_COLS + 512, 512)
    vb = seg(A_COLS + 1024, 512)
    ob_ref[:, 512:1024] = kb.astype(BF16)
    ob_ref[:, 1024:1536] = vb.astype(BF16)
    c0 = A_COLS + B_COLS
    oc_ref[:, 0:512] = (maybe_rope(seg(c0, 512)) * C_QSCALE).astype(BF16)
    kc = maybe_rope(seg(c0 + 512, 512))
    vc = seg(c0 + 1024, 512)
    oc_ref[:, 512:1024] = kc.astype(BF16)
    oc_ref[:, 1024:1536] = vc.astype(BF16)
    g0 = c0 + C_COLS
    for j in range(G_COLS // 512):
        gv = seg(g0 + j * 512, 512)
        og_ref[:, j * 512:(j + 1) * 512] = (1.0 / (1.0 + jnp.exp(-gv))).astype(BF16)
    if emit_kv:
        kva_ref, kvb_ref, kvc_ref = out_refs[4:]
        kva_ref[:, 0:256] = ka
        kva_ref[:, 256:512] = va
        kvb_ref[:, 0:512] = kb
        kvb_ref[:, 512:1024] = vb
        kvc_ref[:, 0:512] = kc
        kvc_ref[:, 512:1024] = vc


def _project(x, mod, mod_row0, tiles_per_row, g, w_ext, cos, sin, *, rope, emit_kv):
    n = x.shape[0]
    nt = n // TM
    pos_tiles = cos.shape[0] // TM
    out_shape = [jax.ShapeDtypeStruct((n, A_COLS), BF16), jax.ShapeDtypeStruct((n, B_COLS), BF16),
                 jax.ShapeDtypeStruct((n, C_COLS), BF16), jax.ShapeDtypeStruct((n, G_COLS), BF16)]
    out_specs = [pl.BlockSpec((TM, A_COLS), lambda i: (i, 0)), pl.BlockSpec((TM, B_COLS), lambda i: (i, 0)),
                 pl.BlockSpec((TM, C_COLS), lambda i: (i, 0)), pl.BlockSpec((TM, G_COLS), lambda i: (i, 0))]
    if emit_kv:
        out_shape += [jax.ShapeDtypeStruct((n, 512), F32), jax.ShapeDtypeStruct((n, 1024), F32),
                      jax.ShapeDtypeStruct((n, 1024), F32)]
        out_specs += [pl.BlockSpec((TM, 512), lambda i: (i, 0)), pl.BlockSpec((TM, 1024), lambda i: (i, 0)),
                      pl.BlockSpec((TM, 1024), lambda i: (i, 0))]
    return pl.pallas_call(
        functools.partial(_proj_kernel, rope=rope, emit_kv=emit_kv),
        grid=(nt,),
        in_specs=[
            pl.BlockSpec((TM, D_MODEL), lambda i: (i, 0)),
            pl.BlockSpec((1, N_MOD, D_MODEL), lambda i: (mod_row0 + i // tiles_per_row, 0, 0)),
            pl.BlockSpec((1, D_MODEL), lambda i: (0, 0)),
            pl.BlockSpec((D_MODEL, P_COLS), lambda i: (0, 0)),
            pl.BlockSpec((TM, LANES), lambda i: (i % pos_tiles, 0)),
            pl.BlockSpec((TM, LANES), lambda i: (i % pos_tiles, 0)),
        ],
        out_specs=out_specs,
        out_shape=out_shape,
        compiler_params=_cparams(("arbitrary",)),
        name="project_rope" if rope else "project",
    )(x, mod, g, w_ext, cos, sin)


def _softmax_pv(scores, values, sink=None):
    m = scores[0].max(axis=-1, keepdims=True)
    for s in scores[1:]:
        m = jnp.maximum(m, s.max(axis=-1, keepdims=True))
    if sink is not None:
        m = jnp.maximum(m, sink)
    l = None
    acc = None
    for s, v in zip(scores, values):
        e = jnp.exp(s - m)
        ls = e.sum(axis=-1, keepdims=True)
        pv = _dot(e.astype(BF16), v)
        l = ls if l is None else l + ls
        acc = pv if acc is None else acc + pv
    if sink is not None:
        l = l + jnp.exp(sink - m)
    return acc * (1.0 / l)


def _sink_column(sink_ref, first_head, rows_per_head, n_heads):
    row = lax.broadcasted_iota(jnp.int32, (rows_per_head * n_heads, 1), 0)
    col = jnp.full((rows_per_head * n_heads, 1), sink_ref[first_head], F32)
    for j in range(1, n_heads):
        col = jnp.where(row >= j * rows_per_head, sink_ref[first_head + j], col)
    return col


def _diff_lambda(lq_ref, lambda_init):
    lf = lq_ref[...]
    a = jnp.sum(lf[0:1] * lf[1:2], axis=-1, keepdims=True)
    b = jnp.sum(lf[2:3] * lf[3:4], axis=-1, keepdims=True)
    return jnp.exp(a) - jnp.exp(b) + lambda_init


def _subln(o, g, lambda_init):
    ms = jnp.mean(o * o, axis=-1, keepdims=True)
    return (o * lax.rsqrt(ms + EPS) * g) * (1.0 - lambda_init)


def _ctx_attn_kernel(sink_ref, lq_ref, sg_ref, a_ref, b_ref, c_ref, oa_ref, ob_ref, oc_ref, *, lambda_init):
    t = a_ref.shape[0]
    for g in range(A_KV_HEADS):
        q = a_ref[:, g * 256:(g + 1) * 256]
        lhs = jnp.concatenate([_split_heads(q[:, :LANES]), _split_heads(q[:, LANES:])], axis=0)
        k = a_ref[:, 512 + g * LANES:512 + (g + 1) * LANES]
        v = a_ref[:, 768 + g * LANES:768 + (g + 1) * LANES]
        sink = _sink_column(sink_ref, 4 * g, t, 4)
        o = _softmax_pv([_dot_nt(lhs, k)], [v], sink)
        oa_ref[:, g * 256:g * 256 + LANES] = _merge_heads(o[:2 * t], t).astype(BF16)
        oa_ref[:, g * 256 + LANES:(g + 1) * 256] = _merge_heads(o[2 * t:], t).astype(BF16)
    for c in range(B_HEADS // 2):
        lhs = _split_heads(b_ref[:, c * LANES:(c + 1) * LANES])
        k = b_ref[:, 512 + c * LANES:512 + (c + 1) * LANES]
        v = b_ref[:, 1024 + c * LANES:1024 + (c + 1) * LANES]
        o = _softmax_pv([_dot_nt(lhs, k)], [v])
        ob_ref[:, c * LANES:(c + 1) * LANES] = _merge_heads(o, t).astype(BF16)
    lam = _diff_lambda(lq_ref, lambda_init)
    for h in range(C_HEADS):
        lhs = _split_heads(c_ref[:, h * LANES:(h + 1) * LANES])
        k = c_ref[:, 512 + h * LANES:512 + (h + 1) * LANES]
        v = c_ref[:, 1024 + h * LANES:1024 + (h + 1) * LANES]
        s = _dot_nt(lhs, k)
        m = s.max(axis=-1, keepdims=True)
        e = jnp.exp2(s - m)
        p = e * (1.0 / e.sum(axis=-1, keepdims=True))
        a = (p[:t] - lam * p[t:]).astype(BF16)
        oc_ref[:, h * LANES:(h + 1) * LANES] = _subln(_dot(a, v), sg_ref[...], lambda_init).astype(BF16)


def _ctx_attention(pa, pb, pc, sink, lq, sg, seq, lambda_init):
    n = pa.shape[0]
    smem = pl.BlockSpec(memory_space=pltpu.SMEM)
    return pl.pallas_call(
        functools.partial(_ctx_attn_kernel, lambda_init=lambda_init),
        grid=(n // seq,),
        in_specs=[
            smem,
            pl.BlockSpec((4, HEAD_DIM), lambda b: (0, 0)),
            pl.BlockSpec((1, C_VDIM), lambda b: (0, 0)),
            pl.BlockSpec((seq, A_COLS), lambda b: (b, 0)),
            pl.BlockSpec((seq, B_COLS), lambda b: (b, 0)),
            pl.BlockSpec((seq, C_COLS), lambda b: (b, 0)),
        ],
        out_specs=[pl.BlockSpec((seq, BRANCH_W), lambda b: (b, 0))] * 3,
        out_shape=[jax.ShapeDtypeStruct((n, BRANCH_W), BF16)] * 3,
        compiler_params=_cparams(("arbitrary",)),
        name="context_attention",
    )(sink, lq, sg, pa, pb, pc)


def _win_attn_kernel(sink_ref, q_ref, k_ref, v_ref, kc_ref, vc_ref, o_ref, *, n_lat):
    qi = pl.program_id(1)
    ws = pl.multiple_of(jnp.clip(qi * TQ - WINDOW, 0, n_lat - WIN_KEYS), WINDOW)
    qpos = qi * TQ + lax.broadcasted_iota(jnp.int32, (TQ, WIN_KEYS), 0)
    kpos = ws + lax.broadcasted_iota(jnp.int32, (TQ, WIN_KEYS), 1)
    band = jnp.where(jnp.abs(kpos - qpos) <= WINDOW, 0.0, NEG_INF)
    band = jnp.concatenate([band] * 4, axis=0)
    for g in range(A_KV_HEADS):
        q = q_ref[:, g * 256:(g + 1) * 256]
        lhs = jnp.concatenate([_split_heads(q[:, :LANES]), _split_heads(q[:, LANES:])], axis=0)
        cols = slice(g * LANES, (g + 1) * LANES)
        s_loc = _dot_nt(lhs, k_ref[pl.ds(ws, WIN_KEYS), cols]) + band
        s_ctx = _dot_nt(lhs, kc_ref[0, :, cols])
        sink = _sink_column(sink_ref, 4 * g, TQ, 4)
        o = _softmax_pv([s_loc, s_ctx], [v_ref[pl.ds(ws, WIN_KEYS), cols], vc_ref[0, :, cols]], sink)
        o_ref[:, g * 256:g * 256 + LANES] = _merge_heads(o[:2 * TQ], TQ).astype(BF16)
        o_ref[:, g * 256 + LANES:(g + 1) * 256] = _merge_heads(o[2 * TQ:], TQ).astype(BF16)


def _win_attention(pa, kctx, vctx, sink, n_batch, n_lat):
    nq = n_lat // TQ
    past = kctx.shape[1]
    return pl.pallas_call(
        functools.partial(_win_attn_kernel, n_lat=n_lat),
        grid=(n_batch, nq),
        in_specs=[
            pl.BlockSpec(memory_space=pltpu.SMEM),
            pl.BlockSpec((TQ, 512), lambda b, i: (b * nq + i, 0)),
            pl.BlockSpec((n_lat, 256), lambda b, i: (b, 2)),
            pl.BlockSpec((n_lat, 256), lambda b, i: (b, 3)),
            pl.BlockSpec((1, past, 256), lambda b, i: (b, 0, 0)),
            pl.BlockSpec((1, past, 256), lambda b, i: (b, 0, 0)),
        ],
        out_specs=pl.BlockSpec((TQ, BRANCH_W), lambda b, i: (b * nq + i, 0)),
        out_shape=jax.ShapeDtypeStruct((n_batch * n_lat, BRANCH_W), BF16),
        compiler_params=_cparams(("arbitrary", "arbitrary")),
        name="window_attention",
    )(sink, pa, pa, pa, kctx, vctx)


def _na_window_start(qi, rows):
    return jnp.clip(qi * NA_ROWS_Q - NA_KH // 2, 0, rows - NA_ROWS_K)


def _na_attn_kernel(q_ref, k_ref, v_ref, kc_ref, vc_ref, bias_ref, o_ref, *, rows):
    qi = pl.program_id(2)
    ws = pl.multiple_of(_na_window_start(qi, rows) * GRID_W, GRID_W)
    for c in range(NA_COLS_PER_STEP):
        cols = slice(c * LANES, (c + 1) * LANES)
        lhs = _split_heads(q_ref[:, cols])
        s_loc = _dot_nt(lhs, k_ref[pl.ds(ws, NA_KEYS), cols]) + bias_ref[0, 2 * c:2 * c + 2].reshape(2 * TQ, NA_KEYS)
        s_ctx = _dot_nt(lhs, kc_ref[0, :, cols])
        o = _softmax_pv([s_loc, s_ctx], [v_ref[pl.ds(ws, NA_KEYS), cols], vc_ref[0, :, cols]])
        o_ref[:, cols] = _merge_heads(o, TQ).astype(BF16)


def _na_bias_kernel(rpb_ref, o_ref, *, layer, rows):
    h = pl.program_id(0)
    n_dr = 2 * NA_KH - 1
    n_dc = 2 * NA_KW - 1
    base = (layer * B_HEADS + h) * n_dr * n_dc
    lane = lax.broadcasted_iota(jnp.int32, (GRID_W, LANES), 1)
    qc = lax.broadcasted_iota(jnp.int32, (GRID_W, LANES), 0)
    kc = lane & (GRID_W - 1)
    second = lane >= GRID_W
    cs = jnp.clip(qc - NA_KW // 2, 0, GRID_W - NA_KW)
    col_ok = (kc >= cs) & (kc < cs + NA_KW)
    dc_map = kc - qc + (NA_KW - 1)
    cache = {}

    def pair_tile(dr0, ok0, ok1):
        key = (dr0 if ok0 else None, dr0 + 1 if ok1 else None)
        if key not in cache:
            if not (ok0 or ok1):
                cache[key] = jnp.full((GRID_W, LANES), NEG_INF, F32)
            else:
                acc = jnp.zeros((GRID_W, LANES), F32)
                for dc in range(n_dc):
                    v0 = rpb_ref[base + dr0 * n_dc + dc] if ok0 else 0.0
                    v1 = rpb_ref[base + (dr0 + 1) * n_dc + dc] if ok1 else 0.0
                    acc = jnp.where(dc_map == dc, jnp.where(second, v1, v0), acc)
                if ok0 and ok1:
                    ok = col_ok
                elif ok0:
                    ok = col_ok & jnp.logical_not(second)
                else:
                    ok = col_ok & second
                cache[key] = jnp.where(ok, acc, NEG_INF)
        return cache[key]

    nq = rows // NA_ROWS_Q
    for p, qi in enumerate((0, 1, nq - 1)):
        r0 = qi * NA_ROWS_Q
        ws = min(max(r0 - NA_KH // 2, 0), rows - NA_ROWS_K)
        for i in range(NA_ROWS_Q):
            r = r0 + i
            lo = min(max(r - NA_KH // 2, 0), rows - NA_KH)
            for jp in range(NA_ROWS_K // 2):
                kr = ws + 2 * jp
                ok0 = lo <= kr < lo + NA_KH
                ok1 = lo <= kr + 1 < lo + NA_KH
                o_ref[p, 0, i * GRID_W:(i + 1) * GRID_W, jp * LANES:(jp + 1) * LANES] = pair_tile(
                    kr - r + NA_KH - 1, ok0, ok1)


def _na_bias_table(rpb_flat, layer, rows):
    return pl.pallas_call(
        functools.partial(_na_bias_kernel, layer=layer, rows=rows),
        grid=(B_HEADS,),
        in_specs=[pl.BlockSpec(memory_space=pltpu.SMEM)],
        out_specs=pl.BlockSpec((3, 1, TQ, NA_KEYS), lambda h: (0, h, 0, 0)),
        out_shape=jax.ShapeDtypeStruct((3, B_HEADS, TQ, NA_KEYS), F32),
        compiler_params=_cparams(("arbitrary",)),
        name="neighborhood_bias",
    )(rpb_flat)


def _na_attention(pb, kctx, vctx, bias, n_batch, n_lat):
    nq = n_lat // TQ
    rows = n_lat // GRID_W
    past = kctx.shape[1]
    w = NA_COLS_PER_STEP * LANES
    n_cb = BRANCH_W // w

    def bias_idx(b, c, i):
        return (jnp.where(i == 0, 0, jnp.where(i == nq - 1, 2, 1)), c, 0, 0)

    return pl.pallas_call(
        functools.partial(_na_attn_kernel, rows=rows),
        grid=(n_batch, n_cb, nq),
        in_specs=[
            pl.BlockSpec((TQ, w), lambda b, c, i: (b * nq + i, c)),
            pl.BlockSpec((n_lat, w), lambda b, c, i: (b, n_cb + c)),
            pl.BlockSpec((n_lat, w), lambda b, c, i: (b, 2 * n_cb + c)),
            pl.BlockSpec((1, past, w), lambda b, c, i: (b, 0, c)),
            pl.BlockSpec((1, past, w), lambda b, c, i: (b, 0, c)),
            pl.BlockSpec((1, 2 * NA_COLS_PER_STEP, TQ, NA_KEYS), bias_idx),
        ],
        out_specs=pl.BlockSpec((TQ, w), lambda b, c, i: (b * nq + i, c)),
        out_shape=jax.ShapeDtypeStruct((n_batch * n_lat, BRANCH_W), BF16),
        compiler_params=_cparams(("arbitrary", "arbitrary", "arbitrary")),
        name="neighborhood_attention",
    )(pb, pb, pb, kctx, vctx, bias)


def _diff_attn_kernel(lq_ref, sg_ref, q_ref, k_ref, v_ref, kc_ref, vc_ref, o_ref, vt_s, vtc_s, acc_s, *,
                      n_lat, lambda_init):
    n_chunks = n_lat // C_CHUNK

    @pl.when(pl.program_id(2) == 0)
    def _():
        for j in range(n_chunks):
            vt_s[j] = v_ref[j * C_CHUNK:(j + 1) * C_CHUNK, :].astype(F32).T.astype(BF16)
        vtc_s[...] = vc_ref[0].astype(F32).T.astype(BF16)

    lhs = _split_heads(q_ref[...])
    acc_s[...] = jnp.zeros_like(acc_s)

    def step(k, vt, m, l):
        st = _dot_nt(k, lhs)
        m_new = jnp.maximum(m, st.max(axis=0, keepdims=True))
        alpha = jnp.exp2(m - m_new)
        e = jnp.exp2(st - m_new)
        acc_s[...] = alpha * acc_s[...] + _dot(vt, e.astype(BF16))
        return m_new, alpha * l + e.sum(axis=0, keepdims=True)

    m, l = jnp.full((1, 2 * TQ), NEG_INF, F32), jnp.zeros((1, 2 * TQ), F32)
    for j in range(n_chunks):
        m, l = step(k_ref[j * C_CHUNK:(j + 1) * C_CHUNK, :], vt_s[j], m, l)
    m, l = step(kc_ref[0], vtc_s[...], m, l)
    o = acc_s[...] * (1.0 / l)
    d = o[:, :TQ] - _diff_lambda(lq_ref, lambda_init) * o[:, TQ:]
    ms = jnp.mean(d * d, axis=0, keepdims=True)
    y = (d * lax.rsqrt(ms + EPS)).T
    o_ref[...] = (y * sg_ref[...] * (1.0 - lambda_init)).astype(BF16)


def _diff_attention(pc, kctx, vctx, lq, sg, n_batch, n_lat, lambda_init):
    nq = n_lat // TQ
    past = kctx.shape[1]
    return pl.pallas_call(
        functools.partial(_diff_attn_kernel, n_lat=n_lat, lambda_init=lambda_init),
        grid=(n_batch, C_HEADS, nq),
        in_specs=[
            pl.BlockSpec((4, HEAD_DIM), lambda b, h, i: (0, 0)),
            pl.BlockSpec((1, C_VDIM), lambda b, h, i: (0, 0)),
            pl.BlockSpec((TQ, LANES), lambda b, h, i: (b * nq + i, h)),
            pl.BlockSpec((n_lat, LANES), lambda b, h, i: (b, 4 + h)),
            pl.BlockSpec((n_lat, LANES), lambda b, h, i: (b, 8 + h)),
            pl.BlockSpec((1, past, LANES), lambda b, h, i: (b, 0, h)),
            pl.BlockSpec((1, past, LANES), lambda b, h, i: (b, 0, h)),
        ],
        out_specs=pl.BlockSpec((TQ, LANES), lambda b, h, i: (b * nq + i, h)),
        out_shape=jax.ShapeDtypeStruct((n_batch * n_lat, BRANCH_W), BF16),
        scratch_shapes=[pltpu.VMEM((n_lat // C_CHUNK, C_VDIM, C_CHUNK), BF16), pltpu.VMEM((C_VDIM, past), BF16),
                        pltpu.VMEM((C_VDIM, 2 * TQ), F32)],
        compiler_params=_cparams(("arbitrary", "arbitrary", "arbitrary")),
        name="differential_attention",
    )(lq, sg, pc, pc, pc, kctx, vctx)


def _merge_kernel(x_ref, mod_ref, g_ref, oa_ref, ob_ref, oc_ref, sg_ref, wb_ref, wo_ref, wr_ref, br_ref,
                  xo_ref, h_ref, idx_ref, gate_ref):
    y = None
    for j, o_ref in enumerate((oa_ref, ob_ref, oc_ref)):
        t = sg_ref[:, j * D_MODEL:(j + 1) * D_MODEL].astype(F32) * _dot(o_ref[...], wb_ref[j])
        y = t if y is None else y + t
    x = x_ref[...] + mod_ref[0, 2:3, :] * _dot(y.astype(BF16), wo_ref[...])
    xo_ref[...] = x
    h = _rms_modulate(x, g_ref[...], mod_ref[0, 3:4, :], mod_ref[0, 4:5, :])
    h_ref[...] = h
    wr = wr_ref[...]
    h_hi = h.astype(BF16)
    h_lo = (h - h_hi.astype(F32)).astype(BF16)
    w_hi = wr.astype(BF16)
    w_lo = (wr - w_hi.astype(F32)).astype(BF16)
    logits = _dot(h_hi, w_hi) + (_dot(h_hi, w_lo) + _dot(h_lo, w_hi)) + br_ref[...]
    tm = logits.shape[0]
    lane_e = lax.broadcasted_iota(jnp.int32, (tm, N_EXPERTS), 1)
    lane_o = lax.broadcasted_iota(jnp.int32, (tm, LANES), 1)
    idx_out = jnp.zeros((tm, LANES), jnp.int32)
    val_out = jnp.zeros((tm, LANES), F32)
    top = None
    denom = None
    for k in range(TOP_K):
        mx = logits.max(axis=-1, keepdims=True)
        sel = jnp.min(jnp.where(logits == mx, lane_e, N_EXPERTS), axis=-1, keepdims=True)
        logits = jnp.where(lane_e == sel, -jnp.inf, logits)
        if top is None:
            top = mx
        e = jnp.exp(mx - top)
        denom = e if denom is None else denom + e
        idx_out = jnp.where(lane_o == k, sel, idx_out)
        val_out = jnp.where(lane_o == k, e, val_out)
    idx_ref[...] = idx_out
    gate_ref[...] = val_out * (1.0 / denom)


def _merge_route(x, mod, mod_row0, tiles_per_row, g_ffn, oa, ob, oc, sg, wb, wo, wr, br):
    n = x.shape[0]
    row = lambda i: (i, 0)
    fixed2 = lambda i: (0, 0)
    return pl.pallas_call(
        _merge_kernel,
        grid=(n // TM,),
        in_specs=[
            pl.BlockSpec((TM, D_MODEL), row),
            pl.BlockSpec((1, N_MOD, D_MODEL), lambda i: (mod_row0 + i // tiles_per_row, 0, 0)),
            pl.BlockSpec((1, D_MODEL), fixed2),
            pl.BlockSpec((TM, BRANCH_W), row),
            pl.BlockSpec((TM, BRANCH_W), row),
            pl.BlockSpec((TM, BRANCH_W), row),
            pl.BlockSpec((TM, G_COLS), row),
            pl.BlockSpec((3, BRANCH_W, D_MODEL), lambda i: (0, 0, 0)),
            pl.BlockSpec((D_MODEL, D_MODEL), fixed2),
            pl.BlockSpec((D_MODEL, N_EXPERTS), fixed2),
            pl.BlockSpec((1, N_EXPERTS), fixed2),
        ],
        out_specs=[pl.BlockSpec((TM, D_MODEL), row), pl.BlockSpec((TM, D_MODEL), row),
                   pl.BlockSpec((TM, LANES), row), pl.BlockSpec((TM, LANES), row)],
        out_shape=[jax.ShapeDtypeStruct((n, D_MODEL), F32), jax.ShapeDtypeStruct((n, D_MODEL), F32),
                   jax.ShapeDtypeStruct((n, LANES), jnp.int32), jax.ShapeDtypeStruct((n, LANES), F32)],
        compiler_params=_cparams(("arbitrary",)),
        name="merge_route",
    )(x, mod, g_ffn, oa, ob, oc, sg, wb, wo, wr, br)


def _moe_kernel(be_ref, first_ref, valid_ref, x_ref, wgu_ref, bgu_ref, wd_ref, bd_ref, o_ref, wgu_s, wd_s):
    i = pl.program_id(0)

    @pl.when(first_ref[i] == 1)
    def _():
        wgu_s[...] = wgu_ref[0, 0].astype(BF16)
        wd_s[...] = wd_ref[0, 0].astype(BF16)

    @pl.when(valid_ref[i] == 1)
    def _():
        gu = _dot(x_ref[...].astype(BF16), wgu_s[...]) + bgu_ref[0, 0]
        gate = jnp.minimum(gu[:, :D_FF], SWIGLU_LIMIT)
        lin = jnp.clip(gu[:, D_FF:], -SWIGLU_LIMIT, SWIGLU_LIMIT)
        act = gate * (1.0 / (1.0 + jnp.exp(-SWIGLU_ALPHA * gate))) * (lin + 1.0)
        o_ref[...] = _dot(act.astype(BF16), wd_s[...]) + bd_ref[0, 0]

    @pl.when(valid_ref[i] == 0)
    def _():
        o_ref[...] = jnp.zeros_like(o_ref)


def _moe_experts(xb, block_e, block_first, block_valid, layer, w_gu, b_gu, w_down, b_down):
    m = xb.shape[0]
    nb = m // MOE_TM
    grid_spec = pltpu.PrefetchScalarGridSpec(
        num_scalar_prefetch=3,
        grid=(nb,),
        in_specs=[
            pl.BlockSpec((MOE_TM, D_MODEL), lambda i, be, bf, bv: (i, 0)),
            pl.BlockSpec((1, 1, D_MODEL, 2 * D_FF), lambda i, be, bf, bv: (layer, be[i], 0, 0)),
            pl.BlockSpec((1, 1, 1, 2 * D_FF), lambda i, be, bf, bv: (layer, be[i], 0, 0)),
            pl.BlockSpec((1, 1, D_FF, D_MODEL), lambda i, be, bf, bv: (layer, be[i], 0, 0)),
            pl.BlockSpec((1, 1, 1, D_MODEL), lambda i, be, bf, bv: (layer, be[i], 0, 0)),
        ],
        out_specs=pl.BlockSpec((MOE_TM, D_MODEL), lambda i, be, bf, bv: (i, 0)),
        scratch_shapes=[pltpu.VMEM((D_MODEL, 2 * D_FF), BF16), pltpu.VMEM((D_FF, D_MODEL), BF16)],
    )
    return pl.pallas_call(
        _moe_kernel,
        grid_spec=grid_spec,
        out_shape=jax.ShapeDtypeStruct((m, D_MODEL), F32),
        compiler_params=_cparams(("arbitrary",)),
        name="moe_experts",
    )(block_e, block_first, block_valid, xb, w_gu, b_gu.reshape(DEPTH, N_EXPERTS, 1, 2 * D_FF),
      w_down, b_down.reshape(DEPTH, N_EXPERTS, 1, D_MODEL))


def _moe_block_tables(counts, nb):
    padded = (counts + MOE_TM - 1) // MOE_TM * MOE_TM
    pad_end = jnp.cumsum(padded).astype(jnp.int32)
    blk = jnp.arange(nb, dtype=jnp.int32) * MOE_TM
    block_e = jnp.minimum(jnp.sum((blk[:, None] >= pad_end[None, :]).astype(jnp.int32), axis=1), N_EXPERTS - 1)
    block_valid = (blk < pad_end[-1]).astype(jnp.int32)
    prev = jnp.concatenate([jnp.full((1,), -1, jnp.int32), block_e[:-1]])
    block_first = (block_e != prev).astype(jnp.int32)
    return pad_end, block_e, block_first, block_valid


def _route_kernel(idx_ref, dest_ref, cnt_ref, run_s, start_s):
    p = pl.program_id(0)
    i = pl.program_id(1)
    tm = idx_ref.shape[0]
    lane = lax.broadcasted_iota(jnp.int32, (tm, LANES), 1)
    idx = idx_ref[...]
    sel = [idx[:, k:k + 1] for k in range(TOP_K)]
    onehot = jnp.zeros((tm, LANES), F32)
    for k in range(TOP_K):
        onehot = onehot + jnp.where(lane == sel[k], 1.0, 0.0)
    tile_cnt = jnp.sum(onehot, axis=0, keepdims=True)

    @pl.when((p == 0) & (i == 0))
    def _():
        run_s[...] = jnp.zeros_like(run_s)
        cnt_ref[...] = jnp.zeros_like(cnt_ref)

    @pl.when(p == 0)
    def _():
        run_s[...] += tile_cnt

    @pl.when((p == 1) & (i == 0))
    def _():
        cnt = run_s[...]
        padded = ((cnt.astype(jnp.int32) + (MOE_TM - 1)) // MOE_TM * MOE_TM).astype(F32)
        r = lax.broadcasted_iota(jnp.int32, (LANES, LANES), 0)
        c = lax.broadcasted_iota(jnp.int32, (LANES, LANES), 1)
        before = jnp.where(r < c, 1.0, 0.0)
        start = jnp.dot(jnp.broadcast_to(padded, (8, LANES)), before, preferred_element_type=F32,
                        precision=lax.Precision.HIGHEST)
        start_s[...] = start[0:1]
        cnt_ref[...] = jnp.broadcast_to(cnt, cnt_ref.shape)
        run_s[...] = jnp.zeros_like(run_s)

    @pl.when(p == 1)
    def _():
        r = lax.broadcasted_iota(jnp.int32, (tm, tm), 0)
        c = lax.broadcasted_iota(jnp.int32, (tm, tm), 1)
        earlier = jnp.where(c < r, 1.0, 0.0).astype(BF16)
        pos = _dot(earlier, onehot.astype(BF16)) + run_s[...] + start_s[...]
        out = jnp.zeros((tm, LANES), F32)
        for k in range(TOP_K):
            d = jnp.sum(jnp.where(lane == sel[k], pos, 0.0), axis=-1, keepdims=True)
            out = jnp.where(lane == k, d, out)
        dest_ref[...] = out.astype(jnp.int32)
        run_s[...] += tile_cnt


def _route(idx_slab):
    n = idx_slab.shape[0]
    return pl.pallas_call(
        _route_kernel,
        grid=(2, n // ROUTE_TM),
        in_specs=[pl.BlockSpec((ROUTE_TM, LANES), lambda p, i: (i, 0))],
        out_specs=[pl.BlockSpec((ROUTE_TM, LANES), lambda p, i: (i * p, 0)),
                   pl.BlockSpec((8, LANES), lambda p, i: (0, 0))],
        out_shape=[jax.ShapeDtypeStruct((n, LANES), jnp.int32), jax.ShapeDtypeStruct((8, LANES), F32)],
        scratch_shapes=[pltpu.VMEM((1, LANES), F32), pltpu.VMEM((1, LANES), F32)],
        compiler_params=_cparams(("arbitrary", "arbitrary")),
        name="route",
    )(idx_slab)


def _row_copy(src_ref, src_row, dst_ref, dst_row, sem):
    return pltpu.make_async_copy(src_ref.at[pl.ds(src_row, 1)], dst_ref.at[pl.ds(dst_row, 1)], sem)


def _dispatch_kernel(pe_ref, cnt_ref, dest_ref, hp_ref, hs_ref, xb_ref, zero_s, sem, *, tiles_p, first_tail_block):
    i = pl.program_id(0)
    tm = hp_ref.shape[0]
    nb = xb_ref.shape[0] // MOE_TM

    def zero_block(row0):
        return pltpu.make_async_copy(zero_s, xb_ref.at[pl.ds(pl.multiple_of(row0, MOE_TM), MOE_TM)], sem)

    @pl.when(i == 0)
    def _():
        zero_s[...] = jnp.zeros_like(zero_s)
        for start in (True, False):
            for e in range(N_EXPERTS):
                @pl.when(cnt_ref[e] > 0)
                def _():
                    cp = zero_block(pe_ref[e] - MOE_TM)
                    cp.start() if start else cp.wait()
            for b in range(first_tail_block, nb):
                @pl.when(b * MOE_TM >= pe_ref[N_EXPERTS - 1])
                def _():
                    cp = zero_block(b * MOE_TM)
                    cp.start() if start else cp.wait()

    def scatter_rows(h_ref):
        def issue(t, carry):
            for k in range(TOP_K):
                _row_copy(h_ref, t, xb_ref, dest_ref[0, 0, t * TOP_K + k], sem).start(priority=k % 2)
            return carry

        def drain(t, carry):
            for k in range(TOP_K):
                _row_copy(h_ref, 0, xb_ref, 0, sem).wait()
            return carry

        lax.fori_loop(0, tm, issue, 0)
        lax.fori_loop(0, tm, drain, 0)

    @pl.when(i < tiles_p)
    def _():
        scatter_rows(hp_ref)

    @pl.when(i >= tiles_p)
    def _():
        scatter_rows(hs_ref)


def _dispatch(pad_end, counts, dest, hp, hs, m):
    tiles_p = hp.shape[0] // TM
    tiles_s = hs.shape[0] // TM
    grid_spec = pltpu.PrefetchScalarGridSpec(
        num_scalar_prefetch=2,
        grid=(tiles_p + tiles_s,),
        in_specs=[
            pl.BlockSpec((1, 1, TM * TOP_K), lambda i, pe, cn: (i, 0, 0), memory_space=pltpu.SMEM),
            pl.BlockSpec((TM, D_MODEL), lambda i, pe, cn: (jnp.minimum(i, tiles_p - 1), 0)),
            pl.BlockSpec((TM, D_MODEL), lambda i, pe, cn: (jnp.maximum(i - tiles_p, 0), 0)),
        ],
        out_specs=pl.BlockSpec(memory_space=pl.ANY),
        scratch_shapes=[pltpu.VMEM((MOE_TM, D_MODEL), F32), pltpu.SemaphoreType.DMA(())],
    )
    return pl.pallas_call(
        functools.partial(_dispatch_kernel, tiles_p=tiles_p, first_tail_block=m // MOE_TM - N_EXPERTS),
        grid_spec=grid_spec,
        out_shape=jax.ShapeDtypeStruct((m, D_MODEL), F32),
        compiler_params=_cparams(("arbitrary",)),
        name="dispatch",
    )(pad_end, counts, dest, hp, hs)


def _combine_kernel(dest_ref, x_ref, mod_ref, gate_ref, gfin_ref, yb_ref, o_ref, buf, sem, *, final):
    tm = x_ref.shape[0]

    def issue(t, carry):
        for k in range(TOP_K):
            _row_copy(yb_ref, dest_ref[0, 0, t * TOP_K + k], buf.at[k], t, sem).start(priority=k % 2)
        return carry

    def drain(t, carry):
        for k in range(TOP_K):
            _row_copy(yb_ref, 0, buf.at[k], 0, sem).wait()
        return carry

    lax.fori_loop(0, tm, issue, 0)
    lax.fori_loop(0, tm, drain, 0)
    g = gate_ref[...]
    acc = g[:, 0:1] * buf[0]
    for k in range(1, TOP_K):
        acc = acc + g[:, k:k + 1] * buf[k]
    x = x_ref[...] + mod_ref[0, 5:6, :] * acc
    if final:
        ms = jnp.mean(x * x, axis=-1, keepdims=True)
        x = x * lax.rsqrt(ms + EPS) * gfin_ref[...]
    o_ref[...] = x


def _combine(dest, x, mod, mod_row0, tiles_per_row, gates, g_final, yb, *, final):
    n = x.shape[0]
    return pl.pallas_call(
        functools.partial(_combine_kernel, final=final),
        grid=(n // TM,),
        in_specs=[
            pl.BlockSpec((1, 1, TM * TOP_K), lambda i: (i, 0, 0), memory_space=pltpu.SMEM),
            pl.BlockSpec((TM, D_MODEL), lambda i: (i, 0)),
            pl.BlockSpec((1, N_MOD, D_MODEL), lambda i: (mod_row0 + i // tiles_per_row, 0, 0)),
            pl.BlockSpec((TM, LANES), lambda i: (i, 0)),
            pl.BlockSpec((1, D_MODEL), lambda i: (0, 0)),
            pl.BlockSpec(memory_space=pl.ANY),
        ],
        out_specs=pl.BlockSpec((TM, D_MODEL), lambda i: (i, 0)),
        out_shape=jax.ShapeDtypeStruct((n, D_MODEL), F32),
        scratch_shapes=[pltpu.VMEM((TOP_K, TM, D_MODEL), F32), pltpu.SemaphoreType.DMA(())],
        compiler_params=_cparams(("arbitrary",)),
        name="combine_final" if final else "combine",
    )(dest, x, mod, gates, g_final, yb)


def _rope_tables(n_tokens):
    t = np.arange(n_tokens)
    row = (t // GRID_W).astype(np.float32)
    col = (t % GRID_W).astype(np.float32)
    inv = jnp.asarray(ROPE_THETA, F32) ** (-jnp.arange(ROPE_AXIS_PAIRS, dtype=F32) / ROPE_AXIS_PAIRS)
    ang = jnp.concatenate([jnp.asarray(row)[:, None] * inv, jnp.asarray(col)[:, None] * inv], axis=-1)
    cos, sin = jnp.cos(ang), jnp.sin(ang)
    cos = jnp.tile(cos, (1, LANES // ROPE_HALF))
    sin = jnp.tile(jnp.concatenate([-sin, sin], axis=-1), (1, LANES // HEAD_DIM))
    return cos, sin


def _extended_w_in(w):
    qa, ka, va, rest = w[:, :512], w[:, 512:640], w[:, 640:768], w[:, 768:]

    def dup(t):
        t = t.reshape(D_MODEL, A_KV_HEADS, 1, HEAD_DIM)
        return jnp.broadcast_to(t, (D_MODEL, A_KV_HEADS, 2, HEAD_DIM)).reshape(D_MODEL, 4 * HEAD_DIM)

    return jnp.concatenate([qa, dup(ka), dup(va), rest], axis=-1).astype(BF16)


def _dup_kv_heads(t):
    b, p = t.shape[:2]
    t = jnp.broadcast_to(t[:, :, :, None, :], (b, p, A_KV_HEADS, 2, HEAD_DIM))
    return t.reshape(b, p, 4 * HEAD_DIM).astype(BF16)


def kernel(x_prompt, x_sample, cache_k_a, cache_v_a, cache_k_b, cache_v_b, cache_k_c, cache_v_c, c, c_ctx,
           w_ada, b_ada, norm_attn, norm_ffn, w_in, sinks, rpb, lambda_qk, subln, w_branch, w_out,
           w_router, b_router, w_gu, b_gu, w_down, b_down, norm_final):
    bsz, seq = x_prompt.shape[:2]
    bsz_d, n_lat = x_sample.shape[:2]
    past = cache_k_a.shape[2]
    n_p = bsz * seq
    n_s = bsz_d * n_lat
    assert seq % TM == 0 and n_lat % TQ == 0 and n_lat % C_CHUNK == 0 and bsz_d + 1 <= 8
    assert (n_p + n_s) % ROUTE_TM == 0

    cond = jnp.zeros((8, D_MODEL), F32).at[0].set(c_ctx).at[1:1 + bsz_d].set(c)
    mod_all = _modulation(cond, w_ada, b_ada).reshape(DEPTH, 8, N_MOD, D_MODEL)
    cos, sin = _rope_tables(n_lat)
    g_final = norm_final[None, :]
    n_assign = (n_p + n_s) * TOP_K
    m_rows = (n_assign + N_EXPERTS * (MOE_TM - 1) + MOE_TM - 1) // MOE_TM * MOE_TM

    xp = x_prompt.reshape(n_p, D_MODEL)
    xs = x_sample.reshape(n_s, D_MODEL)
    kv_out = []
    for l in range(DEPTH):
        lambda_init = 0.8 - 0.6 * math.exp(-0.3 * l)
        mod = mod_all[l]
        w_ext = _extended_w_in(w_in[l])
        g_attn = norm_attn[l][None, :]
        g_ffn = norm_ffn[l][None, :]
        sg = subln[l][None, :]
        wb = w_branch[l].astype(BF16)
        wo = w_out[l].astype(BF16)
        br = b_router[l][None, :]

        pa, pb, pc, pg, kva, kvb, kvc = _project(xp, mod, 0, n_p // TM, g_attn, w_ext, cos, sin,
                                                 rope=False, emit_kv=True)
        oa, ob, oc = _ctx_attention(pa, pb, pc, sinks[l], lambda_qk[l], sg, seq, lambda_init)
        xp_mid, hp, idx_p, gate_p = _merge_route(xp, mod, 0, n_p // TM, g_ffn, oa, ob, oc, pg, wb, wo,
                                                 w_router[l], br)
        kv_out.append((kva, kvb, kvc))

        pa, pb, pc, pg = _project(xs, mod, 1, n_lat // TM, g_attn, w_ext, cos, sin, rope=True, emit_kv=False)
        oa = _win_attention(pa, _dup_kv_heads(cache_k_a[:, l]), _dup_kv_heads(cache_v_a[:, l]), sinks[l],
                            bsz_d, n_lat)
        ob = _na_attention(pb, cache_k_b[:, l].reshape(bsz_d, past, -1).astype(BF16),
                           cache_v_b[:, l].reshape(bsz_d, past, -1).astype(BF16),
                           _na_bias_table(rpb.reshape(-1), l, n_lat // GRID_W), bsz_d, n_lat)
        oc = _diff_attention(pc, cache_k_c[:, l].reshape(bsz_d, past, -1).astype(BF16),
                             cache_v_c[:, l].reshape(bsz_d, past, -1).astype(BF16),
                             lambda_qk[l], sg, bsz_d, n_lat, lambda_init)
        xs_mid, hs, idx_s, gate_s = _merge_route(xs, mod, 1, n_lat // TM, g_ffn, oa, ob, oc, pg, wb, wo,
                                                 w_router[l], br)

        last = l == DEPTH - 1
        dest_slab, cnt = _route(jnp.concatenate([idx_p, idx_s], axis=0))
        counts = cnt[0, :N_EXPERTS].astype(jnp.int32)
        pad_end, block_e, block_first, block_valid = _moe_block_tables(counts, m_rows // MOE_TM)
        dest = dest_slab[:, :TOP_K].reshape(-1, 1, TM * TOP_K)
        dest_p, dest_s = dest[:n_p // TM], dest[n_p // TM:]
        xb = _dispatch(pad_end, counts, dest, hp, hs, m_rows)
        yb = _moe_experts(xb, block_e, block_first, block_valid, l, w_gu, b_gu, w_down, b_down)
        xp = _combine(dest_p, xp_mid, mod, 0, n_p // TM, gate_p, g_final, yb, final=last)
        xs = _combine(dest_s, xs_mid, mod, 1, n_lat // TM, gate_s, g_final, yb, final=last)

    y_prompt = xp.reshape(bsz, seq, D_MODEL)
    y_sample = xs.reshape(bsz_d, n_lat, D_MODEL)

    def stack(pick):
        return jnp.stack([pick(kv) for kv in kv_out], axis=1)

    def heads_a(t):
        return t.reshape(bsz, seq, A_KV_HEADS, 2, HEAD_DIM)[:, :, :, 0]

    new_k_a = stack(lambda kv: heads_a(kv[0][:, 0:256]))
    new_v_a = stack(lambda kv: heads_a(kv[0][:, 256:512]))
    new_k_b = stack(lambda kv: kv[1][:, 0:512].reshape(bsz, seq, B_HEADS, HEAD_DIM))
    new_v_b = stack(lambda kv: kv[1][:, 512:1024].reshape(bsz, seq, B_HEADS, HEAD_DIM))
    new_k_c = stack(lambda kv: kv[2][:, 0:512].reshape(bsz, seq, C_HEADS, 2 * HEAD_DIM))
    new_v_c = stack(lambda kv: kv[2][:, 512:1024].reshape(bsz, seq, C_HEADS, C_VDIM))
    return (y_prompt, y_sample, new_k_a, new_v_a, new_k_b, new_v_b, new_k_c, new_v_c)
```

```python
import functools
import math

import jax
import jax.numpy as jnp
import numpy as np
from jax import lax
from jax.experimental import pallas as pl
from jax.experimental.pallas import tpu as pltpu

F32 = jnp.float32
BF16 = jnp.bfloat16

D_MODEL = 1024
DEPTH = 2
GRID_W = 64
HEAD_DIM = 64
ROPE_HALF = HEAD_DIM // 2
ROPE_AXIS_PAIRS = HEAD_DIM // 4
ROPE_THETA = 10000.0
A_HEADS = 8
A_KV_HEADS = 2
WINDOW = 128
B_HEADS = 8
NA_KH = 8
NA_KW = 16
C_HEADS = 4
C_VDIM = 2 * HEAD_DIM
BRANCH_W = 512
N_EXPERTS = 32
TOP_K = 4
D_FF = D_MODEL
SWIGLU_LIMIT = 7.0
SWIGLU_ALPHA = 1.702
N_MOD = 6
EPS = 1e-6
NEG_INF = -1e30
ATTN_SCALE = HEAD_DIM ** -0.5
LOG2E = 1.4426950408889634
C_QSCALE = ATTN_SCALE * LOG2E

LANES = 128
VMEM_LIMIT = 56 * 1024 * 1024

A_COLS = 1024
B_COLS = 1536
C_COLS = 1536
G_COLS = 3 * D_MODEL
P_COLS = A_COLS + B_COLS + C_COLS + G_COLS

TM = 256
TQ = 256
NA_ROWS_Q = TQ // GRID_W
NA_ROWS_K = NA_ROWS_Q + NA_KH
NA_KEYS = NA_ROWS_K * GRID_W
WIN_KEYS = TQ + 2 * WINDOW
NA_COLS_PER_STEP = 2
C_CHUNK = 512
MOE_TM = 512
ROUTE_TM = 1024
D_PACK = D_MODEL // 2


def _cparams(sem):
    return pltpu.CompilerParams(dimension_semantics=sem, vmem_limit_bytes=VMEM_LIMIT)


def _head_masks():
    lane = lax.broadcasted_iota(jnp.int32, (1, LANES), 1)
    lo = jnp.where(lane < HEAD_DIM, 1.0, 0.0).astype(BF16)
    hi = jnp.where(lane >= HEAD_DIM, 1.0, 0.0).astype(BF16)
    return lo, hi


def _split_heads(q):
    lo, hi = _head_masks()
    return jnp.concatenate([q * lo, q * hi], axis=0)


def _merge_heads(pv, t):
    lane = lax.broadcasted_iota(jnp.int32, (t, LANES), 1)
    return jnp.where(lane < HEAD_DIM, pv[:t], pv[t:])


def _pack_rows(x):
    half = x.shape[1] // 2
    lo = lax.bitcast_convert_type(x[:, :half].astype(BF16).astype(F32), jnp.uint32)
    hi = lax.bitcast_convert_type(x[:, half:].astype(BF16).astype(F32), jnp.uint32)
    return (lo >> 16) | (hi & jnp.uint32(0xFFFF0000))


def _unpack_rows(w):
    lo = lax.bitcast_convert_type(w << 16, F32)
    hi = lax.bitcast_convert_type(w & jnp.uint32(0xFFFF0000), F32)
    return lo, hi


def _dot_nt(a, b):
    return lax.dot_general(a, b, (((1,), (1,)), ((), ())), preferred_element_type=F32)


def _dot(a, b):
    return jnp.dot(a, b, preferred_element_type=F32)


def _ada_kernel(c_ref, w_ref, b_ref, o_ref):
    c = c_ref[...]
    s = c * (1.0 / (1.0 + jnp.exp(-c)))
    o_ref[0] = jnp.dot(s, w_ref[0], preferred_element_type=F32,
                       precision=lax.Precision.HIGHEST) + b_ref[0]


def _modulation(cond, w_ada, b_ada):
    tn = 1536
    n = N_MOD * D_MODEL
    return pl.pallas_call(
        _ada_kernel,
        grid=(DEPTH, n // tn),
        in_specs=[
            pl.BlockSpec((8, D_MODEL), lambda l, j: (0, 0)),
            pl.BlockSpec((1, D_MODEL, tn), lambda l, j: (l, 0, j)),
            pl.BlockSpec((1, 1, tn), lambda l, j: (l, 0, j)),
        ],
        out_specs=pl.BlockSpec((1, 8, tn), lambda l, j: (l, 0, j)),
        out_shape=jax.ShapeDtypeStruct((DEPTH, 8, n), F32),
        compiler_params=_cparams(("arbitrary", "arbitrary")),
        name="modulation",
    )(cond, w_ada, b_ada.reshape(DEPTH, 1, n))


def _rms_modulate(x, g, shift, scale):
    ms = jnp.mean(x * x, axis=-1, keepdims=True)
    return (x * lax.rsqrt(ms + EPS) * g) * (1.0 + scale) + shift


def _rope_cols(v, cos, sin_signed):
    t, w = v.shape
    lane = lax.broadcasted_iota(jnp.int32, (t, LANES), 1)
    first_half = (lane & (HEAD_DIM - 1)) < ROPE_HALF
    cols = []
    for c in range(w // LANES):
        xc = v[:, c * LANES:(c + 1) * LANES]
        partner = jnp.where(first_half,
                            pltpu.roll(xc, LANES - ROPE_HALF, axis=1),
                            pltpu.roll(xc, ROPE_HALF, axis=1))
        cols.append(xc * cos + partner * sin_signed)
    return jnp.concatenate(cols, axis=1) if len(cols) > 1 else cols[0]


def _proj_kernel(x_ref, mod_ref, g_ref, w_ref, cos_ref, sin_ref, *out_refs, rope, emit_kv):
    oa_ref, ob_ref, oc_ref, og_ref = out_refs[:4]
    h = _rms_modulate(x_ref[...], g_ref[...], mod_ref[0, 0:1, :], mod_ref[0, 1:2, :]).astype(BF16)
    if rope:
        cos = cos_ref[...]
        sin = sin_ref[...]

    def seg(start, width):
        return _dot(h, w_ref[:, start:start + width])

    def maybe_rope(v):
        return _rope_cols(v, cos, sin) if rope else v

    oa_ref[:, 0:512] = (maybe_rope(seg(0, 512)) * ATTN_SCALE).astype(BF16)
    ka = maybe_rope(seg(512, 256))
    va = seg(768, 256)
    oa_ref[:, 512:768] = ka.astype(BF16)
    oa_ref[:, 768:1024] = va.astype(BF16)
    ob_ref[:, 0:512] = (seg(A_COLS, 512) * ATTN_SCALE).astype(BF16)
    kb = seg(A_COLS + 512, 512)
    vb = seg(A_COLS + 1024, 512)
    ob_ref[:, 512:1024] = kb.astype(BF16)
    ob_ref[:, 1024:1536] = vb.astype(BF16)
    c0 = A_COLS + B_COLS
    oc_ref[:, 0:512] = (maybe_rope(seg(c0, 512)) * C_QSCALE).astype(BF16)
    kc = maybe_rope(seg(c0 + 512, 512))
    vc = seg(c0 + 1024, 512)
    oc_ref[:, 512:1024] = kc.astype(BF16)
    oc_ref[:, 1024:1536] = vc.astype(BF16)
    g0 = c0 + C_COLS
    for j in range(G_COLS // 512):
        gv = seg(g0 + j * 512, 512)
        og_ref[:, j * 512:(j + 1) * 512] = (1.0 / (1.0 + jnp.exp(-gv))).astype(BF16)
    if emit_kv:
        kva_ref, kvb_ref, kvc_ref = out_refs[4:]
        kva_ref[:, 0:256] = ka
        kva_ref[:, 256:512] = va
        kvb_ref[:, 0:512] = kb
        kvb_ref[:, 512:1024] = vb
        kvc_ref[:, 0:512] = kc
        kvc_ref[:, 512:1024] = vc


def _project(x, mod, mod_row0, tiles_per_row, g, w_ext, cos, sin, *, rope, emit_kv):
    n = x.shape[0]
    nt = n // TM
    pos_tiles = cos.shape[0] // TM
    out_shape = [jax.ShapeDtypeStruct((n, A_COLS), BF16), jax.ShapeDtypeStruct((n, B_COLS), BF16),
                 jax.ShapeDtypeStruct((n, C_COLS), BF16), jax.ShapeDtypeStruct((n, G_COLS), BF16)]
    out_specs = [pl.BlockSpec((TM, A_COLS), lambda i: (i, 0)), pl.BlockSpec((TM, B_COLS), lambda i: (i, 0)),
                 pl.BlockSpec((TM, C_COLS), lambda i: (i, 0)), pl.BlockSpec((TM, G_COLS), lambda i: (i, 0))]
    if emit_kv:
        out_shape += [jax.ShapeDtypeStruct((n, 512), F32), jax.ShapeDtypeStruct((n, 1024), F32),
                      jax.ShapeDtypeStruct((n, 1024), F32)]
        out_specs += [pl.BlockSpec((TM, 512), lambda i: (i, 0)), pl.BlockSpec((TM, 1024), lambda i: (i, 0)),
                      pl.BlockSpec((TM, 1024), lambda i: (i, 0))]
    return pl.pallas_call(
        functools.partial(_proj_kernel, rope=rope, emit_kv=emit_kv),
        grid=(nt,),
        in_specs=[
            pl.BlockSpec((TM, D_MODEL), lambda i: (i, 0)),
            pl.BlockSpec((1, N_MOD, D_MODEL), lambda i: (mod_row0 + i // tiles_per_row, 0, 0)),
            pl.BlockSpec((1, D_MODEL), lambda i: (0, 0)),
            pl.BlockSpec((D_MODEL, P_COLS), lambda i: (0, 0)),
            pl.BlockSpec((TM, LANES), lambda i: (i % pos_tiles, 0)),
            pl.BlockSpec((TM, LANES), lambda i: (i % pos_tiles, 0)),
        ],
        out_specs=out_specs,
        out_shape=out_shape,
        compiler_params=_cparams(("arbitrary",)),
        name="project_rope" if rope else "project",
    )(x, mod, g, w_ext, cos, sin)


def _softmax_pv(scores, values, sink=None):
    m = scores[0].max(axis=-1, keepdims=True)
    for s in scores[1:]:
        m = jnp.maximum(m, s.max(axis=-1, keepdims=True))
    if sink is not None:
        m = jnp.maximum(m, sink)
    l = None
    acc = None
    for s, v in zip(scores, values):
        e = jnp.exp(s - m)
        ls = e.sum(axis=-1, keepdims=True)
        pv = _dot(e.astype(BF16), v)
        l = ls if l is None else l + ls
        acc = pv if acc is None else acc + pv
    if sink is not None:
        l = l + jnp.exp(sink - m)
    return acc * (1.0 / l)


def _sink_column(sink_ref, first_head, rows_per_head, n_heads):
    row = lax.broadcasted_iota(jnp.int32, (rows_per_head * n_heads, 1), 0)
    col = jnp.full((rows_per_head * n_heads, 1), sink_ref[first_head], F32)
    for j in range(1, n_heads):
        col = jnp.where(row >= j * rows_per_head, sink_ref[first_head + j], col)
    return col


def _diff_lambda(lq_ref, lambda_init):
    lf = lq_ref[...]
    a = jnp.sum(lf[0:1] * lf[1:2], axis=-1, keepdims=True)
    b = jnp.sum(lf[2:3] * lf[3:4], axis=-1, keepdims=True)
    return jnp.exp(a) - jnp.exp(b) + lambda_init


def _subln(o, g, lambda_init):
    ms = jnp.mean(o * o, axis=-1, keepdims=True)
    return (o * lax.rsqrt(ms + EPS) * g) * (1.0 - lambda_init)


def _ctx_attn_kernel(sink_ref, lq_ref, sg_ref, a_ref, b_ref, c_ref, oa_ref, ob_ref, oc_ref, *, lambda_init):
    t = a_ref.shape[0]
    for g in range(A_KV_HEADS):
        q = a_ref[:, g * 256:(g + 1) * 256]
        lhs = jnp.concatenate([_split_heads(q[:, :LANES]), _split_heads(q[:, LANES:])], axis=0)
        k = a_ref[:, 512 + g * LANES:512 + (g + 1) * LANES]
        v = a_ref[:, 768 + g * LANES:768 + (g + 1) * LANES]
        sink = _sink_column(sink_ref, 4 * g, t, 4)
        o = _softmax_pv([_dot_nt(lhs, k)], [v], sink)
        oa_ref[:, g * 256:g * 256 + LANES] = _merge_heads(o[:2 * t], t).astype(BF16)
        oa_ref[:, g * 256 + LANES:(g + 1) * 256] = _merge_heads(o[2 * t:], t).astype(BF16)
    for c in range(B_HEADS // 2):
        lhs = _split_heads(b_ref[:, c * LANES:(c + 1) * LANES])
        k = b_ref[:, 512 + c * LANES:512 + (c + 1) * LANES]
        v = b_ref[:, 1024 + c * LANES:1024 + (c + 1) * LANES]
        o = _softmax_pv([_dot_nt(lhs, k)], [v])
        ob_ref[:, c * LANES:(c + 1) * LANES] = _merge_heads(o, t).astype(BF16)
    lam = _diff_lambda(lq_ref, lambda_init)
    for h in range(C_HEADS):
        lhs = _split_heads(c_ref[:, h * LANES:(h + 1) * LANES])
        k = c_ref[:, 512 + h * LANES:512 + (h + 1) * LANES]
        v = c_ref[:, 1024 + h * LANES:1024 + (h + 1) * LANES]
        s = _dot_nt(lhs, k)
        m = s.max(axis=-1, keepdims=True)
        e = jnp.exp2(s - m)
        p = e * (1.0 / e.sum(axis=-1, keepdims=True))
        a = (p[:t] - lam * p[t:]).astype(BF16)
        oc_ref[:, h * LANES:(h + 1) * LANES] = _subln(_dot(a, v), sg_ref[...], lambda_init).astype(BF16)


def _ctx_attention(pa, pb, pc, sink, lq, sg, seq, lambda_init):
    n = pa.shape[0]
    smem = pl.BlockSpec(memory_space=pltpu.SMEM)
    return pl.pallas_call(
        functools.partial(_ctx_attn_kernel, lambda_init=lambda_init),
        grid=(n // seq,),
        in_specs=[
            smem,
            pl.BlockSpec((4, HEAD_DIM), lambda b: (0, 0)),
            pl.BlockSpec((1, C_VDIM), lambda b: (0, 0)),
            pl.BlockSpec((seq, A_COLS), lambda b: (b, 0)),
            pl.BlockSpec((seq, B_COLS), lambda b: (b, 0)),
            pl.BlockSpec((seq, C_COLS), lambda b: (b, 0)),
        ],
        out_specs=[pl.BlockSpec((seq, BRANCH_W), lambda b: (b, 0))] * 3,
        out_shape=[jax.ShapeDtypeStruct((n, BRANCH_W), BF16)] * 3,
        compiler_params=_cparams(("arbitrary",)),
        name="context_attention",
    )(sink, lq, sg, pa, pb, pc)


def _win_attn_kernel(sink_ref, q_ref, k_ref, v_ref, kc_ref, vc_ref, o_ref, *, n_lat):
    qi = pl.program_id(1)
    ws = pl.multiple_of(jnp.clip(qi * TQ - WINDOW, 0, n_lat - WIN_KEYS), WINDOW)
    qpos = qi * TQ + lax.broadcasted_iota(jnp.int32, (TQ, WIN_KEYS), 0)
    kpos = ws + lax.broadcasted_iota(jnp.int32, (TQ, WIN_KEYS), 1)
    band = jnp.where(jnp.abs(kpos - qpos) <= WINDOW, 0.0, NEG_INF)
    band = jnp.concatenate([band] * 4, axis=0)
    for g in range(A_KV_HEADS):
        q = q_ref[:, g * 256:(g + 1) * 256]
        lhs = jnp.concatenate([_split_heads(q[:, :LANES]), _split_heads(q[:, LANES:])], axis=0)
        cols = slice(g * LANES, (g + 1) * LANES)
        s_loc = _dot_nt(lhs, k_ref[pl.ds(ws, WIN_KEYS), cols]) + band
        s_ctx = _dot_nt(lhs, kc_ref[0, :, cols])
        sink = _sink_column(sink_ref, 4 * g, TQ, 4)
        o = _softmax_pv([s_loc, s_ctx], [v_ref[pl.ds(ws, WIN_KEYS), cols], vc_ref[0, :, cols]], sink)
        o_ref[:, g * 256:g * 256 + LANES] = _merge_heads(o[:2 * TQ], TQ).astype(BF16)
        o_ref[:, g * 256 + LANES:(g + 1) * 256] = _merge_heads(o[2 * TQ:], TQ).astype(BF16)


def _win_attention(pa, kctx, vctx, sink, n_batch, n_lat):
    nq = n_lat // TQ
    past = kctx.shape[1]
    return pl.pallas_call(
        functools.partial(_win_attn_kernel, n_lat=n_lat),
        grid=(n_batch, nq),
        in_specs=[
            pl.BlockSpec(memory_space=pltpu.SMEM),
            pl.BlockSpec((TQ, 512), lambda b, i: (b * nq + i, 0)),
            pl.BlockSpec((n_lat, 256), lambda b, i: (b, 2)),
            pl.BlockSpec((n_lat, 256), lambda b, i: (b, 3)),
            pl.BlockSpec((1, past, 256), lambda b, i: (b, 0, 0)),
            pl.BlockSpec((1, past, 256), lambda b, i: (b, 0, 0)),
        ],
        out_specs=pl.BlockSpec((TQ, BRANCH_W), lambda b, i: (b * nq + i, 0)),
        out_shape=jax.ShapeDtypeStruct((n_batch * n_lat, BRANCH_W), BF16),
        compiler_params=_cparams(("arbitrary", "arbitrary")),
        name="window_attention",
    )(sink, pa, pa, pa, kctx, vctx)


def _na_window_start(qi, rows):
    return jnp.clip(qi * NA_ROWS_Q - NA_KH // 2, 0, rows - NA_ROWS_K)


def _na_attn_kernel(q_ref, k_ref, v_ref, kc_ref, vc_ref, bias_ref, o_ref, *, rows):
    qi = pl.program_id(2)
    ws = pl.multiple_of(_na_window_start(qi, rows) * GRID_W, GRID_W)
    for c in range(NA_COLS_PER_STEP):
        cols = slice(c * LANES, (c + 1) * LANES)
        lhs = _split_heads(q_ref[:, cols])
        s_loc = _dot_nt(lhs, k_ref[pl.ds(ws, NA_KEYS), cols]) + bias_ref[0, 2 * c:2 * c + 2].reshape(2 * TQ, NA_KEYS)
        s_ctx = _dot_nt(lhs, kc_ref[0, :, cols])
        o = _softmax_pv([s_loc, s_ctx], [v_ref[pl.ds(ws, NA_KEYS), cols], vc_ref[0, :, cols]])
        o_ref[:, cols] = _merge_heads(o, TQ).astype(BF16)


def _na_bias_kernel(rpb_ref, o_ref, *, layer, rows):
    h = pl.program_id(0)
    n_dr = 2 * NA_KH - 1
    n_dc = 2 * NA_KW - 1
    base = (layer * B_HEADS + h) * n_dr * n_dc
    lane = lax.broadcasted_iota(jnp.int32, (GRID_W, LANES), 1)
    qc = lax.broadcasted_iota(jnp.int32, (GRID_W, LANES), 0)
    kc = lane & (GRID_W - 1)
    second = lane >= GRID_W
    cs = jnp.clip(qc - NA_KW // 2, 0, GRID_W - NA_KW)
    col_ok = (kc >= cs) & (kc < cs + NA_KW)
    dc_map = kc - qc + (NA_KW - 1)
    cache = {}

    def pair_tile(dr0, ok0, ok1):
        key = (dr0 if ok0 else None, dr0 + 1 if ok1 else None)
        if key not in cache:
            if not (ok0 or ok1):
                cache[key] = jnp.full((GRID_W, LANES), NEG_INF, F32)
            else:
                acc = jnp.zeros((GRID_W, LANES), F32)
                for dc in range(n_dc):
                    v0 = rpb_ref[base + dr0 * n_dc + dc] if ok0 else 0.0
                    v1 = rpb_ref[base + (dr0 + 1) * n_dc + dc] if ok1 else 0.0
                    acc = jnp.where(dc_map == dc, jnp.where(second, v1, v0), acc)
                if ok0 and ok1:
                    ok = col_ok
                elif ok0:
                    ok = col_ok & jnp.logical_not(second)
                else:
                    ok = col_ok & second
                cache[key] = jnp.where(ok, acc, NEG_INF)
        return cache[key]

    nq = rows // NA_ROWS_Q
    for p, qi in enumerate((0, 1, nq - 1)):
        r0 = qi * NA_ROWS_Q
        ws = min(max(r0 - NA_KH // 2, 0), rows - NA_ROWS_K)
        for i in range(NA_ROWS_Q):
            r = r0 + i
            lo = min(max(r - NA_KH // 2, 0), rows - NA_KH)
            for jp in range(NA_ROWS_K // 2):
                kr = ws + 2 * jp
                ok0 = lo <= kr < lo + NA_KH
                ok1 = lo <= kr + 1 < lo + NA_KH
                o_ref[p, 0, i * GRID_W:(i + 1) * GRID_W, jp * LANES:(jp + 1) * LANES] = pair_tile(
                    kr - r + NA_KH - 1, ok0, ok1)


def _na_bias_table(rpb_flat, layer, rows):
    return pl.pallas_call(
        functools.partial(_na_bias_kernel, layer=layer, rows=rows),
        grid=(B_HEADS,),
        in_specs=[pl.BlockSpec(memory_space=pltpu.SMEM)],
        out_specs=pl.BlockSpec((3, 1, TQ, NA_KEYS), lambda h: (0, h, 0, 0)),
        out_shape=jax.ShapeDtypeStruct((3, B_HEADS, TQ, NA_KEYS), F32),
        compiler_params=_cparams(("arbitrary",)),
        name="neighborhood_bias",
    )(rpb_flat)


def _na_attention(pb, kctx, vctx, bias, n_batch, n_lat):
    nq = n_lat // TQ
    rows = n_lat // GRID_W
    past = kctx.shape[1]
    w = NA_COLS_PER_STEP * LANES
    n_cb = BRANCH_W // w

    def bias_idx(b, c, i):
        return (jnp.where(i == 0, 0, jnp.where(i == nq - 1, 2, 1)), c, 0, 0)

    return pl.pallas_call(
        functools.partial(_na_attn_kernel, rows=rows),
        grid=(n_batch, n_cb, nq),
        in_specs=[
            pl.BlockSpec((TQ, w), lambda b, c, i: (b * nq + i, c)),
            pl.BlockSpec((n_lat, w), lambda b, c, i: (b, n_cb + c)),
            pl.BlockSpec((n_lat, w), lambda b, c, i: (b, 2 * n_cb + c)),
            pl.BlockSpec((1, past, w), lambda b, c, i: (b, 0, c)),
            pl.BlockSpec((1, past, w), lambda b, c, i: (b, 0, c)),
            pl.BlockSpec((1, 2 * NA_COLS_PER_STEP, TQ, NA_KEYS), bias_idx),
        ],
        out_specs=pl.BlockSpec((TQ, w), lambda b, c, i: (b * nq + i, c)),
        out_shape=jax.ShapeDtypeStruct((n_batch * n_lat, BRANCH_W), BF16),
        compiler_params=_cparams(("arbitrary", "arbitrary", "arbitrary")),
        name="neighborhood_attention",
    )(pb, pb, pb, kctx, vctx, bias)


def _diff_attn_kernel(lq_ref, sg_ref, q_ref, k_ref, v_ref, kc_ref, vc_ref, o_ref, vt_s, vtc_s, acc_s, *,
                      n_lat, lambda_init):
    n_chunks = n_lat // C_CHUNK

    @pl.when(pl.program_id(2) == 0)
    def _():
        for j in range(n_chunks):
            vt_s[j] = v_ref[j * C_CHUNK:(j + 1) * C_CHUNK, :].astype(F32).T.astype(BF16)
        vtc_s[...] = vc_ref[0].astype(F32).T.astype(BF16)

    lhs = _split_heads(q_ref[...])
    acc_s[...] = jnp.zeros_like(acc_s)

    def step(k, vt, m, l):
        st = _dot_nt(k, lhs)
        m_new = jnp.maximum(m, st.max(axis=0, keepdims=True))
        alpha = jnp.exp2(m - m_new)
        e = jnp.exp2(st - m_new)
        acc_s[...] = alpha * acc_s[...] + _dot(vt, e.astype(BF16))
        return m_new, alpha * l + e.sum(axis=0, keepdims=True)

    m, l = jnp.full((1, 2 * TQ), NEG_INF, F32), jnp.zeros((1, 2 * TQ), F32)
    for j in range(n_chunks):
        m, l = step(k_ref[j * C_CHUNK:(j + 1) * C_CHUNK, :], vt_s[j], m, l)
    m, l = step(kc_ref[0], vtc_s[...], m, l)
    o = acc_s[...] * (1.0 / l)
    d = o[:, :TQ] - _diff_lambda(lq_ref, lambda_init) * o[:, TQ:]
    ms = jnp.mean(d * d, axis=0, keepdims=True)
    y = (d * lax.rsqrt(ms + EPS)).T
    o_ref[...] = (y * sg_ref[...] * (1.0 - lambda_init)).astype(BF16)


def _diff_attention(pc, kctx, vctx, lq, sg, n_batch, n_lat, lambda_init):
    nq = n_lat // TQ
    past = kctx.shape[1]
    return pl.pallas_call(
        functools.partial(_diff_attn_kernel, n_lat=n_lat, lambda_init=lambda_init),
        grid=(n_batch, C_HEADS, nq),
        in_specs=[
            pl.BlockSpec((4, HEAD_DIM), lambda b, h, i: (0, 0)),
            pl.BlockSpec((1, C_VDIM), lambda b, h, i: (0, 0)),
            pl.BlockSpec((TQ, LANES), lambda b, h, i: (b * nq + i, h)),
            pl.BlockSpec((n_lat, LANES), lambda b, h, i: (b, 4 + h)),
            pl.BlockSpec((n_lat, LANES), lambda b, h, i: (b, 8 + h)),
            pl.BlockSpec((1, past, LANES), lambda b, h, i: (b, 0, h)),
            pl.BlockSpec((1, past, LANES), lambda b, h, i: (b, 0, h)),
        ],
        out_specs=pl.BlockSpec((TQ, LANES), lambda b, h, i: (b * nq + i, h)),
        out_shape=jax.ShapeDtypeStruct((n_batch * n_lat, BRANCH_W), BF16),
        scratch_shapes=[pltpu.VMEM((n_lat // C_CHUNK, C_VDIM, C_CHUNK), BF16), pltpu.VMEM((C_VDIM, past), BF16),
                        pltpu.VMEM((C_VDIM, 2 * TQ), F32)],
        compiler_params=_cparams(("arbitrary", "arbitrary", "arbitrary")),
        name="differential_attention",
    )(lq, sg, pc, pc, pc, kctx, vctx)


def _merge_kernel(x_ref, mod_ref, g_ref, oa_ref, ob_ref, oc_ref, sg_ref, wb_ref, wo_ref, wr_ref, br_ref,
                  xo_ref, h_ref, idx_ref, gate_ref):
    y = None
    for j, o_ref in enumerate((oa_ref, ob_ref, oc_ref)):
        t = sg_ref[:, j * D_MODEL:(j + 1) * D_MODEL].astype(F32) * _dot(o_ref[...], wb_ref[j])
        y = t if y is None else y + t
    x = x_ref[...] + mod_ref[0, 2:3, :] * _dot(y.astype(BF16), wo_ref[...])
    xo_ref[...] = x
    h = _rms_modulate(x, g_ref[...], mod_ref[0, 3:4, :], mod_ref[0, 4:5, :])
    h_ref[...] = _pack_rows(h)
    wr = wr_ref[...]
    h_hi = h.astype(BF16)
    h_lo = (h - h_hi.astype(F32)).astype(BF16)
    w_hi = wr.astype(BF16)
    w_lo = (wr - w_hi.astype(F32)).astype(BF16)
    logits = _dot(h_hi, w_hi) + (_dot(h_hi, w_lo) + _dot(h_lo, w_hi)) + br_ref[...]
    tm = logits.shape[0]
    lane_e = lax.broadcasted_iota(jnp.int32, (tm, N_EXPERTS), 1)
    lane_o = lax.broadcasted_iota(jnp.int32, (tm, LANES), 1)
    idx_out = jnp.zeros((tm, LANES), jnp.int32)
    val_out = jnp.zeros((tm, LANES), F32)
    top = None
    denom = None
    for k in range(TOP_K):
        mx = logits.max(axis=-1, keepdims=True)
        sel = jnp.min(jnp.where(logits == mx, lane_e, N_EXPERTS), axis=-1, keepdims=True)
        logits = jnp.where(lane_e == sel, -jnp.inf, logits)
        if top is None:
            top = mx
        e = jnp.exp(mx - top)
        denom = e if denom is None else denom + e
        idx_out = jnp.where(lane_o == k, sel, idx_out)
        val_out = jnp.where(lane_o == k, e, val_out)
    idx_ref[...] = idx_out
    gate_ref[...] = val_out * (1.0 / denom)


def _merge_route(x, mod, mod_row0, tiles_per_row, g_ffn, oa, ob, oc, sg, wb, wo, wr, br):
    n = x.shape[0]
    row = lambda i: (i, 0)
    fixed2 = lambda i: (0, 0)
    return pl.pallas_call(
        _merge_kernel,
        grid=(n // TM,),
        in_specs=[
            pl.BlockSpec((TM, D_MODEL), row),
            pl.BlockSpec((1, N_MOD, D_MODEL), lambda i: (mod_row0 + i // tiles_per_row, 0, 0)),
            pl.BlockSpec((1, D_MODEL), fixed2),
            pl.BlockSpec((TM, BRANCH_W), row),
            pl.BlockSpec((TM, BRANCH_W), row),
            pl.BlockSpec((TM, BRANCH_W), row),
            pl.BlockSpec((TM, G_COLS), row),
            pl.BlockSpec((3, BRANCH_W, D_MODEL), lambda i: (0, 0, 0)),
            pl.BlockSpec((D_MODEL, D_MODEL), fixed2),
            pl.BlockSpec((D_MODEL, N_EXPERTS), fixed2),
            pl.BlockSpec((1, N_EXPERTS), fixed2),
        ],
        out_specs=[pl.BlockSpec((TM, D_MODEL), row), pl.BlockSpec((TM, D_PACK), row),
                   pl.BlockSpec((TM, LANES), row), pl.BlockSpec((TM, LANES), row)],
        out_shape=[jax.ShapeDtypeStruct((n, D_MODEL), F32), jax.ShapeDtypeStruct((n, D_PACK), jnp.uint32),
                   jax.ShapeDtypeStruct((n, LANES), jnp.int32), jax.ShapeDtypeStruct((n, LANES), F32)],
        compiler_params=_cparams(("arbitrary",)),
        name="merge_route",
    )(x, mod, g_ffn, oa, ob, oc, sg, wb, wo, wr, br)


def _moe_kernel(be_ref, first_ref, valid_ref, x_ref, wgu_ref, bgu_ref, wd_ref, bd_ref, o_ref, wgu_s, wd_s):
    i = pl.program_id(0)

    @pl.when(first_ref[i] == 1)
    def _():
        wgu_s[...] = wgu_ref[0, 0].astype(BF16)
        wd_s[...] = wd_ref[0, 0].astype(BF16)

    @pl.when(valid_ref[i] == 1)
    def _():
        x_lo, x_hi = _unpack_rows(x_ref[...])
        gu = (_dot(x_lo.astype(BF16), wgu_s[:D_PACK, :]) + _dot(x_hi.astype(BF16), wgu_s[D_PACK:, :])
              + bgu_ref[0, 0])
        gate = jnp.minimum(gu[:, :D_FF], SWIGLU_LIMIT)
        lin = jnp.clip(gu[:, D_FF:], -SWIGLU_LIMIT, SWIGLU_LIMIT)
        act = gate * (1.0 / (1.0 + jnp.exp(-SWIGLU_ALPHA * gate))) * (lin + 1.0)
        o_ref[...] = _pack_rows(_dot(act.astype(BF16), wd_s[...]) + bd_ref[0, 0])

    @pl.when(valid_ref[i] == 0)
    def _():
        o_ref[...] = jnp.zeros_like(o_ref)


def _moe_experts(xb, block_e, block_first, block_valid, layer, w_gu, b_gu, w_down, b_down):
    m = xb.shape[0]
    nb = m // MOE_TM
    grid_spec = pltpu.PrefetchScalarGridSpec(
        num_scalar_prefetch=3,
        grid=(nb,),
        in_specs=[
            pl.BlockSpec((MOE_TM, D_PACK), lambda i, be, bf, bv: (i, 0)),
            pl.BlockSpec((1, 1, D_MODEL, 2 * D_FF), lambda i, be, bf, bv: (layer, be[i], 0, 0)),
            pl.BlockSpec((1, 1, 1, 2 * D_FF), lambda i, be, bf, bv: (layer, be[i], 0, 0)),
            pl.BlockSpec((1, 1, D_FF, D_MODEL), lambda i, be, bf, bv: (layer, be[i], 0, 0)),
            pl.BlockSpec((1, 1, 1, D_MODEL), lambda i, be, bf, bv: (layer, be[i], 0, 0)),
        ],
        out_specs=pl.BlockSpec((MOE_TM, D_PACK), lambda i, be, bf, bv: (i, 0)),
        scratch_shapes=[pltpu.VMEM((D_MODEL, 2 * D_FF), BF16), pltpu.VMEM((D_FF, D_MODEL), BF16)],
    )
    return pl.pallas_call(
        _moe_kernel,
        grid_spec=grid_spec,
        out_shape=jax.ShapeDtypeStruct((m, D_PACK), jnp.uint32),
        compiler_params=_cparams(("arbitrary",)),
        name="moe_experts",
    )(block_e, block_first, block_valid, xb, w_gu, b_gu.reshape(DEPTH, N_EXPERTS, 1, 2 * D_FF),
      w_down, b_down.reshape(DEPTH, N_EXPERTS, 1, D_MODEL))


def _moe_block_tables(counts, nb):
    padded = (counts + MOE_TM - 1) // MOE_TM * MOE_TM
    pad_end = jnp.cumsum(padded).astype(jnp.int32)
    blk = jnp.arange(nb, dtype=jnp.int32) * MOE_TM
    block_e = jnp.minimum(jnp.sum((blk[:, None] >= pad_end[None, :]).astype(jnp.int32), axis=1), N_EXPERTS - 1)
    block_valid = (blk < pad_end[-1]).astype(jnp.int32)
    prev = jnp.concatenate([jnp.full((1,), -1, jnp.int32), block_e[:-1]])
    block_first = (block_e != prev).astype(jnp.int32)
    return pad_end, block_e, block_first, block_valid


def _route_kernel(idx_ref, dest_ref, cnt_ref, run_s, start_s):
    p = pl.program_id(0)
    i = pl.program_id(1)
    tm = idx_ref.shape[0]
    lane = lax.broadcasted_iota(jnp.int32, (tm, LANES), 1)
    idx = idx_ref[...]
    sel = [idx[:, k:k + 1] for k in range(TOP_K)]
    onehot = jnp.zeros((tm, LANES), F32)
    for k in range(TOP_K):
        onehot = onehot + jnp.where(lane == sel[k], 1.0, 0.0)
    tile_cnt = jnp.sum(onehot, axis=0, keepdims=True)

    @pl.when((p == 0) & (i == 0))
    def _():
        run_s[...] = jnp.zeros_like(run_s)
        cnt_ref[...] = jnp.zeros_like(cnt_ref)

    @pl.when(p == 0)
    def _():
        run_s[...] += tile_cnt

    @pl.when((p == 1) & (i == 0))
    def _():
        cnt = run_s[...]
        padded = ((cnt.astype(jnp.int32) + (MOE_TM - 1)) // MOE_TM * MOE_TM).astype(F32)
        r = lax.broadcasted_iota(jnp.int32, (LANES, LANES), 0)
        c = lax.broadcasted_iota(jnp.int32, (LANES, LANES), 1)
        before = jnp.where(r < c, 1.0, 0.0)
        start = jnp.dot(jnp.broadcast_to(padded, (8, LANES)), before, preferred_element_type=F32,
                        precision=lax.Precision.HIGHEST)
        start_s[...] = start[0:1]
        cnt_ref[...] = jnp.broadcast_to(cnt, cnt_ref.shape)
        run_s[...] = jnp.zeros_like(run_s)

    @pl.when(p == 1)
    def _():
        r = lax.broadcasted_iota(jnp.int32, (tm, tm), 0)
        c = lax.broadcasted_iota(jnp.int32, (tm, tm), 1)
        earlier = jnp.where(c < r, 1.0, 0.0).astype(BF16)
        pos = _dot(earlier, onehot.astype(BF16)) + run_s[...] + start_s[...]
        out = jnp.zeros((tm, LANES), F32)
        for k in range(TOP_K):
            d = jnp.sum(jnp.where(lane == sel[k], pos, 0.0), axis=-1, keepdims=True)
            out = jnp.where(lane == k, d, out)
        dest_ref[...] = out.astype(jnp.int32)
        run_s[...] += tile_cnt


def _route(idx_slab):
    n = idx_slab.shape[0]
    return pl.pallas_call(
        _route_kernel,
        grid=(2, n // ROUTE_TM),
        in_specs=[pl.BlockSpec((ROUTE_TM, LANES), lambda p, i: (i, 0))],
        out_specs=[pl.BlockSpec((ROUTE_TM, LANES), lambda p, i: (i * p, 0)),
                   pl.BlockSpec((8, LANES), lambda p, i: (0, 0))],
        out_shape=[jax.ShapeDtypeStruct((n, LANES), jnp.int32), jax.ShapeDtypeStruct((8, LANES), F32)],
        scratch_shapes=[pltpu.VMEM((1, LANES), F32), pltpu.VMEM((1, LANES), F32)],
        compiler_params=_cparams(("arbitrary", "arbitrary")),
        name="route",
    )(idx_slab)


def _row_copy(src_ref, src_row, dst_ref, dst_row, sem):
    return pltpu.make_async_copy(src_ref.at[pl.ds(src_row, 1)], dst_ref.at[pl.ds(dst_row, 1)], sem)


def _dispatch_kernel(pe_ref, cnt_ref, dest_ref, hp_ref, hs_ref, xb_ref, zero_s, sem, *, tiles_p, first_tail_block):
    i = pl.program_id(0)
    tm = hp_ref.shape[0]
    nb = xb_ref.shape[0] // MOE_TM

    def zero_block(row0):
        return pltpu.make_async_copy(zero_s, xb_ref.at[pl.ds(pl.multiple_of(row0, MOE_TM), MOE_TM)], sem)

    @pl.when(i == 0)
    def _():
        zero_s[...] = jnp.zeros_like(zero_s)
        for start in (True, False):
            for e in range(N_EXPERTS):
                @pl.when(cnt_ref[e] > 0)
                def _():
                    cp = zero_block(pe_ref[e] - MOE_TM)
                    cp.start() if start else cp.wait()
            for b in range(first_tail_block, nb):
                @pl.when(b * MOE_TM >= pe_ref[N_EXPERTS - 1])
                def _():
                    cp = zero_block(b * MOE_TM)
                    cp.start() if start else cp.wait()

    def scatter_rows(h_ref):
        def issue(t, carry):
            for k in range(TOP_K):
                _row_copy(h_ref, t, xb_ref, dest_ref[0, 0, t * TOP_K + k], sem).start(priority=k % 2)
            return carry

        def drain(t, carry):
            for k in range(TOP_K):
                _row_copy(h_ref, 0, xb_ref, 0, sem).wait()
            return carry

        lax.fori_loop(0, tm, issue, 0)
        lax.fori_loop(0, tm, drain, 0)

    @pl.when(i < tiles_p)
    def _():
        scatter_rows(hp_ref)

    @pl.when(i >= tiles_p)
    def _():
        scatter_rows(hs_ref)


def _dispatch(pad_end, counts, dest, hp, hs, m):
    tiles_p = hp.shape[0] // TM
    tiles_s = hs.shape[0] // TM
    grid_spec = pltpu.PrefetchScalarGridSpec(
        num_scalar_prefetch=2,
        grid=(tiles_p + tiles_s,),
        in_specs=[
            pl.BlockSpec((1, 1, TM * TOP_K), lambda i, pe, cn: (i, 0, 0), memory_space=pltpu.SMEM),
            pl.BlockSpec((TM, D_PACK), lambda i, pe, cn: (jnp.minimum(i, tiles_p - 1), 0)),
            pl.BlockSpec((TM, D_PACK), lambda i, pe, cn: (jnp.maximum(i - tiles_p, 0), 0)),
        ],
        out_specs=pl.BlockSpec(memory_space=pl.ANY),
        scratch_shapes=[pltpu.VMEM((MOE_TM, D_PACK), jnp.uint32), pltpu.SemaphoreType.DMA(())],
    )
    return pl.pallas_call(
        functools.partial(_dispatch_kernel, tiles_p=tiles_p, first_tail_block=m // MOE_TM - N_EXPERTS),
        grid_spec=grid_spec,
        out_shape=jax.ShapeDtypeStruct((m, D_PACK), jnp.uint32),
        compiler_params=_cparams(("arbitrary",)),
        name="dispatch",
    )(pad_end, counts, dest, hp, hs)


def _combine_kernel(dest_ref, x_ref, mod_ref, gate_ref, gfin_ref, yb_ref, o_ref, buf, sem, *, final):
    tm = x_ref.shape[0]

    def issue(t, carry):
        for k in range(TOP_K):
            _row_copy(yb_ref, dest_ref[0, 0, t * TOP_K + k], buf.at[k], t, sem).start(priority=k % 2)
        return carry

    def drain(t, carry):
        for k in range(TOP_K):
            _row_copy(yb_ref, 0, buf.at[k], 0, sem).wait()
        return carry

    lax.fori_loop(0, tm, issue, 0)
    lax.fori_loop(0, tm, drain, 0)
    g = gate_ref[...]
    acc_lo = acc_hi = None
    for k in range(TOP_K):
        y_lo, y_hi = _unpack_rows(buf[k])
        acc_lo = g[:, k:k + 1] * y_lo if acc_lo is None else acc_lo + g[:, k:k + 1] * y_lo
        acc_hi = g[:, k:k + 1] * y_hi if acc_hi is None else acc_hi + g[:, k:k + 1] * y_hi
    x = x_ref[...] + mod_ref[0, 5:6, :] * jnp.concatenate([acc_lo, acc_hi], axis=1)
    if final:
        ms = jnp.mean(x * x, axis=-1, keepdims=True)
        x = x * lax.rsqrt(ms + EPS) * gfin_ref[...]
    o_ref[...] = x


def _combine(dest, x, mod, mod_row0, tiles_per_row, gates, g_final, yb, *, final):
    n = x.shape[0]
    return pl.pallas_call(
        functools.partial(_combine_kernel, final=final),
        grid=(n // TM,),
        in_specs=[
            pl.BlockSpec((1, 1, TM * TOP_K), lambda i: (i, 0, 0), memory_space=pltpu.SMEM),
            pl.BlockSpec((TM, D_MODEL), lambda i: (i, 0)),
            pl.BlockSpec((1, N_MOD, D_MODEL), lambda i: (mod_row0 + i // tiles_per_row, 0, 0)),
            pl.BlockSpec((TM, LANES), lambda i: (i, 0)),
            pl.BlockSpec((1, D_MODEL), lambda i: (0, 0)),
            pl.BlockSpec(memory_space=pl.ANY),
        ],
        out_specs=pl.BlockSpec((TM, D_MODEL), lambda i: (i, 0)),
        out_shape=jax.ShapeDtypeStruct((n, D_MODEL), F32),
        scratch_shapes=[pltpu.VMEM((TOP_K, TM, D_PACK), jnp.uint32), pltpu.SemaphoreType.DMA(())],
        compiler_params=_cparams(("arbitrary",)),
        name="combine_final" if final else "combine",
    )(dest, x, mod, gates, g_final, yb)


def _rope_tables(n_tokens):
    t = np.arange(n_tokens)
    row = (t // GRID_W).astype(np.float32)
    col = (t % GRID_W).astype(np.float32)
    inv = jnp.asarray(ROPE_THETA, F32) ** (-jnp.arange(ROPE_AXIS_PAIRS, dtype=F32) / ROPE_AXIS_PAIRS)
    ang = jnp.concatenate([jnp.asarray(row)[:, None] * inv, jnp.asarray(col)[:, None] * inv], axis=-1)
    cos, sin = jnp.cos(ang), jnp.sin(ang)
    cos = jnp.tile(cos, (1, LANES // ROPE_HALF))
    sin = jnp.tile(jnp.concatenate([-sin, sin], axis=-1), (1, LANES // HEAD_DIM))
    return cos, sin


def _extended_w_in(w):
    qa, ka, va, rest = w[:, :512], w[:, 512:640], w[:, 640:768], w[:, 768:]

    def dup(t):
        t = t.reshape(D_MODEL, A_KV_HEADS, 1, HEAD_DIM)
        return jnp.broadcast_to(t, (D_MODEL, A_KV_HEADS, 2, HEAD_DIM)).reshape(D_MODEL, 4 * HEAD_DIM)

    return jnp.concatenate([qa, dup(ka), dup(va), rest], axis=-1).astype(BF16)


def _dup_kv_heads(t):
    b, p = t.shape[:2]
    t = jnp.broadcast_to(t[:, :, :, None, :], (b, p, A_KV_HEADS, 2, HEAD_DIM))
    return t.reshape(b, p, 4 * HEAD_DIM).astype(BF16)


def kernel(x_prompt, x_sample, cache_k_a, cache_v_a, cache_k_b, cache_v_b, cache_k_c, cache_v_c, c, c_ctx,
           w_ada, b_ada, norm_attn, norm_ffn, w_in, sinks, rpb, lambda_qk, subln, w_branch, w_out,
           w_router, b_router, w_gu, b_gu, w_down, b_down, norm_final):
    bsz, seq = x_prompt.shape[:2]
    bsz_d, n_lat = x_sample.shape[:2]
    past = cache_k_a.shape[2]
    n_p = bsz * seq
    n_s = bsz_d * n_lat
    assert seq % TM == 0 and n_lat % TQ == 0 and n_lat % C_CHUNK == 0 and bsz_d + 1 <= 8
    assert (n_p + n_s) % ROUTE_TM == 0

    cond = jnp.zeros((8, D_MODEL), F32).at[0].set(c_ctx).at[1:1 + bsz_d].set(c)
    mod_all = _modulation(cond, w_ada, b_ada).reshape(DEPTH, 8, N_MOD, D_MODEL)
    cos, sin = _rope_tables(n_lat)
    g_final = norm_final[None, :]
    n_assign = (n_p + n_s) * TOP_K
    m_rows = (n_assign + N_EXPERTS * (MOE_TM - 1) + MOE_TM - 1) // MOE_TM * MOE_TM

    xp = x_prompt.reshape(n_p, D_MODEL)
    xs = x_sample.reshape(n_s, D_MODEL)
    kv_out = []
    for l in range(DEPTH):
        lambda_init = 0.8 - 0.6 * math.exp(-0.3 * l)
        mod = mod_all[l]
        w_ext = _extended_w_in(w_in[l])
        g_attn = norm_attn[l][None, :]
        g_ffn = norm_ffn[l][None, :]
        sg = subln[l][None, :]
        wb = w_branch[l].astype(BF16)
        wo = w_out[l].astype(BF16)
        br = b_router[l][None, :]

        pa, pb, pc, pg, kva, kvb, kvc = _project(xp, mod, 0, n_p // TM, g_attn, w_ext, cos, sin,
                                                 rope=False, emit_kv=True)
        oa, ob, oc = _ctx_attention(pa, pb, pc, sinks[l], lambda_qk[l], sg, seq, lambda_init)
        xp_mid, hp, idx_p, gate_p = _merge_route(xp, mod, 0, n_p // TM, g_ffn, oa, ob, oc, pg, wb, wo,
                                                 w_router[l], br)
        kv_out.append((kva, kvb, kvc))

        pa, pb, pc, pg = _project(xs, mod, 1, n_lat // TM, g_attn, w_ext, cos, sin, rope=True, emit_kv=False)
        oa = _win_attention(pa, _dup_kv_heads(cache_k_a[:, l]), _dup_kv_heads(cache_v_a[:, l]), sinks[l],
                            bsz_d, n_lat)
        ob = _na_attention(pb, cache_k_b[:, l].reshape(bsz_d, past, -1).astype(BF16),
                           cache_v_b[:, l].reshape(bsz_d, past, -1).astype(BF16),
                           _na_bias_table(rpb.reshape(-1), l, n_lat // GRID_W), bsz_d, n_lat)
        oc = _diff_attention(pc, cache_k_c[:, l].reshape(bsz_d, past, -1).astype(BF16),
                             cache_v_c[:, l].reshape(bsz_d, past, -1).astype(BF16),
                             lambda_qk[l], sg, bsz_d, n_lat, lambda_init)
        xs_mid, hs, idx_s, gate_s = _merge_route(xs, mod, 1, n_lat // TM, g_ffn, oa, ob, oc, pg, wb, wo,
                                                 w_router[l], br)

        last = l == DEPTH - 1
        dest_slab, cnt = _route(jnp.concatenate([idx_p, idx_s], axis=0))
        counts = cnt[0, :N_EXPERTS].astype(jnp.int32)
        pad_end, block_e, block_first, block_valid = _moe_block_tables(counts, m_rows // MOE_TM)
        dest = dest_slab[:, :TOP_K].reshape(-1, 1, TM * TOP_K)
        dest_p, dest_s = dest[:n_p // TM], dest[n_p // TM:]
        xb = _dispatch(pad_end, counts, dest, hp, hs, m_rows)
        yb = _moe_experts(xb, block_e, block_first, block_valid, l, w_gu, b_gu, w_down, b_down)
        xp = _combine(dest_p, xp_mid, mod, 0, n_p // TM, gate_p, g_final, yb, final=last)
        xs = _combine(dest_s, xs_mid, mod, 1, n_lat // TM, gate_s, g_final, yb, final=last)

    y_prompt = xp.reshape(bsz, seq, D_MODEL)
    y_sample = xs.reshape(bsz_d, n_lat, D_MODEL)

    def stack(pick):
        return jnp.stack([pick(kv) for kv in kv_out], axis=1)

    def heads_a(t):
        return t.reshape(bsz, seq, A_KV_HEADS, 2, HEAD_DIM)[:, :, :, 0]

    new_k_a = stack(lambda kv: heads_a(kv[0][:, 0:256]))
    new_v_a = stack(lambda kv: heads_a(kv[0][:, 256:512]))
    new_k_b = stack(lambda kv: kv[1][:, 0:512].reshape(bsz, seq, B_HEADS, HEAD_DIM))
    new_v_b = stack(lambda kv: kv[1][:, 512:1024].reshape(bsz, seq, B_HEADS, HEAD_DIM))
    new_k_c = stack(lambda kv: kv[2][:, 0:512].reshape(bsz, seq, C_HEADS, 2 * HEAD_DIM))
    new_v_c = stack(lambda kv: kv[2][:, 512:1024].reshape(bsz, seq, C_HEADS, C_VDIM))
    return (y_prompt, y_sample, new_k_a, new_v_a, new_k_b, new_v_b, new_k_c, new_v_c)
```

```python
import functools
import math

import jax
import jax.numpy as jnp
import numpy as np
from jax import lax
from jax.experimental import pallas as pl
from jax.experimental.pallas import tpu as pltpu

F32 = jnp.float32
BF16 = jnp.bfloat16

D_MODEL = 1024
DEPTH = 2
GRID_W = 64
HEAD_DIM = 64
ROPE_HALF = HEAD_DIM // 2
ROPE_AXIS_PAIRS = HEAD_DIM // 4
ROPE_THETA = 10000.0
A_HEADS = 8
A_KV_HEADS = 2
WINDOW = 128
B_HEADS = 8
NA_KH = 8
NA_KW = 16
C_HEADS = 4
C_VDIM = 2 * HEAD_DIM
BRANCH_W = 512
N_EXPERTS = 32
TOP_K = 4
D_FF = D_MODEL
SWIGLU_LIMIT = 7.0
SWIGLU_ALPHA = 1.702
N_MOD = 6
EPS = 1e-6
NEG_INF = -1e30
ATTN_SCALE = HEAD_DIM ** -0.5
LOG2E = 1.4426950408889634
C_QSCALE = ATTN_SCALE * LOG2E

LANES = 128
VMEM_LIMIT = 56 * 1024 * 1024

A_COLS = 1024
B_COLS = 1536
C_COLS = 1536
G_COLS = 3 * D_MODEL
P_COLS = A_COLS + B_COLS + C_COLS + G_COLS

TM = 256
TQ = 256
NA_ROWS_Q = TQ // GRID_W
NA_ROWS_K = NA_ROWS_Q + NA_KH
NA_KEYS = NA_ROWS_K * GRID_W
WIN_KEYS = TQ + 2 * WINDOW
NA_COLS_PER_STEP = 4
C_CHUNK = 512
C_VT_ROWS = C_VDIM + 16
MOE_TM = 512
ROUTE_TM = 1024
D_PACK = D_MODEL // 2


def _cparams(sem):
    return pltpu.CompilerParams(dimension_semantics=sem, vmem_limit_bytes=VMEM_LIMIT)


def _head_masks():
    lane = lax.broadcasted_iota(jnp.int32, (1, LANES), 1)
    lo = jnp.where(lane < HEAD_DIM, 1.0, 0.0).astype(BF16)
    hi = jnp.where(lane >= HEAD_DIM, 1.0, 0.0).astype(BF16)
    return lo, hi


def _split_heads(q):
    lo, hi = _head_masks()
    return jnp.concatenate([q * lo, q * hi], axis=0)


def _merge_heads(pv, t):
    lane = lax.broadcasted_iota(jnp.int32, (t, LANES), 1)
    return jnp.where(lane < HEAD_DIM, pv[:t], pv[t:])


def _pack_rows(x):
    half = x.shape[1] // 2
    lo = lax.bitcast_convert_type(x[:, :half].astype(BF16).astype(F32), jnp.uint32)
    hi = lax.bitcast_convert_type(x[:, half:].astype(BF16).astype(F32), jnp.uint32)
    return (lo >> 16) | (hi & jnp.uint32(0xFFFF0000))


def _unpack_rows(w):
    lo = lax.bitcast_convert_type(w << 16, F32)
    hi = lax.bitcast_convert_type(w & jnp.uint32(0xFFFF0000), F32)
    return lo, hi


def _dot_nt(a, b):
    return lax.dot_general(a, b, (((1,), (1,)), ((), ())), preferred_element_type=F32)


def _dot(a, b):
    return jnp.dot(a, b, preferred_element_type=F32)


def _ada_kernel(c_ref, w_ref, b_ref, o_ref):
    c = c_ref[...]
    s = c * (1.0 / (1.0 + jnp.exp(-c)))
    o_ref[0] = jnp.dot(s, w_ref[0], preferred_element_type=F32,
                       precision=lax.Precision.HIGHEST) + b_ref[0]


def _modulation(cond, w_ada, b_ada):
    tn = 1536
    n = N_MOD * D_MODEL
    return pl.pallas_call(
        _ada_kernel,
        grid=(DEPTH, n // tn),
        in_specs=[
            pl.BlockSpec((8, D_MODEL), lambda l, j: (0, 0)),
            pl.BlockSpec((1, D_MODEL, tn), lambda l, j: (l, 0, j)),
            pl.BlockSpec((1, 1, tn), lambda l, j: (l, 0, j)),
        ],
        out_specs=pl.BlockSpec((1, 8, tn), lambda l, j: (l, 0, j)),
        out_shape=jax.ShapeDtypeStruct((DEPTH, 8, n), F32),
        compiler_params=_cparams(("arbitrary", "arbitrary")),
        name="modulation",
    )(cond, w_ada, b_ada.reshape(DEPTH, 1, n))


def _rms_modulate(x, g, shift, scale):
    ms = jnp.mean(x * x, axis=-1, keepdims=True)
    return (x * lax.rsqrt(ms + EPS) * g) * (1.0 + scale) + shift


def _rope_cols(v, cos, sin_signed):
    t, w = v.shape
    lane = lax.broadcasted_iota(jnp.int32, (t, LANES), 1)
    first_half = (lane & (HEAD_DIM - 1)) < ROPE_HALF
    cols = []
    for c in range(w // LANES):
        xc = v[:, c * LANES:(c + 1) * LANES]
        partner = jnp.where(first_half,
                            pltpu.roll(xc, LANES - ROPE_HALF, axis=1),
                            pltpu.roll(xc, ROPE_HALF, axis=1))
        cols.append(xc * cos + partner * sin_signed)
    return jnp.concatenate(cols, axis=1) if len(cols) > 1 else cols[0]


def _proj_kernel(x_ref, mod_ref, g_ref, w_ref, cos_ref, sin_ref, *out_refs, rope, emit_kv):
    oa_ref, ob_ref, oc_ref, og_ref = out_refs[:4]
    h = _rms_modulate(x_ref[...], g_ref[...], mod_ref[0, 0:1, :], mod_ref[0, 1:2, :]).astype(BF16)
    if rope:
        cos = cos_ref[...]
        sin = sin_ref[...]

    def seg(start, width):
        return _dot(h, w_ref[:, start:start + width])

    def maybe_rope(v):
        return _rope_cols(v, cos, sin) if rope else v

    oa_ref[:, 0:512] = (maybe_rope(seg(0, 512)) * ATTN_SCALE).astype(BF16)
    ka = maybe_rope(seg(512, 256))
    va = seg(768, 256)
    oa_ref[:, 512:768] = ka.astype(BF16)
    oa_ref[:, 768:1024] = va.astype(BF16)
    ob_ref[:, 0:512] = (seg(A_COLS, 512) * ATTN_SCALE).astype(BF16)
    kb = seg(A_COLS + 512, 512)
    vb = seg(A_COLS + 1024, 512)
    ob_ref[:, 512:1024] = kb.astype(BF16)
    ob_ref[:, 1024:1536] = vb.astype(BF16)
    c0 = A_COLS + B_COLS
    oc_ref[:, 0:512] = (maybe_rope(seg(c0, 512)) * C_QSCALE).astype(BF16)
    kc = maybe_rope(seg(c0 + 512, 512))
    vc = seg(c0 + 1024, 512)
    oc_ref[:, 512:1024] = kc.astype(BF16)
    oc_ref[:, 1024:1536] = vc.astype(BF16)
    g0 = c0 + C_COLS
    for j in range(G_COLS // 512):
        gv = seg(g0 + j * 512, 512)
        og_ref[:, j * 512:(j + 1) * 512] = (1.0 / (1.0 + jnp.exp(-gv))).astype(BF16)
    if emit_kv:
        kva_ref, kvb_ref, kvc_ref = out_refs[4:]
        kva_ref[:, 0:256] = ka
        kva_ref[:, 256:512] = va
        kvb_ref[:, 0:512] = kb
        kvb_ref[:, 512:1024] = vb
        kvc_ref[:, 0:512] = kc
        kvc_ref[:, 512:1024] = vc


def _project(x, mod, mod_row0, tiles_per_row, g, w_ext, cos, sin, *, rope, emit_kv):
    n = x.shape[0]
    nt = n // TM
    pos_tiles = cos.shape[0] // TM
    out_shape = [jax.ShapeDtypeStruct((n, A_COLS), BF16), jax.ShapeDtypeStruct((n, B_COLS), BF16),
                 jax.ShapeDtypeStruct((n, C_COLS), BF16), jax.ShapeDtypeStruct((n, G_COLS), BF16)]
    out_specs = [pl.BlockSpec((TM, A_COLS), lambda i: (i, 0)), pl.BlockSpec((TM, B_COLS), lambda i: (i, 0)),
                 pl.BlockSpec((TM, C_COLS), lambda i: (i, 0)), pl.BlockSpec((TM, G_COLS), lambda i: (i, 0))]
    if emit_kv:
        out_shape += [jax.ShapeDtypeStruct((n, 512), F32), jax.ShapeDtypeStruct((n, 1024), F32),
                      jax.ShapeDtypeStruct((n, 1024), F32)]
        out_specs += [pl.BlockSpec((TM, 512), lambda i: (i, 0)), pl.BlockSpec((TM, 1024), lambda i: (i, 0)),
                      pl.BlockSpec((TM, 1024), lambda i: (i, 0))]
    return pl.pallas_call(
        functools.partial(_proj_kernel, rope=rope, emit_kv=emit_kv),
        grid=(nt,),
        in_specs=[
            pl.BlockSpec((TM, D_MODEL), lambda i: (i, 0)),
            pl.BlockSpec((1, N_MOD, D_MODEL), lambda i: (mod_row0 + i // tiles_per_row, 0, 0)),
            pl.BlockSpec((1, D_MODEL), lambda i: (0, 0)),
            pl.BlockSpec((D_MODEL, P_COLS), lambda i: (0, 0)),
            pl.BlockSpec((TM, LANES), lambda i: (i % pos_tiles, 0)),
            pl.BlockSpec((TM, LANES), lambda i: (i % pos_tiles, 0)),
        ],
        out_specs=out_specs,
        out_shape=out_shape,
        compiler_params=_cparams(("arbitrary",)),
        name="project_rope" if rope else "project",
    )(x, mod, g, w_ext, cos, sin)


def _softmax_pv(scores, values, sink=None):
    m = scores[0].max(axis=-1, keepdims=True)
    for s in scores[1:]:
        m = jnp.maximum(m, s.max(axis=-1, keepdims=True))
    if sink is not None:
        m = jnp.maximum(m, sink)
    l = None
    acc = None
    for s, v in zip(scores, values):
        e = jnp.exp(s - m)
        ls = e.sum(axis=-1, keepdims=True)
        pv = _dot(e.astype(BF16), v)
        l = ls if l is None else l + ls
        acc = pv if acc is None else acc + pv
    if sink is not None:
        l = l + jnp.exp(sink - m)
    return acc * (1.0 / l)


def _sink_column(sink_ref, first_head, rows_per_head, n_heads):
    row = lax.broadcasted_iota(jnp.int32, (rows_per_head * n_heads, 1), 0)
    col = jnp.full((rows_per_head * n_heads, 1), sink_ref[first_head], F32)
    for j in range(1, n_heads):
        col = jnp.where(row >= j * rows_per_head, sink_ref[first_head + j], col)
    return col


def _diff_lambda(lq_ref, lambda_init):
    lf = lq_ref[...]
    a = jnp.sum(lf[0:1] * lf[1:2], axis=-1, keepdims=True)
    b = jnp.sum(lf[2:3] * lf[3:4], axis=-1, keepdims=True)
    return jnp.exp(a) - jnp.exp(b) + lambda_init


def _subln(o, g, lambda_init):
    ms = jnp.mean(o * o, axis=-1, keepdims=True)
    return (o * lax.rsqrt(ms + EPS) * g) * (1.0 - lambda_init)


def _ctx_attn_kernel(sink_ref, lq_ref, sg_ref, a_ref, b_ref, c_ref, oa_ref, ob_ref, oc_ref, *, lambda_init):
    t = a_ref.shape[0]
    for g in range(A_KV_HEADS):
        q = a_ref[:, g * 256:(g + 1) * 256]
        lhs = jnp.concatenate([_split_heads(q[:, :LANES]), _split_heads(q[:, LANES:])], axis=0)
        k = a_ref[:, 512 + g * LANES:512 + (g + 1) * LANES]
        v = a_ref[:, 768 + g * LANES:768 + (g + 1) * LANES]
        sink = _sink_column(sink_ref, 4 * g, t, 4)
        o = _softmax_pv([_dot_nt(lhs, k)], [v], sink)
        oa_ref[:, g * 256:g * 256 + LANES] = _merge_heads(o[:2 * t], t).astype(BF16)
        oa_ref[:, g * 256 + LANES:(g + 1) * 256] = _merge_heads(o[2 * t:], t).astype(BF16)
    for c in range(B_HEADS // 2):
        lhs = _split_heads(b_ref[:, c * LANES:(c + 1) * LANES])
        k = b_ref[:, 512 + c * LANES:512 + (c + 1) * LANES]
        v = b_ref[:, 1024 + c * LANES:1024 + (c + 1) * LANES]
        o = _softmax_pv([_dot_nt(lhs, k)], [v])
        ob_ref[:, c * LANES:(c + 1) * LANES] = _merge_heads(o, t).astype(BF16)
    lam = _diff_lambda(lq_ref, lambda_init)
    for h in range(C_HEADS):
        lhs = _split_heads(c_ref[:, h * LANES:(h + 1) * LANES])
        k = c_ref[:, 512 + h * LANES:512 + (h + 1) * LANES]
        v = c_ref[:, 1024 + h * LANES:1024 + (h + 1) * LANES]
        s = _dot_nt(lhs, k)
        m = s.max(axis=-1, keepdims=True)
        e = jnp.exp2(s - m)
        p = e * (1.0 / e.sum(axis=-1, keepdims=True))
        a = (p[:t] - lam * p[t:]).astype(BF16)
        oc_ref[:, h * LANES:(h + 1) * LANES] = _subln(_dot(a, v), sg_ref[...], lambda_init).astype(BF16)


def _ctx_attention(pa, pb, pc, sink, lq, sg, seq, lambda_init):
    n = pa.shape[0]
    smem = pl.BlockSpec(memory_space=pltpu.SMEM)
    return pl.pallas_call(
        functools.partial(_ctx_attn_kernel, lambda_init=lambda_init),
        grid=(n // seq,),
        in_specs=[
            smem,
            pl.BlockSpec((4, HEAD_DIM), lambda b: (0, 0)),
            pl.BlockSpec((1, C_VDIM), lambda b: (0, 0)),
            pl.BlockSpec((seq, A_COLS), lambda b: (b, 0)),
            pl.BlockSpec((seq, B_COLS), lambda b: (b, 0)),
            pl.BlockSpec((seq, C_COLS), lambda b: (b, 0)),
        ],
        out_specs=[pl.BlockSpec((seq, BRANCH_W), lambda b: (b, 0))] * 3,
        out_shape=[jax.ShapeDtypeStruct((n, BRANCH_W), BF16)] * 3,
        compiler_params=_cparams(("arbitrary",)),
        name="context_attention",
    )(sink, lq, sg, pa, pb, pc)


def _win_attn_kernel(sink_ref, q_ref, k_ref, v_ref, kc_ref, vc_ref, o_ref, *, n_lat):
    qi = pl.program_id(1)
    ws = pl.multiple_of(jnp.clip(qi * TQ - WINDOW, 0, n_lat - WIN_KEYS), WINDOW)
    qpos = qi * TQ + lax.broadcasted_iota(jnp.int32, (TQ, WIN_KEYS), 0)
    kpos = ws + lax.broadcasted_iota(jnp.int32, (TQ, WIN_KEYS), 1)
    band = jnp.where(jnp.abs(kpos - qpos) <= WINDOW, 0.0, NEG_INF)
    band = jnp.concatenate([band] * 4, axis=0)
    def scores(g):
        q = q_ref[:, g * 256:(g + 1) * 256]
        lhs = jnp.concatenate([_split_heads(q[:, :LANES]), _split_heads(q[:, LANES:])], axis=0)
        cols = slice(g * LANES, (g + 1) * LANES)
        return [_dot_nt(lhs, k_ref[pl.ds(ws, WIN_KEYS), cols]) + band, _dot_nt(lhs, kc_ref[0, :, cols])]

    s = scores(0)
    for g in range(A_KV_HEADS):
        s_next = scores(g + 1) if g + 1 < A_KV_HEADS else None
        cols = slice(g * LANES, (g + 1) * LANES)
        sink = _sink_column(sink_ref, 4 * g, TQ, 4)
        o = _softmax_pv(s, [v_ref[pl.ds(ws, WIN_KEYS), cols], vc_ref[0, :, cols]], sink)
        o_ref[:, g * 256:g * 256 + LANES] = _merge_heads(o[:2 * TQ], TQ).astype(BF16)
        o_ref[:, g * 256 + LANES:(g + 1) * 256] = _merge_heads(o[2 * TQ:], TQ).astype(BF16)
        s = s_next


def _win_attention(pa, kctx, vctx, sink, n_batch, n_lat):
    nq = n_lat // TQ
    past = kctx.shape[1]
    return pl.pallas_call(
        functools.partial(_win_attn_kernel, n_lat=n_lat),
        grid=(n_batch, nq),
        in_specs=[
            pl.BlockSpec(memory_space=pltpu.SMEM),
            pl.BlockSpec((TQ, 512), lambda b, i: (b * nq + i, 0)),
            pl.BlockSpec((n_lat, 256), lambda b, i: (b, 2)),
            pl.BlockSpec((n_lat, 256), lambda b, i: (b, 3)),
            pl.BlockSpec((1, past, 256), lambda b, i: (b, 0, 0)),
            pl.BlockSpec((1, past, 256), lambda b, i: (b, 0, 0)),
        ],
        out_specs=pl.BlockSpec((TQ, BRANCH_W), lambda b, i: (b * nq + i, 0)),
        out_shape=jax.ShapeDtypeStruct((n_batch * n_lat, BRANCH_W), BF16),
        compiler_params=_cparams(("arbitrary", "arbitrary")),
        name="window_attention",
    )(sink, pa, pa, pa, kctx, vctx)


def _na_window_start(qi, rows):
    return jnp.clip(qi * NA_ROWS_Q - NA_KH // 2, 0, rows - NA_ROWS_K)


def _na_attn_kernel(q_ref, k_ref, v_ref, kc_ref, vc_ref, bias_ref, o_ref, *, rows):
    qi = pl.program_id(2)
    ws = pl.multiple_of(_na_window_start(qi, rows) * GRID_W, GRID_W)
    def scores(c):
        cols = slice(c * LANES, (c + 1) * LANES)
        lhs = _split_heads(q_ref[:, cols])
        s_loc = _dot_nt(lhs, k_ref[pl.ds(ws, NA_KEYS), cols]) + bias_ref[0, 2 * c:2 * c + 2].reshape(2 * TQ, NA_KEYS)
        return [s_loc, _dot_nt(lhs, kc_ref[0, :, cols])]

    s = scores(0)
    for c in range(NA_COLS_PER_STEP):
        s_next = scores(c + 1) if c + 1 < NA_COLS_PER_STEP else None
        cols = slice(c * LANES, (c + 1) * LANES)
        o = _softmax_pv(s, [v_ref[pl.ds(ws, NA_KEYS), cols], vc_ref[0, :, cols]])
        o_ref[:, cols] = _merge_heads(o, TQ).astype(BF16)
        s = s_next


def _na_bias_kernel(rpb_ref, o_ref, *, layer, rows):
    h = pl.program_id(0)
    n_dr = 2 * NA_KH - 1
    n_dc = 2 * NA_KW - 1
    base = (layer * B_HEADS + h) * n_dr * n_dc
    lane = lax.broadcasted_iota(jnp.int32, (GRID_W, LANES), 1)
    qc = lax.broadcasted_iota(jnp.int32, (GRID_W, LANES), 0)
    kc = lane & (GRID_W - 1)
    second = lane >= GRID_W
    cs = jnp.clip(qc - NA_KW // 2, 0, GRID_W - NA_KW)
    col_ok = (kc >= cs) & (kc < cs + NA_KW)
    dc_map = kc - qc + (NA_KW - 1)
    cache = {}

    def pair_tile(dr0, ok0, ok1):
        key = (dr0 if ok0 else None, dr0 + 1 if ok1 else None)
        if key not in cache:
            if not (ok0 or ok1):
                cache[key] = jnp.full((GRID_W, LANES), NEG_INF, F32)
            else:
                acc = jnp.zeros((GRID_W, LANES), F32)
                for dc in range(n_dc):
                    v0 = rpb_ref[base + dr0 * n_dc + dc] if ok0 else 0.0
                    v1 = rpb_ref[base + (dr0 + 1) * n_dc + dc] if ok1 else 0.0
                    acc = jnp.where(dc_map == dc, jnp.where(second, v1, v0), acc)
                if ok0 and ok1:
                    ok = col_ok
                elif ok0:
                    ok = col_ok & jnp.logical_not(second)
                else:
                    ok = col_ok & second
                cache[key] = jnp.where(ok, acc, NEG_INF)
        return cache[key]

    nq = rows // NA_ROWS_Q
    for p, qi in enumerate((0, 1, nq - 1)):
        r0 = qi * NA_ROWS_Q
        ws = min(max(r0 - NA_KH // 2, 0), rows - NA_ROWS_K)
        for i in range(NA_ROWS_Q):
            r = r0 + i
            lo = min(max(r - NA_KH // 2, 0), rows - NA_KH)
            for jp in range(NA_ROWS_K // 2):
                kr = ws + 2 * jp
                ok0 = lo <= kr < lo + NA_KH
                ok1 = lo <= kr + 1 < lo + NA_KH
                o_ref[p, 0, i * GRID_W:(i + 1) * GRID_W, jp * LANES:(jp + 1) * LANES] = pair_tile(
                    kr - r + NA_KH - 1, ok0, ok1)


def _na_bias_table(rpb_flat, layer, rows):
    return pl.pallas_call(
        functools.partial(_na_bias_kernel, layer=layer, rows=rows),
        grid=(B_HEADS,),
        in_specs=[pl.BlockSpec(memory_space=pltpu.SMEM)],
        out_specs=pl.BlockSpec((3, 1, TQ, NA_KEYS), lambda h: (0, h, 0, 0)),
        out_shape=jax.ShapeDtypeStruct((3, B_HEADS, TQ, NA_KEYS), F32),
        compiler_params=_cparams(("arbitrary",)),
        name="neighborhood_bias",
    )(rpb_flat)


def _na_attention(pb, kctx, vctx, bias, n_batch, n_lat):
    nq = n_lat // TQ
    rows = n_lat // GRID_W
    past = kctx.shape[1]
    w = NA_COLS_PER_STEP * LANES
    n_cb = BRANCH_W // w

    def bias_idx(b, c, i):
        return (jnp.where(i == 0, 0, jnp.where(i == nq - 1, 2, 1)), c, 0, 0)

    return pl.pallas_call(
        functools.partial(_na_attn_kernel, rows=rows),
        grid=(n_batch, n_cb, nq),
        in_specs=[
            pl.BlockSpec((TQ, w), lambda b, c, i: (b * nq + i, c)),
            pl.BlockSpec((n_lat, w), lambda b, c, i: (b, n_cb + c)),
            pl.BlockSpec((n_lat, w), lambda b, c, i: (b, 2 * n_cb + c)),
            pl.BlockSpec((1, past, w), lambda b, c, i: (b, 0, c)),
            pl.BlockSpec((1, past, w), lambda b, c, i: (b, 0, c)),
            pl.BlockSpec((1, 2 * NA_COLS_PER_STEP, TQ, NA_KEYS), bias_idx),
        ],
        out_specs=pl.BlockSpec((TQ, w), lambda b, c, i: (b * nq + i, c)),
        out_shape=jax.ShapeDtypeStruct((n_batch * n_lat, BRANCH_W), BF16),
        compiler_params=_cparams(("arbitrary", "arbitrary", "arbitrary")),
        name="neighborhood_attention",
    )(pb, pb, pb, kctx, vctx, bias)


def _diff_attn_kernel(lq_ref, sg_ref, q_ref, k_ref, v_ref, kc_ref, vc_ref, o_ref, vt_s, vtc_s, acc_s, *,
                      n_lat, lambda_init):
    n_chunks = n_lat // C_CHUNK

    @pl.when(pl.program_id(2) == 0)
    def _():
        for j in range(n_chunks):
            vt_s[j, :C_VDIM] = v_ref[j * C_CHUNK:(j + 1) * C_CHUNK, :].astype(F32).T.astype(BF16)
            vt_s[j, C_VDIM:] = jnp.ones((C_VT_ROWS - C_VDIM, C_CHUNK), BF16)
        vtc_s[:C_VDIM] = vc_ref[0].astype(F32).T.astype(BF16)
        vtc_s[C_VDIM:] = jnp.ones((C_VT_ROWS - C_VDIM, vtc_s.shape[1]), BF16)

    lhs = _split_heads(q_ref[...])
    acc_s[...] = jnp.zeros_like(acc_s)

    def scores(j):
        k = k_ref[j * C_CHUNK:(j + 1) * C_CHUNK, :] if j < n_chunks else kc_ref[0]
        return _dot_nt(k, lhs)

    m = jnp.full((1, 2 * TQ), NEG_INF, F32)
    st = scores(0)
    for j in range(n_chunks + 1):
        st_next = scores(j + 1) if j < n_chunks else None
        m_new = jnp.maximum(m, st.max(axis=0, keepdims=True))
        e = jnp.exp2((st - m_new).astype(BF16))
        vt = vt_s[j] if j < n_chunks else vtc_s[...]
        acc_s[...] = jnp.exp2(m - m_new) * acc_s[...] + _dot(vt, e)
        m, st = m_new, st_next
    o = acc_s[:C_VDIM, :] * (1.0 / acc_s[C_VDIM:C_VDIM + 1, :])
    d = o[:, :TQ] - _diff_lambda(lq_ref, lambda_init) * o[:, TQ:]
    ms = jnp.mean(d * d, axis=0, keepdims=True)
    y = (d * lax.rsqrt(ms + EPS)).T
    o_ref[...] = (y * sg_ref[...] * (1.0 - lambda_init)).astype(BF16)


def _diff_attention(pc, kctx, vctx, lq, sg, n_batch, n_lat, lambda_init):
    nq = n_lat // TQ
    past = kctx.shape[1]
    return pl.pallas_call(
        functools.partial(_diff_attn_kernel, n_lat=n_lat, lambda_init=lambda_init),
        grid=(n_batch, C_HEADS, nq),
        in_specs=[
            pl.BlockSpec((4, HEAD_DIM), lambda b, h, i: (0, 0)),
            pl.BlockSpec((1, C_VDIM), lambda b, h, i: (0, 0)),
            pl.BlockSpec((TQ, LANES), lambda b, h, i: (b * nq + i, h)),
            pl.BlockSpec((n_lat, LANES), lambda b, h, i: (b, 4 + h)),
            pl.BlockSpec((n_lat, LANES), lambda b, h, i: (b, 8 + h)),
            pl.BlockSpec((1, past, LANES), lambda b, h, i: (b, 0, h)),
            pl.BlockSpec((1, past, LANES), lambda b, h, i: (b, 0, h)),
        ],
        out_specs=pl.BlockSpec((TQ, LANES), lambda b, h, i: (b * nq + i, h)),
        out_shape=jax.ShapeDtypeStruct((n_batch * n_lat, BRANCH_W), BF16),
        scratch_shapes=[pltpu.VMEM((n_lat // C_CHUNK, C_VT_ROWS, C_CHUNK), BF16), pltpu.VMEM((C_VT_ROWS, past), BF16),
                        pltpu.VMEM((C_VT_ROWS, 2 * TQ), F32)],
        compiler_params=_cparams(("arbitrary", "arbitrary", "arbitrary")),
        name="differential_attention",
    )(lq, sg, pc, pc, pc, kctx, vctx)


def _merge_kernel(x_ref, mod_ref, g_ref, oa_ref, ob_ref, oc_ref, sg_ref, wb_ref, wo_ref, wr_ref, br_ref,
                  xo_ref, h_ref, idx_ref, gate_ref):
    y = None
    for j, o_ref in enumerate((oa_ref, ob_ref, oc_ref)):
        t = sg_ref[:, j * D_MODEL:(j + 1) * D_MODEL].astype(F32) * _dot(o_ref[...], wb_ref[j])
        y = t if y is None else y + t
    x = x_ref[...] + mod_ref[0, 2:3, :] * _dot(y.astype(BF16), wo_ref[...])
    xo_ref[...] = x
    h = _rms_modulate(x, g_ref[...], mod_ref[0, 3:4, :], mod_ref[0, 4:5, :])
    h_ref[...] = _pack_rows(h)
    wr = wr_ref[...]
    h_hi = h.astype(BF16)
    h_lo = (h - h_hi.astype(F32)).astype(BF16)
    w_hi = wr.astype(BF16)
    w_lo = (wr - w_hi.astype(F32)).astype(BF16)
    logits = _dot(h_hi, w_hi) + (_dot(h_hi, w_lo) + _dot(h_lo, w_hi)) + br_ref[...]
    tm = logits.shape[0]
    lane_e = lax.broadcasted_iota(jnp.int32, (tm, N_EXPERTS), 1)
    lane_o = lax.broadcasted_iota(jnp.int32, (tm, LANES), 1)
    idx_out = jnp.zeros((tm, LANES), jnp.int32)
    val_out = jnp.zeros((tm, LANES), F32)
    top = None
    denom = None
    for k in range(TOP_K):
        mx = logits.max(axis=-1, keepdims=True)
        sel = jnp.min(jnp.where(logits == mx, lane_e, N_EXPERTS), axis=-1, keepdims=True)
        logits = jnp.where(lane_e == sel, -jnp.inf, logits)
        if top is None:
            top = mx
        e = jnp.exp(mx - top)
        denom = e if denom is None else denom + e
        idx_out = jnp.where(lane_o == k, sel, idx_out)
        val_out = jnp.where(lane_o == k, e, val_out)
    idx_ref[...] = idx_out
    gate_ref[...] = val_out * (1.0 / denom)


def _merge_route(x, mod, mod_row0, tiles_per_row, g_ffn, oa, ob, oc, sg, wb, wo, wr, br):
    n = x.shape[0]
    row = lambda i: (i, 0)
    fixed2 = lambda i: (0, 0)
    return pl.pallas_call(
        _merge_kernel,
        grid=(n // TM,),
        in_specs=[
            pl.BlockSpec((TM, D_MODEL), row),
            pl.BlockSpec((1, N_MOD, D_MODEL), lambda i: (mod_row0 + i // tiles_per_row, 0, 0)),
            pl.BlockSpec((1, D_MODEL), fixed2),
            pl.BlockSpec((TM, BRANCH_W), row),
            pl.BlockSpec((TM, BRANCH_W), row),
            pl.BlockSpec((TM, BRANCH_W), row),
            pl.BlockSpec((TM, G_COLS), row),
            pl.BlockSpec((3, BRANCH_W, D_MODEL), lambda i: (0, 0, 0)),
            pl.BlockSpec((D_MODEL, D_MODEL), fixed2),
            pl.BlockSpec((D_MODEL, N_EXPERTS), fixed2),
            pl.BlockSpec((1, N_EXPERTS), fixed2),
        ],
        out_specs=[pl.BlockSpec((TM, D_MODEL), row), pl.BlockSpec((TM, D_PACK), row),
                   pl.BlockSpec((TM, LANES), row), pl.BlockSpec((TM, LANES), row)],
        out_shape=[jax.ShapeDtypeStruct((n, D_MODEL), F32), jax.ShapeDtypeStruct((n, D_PACK), jnp.uint32),
                   jax.ShapeDtypeStruct((n, LANES), jnp.int32), jax.ShapeDtypeStruct((n, LANES), F32)],
        compiler_params=_cparams(("arbitrary",)),
        name="merge_route",
    )(x, mod, g_ffn, oa, ob, oc, sg, wb, wo, wr, br)


def _moe_kernel(be_ref, first_ref, valid_ref, x_ref, wgu_ref, bgu_ref, wd_ref, bd_ref, o_ref, wgu_s, wd_s):
    i = pl.program_id(0)

    @pl.when(first_ref[i] == 1)
    def _():
        wgu_s[...] = wgu_ref[0, 0].astype(BF16)
        wd_s[...] = wd_ref[0, 0].astype(BF16)

    @pl.when(valid_ref[i] == 1)
    def _():
        x_lo, x_hi = _unpack_rows(x_ref[...])
        gu = (_dot(x_lo.astype(BF16), wgu_s[:D_PACK, :]) + _dot(x_hi.astype(BF16), wgu_s[D_PACK:, :])
              + bgu_ref[0, 0])
        gate = jnp.minimum(gu[:, :D_FF], SWIGLU_LIMIT)
        lin = jnp.clip(gu[:, D_FF:], -SWIGLU_LIMIT, SWIGLU_LIMIT)
        act = gate * (1.0 / (1.0 + jnp.exp(-SWIGLU_ALPHA * gate))) * (lin + 1.0)
        o_ref[...] = _pack_rows(_dot(act.astype(BF16), wd_s[...]) + bd_ref[0, 0])

    @pl.when(valid_ref[i] == 0)
    def _():
        o_ref[...] = jnp.zeros_like(o_ref)


def _moe_experts(xb, block_e, block_first, block_valid, layer, w_gu, b_gu, w_down, b_down):
    m = xb.shape[0]
    nb = m // MOE_TM
    grid_spec = pltpu.PrefetchScalarGridSpec(
        num_scalar_prefetch=3,
        grid=(nb,),
        in_specs=[
            pl.BlockSpec((MOE_TM, D_PACK), lambda i, be, bf, bv: (i, 0)),
            pl.BlockSpec((1, 1, D_MODEL, 2 * D_FF), lambda i, be, bf, bv: (layer, be[i], 0, 0)),
            pl.BlockSpec((1, 1, 1, 2 * D_FF), lambda i, be, bf, bv: (layer, be[i], 0, 0)),
            pl.BlockSpec((1, 1, D_FF, D_MODEL), lambda i, be, bf, bv: (layer, be[i], 0, 0)),
            pl.BlockSpec((1, 1, 1, D_MODEL), lambda i, be, bf, bv: (layer, be[i], 0, 0)),
        ],
        out_specs=pl.BlockSpec((MOE_TM, D_PACK), lambda i, be, bf, bv: (i, 0)),
        scratch_shapes=[pltpu.VMEM((D_MODEL, 2 * D_FF), BF16), pltpu.VMEM((D_FF, D_MODEL), BF16)],
    )
    return pl.pallas_call(
        _moe_kernel,
        grid_spec=grid_spec,
        out_shape=jax.ShapeDtypeStruct((m, D_PACK), jnp.uint32),
        compiler_params=_cparams(("arbitrary",)),
        name="moe_experts",
    )(block_e, block_first, block_valid, xb, w_gu, b_gu.reshape(DEPTH, N_EXPERTS, 1, 2 * D_FF),
      w_down, b_down.reshape(DEPTH, N_EXPERTS, 1, D_MODEL))


def _moe_block_tables(counts, nb):
    padded = (counts + MOE_TM - 1) // MOE_TM * MOE_TM
    pad_end = jnp.cumsum(padded).astype(jnp.int32)
    blk = jnp.arange(nb, dtype=jnp.int32) * MOE_TM
    block_e = jnp.minimum(jnp.sum((blk[:, None] >= pad_end[None, :]).astype(jnp.int32), axis=1), N_EXPERTS - 1)
    block_valid = (blk < pad_end[-1]).astype(jnp.int32)
    prev = jnp.concatenate([jnp.full((1,), -1, jnp.int32), block_e[:-1]])
    block_first = (block_e != prev).astype(jnp.int32)
    return pad_end, block_e, block_first, block_valid


def _route_kernel(idx_ref, dest_ref, cnt_ref, run_s, start_s):
    p = pl.program_id(0)
    i = pl.program_id(1)
    tm = idx_ref.shape[0]
    lane = lax.broadcasted_iota(jnp.int32, (tm, LANES), 1)
    idx = idx_ref[...]
    sel = [idx[:, k:k + 1] for k in range(TOP_K)]
    onehot = jnp.zeros((tm, LANES), F32)
    for k in range(TOP_K):
        onehot = onehot + jnp.where(lane == sel[k], 1.0, 0.0)
    tile_cnt = jnp.sum(onehot, axis=0, keepdims=True)

    @pl.when((p == 0) & (i == 0))
    def _():
        run_s[...] = jnp.zeros_like(run_s)
        cnt_ref[...] = jnp.zeros_like(cnt_ref)

    @pl.when(p == 0)
    def _():
        run_s[...] += tile_cnt

    @pl.when((p == 1) & (i == 0))
    def _():
        cnt = run_s[...]
        padded = ((cnt.astype(jnp.int32) + (MOE_TM - 1)) // MOE_TM * MOE_TM).astype(F32)
        r = lax.broadcasted_iota(jnp.int32, (LANES, LANES), 0)
        c = lax.broadcasted_iota(jnp.int32, (LANES, LANES), 1)
        before = jnp.where(r < c, 1.0, 0.0)
        start = jnp.dot(jnp.broadcast_to(padded, (8, LANES)), before, preferred_element_type=F32,
                        precision=lax.Precision.HIGHEST)
        start_s[...] = start[0:1]
        cnt_ref[...] = jnp.broadcast_to(cnt, cnt_ref.shape)
        run_s[...] = jnp.zeros_like(run_s)

    @pl.when(p == 1)
    def _():
        r = lax.broadcasted_iota(jnp.int32, (tm, tm), 0)
        c = lax.broadcasted_iota(jnp.int32, (tm, tm), 1)
        earlier = jnp.where(c < r, 1.0, 0.0).astype(BF16)
        pos = _dot(earlier, onehot.astype(BF16)) + run_s[...] + start_s[...]
        out = jnp.zeros((tm, LANES), F32)
        for k in range(TOP_K):
            d = jnp.sum(jnp.where(lane == sel[k], pos, 0.0), axis=-1, keepdims=True)
            out = jnp.where(lane == k, d, out)
        dest_ref[...] = out.astype(jnp.int32)
        run_s[...] += tile_cnt


def _route(idx_slab):
    n = idx_slab.shape[0]
    return pl.pallas_call(
        _route_kernel,
        grid=(2, n // ROUTE_TM),
        in_specs=[pl.BlockSpec((ROUTE_TM, LANES), lambda p, i: (i, 0))],
        out_specs=[pl.BlockSpec((ROUTE_TM, LANES), lambda p, i: (i * p, 0)),
                   pl.BlockSpec((8, LANES), lambda p, i: (0, 0))],
        out_shape=[jax.ShapeDtypeStruct((n, LANES), jnp.int32), jax.ShapeDtypeStruct((8, LANES), F32)],
        scratch_shapes=[pltpu.VMEM((1, LANES), F32), pltpu.VMEM((1, LANES), F32)],
        compiler_params=_cparams(("arbitrary", "arbitrary")),
        name="route",
    )(idx_slab)


def _row_copy(src_ref, src_row, dst_ref, dst_row, sem):
    return pltpu.make_async_copy(src_ref.at[pl.ds(src_row, 1)], dst_ref.at[pl.ds(dst_row, 1)], sem)


def _dispatch_kernel(pe_ref, cnt_ref, dest_ref, hp_ref, hs_ref, xb_ref, zero_s, sem, *, tiles_p, first_tail_block):
    i = pl.program_id(0)
    tm = hp_ref.shape[0]
    nb = xb_ref.shape[0] // MOE_TM

    def zero_block(row0):
        return pltpu.make_async_copy(zero_s, xb_ref.at[pl.ds(pl.multiple_of(row0, MOE_TM), MOE_TM)], sem)

    @pl.when(i == 0)
    def _():
        zero_s[...] = jnp.zeros_like(zero_s)
        for start in (True, False):
            for e in range(N_EXPERTS):
                @pl.when(cnt_ref[e] > 0)
                def _():
                    cp = zero_block(pe_ref[e] - MOE_TM)
                    cp.start() if start else cp.wait()
            for b in range(first_tail_block, nb):
                @pl.when(b * MOE_TM >= pe_ref[N_EXPERTS - 1])
                def _():
                    cp = zero_block(b * MOE_TM)
                    cp.start() if start else cp.wait()

    def scatter_rows(h_ref):
        def issue(t, carry):
            for k in range(TOP_K):
                _row_copy(h_ref, t, xb_ref, dest_ref[0, 0, t * TOP_K + k], sem).start(priority=k % 2)
            return carry

        def drain(t, carry):
            for k in range(TOP_K):
                _row_copy(h_ref, 0, xb_ref, 0, sem).wait()
            return carry

        lax.fori_loop(0, tm, issue, 0)
        lax.fori_loop(0, tm, drain, 0)

    @pl.when(i < tiles_p)
    def _():
        scatter_rows(hp_ref)

    @pl.when(i >= tiles_p)
    def _():
        scatter_rows(hs_ref)


def _dispatch(pad_end, counts, dest, hp, hs, m):
    tiles_p = hp.shape[0] // TM
    tiles_s = hs.shape[0] // TM
    grid_spec = pltpu.PrefetchScalarGridSpec(
        num_scalar_prefetch=2,
        grid=(tiles_p + tiles_s,),
        in_specs=[
            pl.BlockSpec((1, 1, TM * TOP_K), lambda i, pe, cn: (i, 0, 0), memory_space=pltpu.SMEM),
            pl.BlockSpec((TM, D_PACK), lambda i, pe, cn: (jnp.minimum(i, tiles_p - 1), 0)),
            pl.BlockSpec((TM, D_PACK), lambda i, pe, cn: (jnp.maximum(i - tiles_p, 0), 0)),
        ],
        out_specs=pl.BlockSpec(memory_space=pl.ANY),
        scratch_shapes=[pltpu.VMEM((MOE_TM, D_PACK), jnp.uint32), pltpu.SemaphoreType.DMA(())],
    )
    return pl.pallas_call(
        functools.partial(_dispatch_kernel, tiles_p=tiles_p, first_tail_block=m // MOE_TM - N_EXPERTS),
        grid_spec=grid_spec,
        out_shape=jax.ShapeDtypeStruct((m, D_PACK), jnp.uint32),
        compiler_params=_cparams(("arbitrary",)),
        name="dispatch",
    )(pad_end, counts, dest, hp, hs)


def _combine_kernel(dest_ref, x_ref, mod_ref, gate_ref, gfin_ref, yb_ref, o_ref, buf, sem, *, final):
    tm = x_ref.shape[0]

    def issue(t, carry):
        for k in range(TOP_K):
            _row_copy(yb_ref, dest_ref[0, 0, t * TOP_K + k], buf.at[k], t, sem).start(priority=k % 2)
        return carry

    def drain(t, carry):
        for k in range(TOP_K):
            _row_copy(yb_ref, 0, buf.at[k], 0, sem).wait()
        return carry

    lax.fori_loop(0, tm, issue, 0)
    lax.fori_loop(0, tm, drain, 0)
    g = gate_ref[...]
    acc_lo = acc_hi = None
    for k in range(TOP_K):
        y_lo, y_hi = _unpack_rows(buf[k])
        acc_lo = g[:, k:k + 1] * y_lo if acc_lo is None else acc_lo + g[:, k:k + 1] * y_lo
        acc_hi = g[:, k:k + 1] * y_hi if acc_hi is None else acc_hi + g[:, k:k + 1] * y_hi
    x = x_ref[...] + mod_ref[0, 5:6, :] * jnp.concatenate([acc_lo, acc_hi], axis=1)
    if final:
        ms = jnp.mean(x * x, axis=-1, keepdims=True)
        x = x * lax.rsqrt(ms + EPS) * gfin_ref[...]
    o_ref[...] = x


def _combine(dest, x, mod, mod_row0, tiles_per_row, gates, g_final, yb, *, final):
    n = x.shape[0]
    return pl.pallas_call(
        functools.partial(_combine_kernel, final=final),
        grid=(n // TM,),
        in_specs=[
            pl.BlockSpec((1, 1, TM * TOP_K), lambda i: (i, 0, 0), memory_space=pltpu.SMEM),
            pl.BlockSpec((TM, D_MODEL), lambda i: (i, 0)),
            pl.BlockSpec((1, N_MOD, D_MODEL), lambda i: (mod_row0 + i // tiles_per_row, 0, 0)),
            pl.BlockSpec((TM, LANES), lambda i: (i, 0)),
            pl.BlockSpec((1, D_MODEL), lambda i: (0, 0)),
            pl.BlockSpec(memory_space=pl.ANY),
        ],
        out_specs=pl.BlockSpec((TM, D_MODEL), lambda i: (i, 0)),
        out_shape=jax.ShapeDtypeStruct((n, D_MODEL), F32),
        scratch_shapes=[pltpu.VMEM((TOP_K, TM, D_PACK), jnp.uint32), pltpu.SemaphoreType.DMA(())],
        compiler_params=_cparams(("arbitrary",)),
        name="combine_final" if final else "combine",
    )(dest, x, mod, gates, g_final, yb)


def _rope_tables(n_tokens):
    t = np.arange(n_tokens)
    row = (t // GRID_W).astype(np.float32)
    col = (t % GRID_W).astype(np.float32)
    inv = jnp.asarray(ROPE_THETA, F32) ** (-jnp.arange(ROPE_AXIS_PAIRS, dtype=F32) / ROPE_AXIS_PAIRS)
    ang = jnp.concatenate([jnp.asarray(row)[:, None] * inv, jnp.asarray(col)[:, None] * inv], axis=-1)
    cos, sin = jnp.cos(ang), jnp.sin(ang)
    cos = jnp.tile(cos, (1, LANES // ROPE_HALF))
    sin = jnp.tile(jnp.concatenate([-sin, sin], axis=-1), (1, LANES // HEAD_DIM))
    return cos, sin


def _extended_w_in(w):
    qa, ka, va, rest = w[:, :512], w[:, 512:640], w[:, 640:768], w[:, 768:]

    def dup(t):
        t = t.reshape(D_MODEL, A_KV_HEADS, 1, HEAD_DIM)
        return jnp.broadcast_to(t, (D_MODEL, A_KV_HEADS, 2, HEAD_DIM)).reshape(D_MODEL, 4 * HEAD_DIM)

    return jnp.concatenate([qa, dup(ka), dup(va), rest], axis=-1).astype(BF16)


def _dup_kv_heads(t):
    b, p = t.shape[:2]
    t = jnp.broadcast_to(t[:, :, :, None, :], (b, p, A_KV_HEADS, 2, HEAD_DIM))
    return t.reshape(b, p, 4 * HEAD_DIM).astype(BF16)


def kernel(x_prompt, x_sample, cache_k_a, cache_v_a, cache_k_b, cache_v_b, cache_k_c, cache_v_c, c, c_ctx,
           w_ada, b_ada, norm_attn, norm_ffn, w_in, sinks, rpb, lambda_qk, subln, w_branch, w_out,
           w_router, b_router, w_gu, b_gu, w_down, b_down, norm_final):
    bsz, seq = x_prompt.shape[:2]
    bsz_d, n_lat = x_sample.shape[:2]
    past = cache_k_a.shape[2]
    n_p = bsz * seq
    n_s = bsz_d * n_lat
    assert seq % TM == 0 and n_lat % TQ == 0 and n_lat % C_CHUNK == 0 and bsz_d + 1 <= 8
    assert (n_p + n_s) % ROUTE_TM == 0

    cond = jnp.zeros((8, D_MODEL), F32).at[0].set(c_ctx).at[1:1 + bsz_d].set(c)
    mod_all = _modulation(cond, w_ada, b_ada).reshape(DEPTH, 8, N_MOD, D_MODEL)
    cos, sin = _rope_tables(n_lat)
    g_final = norm_final[None, :]
    n_assign = (n_p + n_s) * TOP_K
    m_rows = (n_assign + N_EXPERTS * (MOE_TM - 1) + MOE_TM - 1) // MOE_TM * MOE_TM

    xp = x_prompt.reshape(n_p, D_MODEL)
    xs = x_sample.reshape(n_s, D_MODEL)
    kv_out = []
    for l in range(DEPTH):
        lambda_init = 0.8 - 0.6 * math.exp(-0.3 * l)
        mod = mod_all[l]
        w_ext = _extended_w_in(w_in[l])
        g_attn = norm_attn[l][None, :]
        g_ffn = norm_ffn[l][None, :]
        sg = subln[l][None, :]
        wb = w_branch[l].astype(BF16)
        wo = w_out[l].astype(BF16)
        br = b_router[l][None, :]

        pa, pb, pc, pg, kva, kvb, kvc = _project(xp, mod, 0, n_p // TM, g_attn, w_ext, cos, sin,
                                                 rope=False, emit_kv=True)
        oa, ob, oc = _ctx_attention(pa, pb, pc, sinks[l], lambda_qk[l], sg, seq, lambda_init)
        xp_mid, hp, idx_p, gate_p = _merge_route(xp, mod, 0, n_p // TM, g_ffn, oa, ob, oc, pg, wb, wo,
                                                 w_router[l], br)
        kv_out.append((kva, kvb, kvc))

        pa, pb, pc, pg = _project(xs, mod, 1, n_lat // TM, g_attn, w_ext, cos, sin, rope=True, emit_kv=False)
        oa = _win_attention(pa, _dup_kv_heads(cache_k_a[:, l]), _dup_kv_heads(cache_v_a[:, l]), sinks[l],
                            bsz_d, n_lat)
        ob = _na_attention(pb, cache_k_b[:, l].reshape(bsz_d, past, -1).astype(BF16),
                           cache_v_b[:, l].reshape(bsz_d, past, -1).astype(BF16),
                           _na_bias_table(rpb.reshape(-1), l, n_lat // GRID_W), bsz_d, n_lat)
        oc = _diff_attention(pc, cache_k_c[:, l].reshape(bsz_d, past, -1).astype(BF16),
                             cache_v_c[:, l].reshape(bsz_d, past, -1).astype(BF16),
                             lambda_qk[l], sg, bsz_d, n_lat, lambda_init)
        xs_mid, hs, idx_s, gate_s = _merge_route(xs, mod, 1, n_lat // TM, g_ffn, oa, ob, oc, pg, wb, wo,
                                                 w_router[l], br)

        last = l == DEPTH - 1
        dest_slab, cnt = _route(jnp.concatenate([idx_p, idx_s], axis=0))
        counts = cnt[0, :N_EXPERTS].astype(jnp.int32)
        pad_end, block_e, block_first, block_valid = _moe_block_tables(counts, m_rows // MOE_TM)
        dest = dest_slab[:, :TOP_K].reshape(-1, 1, TM * TOP_K)
        dest_p, dest_s = dest[:n_p // TM], dest[n_p // TM:]
        xb = _dispatch(pad_end, counts, dest, hp, hs, m_rows)
        yb = _moe_experts(xb, block_e, block_first, block_valid, l, w_gu, b_gu, w_down, b_down)
        xp = _combine(dest_p, xp_mid, mod, 0, n_p // TM, gate_p, g_final, yb, final=last)
        xs = _combine(dest_s, xs_mid, mod, 1, n_lat // TM, gate_s, g_final, yb, final=last)

    y_prompt = xp.reshape(bsz, seq, D_MODEL)
    y_sample = xs.reshape(bsz_d, n_lat, D_MODEL)

    def stack(pick):
        return jnp.stack([pick(kv) for kv in kv_out], axis=1)

    def heads_a(t):
        return t.reshape(bsz, seq, A_KV_HEADS, 2, HEAD_DIM)[:, :, :, 0]

    new_k_a = stack(lambda kv: heads_a(kv[0][:, 0:256]))
    new_v_a = stack(lambda kv: heads_a(kv[0][:, 256:512]))
    new_k_b = stack(lambda kv: kv[1][:, 0:512].reshape(bsz, seq, B_HEADS, HEAD_DIM))
    new_v_b = stack(lambda kv: kv[1][:, 512:1024].reshape(bsz, seq, B_HEADS, HEAD_DIM))
    new_k_c = stack(lambda kv: kv[2][:, 0:512].reshape(bsz, seq, C_HEADS, 2 * HEAD_DIM))
    new_v_c = stack(lambda kv: kv[2][:, 512:1024].reshape(bsz, seq, C_HEADS, C_VDIM))
    return (y_prompt, y_sample, new_k_a, new_v_a, new_k_b, new_v_b, new_k_c, new_v_c)
```

```python
import functools
import math

import jax
import jax.numpy as jnp
import numpy as np
from jax import lax
from jax.experimental import pallas as pl
from jax.experimental.pallas import tpu as pltpu

F32 = jnp.float32
BF16 = jnp.bfloat16

D_MODEL = 1024
DEPTH = 2
GRID_W = 64
HEAD_DIM = 64
ROPE_HALF = HEAD_DIM // 2
ROPE_AXIS_PAIRS = HEAD_DIM // 4
ROPE_THETA = 10000.0
A_HEADS = 8
A_KV_HEADS = 2
WINDOW = 128
B_HEADS = 8
NA_KH = 8
NA_KW = 16
C_HEADS = 4
C_VDIM = 2 * HEAD_DIM
BRANCH_W = 512
N_EXPERTS = 32
TOP_K = 4
D_FF = D_MODEL
SWIGLU_LIMIT = 7.0
SWIGLU_ALPHA = 1.702
N_MOD = 6
EPS = 1e-6
NEG_INF = -1e30
ATTN_SCALE = HEAD_DIM ** -0.5
LOG2E = 1.4426950408889634
C_QSCALE = ATTN_SCALE * LOG2E

LANES = 128
VMEM_LIMIT = 56 * 1024 * 1024

A_COLS = 1024
B_COLS = 1536
C_COLS = 1536
G_COLS = 3 * D_MODEL
P_COLS = A_COLS + B_COLS + C_COLS + G_COLS

TM = 256
TQ = 256
NA_ROWS_Q = TQ // GRID_W
NA_ROWS_K = NA_ROWS_Q + NA_KH
NA_KEYS = NA_ROWS_K * GRID_W
WIN_KEYS = TQ + 2 * WINDOW
NA_COLS_PER_STEP = 4
C_TQ = 512
C_CHUNK = 256
C_VT_ROWS = C_VDIM + 16
MOE_TM = 512
ROUTE_TM = 1024
MERGE_TM = 512
MERGE_ROWS = 256
D_PACK = D_MODEL // 2


def _cparams(sem):
    return pltpu.CompilerParams(dimension_semantics=sem, vmem_limit_bytes=VMEM_LIMIT)


def _head_masks():
    lane = lax.broadcasted_iota(jnp.int32, (1, LANES), 1)
    lo = jnp.where(lane < HEAD_DIM, 1.0, 0.0).astype(BF16)
    hi = jnp.where(lane >= HEAD_DIM, 1.0, 0.0).astype(BF16)
    return lo, hi


def _split_heads(q):
    lo, hi = _head_masks()
    return jnp.concatenate([q * lo, q * hi], axis=0)


def _merge_heads(pv, t):
    lane = lax.broadcasted_iota(jnp.int32, (t, LANES), 1)
    return jnp.where(lane < HEAD_DIM, pv[:t], pv[t:])


def _pack_rows(x):
    half = x.shape[1] // 2
    lo = lax.bitcast_convert_type(x[:, :half].astype(BF16).astype(F32), jnp.uint32)
    hi = lax.bitcast_convert_type(x[:, half:].astype(BF16).astype(F32), jnp.uint32)
    return (lo >> 16) | (hi & jnp.uint32(0xFFFF0000))


def _unpack_rows(w):
    lo = lax.bitcast_convert_type(w << 16, F32)
    hi = lax.bitcast_convert_type(w & jnp.uint32(0xFFFF0000), F32)
    return lo, hi


def _dot_nt(a, b):
    return lax.dot_general(a, b, (((1,), (1,)), ((), ())), preferred_element_type=F32)


def _dot(a, b):
    return jnp.dot(a, b, preferred_element_type=F32)


def _ada_kernel(c_ref, w_ref, b_ref, o_ref):
    c = c_ref[...]
    s = c * (1.0 / (1.0 + jnp.exp(-c)))
    o_ref[0] = jnp.dot(s, w_ref[0], preferred_element_type=F32,
                       precision=lax.Precision.HIGHEST) + b_ref[0]


def _modulation(cond, w_ada, b_ada):
    tn = 1536
    n = N_MOD * D_MODEL
    return pl.pallas_call(
        _ada_kernel,
        grid=(DEPTH, n // tn),
        in_specs=[
            pl.BlockSpec((8, D_MODEL), lambda l, j: (0, 0)),
            pl.BlockSpec((1, D_MODEL, tn), lambda l, j: (l, 0, j)),
            pl.BlockSpec((1, 1, tn), lambda l, j: (l, 0, j)),
        ],
        out_specs=pl.BlockSpec((1, 8, tn), lambda l, j: (l, 0, j)),
        out_shape=jax.ShapeDtypeStruct((DEPTH, 8, n), F32),
        compiler_params=_cparams(("arbitrary", "arbitrary")),
        name="modulation",
    )(cond, w_ada, b_ada.reshape(DEPTH, 1, n))


def _rms_modulate(x, g, shift, scale):
    ms = jnp.mean(x * x, axis=-1, keepdims=True)
    return (x * lax.rsqrt(ms + EPS) * g) * (1.0 + scale) + shift


def _rope_cols(v, cos, sin_signed):
    t, w = v.shape
    lane = lax.broadcasted_iota(jnp.int32, (t, LANES), 1)
    first_half = (lane & (HEAD_DIM - 1)) < ROPE_HALF
    cols = []
    for c in range(w // LANES):
        xc = v[:, c * LANES:(c + 1) * LANES]
        partner = jnp.where(first_half,
                            pltpu.roll(xc, LANES - ROPE_HALF, axis=1),
                            pltpu.roll(xc, ROPE_HALF, axis=1))
        cols.append(xc * cos + partner * sin_signed)
    return jnp.concatenate(cols, axis=1) if len(cols) > 1 else cols[0]


def _proj_kernel(x_ref, mod_ref, g_ref, w_ref, cos_ref, sin_ref, *out_refs, rope, emit_kv):
    oa_ref, ob_ref, oc_ref, og_ref = out_refs[:4]
    h = _rms_modulate(x_ref[...], g_ref[...], mod_ref[0, 0:1, :], mod_ref[0, 1:2, :]).astype(BF16)
    if rope:
        cos = cos_ref[...]
        sin = sin_ref[...]

    def seg(start, width):
        return _dot(h, w_ref[:, start:start + width])

    def maybe_rope(v):
        return _rope_cols(v, cos, sin) if rope else v

    oa_ref[:, 0:512] = (maybe_rope(seg(0, 512)) * ATTN_SCALE).astype(BF16)
    ka = maybe_rope(seg(512, 256))
    va = seg(768, 256)
    oa_ref[:, 512:768] = ka.astype(BF16)
    oa_ref[:, 768:1024] = va.astype(BF16)
    ob_ref[:, 0:512] = (seg(A_COLS, 512) * ATTN_SCALE).astype(BF16)
    kb = seg(A_COLS + 512, 512)
    vb = seg(A_COLS + 1024, 512)
    ob_ref[:, 512:1024] = kb.astype(BF16)
    ob_ref[:, 1024:1536] = vb.astype(BF16)
    c0 = A_COLS + B_COLS
    oc_ref[:, 0:512] = (maybe_rope(seg(c0, 512)) * C_QSCALE).astype(BF16)
    kc = maybe_rope(seg(c0 + 512, 512))
    vc = seg(c0 + 1024, 512)
    oc_ref[:, 512:1024] = kc.astype(BF16)
    oc_ref[:, 1024:1536] = vc.astype(BF16)
    g0 = c0 + C_COLS
    for j in range(G_COLS // 512):
        gv = seg(g0 + j * 512, 512)
        og_ref[:, j * 512:(j + 1) * 512] = (1.0 / (1.0 + jnp.exp(-gv))).astype(BF16)
    if emit_kv:
        kva_ref, kvb_ref, kvc_ref = out_refs[4:]
        kva_ref[:, 0:256] = ka
        kva_ref[:, 256:512] = va
        kvb_ref[:, 0:512] = kb
        kvb_ref[:, 512:1024] = vb
        kvc_ref[:, 0:512] = kc
        kvc_ref[:, 512:1024] = vc


def _project(x, mod, mod_row0, tiles_per_row, g, w_ext, cos, sin, *, rope, emit_kv):
    n = x.shape[0]
    nt = n // TM
    pos_tiles = cos.shape[0] // TM
    out_shape = [jax.ShapeDtypeStruct((n, A_COLS), BF16), jax.ShapeDtypeStruct((n, B_COLS), BF16),
                 jax.ShapeDtypeStruct((n, C_COLS), BF16), jax.ShapeDtypeStruct((n, G_COLS), BF16)]
    out_specs = [pl.BlockSpec((TM, A_COLS), lambda i: (i, 0)), pl.BlockSpec((TM, B_COLS), lambda i: (i, 0)),
                 pl.BlockSpec((TM, C_COLS), lambda i: (i, 0)), pl.BlockSpec((TM, G_COLS), lambda i: (i, 0))]
    if emit_kv:
        out_shape += [jax.ShapeDtypeStruct((n, 512), F32), jax.ShapeDtypeStruct((n, 1024), F32),
                      jax.ShapeDtypeStruct((n, 1024), F32)]
        out_specs += [pl.BlockSpec((TM, 512), lambda i: (i, 0)), pl.BlockSpec((TM, 1024), lambda i: (i, 0)),
                      pl.BlockSpec((TM, 1024), lambda i: (i, 0))]
    return pl.pallas_call(
        functools.partial(_proj_kernel, rope=rope, emit_kv=emit_kv),
        grid=(nt,),
        in_specs=[
            pl.BlockSpec((TM, D_MODEL), lambda i: (i, 0)),
            pl.BlockSpec((1, N_MOD, D_MODEL), lambda i: (mod_row0 + i // tiles_per_row, 0, 0)),
            pl.BlockSpec((1, D_MODEL), lambda i: (0, 0)),
            pl.BlockSpec((D_MODEL, P_COLS), lambda i: (0, 0)),
            pl.BlockSpec((TM, LANES), lambda i: (i % pos_tiles, 0)),
            pl.BlockSpec((TM, LANES), lambda i: (i % pos_tiles, 0)),
        ],
        out_specs=out_specs,
        out_shape=out_shape,
        compiler_params=_cparams(("arbitrary",)),
        name="project_rope" if rope else "project",
    )(x, mod, g, w_ext, cos, sin)


def _softmax_pv(scores, values, sink=None):
    m = scores[0].max(axis=-1, keepdims=True)
    for s in scores[1:]:
        m = jnp.maximum(m, s.max(axis=-1, keepdims=True))
    if sink is not None:
        m = jnp.maximum(m, sink)
    l = None
    acc = None
    for s, v in zip(scores, values):
        e = jnp.exp(s - m)
        ls = e.sum(axis=-1, keepdims=True)
        pv = _dot(e.astype(BF16), v)
        l = ls if l is None else l + ls
        acc = pv if acc is None else acc + pv
    if sink is not None:
        l = l + jnp.exp(sink - m)
    return acc * (1.0 / l)


def _sink_column(sink_ref, first_head, rows_per_head, n_heads):
    row = lax.broadcasted_iota(jnp.int32, (rows_per_head * n_heads, 1), 0)
    col = jnp.full((rows_per_head * n_heads, 1), sink_ref[first_head], F32)
    for j in range(1, n_heads):
        col = jnp.where(row >= j * rows_per_head, sink_ref[first_head + j], col)
    return col


def _diff_lambda(lq_ref, lambda_init):
    lf = lq_ref[...]
    a = jnp.sum(lf[0:1] * lf[1:2], axis=-1, keepdims=True)
    b = jnp.sum(lf[2:3] * lf[3:4], axis=-1, keepdims=True)
    return jnp.exp(a) - jnp.exp(b) + lambda_init


def _subln(o, g, lambda_init):
    ms = jnp.mean(o * o, axis=-1, keepdims=True)
    return (o * lax.rsqrt(ms + EPS) * g) * (1.0 - lambda_init)


def _ctx_attn_kernel(sink_ref, lq_ref, sg_ref, a_ref, b_ref, c_ref, oa_ref, ob_ref, oc_ref, *, lambda_init):
    t = a_ref.shape[0]
    lam = _diff_lambda(lq_ref, lambda_init)
    chains = []

    for g in range(A_KV_HEADS):
        def scores_a(g=g):
            q = a_ref[:, g * 256:(g + 1) * 256]
            lhs = jnp.concatenate([_split_heads(q[:, :LANES]), _split_heads(q[:, LANES:])], axis=0)
            return _dot_nt(lhs, a_ref[:, 512 + g * LANES:512 + (g + 1) * LANES])

        def finish_a(s, g=g):
            v = a_ref[:, 768 + g * LANES:768 + (g + 1) * LANES]
            o = _softmax_pv([s], [v], _sink_column(sink_ref, 4 * g, t, 4))
            oa_ref[:, g * 256:g * 256 + LANES] = _merge_heads(o[:2 * t], t).astype(BF16)
            oa_ref[:, g * 256 + LANES:(g + 1) * 256] = _merge_heads(o[2 * t:], t).astype(BF16)

        chains.append((scores_a, finish_a))

    for c in range(B_HEADS // 2):
        def scores_b(c=c):
            return _dot_nt(_split_heads(b_ref[:, c * LANES:(c + 1) * LANES]),
                           b_ref[:, 512 + c * LANES:512 + (c + 1) * LANES])

        def finish_b(s, c=c):
            o = _softmax_pv([s], [b_ref[:, 1024 + c * LANES:1024 + (c + 1) * LANES]])
            ob_ref[:, c * LANES:(c + 1) * LANES] = _merge_heads(o, t).astype(BF16)

        chains.append((scores_b, finish_b))

    for h in range(C_HEADS):
        def scores_c(h=h):
            return _dot_nt(_split_heads(c_ref[:, h * LANES:(h + 1) * LANES]),
                           c_ref[:, 512 + h * LANES:512 + (h + 1) * LANES])

        def finish_c(s, h=h):
            m = s.max(axis=-1, keepdims=True)
            e = jnp.exp2(s - m)
            p = e * (1.0 / e.sum(axis=-1, keepdims=True))
            a = (p[:t] - lam * p[t:]).astype(BF16)
            o = _dot(a, c_ref[:, 1024 + h * LANES:1024 + (h + 1) * LANES])
            oc_ref[:, h * LANES:(h + 1) * LANES] = _subln(o, sg_ref[...], lambda_init).astype(BF16)

        chains.append((scores_c, finish_c))

    s = chains[0][0]()
    for i, (_, finish) in enumerate(chains):
        s_next = chains[i + 1][0]() if i + 1 < len(chains) else None
        finish(s)
        s = s_next


def _ctx_attention(pa, pb, pc, sink, lq, sg, seq, lambda_init):
    n = pa.shape[0]
    smem = pl.BlockSpec(memory_space=pltpu.SMEM)
    return pl.pallas_call(
        functools.partial(_ctx_attn_kernel, lambda_init=lambda_init),
        grid=(n // seq,),
        in_specs=[
            smem,
            pl.BlockSpec((4, HEAD_DIM), lambda b: (0, 0)),
            pl.BlockSpec((1, C_VDIM), lambda b: (0, 0)),
            pl.BlockSpec((seq, A_COLS), lambda b: (b, 0)),
            pl.BlockSpec((seq, B_COLS), lambda b: (b, 0)),
            pl.BlockSpec((seq, C_COLS), lambda b: (b, 0)),
        ],
        out_specs=[pl.BlockSpec((seq, BRANCH_W), lambda b: (b, 0))] * 3,
        out_shape=[jax.ShapeDtypeStruct((n, BRANCH_W), BF16)] * 3,
        compiler_params=_cparams(("arbitrary",)),
        name="context_attention",
    )(sink, lq, sg, pa, pb, pc)


def _win_attn_kernel(sink_ref, q_ref, k_ref, v_ref, kc_ref, vc_ref, o_ref, *, n_lat):
    qi = pl.program_id(1)
    ws = pl.multiple_of(jnp.clip(qi * TQ - WINDOW, 0, n_lat - WIN_KEYS), WINDOW)
    qpos = qi * TQ + lax.broadcasted_iota(jnp.int32, (TQ, WIN_KEYS), 0)
    kpos = ws + lax.broadcasted_iota(jnp.int32, (TQ, WIN_KEYS), 1)
    band = jnp.where(jnp.abs(kpos - qpos) <= WINDOW, 0.0, NEG_INF)
    band = jnp.concatenate([band] * 4, axis=0)
    def scores(g):
        q = q_ref[:, g * 256:(g + 1) * 256]
        lhs = jnp.concatenate([_split_heads(q[:, :LANES]), _split_heads(q[:, LANES:])], axis=0)
        cols = slice(g * LANES, (g + 1) * LANES)
        return [_dot_nt(lhs, k_ref[pl.ds(ws, WIN_KEYS), cols]) + band, _dot_nt(lhs, kc_ref[0, :, cols])]

    s = scores(0)
    for g in range(A_KV_HEADS):
        s_next = scores(g + 1) if g + 1 < A_KV_HEADS else None
        cols = slice(g * LANES, (g + 1) * LANES)
        sink = _sink_column(sink_ref, 4 * g, TQ, 4)
        o = _softmax_pv(s, [v_ref[pl.ds(ws, WIN_KEYS), cols], vc_ref[0, :, cols]], sink)
        o_ref[:, g * 256:g * 256 + LANES] = _merge_heads(o[:2 * TQ], TQ).astype(BF16)
        o_ref[:, g * 256 + LANES:(g + 1) * 256] = _merge_heads(o[2 * TQ:], TQ).astype(BF16)
        s = s_next


def _win_attention(pa, kctx, vctx, sink, n_batch, n_lat):
    nq = n_lat // TQ
    past = kctx.shape[1]
    return pl.pallas_call(
        functools.partial(_win_attn_kernel, n_lat=n_lat),
        grid=(n_batch, nq),
        in_specs=[
            pl.BlockSpec(memory_space=pltpu.SMEM),
            pl.BlockSpec((TQ, 512), lambda b, i: (b * nq + i, 0)),
            pl.BlockSpec((n_lat, 256), lambda b, i: (b, 2)),
            pl.BlockSpec((n_lat, 256), lambda b, i: (b, 3)),
            pl.BlockSpec((1, past, 256), lambda b, i: (b, 0, 0)),
            pl.BlockSpec((1, past, 256), lambda b, i: (b, 0, 0)),
        ],
        out_specs=pl.BlockSpec((TQ, BRANCH_W), lambda b, i: (b * nq + i, 0)),
        out_shape=jax.ShapeDtypeStruct((n_batch * n_lat, BRANCH_W), BF16),
        compiler_params=_cparams(("arbitrary", "arbitrary")),
        name="window_attention",
    )(sink, pa, pa, pa, kctx, vctx)


def _na_window_start(qi, rows):
    return jnp.clip(qi * NA_ROWS_Q - NA_KH // 2, 0, rows - NA_ROWS_K)


def _na_attn_kernel(q_ref, k_ref, v_ref, kc_ref, vc_ref, bias_ref, o_ref, *, rows):
    qi = pl.program_id(2)
    ws = pl.multiple_of(_na_window_start(qi, rows) * GRID_W, GRID_W)
    def scores(c):
        cols = slice(c * LANES, (c + 1) * LANES)
        lhs = _split_heads(q_ref[:, cols])
        s_loc = _dot_nt(lhs, k_ref[pl.ds(ws, NA_KEYS), cols]) + bias_ref[0, 2 * c:2 * c + 2].reshape(2 * TQ, NA_KEYS)
        return [s_loc, _dot_nt(lhs, kc_ref[0, :, cols])]

    s = scores(0)
    for c in range(NA_COLS_PER_STEP):
        s_next = scores(c + 1) if c + 1 < NA_COLS_PER_STEP else None
        cols = slice(c * LANES, (c + 1) * LANES)
        o = _softmax_pv(s, [v_ref[pl.ds(ws, NA_KEYS), cols], vc_ref[0, :, cols]])
        o_ref[:, cols] = _merge_heads(o, TQ).astype(BF16)
        s = s_next


def _na_bias_kernel(rpb_ref, o_ref, *, layer, rows):
    h = pl.program_id(0)
    n_dr = 2 * NA_KH - 1
    n_dc = 2 * NA_KW - 1
    base = (layer * B_HEADS + h) * n_dr * n_dc
    lane = lax.broadcasted_iota(jnp.int32, (GRID_W, LANES), 1)
    qc = lax.broadcasted_iota(jnp.int32, (GRID_W, LANES), 0)
    kc = lane & (GRID_W - 1)
    second = lane >= GRID_W
    cs = jnp.clip(qc - NA_KW // 2, 0, GRID_W - NA_KW)
    col_ok = (kc >= cs) & (kc < cs + NA_KW)
    dc_map = kc - qc + (NA_KW - 1)
    cache = {}

    def pair_tile(dr0, ok0, ok1):
        key = (dr0 if ok0 else None, dr0 + 1 if ok1 else None)
        if key not in cache:
            if not (ok0 or ok1):
                cache[key] = jnp.full((GRID_W, LANES), NEG_INF, F32)
            else:
                acc = jnp.zeros((GRID_W, LANES), F32)
                for dc in range(n_dc):
                    v0 = rpb_ref[base + dr0 * n_dc + dc] if ok0 else 0.0
                    v1 = rpb_ref[base + (dr0 + 1) * n_dc + dc] if ok1 else 0.0
                    acc = jnp.where(dc_map == dc, jnp.where(second, v1, v0), acc)
                if ok0 and ok1:
                    ok = col_ok
                elif ok0:
                    ok = col_ok & jnp.logical_not(second)
                else:
                    ok = col_ok & second
                cache[key] = jnp.where(ok, acc, NEG_INF)
        return cache[key]

    nq = rows // NA_ROWS_Q
    for p, qi in enumerate((0, 1, nq - 1)):
        r0 = qi * NA_ROWS_Q
        ws = min(max(r0 - NA_KH // 2, 0), rows - NA_ROWS_K)
        for i in range(NA_ROWS_Q):
            r = r0 + i
            lo = min(max(r - NA_KH // 2, 0), rows - NA_KH)
            for jp in range(NA_ROWS_K // 2):
                kr = ws + 2 * jp
                ok0 = lo <= kr < lo + NA_KH
                ok1 = lo <= kr + 1 < lo + NA_KH
                o_ref[p, 0, i * GRID_W:(i + 1) * GRID_W, jp * LANES:(jp + 1) * LANES] = pair_tile(
                    kr - r + NA_KH - 1, ok0, ok1)


def _na_bias_table(rpb_flat, layer, rows):
    return pl.pallas_call(
        functools.partial(_na_bias_kernel, layer=layer, rows=rows),
        grid=(B_HEADS,),
        in_specs=[pl.BlockSpec(memory_space=pltpu.SMEM)],
        out_specs=pl.BlockSpec((3, 1, TQ, NA_KEYS), lambda h: (0, h, 0, 0)),
        out_shape=jax.ShapeDtypeStruct((3, B_HEADS, TQ, NA_KEYS), F32),
        compiler_params=_cparams(("arbitrary",)),
        name="neighborhood_bias",
    )(rpb_flat)


def _na_attention(pb, kctx, vctx, bias, n_batch, n_lat):
    nq = n_lat // TQ
    rows = n_lat // GRID_W
    past = kctx.shape[1]
    w = NA_COLS_PER_STEP * LANES
    n_cb = BRANCH_W // w

    def bias_idx(b, c, i):
        return (jnp.where(i == 0, 0, jnp.where(i == nq - 1, 2, 1)), c, 0, 0)

    return pl.pallas_call(
        functools.partial(_na_attn_kernel, rows=rows),
        grid=(n_batch, n_cb, nq),
        in_specs=[
            pl.BlockSpec((TQ, w), lambda b, c, i: (b * nq + i, c)),
            pl.BlockSpec((n_lat, w), lambda b, c, i: (b, n_cb + c)),
            pl.BlockSpec((n_lat, w), lambda b, c, i: (b, 2 * n_cb + c)),
            pl.BlockSpec((1, past, w), lambda b, c, i: (b, 0, c)),
            pl.BlockSpec((1, past, w), lambda b, c, i: (b, 0, c)),
            pl.BlockSpec((1, 2 * NA_COLS_PER_STEP, TQ, NA_KEYS), bias_idx),
        ],
        out_specs=pl.BlockSpec((TQ, w), lambda b, c, i: (b * nq + i, c)),
        out_shape=jax.ShapeDtypeStruct((n_batch * n_lat, BRANCH_W), BF16),
        compiler_params=_cparams(("arbitrary", "arbitrary", "arbitrary")),
        name="neighborhood_attention",
    )(pb, pb, pb, kctx, vctx, bias)


def _diff_attn_kernel(lq_ref, sg_ref, q_ref, k_ref, v_ref, kc_ref, vc_ref, o_ref, vt_s, vtc_s, acc_s, *,
                      n_lat, lambda_init):
    n_chunks = n_lat // C_CHUNK

    @pl.when(pl.program_id(2) == 0)
    def _():
        for j in range(n_chunks):
            vt_s[j, :C_VDIM] = v_ref[j * C_CHUNK:(j + 1) * C_CHUNK, :].astype(F32).T.astype(BF16)
            vt_s[j, C_VDIM:] = jnp.ones((C_VT_ROWS - C_VDIM, C_CHUNK), BF16)
        vtc_s[:C_VDIM] = vc_ref[0].astype(F32).T.astype(BF16)
        vtc_s[C_VDIM:] = jnp.ones((C_VT_ROWS - C_VDIM, vtc_s.shape[1]), BF16)

    lhs = _split_heads(q_ref[...])
    acc_s[...] = jnp.zeros_like(acc_s)

    def scores(j):
        k = k_ref[j * C_CHUNK:(j + 1) * C_CHUNK, :] if j < n_chunks else kc_ref[0]
        st = _dot_nt(k, lhs)
        return st, st.max(axis=0, keepdims=True)

    tq = q_ref.shape[0]
    m = jnp.full((1, 2 * tq), NEG_INF, F32)
    st, st_max = scores(0)
    for j in range(n_chunks + 1):
        st_next, st_max_next = scores(j + 1) if j < n_chunks else (None, None)
        m_new = jnp.maximum(m, st_max)
        e = jnp.exp2((st - m_new).astype(BF16))
        vt = vt_s[j] if j < n_chunks else vtc_s[...]
        acc_s[...] = jnp.exp2(m - m_new) * acc_s[...] + _dot(vt, e)
        m, st, st_max = m_new, st_next, st_max_next
    o = acc_s[:C_VDIM, :] * (1.0 / acc_s[C_VDIM:C_VDIM + 1, :])
    d = o[:, :tq] - _diff_lambda(lq_ref, lambda_init) * o[:, tq:]
    ms = jnp.mean(d * d, axis=0, keepdims=True)
    y = (d * lax.rsqrt(ms + EPS)).T
    o_ref[...] = (y * sg_ref[...] * (1.0 - lambda_init)).astype(BF16)


def _diff_attention(pc, kctx, vctx, lq, sg, n_batch, n_lat, lambda_init):
    nq = n_lat // C_TQ
    past = kctx.shape[1]
    return pl.pallas_call(
        functools.partial(_diff_attn_kernel, n_lat=n_lat, lambda_init=lambda_init),
        grid=(n_batch, C_HEADS, nq),
        in_specs=[
            pl.BlockSpec((4, HEAD_DIM), lambda b, h, i: (0, 0)),
            pl.BlockSpec((1, C_VDIM), lambda b, h, i: (0, 0)),
            pl.BlockSpec((C_TQ, LANES), lambda b, h, i: (b * nq + i, h)),
            pl.BlockSpec((n_lat, LANES), lambda b, h, i: (b, 4 + h)),
            pl.BlockSpec((n_lat, LANES), lambda b, h, i: (b, 8 + h)),
            pl.BlockSpec((1, past, LANES), lambda b, h, i: (b, 0, h)),
            pl.BlockSpec((1, past, LANES), lambda b, h, i: (b, 0, h)),
        ],
        out_specs=pl.BlockSpec((C_TQ, LANES), lambda b, h, i: (b * nq + i, h)),
        out_shape=jax.ShapeDtypeStruct((n_batch * n_lat, BRANCH_W), BF16),
        scratch_shapes=[pltpu.VMEM((n_lat // C_CHUNK, C_VT_ROWS, C_CHUNK), BF16), pltpu.VMEM((C_VT_ROWS, past), BF16),
                        pltpu.VMEM((C_VT_ROWS, 2 * C_TQ), F32)],
        compiler_params=_cparams(("arbitrary", "arbitrary", "arbitrary")),
        name="differential_attention",
    )(lq, sg, pc, pc, pc, kctx, vctx)


def _merge_kernel(x_ref, mod_ref, g_ref, oa_ref, ob_ref, oc_ref, sg_ref, wb_ref, wo_ref, wr_ref, br_ref,
                  xo_ref, h_ref, idx_ref, gate_ref):
    wr = wr_ref[...]
    w_hi = wr.astype(BF16)
    w_lo = (wr - w_hi.astype(F32)).astype(BF16)

    def gated_branches(rows):
        y = None
        for j, o_ref in enumerate((oa_ref, ob_ref, oc_ref)):
            t = sg_ref[rows, j * D_MODEL:(j + 1) * D_MODEL].astype(F32) * _dot(o_ref[rows, :], wb_ref[j])
            y = t if y is None else y + t
        return y.astype(BF16)

    def finish(rows, y):
        x = x_ref[rows, :] + mod_ref[0, 2:3, :] * _dot(y, wo_ref[...])
        xo_ref[rows, :] = x
        h = _rms_modulate(x, g_ref[...], mod_ref[0, 3:4, :], mod_ref[0, 4:5, :])
        h_ref[rows, :] = _pack_rows(h)
        h_hi = h.astype(BF16)
        h_lo = (h - h_hi.astype(F32)).astype(BF16)
        logits = _dot(h_hi, w_hi) + (_dot(h_hi, w_lo) + _dot(h_lo, w_hi)) + br_ref[...]
        tm = logits.shape[0]
        lane_e = lax.broadcasted_iota(jnp.int32, (tm, N_EXPERTS), 1)
        lane_o = lax.broadcasted_iota(jnp.int32, (tm, LANES), 1)
        idx_out = jnp.zeros((tm, LANES), jnp.int32)
        val_out = jnp.zeros((tm, LANES), F32)
        top = None
        denom = None
        for k in range(TOP_K):
            mx = logits.max(axis=-1, keepdims=True)
            sel = jnp.min(jnp.where(logits == mx, lane_e, N_EXPERTS), axis=-1, keepdims=True)
            logits = jnp.where(lane_e == sel, -jnp.inf, logits)
            if top is None:
                top = mx
            e = jnp.exp(mx - top)
            denom = e if denom is None else denom + e
            idx_out = jnp.where(lane_o == k, sel, idx_out)
            val_out = jnp.where(lane_o == k, e, val_out)
        idx_ref[rows, :] = idx_out
        gate_ref[rows, :] = val_out * (1.0 / denom)

    n_groups = x_ref.shape[0] // MERGE_ROWS
    groups = [slice(i * MERGE_ROWS, (i + 1) * MERGE_ROWS) for i in range(n_groups)]
    y = gated_branches(groups[0])
    for i in range(n_groups):
        y_next = gated_branches(groups[i + 1]) if i + 1 < n_groups else None
        finish(groups[i], y)
        y = y_next


def _merge_route(x, mod, mod_row0, tiles_per_row, g_ffn, oa, ob, oc, sg, wb, wo, wr, br):
    n = x.shape[0]
    row = lambda i: (i, 0)
    fixed2 = lambda i: (0, 0)
    return pl.pallas_call(
        _merge_kernel,
        grid=(n // MERGE_TM,),
        in_specs=[
            pl.BlockSpec((MERGE_TM, D_MODEL), row),
            pl.BlockSpec((1, N_MOD, D_MODEL), lambda i: (mod_row0 + i // tiles_per_row, 0, 0)),
            pl.BlockSpec((1, D_MODEL), fixed2),
            pl.BlockSpec((MERGE_TM, BRANCH_W), row),
            pl.BlockSpec((MERGE_TM, BRANCH_W), row),
            pl.BlockSpec((MERGE_TM, BRANCH_W), row),
            pl.BlockSpec((MERGE_TM, G_COLS), row),
            pl.BlockSpec((3, BRANCH_W, D_MODEL), lambda i: (0, 0, 0)),
            pl.BlockSpec((D_MODEL, D_MODEL), fixed2),
            pl.BlockSpec((D_MODEL, N_EXPERTS), fixed2),
            pl.BlockSpec((1, N_EXPERTS), fixed2),
        ],
        out_specs=[pl.BlockSpec((MERGE_TM, D_MODEL), row), pl.BlockSpec((MERGE_TM, D_PACK), row),
                   pl.BlockSpec((MERGE_TM, LANES), row), pl.BlockSpec((MERGE_TM, LANES), row)],
        out_shape=[jax.ShapeDtypeStruct((n, D_MODEL), F32), jax.ShapeDtypeStruct((n, D_PACK), jnp.uint32),
                   jax.ShapeDtypeStruct((n, LANES), jnp.int32), jax.ShapeDtypeStruct((n, LANES), F32)],
        compiler_params=_cparams(("arbitrary",)),
        name="merge_route",
    )(x, mod, g_ffn, oa, ob, oc, sg, wb, wo, wr, br)


def _moe_kernel(be_ref, first_ref, valid_ref, x_ref, wgu_ref, bgu_ref, wd_ref, bd_ref, o_ref, wgu_s, wd_s):
    i = pl.program_id(0)

    @pl.when(first_ref[i] == 1)
    def _():
        wgu_s[...] = wgu_ref[0, 0].astype(BF16)
        wd_s[...] = wd_ref[0, 0].astype(BF16)

    @pl.when(valid_ref[i] == 1)
    def _():
        x_lo, x_hi = _unpack_rows(x_ref[...])
        gu = (_dot(x_lo.astype(BF16), wgu_s[:D_PACK, :]) + _dot(x_hi.astype(BF16), wgu_s[D_PACK:, :])
              + bgu_ref[0, 0])
        gate = jnp.minimum(gu[:, :D_FF], SWIGLU_LIMIT)
        lin = jnp.clip(gu[:, D_FF:], -SWIGLU_LIMIT, SWIGLU_LIMIT)
        act = gate * (1.0 / (1.0 + jnp.exp(-SWIGLU_ALPHA * gate))) * (lin + 1.0)
        o_ref[...] = _pack_rows(_dot(act.astype(BF16), wd_s[...]) + bd_ref[0, 0])

    @pl.when(valid_ref[i] == 0)
    def _():
        o_ref[...] = jnp.zeros_like(o_ref)


def _moe_experts(xb, block_e, block_first, block_valid, layer, w_gu, b_gu, w_down, b_down):
    m = xb.shape[0]
    nb = m // MOE_TM
    grid_spec = pltpu.PrefetchScalarGridSpec(
        num_scalar_prefetch=3,
        grid=(nb,),
        in_specs=[
            pl.BlockSpec((MOE_TM, D_PACK), lambda i, be, bf, bv: (i, 0)),
            pl.BlockSpec((1, 1, D_MODEL, 2 * D_FF), lambda i, be, bf, bv: (layer, be[i], 0, 0)),
            pl.BlockSpec((1, 1, 1, 2 * D_FF), lambda i, be, bf, bv: (layer, be[i], 0, 0)),
            pl.BlockSpec((1, 1, D_FF, D_MODEL), lambda i, be, bf, bv: (layer, be[i], 0, 0)),
            pl.BlockSpec((1, 1, 1, D_MODEL), lambda i, be, bf, bv: (layer, be[i], 0, 0)),
        ],
        out_specs=pl.BlockSpec((MOE_TM, D_PACK), lambda i, be, bf, bv: (i, 0)),
        scratch_shapes=[pltpu.VMEM((D_MODEL, 2 * D_FF), BF16), pltpu.VMEM((D_FF, D_MODEL), BF16)],
    )
    return pl.pallas_call(
        _moe_kernel,
        grid_spec=grid_spec,
        out_shape=jax.ShapeDtypeStruct((m, D_PACK), jnp.uint32),
        compiler_params=_cparams(("arbitrary",)),
        name="moe_experts",
    )(block_e, block_first, block_valid, xb, w_gu, b_gu.reshape(DEPTH, N_EXPERTS, 1, 2 * D_FF),
      w_down, b_down.reshape(DEPTH, N_EXPERTS, 1, D_MODEL))


def _moe_block_tables(counts, nb):
    padded = (counts + MOE_TM - 1) // MOE_TM * MOE_TM
    pad_end = jnp.cumsum(padded).astype(jnp.int32)
    blk = jnp.arange(nb, dtype=jnp.int32) * MOE_TM
    block_e = jnp.minimum(jnp.sum((blk[:, None] >= pad_end[None, :]).astype(jnp.int32), axis=1), N_EXPERTS - 1)
    block_valid = (blk < pad_end[-1]).astype(jnp.int32)
    prev = jnp.concatenate([jnp.full((1,), -1, jnp.int32), block_e[:-1]])
    block_first = (block_e != prev).astype(jnp.int32)
    return pad_end, block_e, block_first, block_valid


def _route_kernel(idx_ref, dest_ref, cnt_ref, run_s, start_s):
    p = pl.program_id(0)
    i = pl.program_id(1)
    tm = idx_ref.shape[0]
    lane = lax.broadcasted_iota(jnp.int32, (tm, LANES), 1)
    idx = idx_ref[...]
    sel = [idx[:, k:k + 1] for k in range(TOP_K)]
    onehot = jnp.zeros((tm, LANES), F32)
    for k in range(TOP_K):
        onehot = onehot + jnp.where(lane == sel[k], 1.0, 0.0)
    tile_cnt = jnp.sum(onehot, axis=0, keepdims=True)

    @pl.when((p == 0) & (i == 0))
    def _():
        run_s[...] = jnp.zeros_like(run_s)
        cnt_ref[...] = jnp.zeros_like(cnt_ref)

    @pl.when(p == 0)
    def _():
        run_s[...] += tile_cnt

    @pl.when((p == 1) & (i == 0))
    def _():
        cnt = run_s[...]
        padded = ((cnt.astype(jnp.int32) + (MOE_TM - 1)) // MOE_TM * MOE_TM).astype(F32)
        r = lax.broadcasted_iota(jnp.int32, (LANES, LANES), 0)
        c = lax.broadcasted_iota(jnp.int32, (LANES, LANES), 1)
        before = jnp.where(r < c, 1.0, 0.0)
        start = jnp.dot(jnp.broadcast_to(padded, (8, LANES)), before, preferred_element_type=F32,
                        precision=lax.Precision.HIGHEST)
        start_s[...] = start[0:1]
        cnt_ref[...] = jnp.broadcast_to(cnt, cnt_ref.shape)
        run_s[...] = jnp.zeros_like(run_s)

    @pl.when(p == 1)
    def _():
        r = lax.broadcasted_iota(jnp.int32, (tm, tm), 0)
        c = lax.broadcasted_iota(jnp.int32, (tm, tm), 1)
        earlier = jnp.where(c < r, 1.0, 0.0).astype(BF16)
        pos = _dot(earlier, onehot.astype(BF16)) + run_s[...] + start_s[...]
        out = jnp.zeros((tm, LANES), F32)
        for k in range(TOP_K):
            d = jnp.sum(jnp.where(lane == sel[k], pos, 0.0), axis=-1, keepdims=True)
            out = jnp.where(lane == k, d, out)
        dest_ref[...] = out.astype(jnp.int32)
        run_s[...] += tile_cnt


def _route(idx_slab):
    n = idx_slab.shape[0]
    return pl.pallas_call(
        _route_kernel,
        grid=(2, n // ROUTE_TM),
        in_specs=[pl.BlockSpec((ROUTE_TM, LANES), lambda p, i: (i, 0))],
        out_specs=[pl.BlockSpec((ROUTE_TM, LANES), lambda p, i: (i * p, 0)),
                   pl.BlockSpec((8, LANES), lambda p, i: (0, 0))],
        out_shape=[jax.ShapeDtypeStruct((n, LANES), jnp.int32), jax.ShapeDtypeStruct((8, LANES), F32)],
        scratch_shapes=[pltpu.VMEM((1, LANES), F32), pltpu.VMEM((1, LANES), F32)],
        compiler_params=_cparams(("arbitrary", "arbitrary")),
        name="route",
    )(idx_slab)


def _row_copy(src_ref, src_row, dst_ref, dst_row, sem):
    return pltpu.make_async_copy(src_ref.at[pl.ds(src_row, 1)], dst_ref.at[pl.ds(dst_row, 1)], sem)


def _dispatch_kernel(pe_ref, cnt_ref, dest_ref, hp_ref, hs_ref, xb_ref, zero_s, sem, *, tiles_p, first_tail_block):
    i = pl.program_id(0)
    tm = hp_ref.shape[0]
    nb = xb_ref.shape[0] // MOE_TM

    def zero_block(row0):
        return pltpu.make_async_copy(zero_s, xb_ref.at[pl.ds(pl.multiple_of(row0, MOE_TM), MOE_TM)], sem)

    @pl.when(i == 0)
    def _():
        zero_s[...] = jnp.zeros_like(zero_s)
        for start in (True, False):
            for e in range(N_EXPERTS):
                @pl.when(cnt_ref[e] > 0)
                def _():
                    cp = zero_block(pe_ref[e] - MOE_TM)
                    cp.start() if start else cp.wait()
            for b in range(first_tail_block, nb):
                @pl.when(b * MOE_TM >= pe_ref[N_EXPERTS - 1])
                def _():
                    cp = zero_block(b * MOE_TM)
                    cp.start() if start else cp.wait()

    def scatter_rows(h_ref):
        def issue(t, carry):
            for k in range(TOP_K):
                _row_copy(h_ref, t, xb_ref, dest_ref[0, 0, t * TOP_K + k], sem).start(priority=k % 2)
            return carry

        def drain(t, carry):
            for k in range(TOP_K):
                _row_copy(h_ref, 0, xb_ref, 0, sem).wait()
            return carry

        lax.fori_loop(0, tm, issue, 0)
        lax.fori_loop(0, tm, drain, 0)

    @pl.when(i < tiles_p)
    def _():
        scatter_rows(hp_ref)

    @pl.when(i >= tiles_p)
    def _():
        scatter_rows(hs_ref)


def _dispatch(pad_end, counts, dest, hp, hs, m):
    tiles_p = hp.shape[0] // TM
    tiles_s = hs.shape[0] // TM
    grid_spec = pltpu.PrefetchScalarGridSpec(
        num_scalar_prefetch=2,
        grid=(tiles_p + tiles_s,),
        in_specs=[
            pl.BlockSpec((1, 1, TM * TOP_K), lambda i, pe, cn: (i, 0, 0), memory_space=pltpu.SMEM),
            pl.BlockSpec((TM, D_PACK), lambda i, pe, cn: (jnp.minimum(i, tiles_p - 1), 0)),
            pl.BlockSpec((TM, D_PACK), lambda i, pe, cn: (jnp.maximum(i - tiles_p, 0), 0)),
        ],
        out_specs=pl.BlockSpec(memory_space=pl.ANY),
        scratch_shapes=[pltpu.VMEM((MOE_TM, D_PACK), jnp.uint32), pltpu.SemaphoreType.DMA(())],
    )
    return pl.pallas_call(
        functools.partial(_dispatch_kernel, tiles_p=tiles_p, first_tail_block=m // MOE_TM - N_EXPERTS),
        grid_spec=grid_spec,
        out_shape=jax.ShapeDtypeStruct((m, D_PACK), jnp.uint32),
        compiler_params=_cparams(("arbitrary",)),
        name="dispatch",
    )(pad_end, counts, dest, hp, hs)


def _combine_kernel(dest_ref, x_ref, mod_ref, gate_ref, gfin_ref, yb_ref, o_ref, buf, sem, *, final):
    tm = x_ref.shape[0]

    def issue(t, carry):
        for k in range(TOP_K):
            _row_copy(yb_ref, dest_ref[0, 0, t * TOP_K + k], buf.at[k], t, sem).start(priority=k % 2)
        return carry

    def drain(t, carry):
        for k in range(TOP_K):
            _row_copy(yb_ref, 0, buf.at[k], 0, sem).wait()
        return carry

    lax.fori_loop(0, tm, issue, 0)
    lax.fori_loop(0, tm, drain, 0)
    g = gate_ref[...]
    acc_lo = acc_hi = None
    for k in range(TOP_K):
        y_lo, y_hi = _unpack_rows(buf[k])
        acc_lo = g[:, k:k + 1] * y_lo if acc_lo is None else acc_lo + g[:, k:k + 1] * y_lo
        acc_hi = g[:, k:k + 1] * y_hi if acc_hi is None else acc_hi + g[:, k:k + 1] * y_hi
    x = x_ref[...] + mod_ref[0, 5:6, :] * jnp.concatenate([acc_lo, acc_hi], axis=1)
    if final:
        ms = jnp.mean(x * x, axis=-1, keepdims=True)
        x = x * lax.rsqrt(ms + EPS) * gfin_ref[...]
    o_ref[...] = x


def _combine(dest, x, mod, mod_row0, tiles_per_row, gates, g_final, yb, *, final):
    n = x.shape[0]
    return pl.pallas_call(
        functools.partial(_combine_kernel, final=final),
        grid=(n // TM,),
        in_specs=[
            pl.BlockSpec((1, 1, TM * TOP_K), lambda i: (i, 0, 0), memory_space=pltpu.SMEM),
            pl.BlockSpec((TM, D_MODEL), lambda i: (i, 0)),
            pl.BlockSpec((1, N_MOD, D_MODEL), lambda i: (mod_row0 + i // tiles_per_row, 0, 0)),
            pl.BlockSpec((TM, LANES), lambda i: (i, 0)),
            pl.BlockSpec((1, D_MODEL), lambda i: (0, 0)),
            pl.BlockSpec(memory_space=pl.ANY),
        ],
        out_specs=pl.BlockSpec((TM, D_MODEL), lambda i: (i, 0)),
        out_shape=jax.ShapeDtypeStruct((n, D_MODEL), F32),
        scratch_shapes=[pltpu.VMEM((TOP_K, TM, D_PACK), jnp.uint32), pltpu.SemaphoreType.DMA(())],
        compiler_params=_cparams(("arbitrary",)),
        name="combine_final" if final else "combine",
    )(dest, x, mod, gates, g_final, yb)


def _rope_tables(n_tokens):
    t = np.arange(n_tokens)
    row = (t // GRID_W).astype(np.float32)
    col = (t % GRID_W).astype(np.float32)
    inv = jnp.asarray(ROPE_THETA, F32) ** (-jnp.arange(ROPE_AXIS_PAIRS, dtype=F32) / ROPE_AXIS_PAIRS)
    ang = jnp.concatenate([jnp.asarray(row)[:, None] * inv, jnp.asarray(col)[:, None] * inv], axis=-1)
    cos, sin = jnp.cos(ang), jnp.sin(ang)
    cos = jnp.tile(cos, (1, LANES // ROPE_HALF))
    sin = jnp.tile(jnp.concatenate([-sin, sin], axis=-1), (1, LANES // HEAD_DIM))
    return cos, sin


def _extended_w_in(w):
    qa, ka, va, rest = w[:, :512], w[:, 512:640], w[:, 640:768], w[:, 768:]

    def dup(t):
        t = t.reshape(D_MODEL, A_KV_HEADS, 1, HEAD_DIM)
        return jnp.broadcast_to(t, (D_MODEL, A_KV_HEADS, 2, HEAD_DIM)).reshape(D_MODEL, 4 * HEAD_DIM)

    return jnp.concatenate([qa, dup(ka), dup(va), rest], axis=-1).astype(BF16)


def _dup_kv_heads(t):
    b, p = t.shape[:2]
    t = jnp.broadcast_to(t[:, :, :, None, :], (b, p, A_KV_HEADS, 2, HEAD_DIM))
    return t.reshape(b, p, 4 * HEAD_DIM).astype(BF16)


def kernel(x_prompt, x_sample, cache_k_a, cache_v_a, cache_k_b, cache_v_b, cache_k_c, cache_v_c, c, c_ctx,
           w_ada, b_ada, norm_attn, norm_ffn, w_in, sinks, rpb, lambda_qk, subln, w_branch, w_out,
           w_router, b_router, w_gu, b_gu, w_down, b_down, norm_final):
    bsz, seq = x_prompt.shape[:2]
    bsz_d, n_lat = x_sample.shape[:2]
    past = cache_k_a.shape[2]
    n_p = bsz * seq
    n_s = bsz_d * n_lat
    assert seq % TM == 0 and n_lat % TQ == 0 and n_lat % C_CHUNK == 0 and bsz_d + 1 <= 8
    assert (n_p + n_s) % ROUTE_TM == 0 and n_lat % C_TQ == 0 and n_p % MERGE_TM == 0 and n_lat % MERGE_TM == 0

    cond = jnp.zeros((8, D_MODEL), F32).at[0].set(c_ctx).at[1:1 + bsz_d].set(c)
    mod_all = _modulation(cond, w_ada, b_ada).reshape(DEPTH, 8, N_MOD, D_MODEL)
    cos, sin = _rope_tables(n_lat)
    g_final = norm_final[None, :]
    n_assign = (n_p + n_s) * TOP_K
    m_rows = (n_assign + N_EXPERTS * (MOE_TM - 1) + MOE_TM - 1) // MOE_TM * MOE_TM

    xp = x_prompt.reshape(n_p, D_MODEL)
    xs = x_sample.reshape(n_s, D_MODEL)
    kv_out = []
    for l in range(DEPTH):
        lambda_init = 0.8 - 0.6 * math.exp(-0.3 * l)
        mod = mod_all[l]
        w_ext = _extended_w_in(w_in[l])
        g_attn = norm_attn[l][None, :]
        g_ffn = norm_ffn[l][None, :]
        sg = subln[l][None, :]
        wb = w_branch[l].astype(BF16)
        wo = w_out[l].astype(BF16)
        br = b_router[l][None, :]

        pa, pb, pc, pg, kva, kvb, kvc = _project(xp, mod, 0, n_p // TM, g_attn, w_ext, cos, sin,
                                                 rope=False, emit_kv=True)
        oa, ob, oc = _ctx_attention(pa, pb, pc, sinks[l], lambda_qk[l], sg, seq, lambda_init)
        xp_mid, hp, idx_p, gate_p = _merge_route(xp, mod, 0, n_p // MERGE_TM, g_ffn, oa, ob, oc, pg, wb, wo,
                                                 w_router[l], br)
        kv_out.append((kva, kvb, kvc))

        pa, pb, pc, pg = _project(xs, mod, 1, n_lat // TM, g_attn, w_ext, cos, sin, rope=True, emit_kv=False)
        oa = _win_attention(pa, _dup_kv_heads(cache_k_a[:, l]), _dup_kv_heads(cache_v_a[:, l]), sinks[l],
                            bsz_d, n_lat)
        ob = _na_attention(pb, cache_k_b[:, l].reshape(bsz_d, past, -1).astype(BF16),
                           cache_v_b[:, l].reshape(bsz_d, past, -1).astype(BF16),
                           _na_bias_table(rpb.reshape(-1), l, n_lat // GRID_W), bsz_d, n_lat)
        oc = _diff_attention(pc, cache_k_c[:, l].reshape(bsz_d, past, -1).astype(BF16),
                             cache_v_c[:, l].reshape(bsz_d, past, -1).astype(BF16),
                             lambda_qk[l], sg, bsz_d, n_lat, lambda_init)
        xs_mid, hs, idx_s, gate_s = _merge_route(xs, mod, 1, n_lat // MERGE_TM, g_ffn, oa, ob, oc, pg, wb, wo,
                                                 w_router[l], br)

        last = l == DEPTH - 1
        dest_slab, cnt = _route(jnp.concatenate([idx_p, idx_s], axis=0))
        counts = cnt[0, :N_EXPERTS].astype(jnp.int32)
        pad_end, block_e, block_first, block_valid = _moe_block_tables(counts, m_rows // MOE_TM)
        dest = dest_slab[:, :TOP_K].reshape(-1, 1, TM * TOP_K)
        dest_p, dest_s = dest[:n_p // TM], dest[n_p // TM:]
        xb = _dispatch(pad_end, counts, dest, hp, hs, m_rows)
        yb = _moe_experts(xb, block_e, block_first, block_valid, l, w_gu, b_gu, w_down, b_down)
        xp = _combine(dest_p, xp_mid, mod, 0, n_p // TM, gate_p, g_final, yb, final=last)
        xs = _combine(dest_s, xs_mid, mod, 1, n_lat // TM, gate_s, g_final, yb, final=last)

    y_prompt = xp.reshape(bsz, seq, D_MODEL)
    y_sample = xs.reshape(bsz_d, n_lat, D_MODEL)

    def stack(pick):
        return jnp.stack([pick(kv) for kv in kv_out], axis=1)

    def heads_a(t):
        return t.reshape(bsz, seq, A_KV_HEADS, 2, HEAD_DIM)[:, :, :, 0]

    new_k_a = stack(lambda kv: heads_a(kv[0][:, 0:256]))
    new_v_a = stack(lambda kv: heads_a(kv[0][:, 256:512]))
    new_k_b = stack(lambda kv: kv[1][:, 0:512].reshape(bsz, seq, B_HEADS, HEAD_DIM))
    new_v_b = stack(lambda kv: kv[1][:, 512:1024].reshape(bsz, seq, B_HEADS, HEAD_DIM))
    new_k_c = stack(lambda kv: kv[2][:, 0:512].reshape(bsz, seq, C_HEADS, 2 * HEAD_DIM))
    new_v_c = stack(lambda kv: kv[2][:, 512:1024].reshape(bsz, seq, C_HEADS, C_VDIM))
    return (y_prompt, y_sample, new_k_a, new_v_a, new_k_b, new_v_b, new_k_c, new_v_c)
```

```python
import functools
import math

import jax
import jax.numpy as jnp
import numpy as np
from jax import lax
from jax.experimental import pallas as pl
from jax.experimental.pallas import tpu as pltpu

F32 = jnp.float32
BF16 = jnp.bfloat16

D_MODEL = 1024
DEPTH = 2
GRID_W = 64
HEAD_DIM = 64
ROPE_HALF = HEAD_DIM // 2
ROPE_AXIS_PAIRS = HEAD_DIM // 4
ROPE_THETA = 10000.0
A_HEADS = 8
A_KV_HEADS = 2
WINDOW = 128
B_HEADS = 8
NA_KH = 8
NA_KW = 16
C_HEADS = 4
C_VDIM = 2 * HEAD_DIM
BRANCH_W = 512
N_EXPERTS = 32
TOP_K = 4
D_FF = D_MODEL
SWIGLU_LIMIT = 7.0
SWIGLU_ALPHA = 1.702
N_MOD = 6
EPS = 1e-6
NEG_INF = -1e30
ATTN_SCALE = HEAD_DIM ** -0.5
LOG2E = 1.4426950408889634
C_QSCALE = ATTN_SCALE * LOG2E

LANES = 128
VMEM_LIMIT = 56 * 1024 * 1024

PROJ_WIDTHS = dict(qa=512, ka=128, va=128, qb=512, kb=512, vb=512, qc=512, kc=512, vc=512,
                   ga=D_MODEL, gb=D_MODEL, gc=D_MODEL)
PROJ_OFFSETS = dict(zip(PROJ_WIDTHS, np.cumsum([0] + list(PROJ_WIDTHS.values())[:-1]).tolist()))
D_PROJ = sum(PROJ_WIDTHS.values())
A_COLS = 1024
B_COLS = 1536
C_COLS = 1536
G_COLS = 3 * D_MODEL

TM = 256
TQ = 256
NA_ROWS_Q = TQ // GRID_W
NA_ROWS_K = NA_ROWS_Q + NA_KH
NA_KEYS = NA_ROWS_K * GRID_W
WIN_KEYS = TQ + 2 * WINDOW
NA_COLS_PER_STEP = 4
C_TQ = 512
C_CHUNK = 256
C_VT_ROWS = C_VDIM + 16
MOE_TM = 512
ROUTE_TM = 1024
MERGE_TM = 512
MERGE_ROWS = 256
D_PACK = D_MODEL // 2


def _cparams(sem):
    return pltpu.CompilerParams(dimension_semantics=sem, vmem_limit_bytes=VMEM_LIMIT)


def _head_masks():
    lane = lax.broadcasted_iota(jnp.int32, (1, LANES), 1)
    lo = jnp.where(lane < HEAD_DIM, 1.0, 0.0).astype(BF16)
    hi = jnp.where(lane >= HEAD_DIM, 1.0, 0.0).astype(BF16)
    return lo, hi


def _split_heads(q):
    lo, hi = _head_masks()
    return jnp.concatenate([q * lo, q * hi], axis=0)


def _merge_heads(pv, t):
    lane = lax.broadcasted_iota(jnp.int32, (t, LANES), 1)
    return jnp.where(lane < HEAD_DIM, pv[:t], pv[t:])


def _pack_rows(x):
    half = x.shape[1] // 2
    lo = lax.bitcast_convert_type(x[:, :half].astype(BF16).astype(F32), jnp.uint32)
    hi = lax.bitcast_convert_type(x[:, half:].astype(BF16).astype(F32), jnp.uint32)
    return (lo >> 16) | (hi & jnp.uint32(0xFFFF0000))


def _unpack_rows(w):
    lo = lax.bitcast_convert_type(w << 16, F32)
    hi = lax.bitcast_convert_type(w & jnp.uint32(0xFFFF0000), F32)
    return lo, hi


def _dot_nt(a, b):
    return lax.dot_general(a, b, (((1,), (1,)), ((), ())), preferred_element_type=F32)


def _dot(a, b):
    return jnp.dot(a, b, preferred_element_type=F32)


def _ada_kernel(c_ref, w_ref, b_ref, o_ref):
    c = c_ref[...]
    s = c * (1.0 / (1.0 + jnp.exp(-c)))
    o_ref[0] = jnp.dot(s, w_ref[0], preferred_element_type=F32,
                       precision=lax.Precision.HIGHEST) + b_ref[0]


def _modulation(cond, w_ada, b_ada):
    tn = 1536
    n = N_MOD * D_MODEL
    return pl.pallas_call(
        _ada_kernel,
        grid=(DEPTH, n // tn),
        in_specs=[
            pl.BlockSpec((8, D_MODEL), lambda l, j: (0, 0)),
            pl.BlockSpec((1, D_MODEL, tn), lambda l, j: (l, 0, j)),
            pl.BlockSpec((1, 1, tn), lambda l, j: (l, 0, j)),
        ],
        out_specs=pl.BlockSpec((1, 8, tn), lambda l, j: (l, 0, j)),
        out_shape=jax.ShapeDtypeStruct((DEPTH, 8, n), F32),
        compiler_params=_cparams(("arbitrary", "arbitrary")),
        name="modulation",
    )(cond, w_ada, b_ada.reshape(DEPTH, 1, n))


def _rms_modulate(x, g, shift, scale):
    ms = jnp.mean(x * x, axis=-1, keepdims=True)
    return (x * lax.rsqrt(ms + EPS) * g) * (1.0 + scale) + shift


def _rope_cols(v, cos, sin_signed):
    t, w = v.shape
    lane = lax.broadcasted_iota(jnp.int32, (t, LANES), 1)
    first_half = (lane & (HEAD_DIM - 1)) < ROPE_HALF
    cols = []
    for c in range(w // LANES):
        xc = v[:, c * LANES:(c + 1) * LANES]
        partner = jnp.where(first_half,
                            pltpu.roll(xc, LANES - ROPE_HALF, axis=1),
                            pltpu.roll(xc, ROPE_HALF, axis=1))
        cols.append(xc * cos + partner * sin_signed)
    return jnp.concatenate(cols, axis=1) if len(cols) > 1 else cols[0]


def _dup_kv_cols(v):
    lane = lax.broadcasted_iota(jnp.int32, v.shape, 1)
    swapped = pltpu.roll(v, HEAD_DIM, axis=1)
    return jnp.concatenate([jnp.where(lane < HEAD_DIM, v, swapped), jnp.where(lane < HEAD_DIM, swapped, v)], axis=1)


def _proj_kernel(x_ref, mod_ref, g_ref, w_ref, cos_ref, sin_ref, *out_refs, rope, emit_kv):
    oa_ref, ob_ref, oc_ref, og_ref = out_refs[:4]
    h = _rms_modulate(x_ref[...], g_ref[...], mod_ref[0, 0:1, :], mod_ref[0, 1:2, :]).astype(BF16)
    if rope:
        cos = cos_ref[...]
        sin = sin_ref[...]

    def seg(name):
        start = PROJ_OFFSETS[name]
        return _dot(h, w_ref[0, :, start:start + PROJ_WIDTHS[name]])

    def maybe_rope(v):
        return _rope_cols(v, cos, sin) if rope else v

    oa_ref[:, 0:512] = (maybe_rope(seg("qa")) * ATTN_SCALE).astype(BF16)
    ka = maybe_rope(seg("ka"))
    va = seg("va")
    oa_ref[:, 512:768] = _dup_kv_cols(ka).astype(BF16)
    oa_ref[:, 768:1024] = _dup_kv_cols(va).astype(BF16)
    ob_ref[:, 0:512] = (seg("qb") * ATTN_SCALE).astype(BF16)
    kb = seg("kb")
    vb = seg("vb")
    ob_ref[:, 512:1024] = kb.astype(BF16)
    ob_ref[:, 1024:1536] = vb.astype(BF16)
    oc_ref[:, 0:512] = (maybe_rope(seg("qc")) * C_QSCALE).astype(BF16)
    kc = maybe_rope(seg("kc"))
    vc = seg("vc")
    oc_ref[:, 512:1024] = kc.astype(BF16)
    oc_ref[:, 1024:1536] = vc.astype(BF16)
    for j, name in enumerate(("ga", "gb", "gc")):
        for c in range(D_MODEL // 512):
            start = PROJ_OFFSETS[name] + c * 512
            gv = _dot(h, w_ref[0, :, start:start + 512])
            og_ref[:, j * D_MODEL + c * 512:j * D_MODEL + (c + 1) * 512] = (1.0 / (1.0 + jnp.exp(-gv))).astype(BF16)
    if emit_kv:
        for ref, val in zip(out_refs[4:], (ka, va, kb, vb, kc, vc)):
            ref[...] = val


def _project(x, mod, mod_row0, tiles_per_row, g, w_bf, layer, cos, sin, *, rope, emit_kv):
    n = x.shape[0]
    nt = n // TM
    pos_tiles = cos.shape[0] // TM
    widths = [A_COLS, B_COLS, C_COLS, G_COLS]
    dtypes = [BF16] * 4
    if emit_kv:
        widths += [PROJ_WIDTHS[k] for k in ("ka", "va", "kb", "vb", "kc", "vc")]
        dtypes += [F32] * 6
    return pl.pallas_call(
        functools.partial(_proj_kernel, rope=rope, emit_kv=emit_kv),
        grid=(nt,),
        in_specs=[
            pl.BlockSpec((TM, D_MODEL), lambda i: (i, 0)),
            pl.BlockSpec((1, N_MOD, D_MODEL), lambda i: (mod_row0 + i // tiles_per_row, 0, 0)),
            pl.BlockSpec((1, D_MODEL), lambda i: (0, 0)),
            pl.BlockSpec((1, D_MODEL, D_PROJ), lambda i: (layer, 0, 0)),
            pl.BlockSpec((TM, LANES), lambda i: (i % pos_tiles, 0)),
            pl.BlockSpec((TM, LANES), lambda i: (i % pos_tiles, 0)),
        ],
        out_specs=[pl.BlockSpec((TM, w), lambda i: (i, 0)) for w in widths],
        out_shape=[jax.ShapeDtypeStruct((n, w), d) for w, d in zip(widths, dtypes)],
        compiler_params=_cparams(("arbitrary",)),
        name="project_rope" if rope else "project",
    )(x, mod, g, w_bf, cos, sin)


def _softmax_pv(scores, values, sink=None):
    m = scores[0].max(axis=-1, keepdims=True)
    for s in scores[1:]:
        m = jnp.maximum(m, s.max(axis=-1, keepdims=True))
    if sink is not None:
        m = jnp.maximum(m, sink)
    l = None
    acc = None
    for s, v in zip(scores, values):
        e = jnp.exp(s - m)
        ls = e.sum(axis=-1, keepdims=True)
        pv = _dot(e.astype(BF16), v)
        l = ls if l is None else l + ls
        acc = pv if acc is None else acc + pv
    if sink is not None:
        l = l + jnp.exp(sink - m)
    return acc * (1.0 / l)


def _sink_column(sink_ref, first_head, rows_per_head, n_heads):
    row = lax.broadcasted_iota(jnp.int32, (rows_per_head * n_heads, 1), 0)
    col = jnp.full((rows_per_head * n_heads, 1), sink_ref[first_head], F32)
    for j in range(1, n_heads):
        col = jnp.where(row >= j * rows_per_head, sink_ref[first_head + j], col)
    return col


def _diff_lambda(lq_ref, lambda_init):
    lf = lq_ref[...]
    a = jnp.sum(lf[0:1] * lf[1:2], axis=-1, keepdims=True)
    b = jnp.sum(lf[2:3] * lf[3:4], axis=-1, keepdims=True)
    return jnp.exp(a) - jnp.exp(b) + lambda_init


def _subln(o, g, lambda_init):
    ms = jnp.mean(o * o, axis=-1, keepdims=True)
    return (o * lax.rsqrt(ms + EPS) * g) * (1.0 - lambda_init)


def _ctx_attn_kernel(sink_ref, lq_ref, sg_ref, a_ref, b_ref, c_ref, oa_ref, ob_ref, oc_ref, *, lambda_init):
    t = a_ref.shape[0]
    lam = _diff_lambda(lq_ref, lambda_init)
    chains = []

    for g in range(A_KV_HEADS):
        def scores_a(g=g):
            q = a_ref[:, g * 256:(g + 1) * 256]
            lhs = jnp.concatenate([_split_heads(q[:, :LANES]), _split_heads(q[:, LANES:])], axis=0)
            return _dot_nt(lhs, a_ref[:, 512 + g * LANES:512 + (g + 1) * LANES])

        def finish_a(s, g=g):
            v = a_ref[:, 768 + g * LANES:768 + (g + 1) * LANES]
            o = _softmax_pv([s], [v], _sink_column(sink_ref, 4 * g, t, 4))
            oa_ref[:, g * 256:g * 256 + LANES] = _merge_heads(o[:2 * t], t).astype(BF16)
            oa_ref[:, g * 256 + LANES:(g + 1) * 256] = _merge_heads(o[2 * t:], t).astype(BF16)

        chains.append((scores_a, finish_a))

    for c in range(B_HEADS // 2):
        def scores_b(c=c):
            return _dot_nt(_split_heads(b_ref[:, c * LANES:(c + 1) * LANES]),
                           b_ref[:, 512 + c * LANES:512 + (c + 1) * LANES])

        def finish_b(s, c=c):
            o = _softmax_pv([s], [b_ref[:, 1024 + c * LANES:1024 + (c + 1) * LANES]])
            ob_ref[:, c * LANES:(c + 1) * LANES] = _merge_heads(o, t).astype(BF16)

        chains.append((scores_b, finish_b))

    for h in range(C_HEADS):
        def scores_c(h=h):
            return _dot_nt(_split_heads(c_ref[:, h * LANES:(h + 1) * LANES]),
                           c_ref[:, 512 + h * LANES:512 + (h + 1) * LANES])

        def finish_c(s, h=h):
            m = s.max(axis=-1, keepdims=True)
            e = jnp.exp2(s - m)
            p = e * (1.0 / e.sum(axis=-1, keepdims=True))
            a = (p[:t] - lam * p[t:]).astype(BF16)
            o = _dot(a, c_ref[:, 1024 + h * LANES:1024 + (h + 1) * LANES])
            oc_ref[:, h * LANES:(h + 1) * LANES] = _subln(o, sg_ref[...], lambda_init).astype(BF16)

        chains.append((scores_c, finish_c))

    s = chains[0][0]()
    for i, (_, finish) in enumerate(chains):
        s_next = chains[i + 1][0]() if i + 1 < len(chains) else None
        finish(s)
        s = s_next


def _ctx_attention(pa, pb, pc, sink, lq, sg, seq, lambda_init):
    n = pa.shape[0]
    smem = pl.BlockSpec(memory_space=pltpu.SMEM)
    return pl.pallas_call(
        functools.partial(_ctx_attn_kernel, lambda_init=lambda_init),
        grid=(n // seq,),
        in_specs=[
            smem,
            pl.BlockSpec((4, HEAD_DIM), lambda b: (0, 0)),
            pl.BlockSpec((1, C_VDIM), lambda b: (0, 0)),
            pl.BlockSpec((seq, A_COLS), lambda b: (b, 0)),
            pl.BlockSpec((seq, B_COLS), lambda b: (b, 0)),
            pl.BlockSpec((seq, C_COLS), lambda b: (b, 0)),
        ],
        out_specs=[pl.BlockSpec((seq, BRANCH_W), lambda b: (b, 0))] * 3,
        out_shape=[jax.ShapeDtypeStruct((n, BRANCH_W), BF16)] * 3,
        compiler_params=_cparams(("arbitrary",)),
        name="context_attention",
    )(sink, lq, sg, pa, pb, pc)


def _win_attn_kernel(sink_ref, q_ref, k_ref, v_ref, kc_ref, vc_ref, o_ref, *, n_lat):
    qi = pl.program_id(1)
    ws = pl.multiple_of(jnp.clip(qi * TQ - WINDOW, 0, n_lat - WIN_KEYS), WINDOW)
    qpos = qi * TQ + lax.broadcasted_iota(jnp.int32, (TQ, WIN_KEYS), 0)
    kpos = ws + lax.broadcasted_iota(jnp.int32, (TQ, WIN_KEYS), 1)
    band = jnp.where(jnp.abs(kpos - qpos) <= WINDOW, 0.0, NEG_INF)
    band = jnp.concatenate([band] * 4, axis=0)
    def scores(g):
        q = q_ref[:, g * 256:(g + 1) * 256]
        lhs = jnp.concatenate([_split_heads(q[:, :LANES]), _split_heads(q[:, LANES:])], axis=0)
        cols = slice(g * LANES, (g + 1) * LANES)
        return [_dot_nt(lhs, k_ref[pl.ds(ws, WIN_KEYS), cols]) + band, _dot_nt(lhs, kc_ref[0, :, cols])]

    s = scores(0)
    for g in range(A_KV_HEADS):
        s_next = scores(g + 1) if g + 1 < A_KV_HEADS else None
        cols = slice(g * LANES, (g + 1) * LANES)
        sink = _sink_column(sink_ref, 4 * g, TQ, 4)
        o = _softmax_pv(s, [v_ref[pl.ds(ws, WIN_KEYS), cols], vc_ref[0, :, cols]], sink)
        o_ref[:, g * 256:g * 256 + LANES] = _merge_heads(o[:2 * TQ], TQ).astype(BF16)
        o_ref[:, g * 256 + LANES:(g + 1) * 256] = _merge_heads(o[2 * TQ:], TQ).astype(BF16)
        s = s_next


def _win_attention(pa, kctx, vctx, sink, n_batch, n_lat):
    nq = n_lat // TQ
    past = kctx.shape[1]
    return pl.pallas_call(
        functools.partial(_win_attn_kernel, n_lat=n_lat),
        grid=(n_batch, nq),
        in_specs=[
            pl.BlockSpec(memory_space=pltpu.SMEM),
            pl.BlockSpec((TQ, 512), lambda b, i: (b * nq + i, 0)),
            pl.BlockSpec((n_lat, 256), lambda b, i: (b, 2)),
            pl.BlockSpec((n_lat, 256), lambda b, i: (b, 3)),
            pl.BlockSpec((1, past, 256), lambda b, i: (b, 0, 0)),
            pl.BlockSpec((1, past, 256), lambda b, i: (b, 0, 0)),
        ],
        out_specs=pl.BlockSpec((TQ, BRANCH_W), lambda b, i: (b * nq + i, 0)),
        out_shape=jax.ShapeDtypeStruct((n_batch * n_lat, BRANCH_W), BF16),
        compiler_params=_cparams(("arbitrary", "arbitrary")),
        name="window_attention",
    )(sink, pa, pa, pa, kctx, vctx)


def _na_window_start(qi, rows):
    return jnp.clip(qi * NA_ROWS_Q - NA_KH // 2, 0, rows - NA_ROWS_K)


def _na_attn_kernel(q_ref, k_ref, v_ref, kc_ref, vc_ref, bias_ref, o_ref, *, rows):
    qi = pl.program_id(2)
    ws = pl.multiple_of(_na_window_start(qi, rows) * GRID_W, GRID_W)
    def scores(c):
        cols = slice(c * LANES, (c + 1) * LANES)
        lhs = _split_heads(q_ref[:, cols])
        s_loc = _dot_nt(lhs, k_ref[pl.ds(ws, NA_KEYS), cols]) + bias_ref[0, 2 * c:2 * c + 2].reshape(2 * TQ, NA_KEYS)
        return [s_loc, _dot_nt(lhs, kc_ref[0, :, cols])]

    s = scores(0)
    for c in range(NA_COLS_PER_STEP):
        s_next = scores(c + 1) if c + 1 < NA_COLS_PER_STEP else None
        cols = slice(c * LANES, (c + 1) * LANES)
        o = _softmax_pv(s, [v_ref[pl.ds(ws, NA_KEYS), cols], vc_ref[0, :, cols]])
        o_ref[:, cols] = _merge_heads(o, TQ).astype(BF16)
        s = s_next


def _na_bias_kernel(rpb_ref, o_ref, *, layer, rows):
    h = pl.program_id(0)
    n_dr = 2 * NA_KH - 1
    n_dc = 2 * NA_KW - 1
    base = (layer * B_HEADS + h) * n_dr * n_dc
    lane = lax.broadcasted_iota(jnp.int32, (GRID_W, LANES), 1)
    qc = lax.broadcasted_iota(jnp.int32, (GRID_W, LANES), 0)
    kc = lane & (GRID_W - 1)
    second = lane >= GRID_W
    cs = jnp.clip(qc - NA_KW // 2, 0, GRID_W - NA_KW)
    col_ok = (kc >= cs) & (kc < cs + NA_KW)
    dc_map = kc - qc + (NA_KW - 1)
    cache = {}

    def pair_tile(dr0, ok0, ok1):
        key = (dr0 if ok0 else None, dr0 + 1 if ok1 else None)
        if key not in cache:
            if not (ok0 or ok1):
                cache[key] = jnp.full((GRID_W, LANES), NEG_INF, F32)
            else:
                acc = jnp.zeros((GRID_W, LANES), F32)
                for dc in range(n_dc):
                    v0 = rpb_ref[base + dr0 * n_dc + dc] if ok0 else 0.0
                    v1 = rpb_ref[base + (dr0 + 1) * n_dc + dc] if ok1 else 0.0
                    acc = jnp.where(dc_map == dc, jnp.where(second, v1, v0), acc)
                if ok0 and ok1:
                    ok = col_ok
                elif ok0:
                    ok = col_ok & jnp.logical_not(second)
                else:
                    ok = col_ok & second
                cache[key] = jnp.where(ok, acc, NEG_INF)
        return cache[key]

    nq = rows // NA_ROWS_Q
    for p, qi in enumerate((0, 1, nq - 1)):
        r0 = qi * NA_ROWS_Q
        ws = min(max(r0 - NA_KH // 2, 0), rows - NA_ROWS_K)
        for i in range(NA_ROWS_Q):
            r = r0 + i
            lo = min(max(r - NA_KH // 2, 0), rows - NA_KH)
            for jp in range(NA_ROWS_K // 2):
                kr = ws + 2 * jp
                ok0 = lo <= kr < lo + NA_KH
                ok1 = lo <= kr + 1 < lo + NA_KH
                o_ref[p, 0, i * GRID_W:(i + 1) * GRID_W, jp * LANES:(jp + 1) * LANES] = pair_tile(
                    kr - r + NA_KH - 1, ok0, ok1)


def _na_bias_table(rpb_flat, layer, rows):
    return pl.pallas_call(
        functools.partial(_na_bias_kernel, layer=layer, rows=rows),
        grid=(B_HEADS,),
        in_specs=[pl.BlockSpec(memory_space=pltpu.SMEM)],
        out_specs=pl.BlockSpec((3, 1, TQ, NA_KEYS), lambda h: (0, h, 0, 0)),
        out_shape=jax.ShapeDtypeStruct((3, B_HEADS, TQ, NA_KEYS), F32),
        compiler_params=_cparams(("arbitrary",)),
        name="neighborhood_bias",
    )(rpb_flat)


def _na_attention(pb, kctx, vctx, bias, n_batch, n_lat):
    nq = n_lat // TQ
    rows = n_lat // GRID_W
    past = kctx.shape[1]
    w = NA_COLS_PER_STEP * LANES
    n_cb = BRANCH_W // w

    def bias_idx(b, c, i):
        return (jnp.where(i == 0, 0, jnp.where(i == nq - 1, 2, 1)), c, 0, 0)

    return pl.pallas_call(
        functools.partial(_na_attn_kernel, rows=rows),
        grid=(n_batch, n_cb, nq),
        in_specs=[
            pl.BlockSpec((TQ, w), lambda b, c, i: (b * nq + i, c)),
            pl.BlockSpec((n_lat, w), lambda b, c, i: (b, n_cb + c)),
            pl.BlockSpec((n_lat, w), lambda b, c, i: (b, 2 * n_cb + c)),
            pl.BlockSpec((1, past, w), lambda b, c, i: (b, 0, c)),
            pl.BlockSpec((1, past, w), lambda b, c, i: (b, 0, c)),
            pl.BlockSpec((1, 2 * NA_COLS_PER_STEP, TQ, NA_KEYS), bias_idx),
        ],
        out_specs=pl.BlockSpec((TQ, w), lambda b, c, i: (b * nq + i, c)),
        out_shape=jax.ShapeDtypeStruct((n_batch * n_lat, BRANCH_W), BF16),
        compiler_params=_cparams(("arbitrary", "arbitrary", "arbitrary")),
        name="neighborhood_attention",
    )(pb, pb, pb, kctx, vctx, bias)


def _diff_attn_kernel(lq_ref, sg_ref, q_ref, k_ref, v_ref, kc_ref, vc_ref, o_ref, vt_s, vtc_s, acc_s, *,
                      n_lat, lambda_init):
    n_chunks = n_lat // C_CHUNK

    @pl.when(pl.program_id(2) == 0)
    def _():
        for j in range(n_chunks):
            vt_s[j, :C_VDIM] = v_ref[j * C_CHUNK:(j + 1) * C_CHUNK, :].astype(F32).T.astype(BF16)
            vt_s[j, C_VDIM:] = jnp.ones((C_VT_ROWS - C_VDIM, C_CHUNK), BF16)
        vtc_s[:C_VDIM] = vc_ref[0].astype(F32).T.astype(BF16)
        vtc_s[C_VDIM:] = jnp.ones((C_VT_ROWS - C_VDIM, vtc_s.shape[1]), BF16)

    lhs = _split_heads(q_ref[...])
    acc_s[...] = jnp.zeros_like(acc_s)

    def scores(j):
        k = k_ref[j * C_CHUNK:(j + 1) * C_CHUNK, :] if j < n_chunks else kc_ref[0]
        st = _dot_nt(k, lhs)
        return st, st.max(axis=0, keepdims=True)

    tq = q_ref.shape[0]
    m = jnp.full((1, 2 * tq), NEG_INF, F32)
    st, st_max = scores(0)
    for j in range(n_chunks + 1):
        st_next, st_max_next = scores(j + 1) if j < n_chunks else (None, None)
        m_new = jnp.maximum(m, st_max)
        e = jnp.exp2((st - m_new).astype(BF16))
        vt = vt_s[j] if j < n_chunks else vtc_s[...]
        acc_s[...] = jnp.exp2(m - m_new) * acc_s[...] + _dot(vt, e)
        m, st, st_max = m_new, st_next, st_max_next
    o = acc_s[:C_VDIM, :] * (1.0 / acc_s[C_VDIM:C_VDIM + 1, :])
    d = o[:, :tq] - _diff_lambda(lq_ref, lambda_init) * o[:, tq:]
    ms = jnp.mean(d * d, axis=0, keepdims=True)
    y = (d * lax.rsqrt(ms + EPS)).T
    o_ref[...] = (y * sg_ref[...] * (1.0 - lambda_init)).astype(BF16)


def _diff_attention(pc, kctx, vctx, lq, sg, n_batch, n_lat, lambda_init):
    nq = n_lat // C_TQ
    past = kctx.shape[1]
    return pl.pallas_call(
        functools.partial(_diff_attn_kernel, n_lat=n_lat, lambda_init=lambda_init),
        grid=(n_batch, C_HEADS, nq),
        in_specs=[
            pl.BlockSpec((4, HEAD_DIM), lambda b, h, i: (0, 0)),
            pl.BlockSpec((1, C_VDIM), lambda b, h, i: (0, 0)),
            pl.BlockSpec((C_TQ, LANES), lambda b, h, i: (b * nq + i, h)),
            pl.BlockSpec((n_lat, LANES), lambda b, h, i: (b, 4 + h)),
            pl.BlockSpec((n_lat, LANES), lambda b, h, i: (b, 8 + h)),
            pl.BlockSpec((1, past, LANES), lambda b, h, i: (b, 0, h)),
            pl.BlockSpec((1, past, LANES), lambda b, h, i: (b, 0, h)),
        ],
        out_specs=pl.BlockSpec((C_TQ, LANES), lambda b, h, i: (b * nq + i, h)),
        out_shape=jax.ShapeDtypeStruct((n_batch * n_lat, BRANCH_W), BF16),
        scratch_shapes=[pltpu.VMEM((n_lat // C_CHUNK, C_VT_ROWS, C_CHUNK), BF16), pltpu.VMEM((C_VT_ROWS, past), BF16),
                        pltpu.VMEM((C_VT_ROWS, 2 * C_TQ), F32)],
        compiler_params=_cparams(("arbitrary", "arbitrary", "arbitrary")),
        name="differential_attention",
    )(lq, sg, pc, pc, pc, kctx, vctx)


def _merge_kernel(x_ref, mod_ref, g_ref, oa_ref, ob_ref, oc_ref, sg_ref, wb_ref, wo_ref, wr_ref, br_ref,
                  xo_ref, h_ref, idx_ref, gate_ref):
    wr = wr_ref[...]
    w_hi = wr.astype(BF16)
    w_lo = (wr - w_hi.astype(F32)).astype(BF16)

    def gated_branches(rows):
        y = None
        for j, o_ref in enumerate((oa_ref, ob_ref, oc_ref)):
            t = sg_ref[rows, j * D_MODEL:(j + 1) * D_MODEL].astype(F32) * _dot(o_ref[rows, :], wb_ref[j])
            y = t if y is None else y + t
        return y.astype(BF16)

    def finish(rows, y):
        x = x_ref[rows, :] + mod_ref[0, 2:3, :] * _dot(y, wo_ref[...])
        xo_ref[rows, :] = x
        h = _rms_modulate(x, g_ref[...], mod_ref[0, 3:4, :], mod_ref[0, 4:5, :])
        h_ref[rows, :] = _pack_rows(h)
        h_hi = h.astype(BF16)
        h_lo = (h - h_hi.astype(F32)).astype(BF16)
        logits = _dot(h_hi, w_hi) + (_dot(h_hi, w_lo) + _dot(h_lo, w_hi)) + br_ref[...]
        tm = logits.shape[0]
        lane_e = lax.broadcasted_iota(jnp.int32, (tm, N_EXPERTS), 1)
        lane_o = lax.broadcasted_iota(jnp.int32, (tm, LANES), 1)
        idx_out = jnp.zeros((tm, LANES), jnp.int32)
        val_out = jnp.zeros((tm, LANES), F32)
        top = None
        denom = None
        for k in range(TOP_K):
            mx = logits.max(axis=-1, keepdims=True)
            sel = jnp.min(jnp.where(logits == mx, lane_e, N_EXPERTS), axis=-1, keepdims=True)
            logits = jnp.where(lane_e == sel, -jnp.inf, logits)
            if top is None:
                top = mx
            e = jnp.exp(mx - top)
            denom = e if denom is None else denom + e
            idx_out = jnp.where(lane_o == k, sel, idx_out)
            val_out = jnp.where(lane_o == k, e, val_out)
        idx_ref[rows, :] = idx_out
        gate_ref[rows, :] = val_out * (1.0 / denom)

    n_groups = x_ref.shape[0] // MERGE_ROWS
    groups = [slice(i * MERGE_ROWS, (i + 1) * MERGE_ROWS) for i in range(n_groups)]
    y = gated_branches(groups[0])
    for i in range(n_groups):
        y_next = gated_branches(groups[i + 1]) if i + 1 < n_groups else None
        finish(groups[i], y)
        y = y_next


def _merge_route(x, mod, mod_row0, tiles_per_row, g_ffn, oa, ob, oc, sg, wb, wo, wr, br):
    n = x.shape[0]
    row = lambda i: (i, 0)
    fixed2 = lambda i: (0, 0)
    return pl.pallas_call(
        _merge_kernel,
        grid=(n // MERGE_TM,),
        in_specs=[
            pl.BlockSpec((MERGE_TM, D_MODEL), row),
            pl.BlockSpec((1, N_MOD, D_MODEL), lambda i: (mod_row0 + i // tiles_per_row, 0, 0)),
            pl.BlockSpec((1, D_MODEL), fixed2),
            pl.BlockSpec((MERGE_TM, BRANCH_W), row),
            pl.BlockSpec((MERGE_TM, BRANCH_W), row),
            pl.BlockSpec((MERGE_TM, BRANCH_W), row),
            pl.BlockSpec((MERGE_TM, G_COLS), row),
            pl.BlockSpec((3, BRANCH_W, D_MODEL), lambda i: (0, 0, 0)),
            pl.BlockSpec((D_MODEL, D_MODEL), fixed2),
            pl.BlockSpec((D_MODEL, N_EXPERTS), fixed2),
            pl.BlockSpec((1, N_EXPERTS), fixed2),
        ],
        out_specs=[pl.BlockSpec((MERGE_TM, D_MODEL), row), pl.BlockSpec((MERGE_TM, D_PACK), row),
                   pl.BlockSpec((MERGE_TM, LANES), row), pl.BlockSpec((MERGE_TM, LANES), row)],
        out_shape=[jax.ShapeDtypeStruct((n, D_MODEL), F32), jax.ShapeDtypeStruct((n, D_PACK), jnp.uint32),
                   jax.ShapeDtypeStruct((n, LANES), jnp.int32), jax.ShapeDtypeStruct((n, LANES), F32)],
        compiler_params=_cparams(("arbitrary",)),
        name="merge_route",
    )(x, mod, g_ffn, oa, ob, oc, sg, wb, wo, wr, br)


def _moe_kernel(be_ref, first_ref, valid_ref, x_ref, wgu_ref, bgu_ref, wd_ref, bd_ref, o_ref, wgu_s, wd_s):
    i = pl.program_id(0)

    @pl.when(first_ref[i] == 1)
    def _():
        wgu_s[...] = wgu_ref[0, 0].astype(BF16)
        wd_s[...] = wd_ref[0, 0].astype(BF16)

    @pl.when(valid_ref[i] == 1)
    def _():
        x_lo, x_hi = _unpack_rows(x_ref[...])
        gu = (_dot(x_lo.astype(BF16), wgu_s[:D_PACK, :]) + _dot(x_hi.astype(BF16), wgu_s[D_PACK:, :])
              + bgu_ref[0, 0])
        gate = jnp.minimum(gu[:, :D_FF], SWIGLU_LIMIT)
        lin = jnp.clip(gu[:, D_FF:], -SWIGLU_LIMIT, SWIGLU_LIMIT)
        act = gate * (1.0 / (1.0 + jnp.exp(-SWIGLU_ALPHA * gate))) * (lin + 1.0)
        o_ref[...] = _pack_rows(_dot(act.astype(BF16), wd_s[...]) + bd_ref[0, 0])

    @pl.when(valid_ref[i] == 0)
    def _():
        o_ref[...] = jnp.zeros_like(o_ref)


def _moe_experts(xb, block_e, block_first, block_valid, layer, w_gu, b_gu, w_down, b_down):
    m = xb.shape[0]
    nb = m // MOE_TM
    grid_spec = pltpu.PrefetchScalarGridSpec(
        num_scalar_prefetch=3,
        grid=(nb,),
        in_specs=[
            pl.BlockSpec((MOE_TM, D_PACK), lambda i, be, bf, bv: (i, 0)),
            pl.BlockSpec((1, 1, D_MODEL, 2 * D_FF), lambda i, be, bf, bv: (layer, be[i], 0, 0)),
            pl.BlockSpec((1, 1, 1, 2 * D_FF), lambda i, be, bf, bv: (layer, be[i], 0, 0)),
            pl.BlockSpec((1, 1, D_FF, D_MODEL), lambda i, be, bf, bv: (layer, be[i], 0, 0)),
            pl.BlockSpec((1, 1, 1, D_MODEL), lambda i, be, bf, bv: (layer, be[i], 0, 0)),
        ],
        out_specs=pl.BlockSpec((MOE_TM, D_PACK), lambda i, be, bf, bv: (i, 0)),
        scratch_shapes=[pltpu.VMEM((D_MODEL, 2 * D_FF), BF16), pltpu.VMEM((D_FF, D_MODEL), BF16)],
    )
    return pl.pallas_call(
        _moe_kernel,
        grid_spec=grid_spec,
        out_shape=jax.ShapeDtypeStruct((m, D_PACK), jnp.uint32),
        compiler_params=_cparams(("arbitrary",)),
        name="moe_experts",
    )(block_e, block_first, block_valid, xb, w_gu, b_gu.reshape(DEPTH, N_EXPERTS, 1, 2 * D_FF),
      w_down, b_down.reshape(DEPTH, N_EXPERTS, 1, D_MODEL))


def _moe_block_tables(counts, nb):
    padded = (counts + MOE_TM - 1) // MOE_TM * MOE_TM
    pad_end = jnp.cumsum(padded).astype(jnp.int32)
    blk = jnp.arange(nb, dtype=jnp.int32) * MOE_TM
    block_e = jnp.minimum(jnp.sum((blk[:, None] >= pad_end[None, :]).astype(jnp.int32), axis=1), N_EXPERTS - 1)
    block_valid = (blk < pad_end[-1]).astype(jnp.int32)
    prev = jnp.concatenate([jnp.full((1,), -1, jnp.int32), block_e[:-1]])
    block_first = (block_e != prev).astype(jnp.int32)
    return pad_end, block_e, block_first, block_valid


def _route_kernel(idx_ref, dest_ref, cnt_ref, run_s, start_s):
    p = pl.program_id(0)
    i = pl.program_id(1)
    tm = idx_ref.shape[0]
    lane = lax.broadcasted_iota(jnp.int32, (tm, LANES), 1)
    idx = idx_ref[...]
    sel = [idx[:, k:k + 1] for k in range(TOP_K)]
    onehot = jnp.zeros((tm, LANES), F32)
    for k in range(TOP_K):
        onehot = onehot + jnp.where(lane == sel[k], 1.0, 0.0)
    tile_cnt = jnp.sum(onehot, axis=0, keepdims=True)

    @pl.when((p == 0) & (i == 0))
    def _():
        run_s[...] = jnp.zeros_like(run_s)
        cnt_ref[...] = jnp.zeros_like(cnt_ref)

    @pl.when(p == 0)
    def _():
        run_s[...] += tile_cnt

    @pl.when((p == 1) & (i == 0))
    def _():
        cnt = run_s[...]
        padded = ((cnt.astype(jnp.int32) + (MOE_TM - 1)) // MOE_TM * MOE_TM).astype(F32)
        r = lax.broadcasted_iota(jnp.int32, (LANES, LANES), 0)
        c = lax.broadcasted_iota(jnp.int32, (LANES, LANES), 1)
        before = jnp.where(r < c, 1.0, 0.0)
        start = jnp.dot(jnp.broadcast_to(padded, (8, LANES)), before, preferred_element_type=F32,
                        precision=lax.Precision.HIGHEST)
        start_s[...] = start[0:1]
        cnt_ref[...] = jnp.broadcast_to(cnt, cnt_ref.shape)
        run_s[...] = jnp.zeros_like(run_s)

    @pl.when(p == 1)
    def _():
        r = lax.broadcasted_iota(jnp.int32, (tm, tm), 0)
        c = lax.broadcasted_iota(jnp.int32, (tm, tm), 1)
        earlier = jnp.where(c < r, 1.0, 0.0).astype(BF16)
        pos = _dot(earlier, onehot.astype(BF16)) + run_s[...] + start_s[...]
        out = jnp.zeros((tm, LANES), F32)
        for k in range(TOP_K):
            d = jnp.sum(jnp.where(lane == sel[k], pos, 0.0), axis=-1, keepdims=True)
            out = jnp.where(lane == k, d, out)
        dest_ref[...] = out.astype(jnp.int32)
        run_s[...] += tile_cnt


def _route(idx_slab):
    n = idx_slab.shape[0]
    return pl.pallas_call(
        _route_kernel,
        grid=(2, n // ROUTE_TM),
        in_specs=[pl.BlockSpec((ROUTE_TM, LANES), lambda p, i: (i, 0))],
        out_specs=[pl.BlockSpec((ROUTE_TM, LANES), lambda p, i: (i * p, 0)),
                   pl.BlockSpec((8, LANES), lambda p, i: (0, 0))],
        out_shape=[jax.ShapeDtypeStruct((n, LANES), jnp.int32), jax.ShapeDtypeStruct((8, LANES), F32)],
        scratch_shapes=[pltpu.VMEM((1, LANES), F32), pltpu.VMEM((1, LANES), F32)],
        compiler_params=_cparams(("arbitrary", "arbitrary")),
        name="route",
    )(idx_slab)


def _row_copy(src_ref, src_row, dst_ref, dst_row, sem):
    return pltpu.make_async_copy(src_ref.at[pl.ds(src_row, 1)], dst_ref.at[pl.ds(dst_row, 1)], sem)


def _dispatch_kernel(pe_ref, cnt_ref, dest_ref, hp_ref, hs_ref, xb_ref, zero_s, sem, *, tiles_p, first_tail_block):
    i = pl.program_id(0)
    tm = hp_ref.shape[0]
    nb = xb_ref.shape[0] // MOE_TM

    def zero_block(row0):
        return pltpu.make_async_copy(zero_s, xb_ref.at[pl.ds(pl.multiple_of(row0, MOE_TM), MOE_TM)], sem)

    @pl.when(i == 0)
    def _():
        zero_s[...] = jnp.zeros_like(zero_s)
        for start in (True, False):
            for e in range(N_EXPERTS):
                @pl.when(cnt_ref[e] > 0)
                def _():
                    cp = zero_block(pe_ref[e] - MOE_TM)
                    cp.start() if start else cp.wait()
            for b in range(first_tail_block, nb):
                @pl.when(b * MOE_TM >= pe_ref[N_EXPERTS - 1])
                def _():
                    cp = zero_block(b * MOE_TM)
                    cp.start() if start else cp.wait()

    def scatter_rows(h_ref):
        def issue(t, carry):
            for k in range(TOP_K):
                _row_copy(h_ref, t, xb_ref, dest_ref[0, 0, t * TOP_K + k], sem).start(priority=k % 2)
            return carry

        def drain(t, carry):
            for k in range(TOP_K):
                _row_copy(h_ref, 0, xb_ref, 0, sem).wait()
            return carry

        lax.fori_loop(0, tm, issue, 0)
        lax.fori_loop(0, tm, drain, 0)

    @pl.when(i < tiles_p)
    def _():
        scatter_rows(hp_ref)

    @pl.when(i >= tiles_p)
    def _():
        scatter_rows(hs_ref)


def _dispatch(pad_end, counts, dest, hp, hs, m):
    tiles_p = hp.shape[0] // TM
    tiles_s = hs.shape[0] // TM
    grid_spec = pltpu.PrefetchScalarGridSpec(
        num_scalar_prefetch=2,
        grid=(tiles_p + tiles_s,),
        in_specs=[
            pl.BlockSpec((1, 1, TM * TOP_K), lambda i, pe, cn: (i, 0, 0), memory_space=pltpu.SMEM),
            pl.BlockSpec((TM, D_PACK), lambda i, pe, cn: (jnp.minimum(i, tiles_p - 1), 0)),
            pl.BlockSpec((TM, D_PACK), lambda i, pe, cn: (jnp.maximum(i - tiles_p, 0), 0)),
        ],
        out_specs=pl.BlockSpec(memory_space=pl.ANY),
        scratch_shapes=[pltpu.VMEM((MOE_TM, D_PACK), jnp.uint32), pltpu.SemaphoreType.DMA(())],
    )
    return pl.pallas_call(
        functools.partial(_dispatch_kernel, tiles_p=tiles_p, first_tail_block=m // MOE_TM - N_EXPERTS),
        grid_spec=grid_spec,
        out_shape=jax.ShapeDtypeStruct((m, D_PACK), jnp.uint32),
        compiler_params=_cparams(("arbitrary",)),
        name="dispatch",
    )(pad_end, counts, dest, hp, hs)


def _combine_kernel(dest_ref, x_ref, mod_ref, gate_ref, gfin_ref, yb_ref, o_ref, buf, sem, *, final):
    tm = x_ref.shape[0]

    def issue(t, carry):
        for k in range(TOP_K):
            _row_copy(yb_ref, dest_ref[0, 0, t * TOP_K + k], buf.at[k], t, sem).start(priority=k % 2)
        return carry

    def drain(t, carry):
        for k in range(TOP_K):
            _row_copy(yb_ref, 0, buf.at[k], 0, sem).wait()
        return carry

    lax.fori_loop(0, tm, issue, 0)
    lax.fori_loop(0, tm, drain, 0)
    g = gate_ref[...]
    acc_lo = acc_hi = None
    for k in range(TOP_K):
        y_lo, y_hi = _unpack_rows(buf[k])
        acc_lo = g[:, k:k + 1] * y_lo if acc_lo is None else acc_lo + g[:, k:k + 1] * y_lo
        acc_hi = g[:, k:k + 1] * y_hi if acc_hi is None else acc_hi + g[:, k:k + 1] * y_hi
    x = x_ref[...] + mod_ref[0, 5:6, :] * jnp.concatenate([acc_lo, acc_hi], axis=1)
    if final:
        ms = jnp.mean(x * x, axis=-1, keepdims=True)
        x = x * lax.rsqrt(ms + EPS) * gfin_ref[...]
    o_ref[...] = x


def _combine(dest, x, mod, mod_row0, tiles_per_row, gates, g_final, yb, *, final):
    n = x.shape[0]
    return pl.pallas_call(
        functools.partial(_combine_kernel, final=final),
        grid=(n // TM,),
        in_specs=[
            pl.BlockSpec((1, 1, TM * TOP_K), lambda i: (i, 0, 0), memory_space=pltpu.SMEM),
            pl.BlockSpec((TM, D_MODEL), lambda i: (i, 0)),
            pl.BlockSpec((1, N_MOD, D_MODEL), lambda i: (mod_row0 + i // tiles_per_row, 0, 0)),
            pl.BlockSpec((TM, LANES), lambda i: (i, 0)),
            pl.BlockSpec((1, D_MODEL), lambda i: (0, 0)),
            pl.BlockSpec(memory_space=pl.ANY),
        ],
        out_specs=pl.BlockSpec((TM, D_MODEL), lambda i: (i, 0)),
        out_shape=jax.ShapeDtypeStruct((n, D_MODEL), F32),
        scratch_shapes=[pltpu.VMEM((TOP_K, TM, D_PACK), jnp.uint32), pltpu.SemaphoreType.DMA(())],
        compiler_params=_cparams(("arbitrary",)),
        name="combine_final" if final else "combine",
    )(dest, x, mod, gates, g_final, yb)


def _rope_tables(n_tokens):
    t = np.arange(n_tokens)
    row = (t // GRID_W).astype(np.float32)
    col = (t % GRID_W).astype(np.float32)
    inv = jnp.asarray(ROPE_THETA, F32) ** (-jnp.arange(ROPE_AXIS_PAIRS, dtype=F32) / ROPE_AXIS_PAIRS)
    ang = jnp.concatenate([jnp.asarray(row)[:, None] * inv, jnp.asarray(col)[:, None] * inv], axis=-1)
    cos, sin = jnp.cos(ang), jnp.sin(ang)
    cos = jnp.tile(cos, (1, LANES // ROPE_HALF))
    sin = jnp.tile(jnp.concatenate([-sin, sin], axis=-1), (1, LANES // HEAD_DIM))
    return cos, sin


def _dup_kv_heads(t):
    b, p = t.shape[:2]
    t = jnp.broadcast_to(t[:, :, :, None, :], (b, p, A_KV_HEADS, 2, HEAD_DIM))
    return t.reshape(b, p, 4 * HEAD_DIM).astype(BF16)


def kernel(x_prompt, x_sample, cache_k_a, cache_v_a, cache_k_b, cache_v_b, cache_k_c, cache_v_c, c, c_ctx,
           w_ada, b_ada, norm_attn, norm_ffn, w_in, sinks, rpb, lambda_qk, subln, w_branch, w_out,
           w_router, b_router, w_gu, b_gu, w_down, b_down, norm_final):
    bsz, seq = x_prompt.shape[:2]
    bsz_d, n_lat = x_sample.shape[:2]
    past = cache_k_a.shape[2]
    n_p = bsz * seq
    n_s = bsz_d * n_lat
    assert seq % TM == 0 and n_lat % TQ == 0 and n_lat % C_CHUNK == 0 and bsz_d + 1 <= 8
    assert (n_p + n_s) % ROUTE_TM == 0 and n_lat % C_TQ == 0 and n_p % MERGE_TM == 0 and n_lat % MERGE_TM == 0

    cond = jnp.zeros((8, D_MODEL), F32).at[0].set(c_ctx).at[1:1 + bsz_d].set(c)
    mod_all = _modulation(cond, w_ada, b_ada).reshape(DEPTH, 8, N_MOD, D_MODEL)
    cos, sin = _rope_tables(n_lat)
    g_final = norm_final[None, :]
    w_bf = w_in.astype(BF16)
    n_assign = (n_p + n_s) * TOP_K
    m_rows = (n_assign + N_EXPERTS * (MOE_TM - 1) + MOE_TM - 1) // MOE_TM * MOE_TM

    xp = x_prompt.reshape(n_p, D_MODEL)
    xs = x_sample.reshape(n_s, D_MODEL)
    kv_out = []
    for l in range(DEPTH):
        lambda_init = 0.8 - 0.6 * math.exp(-0.3 * l)
        mod = mod_all[l]
        g_attn = norm_attn[l][None, :]
        g_ffn = norm_ffn[l][None, :]
        sg = subln[l][None, :]
        wb = w_branch[l].astype(BF16)
        wo = w_out[l].astype(BF16)
        br = b_router[l][None, :]

        pa, pb, pc, pg, *kv = _project(xp, mod, 0, n_p // TM, g_attn, w_bf, l, cos, sin, rope=False, emit_kv=True)
        oa, ob, oc = _ctx_attention(pa, pb, pc, sinks[l], lambda_qk[l], sg, seq, lambda_init)
        xp_mid, hp, idx_p, gate_p = _merge_route(xp, mod, 0, n_p // MERGE_TM, g_ffn, oa, ob, oc, pg, wb, wo,
                                                 w_router[l], br)
        kv_out.append(kv)

        pa, pb, pc, pg = _project(xs, mod, 1, n_lat // TM, g_attn, w_bf, l, cos, sin, rope=True, emit_kv=False)
        oa = _win_attention(pa, _dup_kv_heads(cache_k_a[:, l]), _dup_kv_heads(cache_v_a[:, l]), sinks[l],
                            bsz_d, n_lat)
        ob = _na_attention(pb, cache_k_b[:, l].reshape(bsz_d, past, -1).astype(BF16),
                           cache_v_b[:, l].reshape(bsz_d, past, -1).astype(BF16),
                           _na_bias_table(rpb.reshape(-1), l, n_lat // GRID_W), bsz_d, n_lat)
        oc = _diff_attention(pc, cache_k_c[:, l].reshape(bsz_d, past, -1).astype(BF16),
                             cache_v_c[:, l].reshape(bsz_d, past, -1).astype(BF16),
                             lambda_qk[l], sg, bsz_d, n_lat, lambda_init)
        xs_mid, hs, idx_s, gate_s = _merge_route(xs, mod, 1, n_lat // MERGE_TM, g_ffn, oa, ob, oc, pg, wb, wo,
                                                 w_router[l], br)

        last = l == DEPTH - 1
        dest_slab, cnt = _route(jnp.concatenate([idx_p, idx_s], axis=0))
        counts = cnt[0, :N_EXPERTS].astype(jnp.int32)
        pad_end, block_e, block_first, block_valid = _moe_block_tables(counts, m_rows // MOE_TM)
        dest = dest_slab[:, :TOP_K].reshape(-1, 1, TM * TOP_K)
        dest_p, dest_s = dest[:n_p // TM], dest[n_p // TM:]
        xb = _dispatch(pad_end, counts, dest, hp, hs, m_rows)
        yb = _moe_experts(xb, block_e, block_first, block_valid, l, w_gu, b_gu, w_down, b_down)
        xp = _combine(dest_p, xp_mid, mod, 0, n_p // TM, gate_p, g_final, yb, final=last)
        xs = _combine(dest_s, xs_mid, mod, 1, n_lat // TM, gate_s, g_final, yb, final=last)

    y_prompt = xp.reshape(bsz, seq, D_MODEL)
    y_sample = xs.reshape(bsz_d, n_lat, D_MODEL)

    def stacked(j, heads, width):
        return jnp.stack([kv[j].reshape(bsz, seq, heads, width) for kv in kv_out], axis=1)

    new_k_a, new_v_a = stacked(0, A_KV_HEADS, HEAD_DIM), stacked(1, A_KV_HEADS, HEAD_DIM)
    new_k_b, new_v_b = stacked(2, B_HEADS, HEAD_DIM), stacked(3, B_HEADS, HEAD_DIM)
    new_k_c, new_v_c = stacked(4, C_HEADS, 2 * HEAD_DIM), stacked(5, C_HEADS, C_VDIM)
    return (y_prompt, y_sample, new_k_a, new_v_a, new_k_b, new_v_b, new_k_c, new_v_c)
```

```python
import functools
import math

import jax
import jax.numpy as jnp
import numpy as np
from jax import lax
from jax.experimental import pallas as pl
from jax.experimental.pallas import tpu as pltpu

F32 = jnp.float32
BF16 = jnp.bfloat16

D_MODEL = 1024
DEPTH = 2
GRID_W = 64
HEAD_DIM = 64
ROPE_HALF = HEAD_DIM // 2
ROPE_AXIS_PAIRS = HEAD_DIM // 4
ROPE_THETA = 10000.0
A_HEADS = 8
A_KV_HEADS = 2
WINDOW = 128
B_HEADS = 8
NA_KH = 8
NA_KW = 16
C_HEADS = 4
C_VDIM = 2 * HEAD_DIM
BRANCH_W = 512
N_EXPERTS = 32
TOP_K = 4
D_FF = D_MODEL
SWIGLU_LIMIT = 7.0
SWIGLU_ALPHA = 1.702
N_MOD = 6
EPS = 1e-6
NEG_INF = -1e30
ATTN_SCALE = HEAD_DIM ** -0.5
LOG2E = 1.4426950408889634
C_QSCALE = ATTN_SCALE * LOG2E

LANES = 128
VMEM_LIMIT = 56 * 1024 * 1024

PROJ_WIDTHS = dict(qa=512, ka=128, va=128, qb=512, kb=512, vb=512, qc=512, kc=512, vc=512,
                   ga=D_MODEL, gb=D_MODEL, gc=D_MODEL)
PROJ_OFFSETS = dict(zip(PROJ_WIDTHS, np.cumsum([0] + list(PROJ_WIDTHS.values())[:-1]).tolist()))
D_PROJ = sum(PROJ_WIDTHS.values())
A_COLS = 1024
B_COLS = 1536
C_COLS = 1536
G_COLS = 3 * D_MODEL

TM = 256
TQ = 256
NA_ROWS_Q = TQ // GRID_W
NA_ROWS_K = NA_ROWS_Q + NA_KH
NA_KEYS = NA_ROWS_K * GRID_W
WIN_KEYS = TQ + 2 * WINDOW
NA_COLS_PER_STEP = 4
C_TQ = 512
C_CHUNK = 256
C_VT_ROWS = C_VDIM + 16
MOE_TM = 512
ROUTE_TM = 1024
MERGE_TM = 512
MERGE_ROWS = 256
D_PACK = D_MODEL // 2


def _cparams(sem):
    return pltpu.CompilerParams(dimension_semantics=sem, vmem_limit_bytes=VMEM_LIMIT)


def _head_masks():
    lane = lax.broadcasted_iota(jnp.int32, (1, LANES), 1)
    lo = jnp.where(lane < HEAD_DIM, 1.0, 0.0).astype(BF16)
    hi = jnp.where(lane >= HEAD_DIM, 1.0, 0.0).astype(BF16)
    return lo, hi


def _split_heads(q):
    lo, hi = _head_masks()
    return jnp.concatenate([q * lo, q * hi], axis=0)


def _merge_heads(pv, t):
    lane = lax.broadcasted_iota(jnp.int32, (t, LANES), 1)
    return jnp.where(lane < HEAD_DIM, pv[:t], pv[t:])


def _pack_rows(x):
    half = x.shape[1] // 2
    lo = lax.bitcast_convert_type(x[:, :half].astype(BF16).astype(F32), jnp.uint32)
    hi = lax.bitcast_convert_type(x[:, half:].astype(BF16).astype(F32), jnp.uint32)
    return (lo >> 16) | (hi & jnp.uint32(0xFFFF0000))


def _unpack_rows(w):
    lo = lax.bitcast_convert_type(w << 16, F32)
    hi = lax.bitcast_convert_type(w & jnp.uint32(0xFFFF0000), F32)
    return lo, hi


def _dot_nt(a, b):
    return lax.dot_general(a, b, (((1,), (1,)), ((), ())), preferred_element_type=F32)


def _dot(a, b):
    return jnp.dot(a, b, preferred_element_type=F32)


def _ada_kernel(c_ref, w_ref, b_ref, o_ref):
    c = c_ref[...]
    s = c * (1.0 / (1.0 + jnp.exp(-c)))
    o_ref[0] = jnp.dot(s, w_ref[0], preferred_element_type=F32,
                       precision=lax.Precision.HIGHEST) + b_ref[0]


def _modulation(cond, w_ada, b_ada):
    tn = 1536
    n = N_MOD * D_MODEL
    return pl.pallas_call(
        _ada_kernel,
        grid=(DEPTH, n // tn),
        in_specs=[
            pl.BlockSpec((8, D_MODEL), lambda l, j: (0, 0)),
            pl.BlockSpec((1, D_MODEL, tn), lambda l, j: (l, 0, j)),
            pl.BlockSpec((1, 1, tn), lambda l, j: (l, 0, j)),
        ],
        out_specs=pl.BlockSpec((1, 8, tn), lambda l, j: (l, 0, j)),
        out_shape=jax.ShapeDtypeStruct((DEPTH, 8, n), F32),
        compiler_params=_cparams(("arbitrary", "arbitrary")),
        name="modulation",
    )(cond, w_ada, b_ada.reshape(DEPTH, 1, n))


def _rms_modulate(x, g, shift, scale):
    ms = jnp.mean(x * x, axis=-1, keepdims=True)
    return (x * lax.rsqrt(ms + EPS) * g) * (1.0 + scale) + shift


def _rope_cols(v, cos, sin_signed):
    t, w = v.shape
    lane = lax.broadcasted_iota(jnp.int32, (t, LANES), 1)
    first_half = (lane & (HEAD_DIM - 1)) < ROPE_HALF
    cols = []
    for c in range(w // LANES):
        xc = v[:, c * LANES:(c + 1) * LANES]
        partner = jnp.where(first_half,
                            pltpu.roll(xc, LANES - ROPE_HALF, axis=1),
                            pltpu.roll(xc, ROPE_HALF, axis=1))
        cols.append(xc * cos + partner * sin_signed)
    return jnp.concatenate(cols, axis=1) if len(cols) > 1 else cols[0]


def _dup_kv_cols(v):
    lane = lax.broadcasted_iota(jnp.int32, v.shape, 1)
    swapped = pltpu.roll(v, HEAD_DIM, axis=1)
    return jnp.concatenate([jnp.where(lane < HEAD_DIM, v, swapped), jnp.where(lane < HEAD_DIM, swapped, v)], axis=1)


def _proj_kernel(x_ref, mod_ref, g_ref, w_ref, cos_ref, sin_ref, *out_refs, rope, emit_kv):
    oa_ref, ob_ref, oc_ref, og_ref = out_refs[:4]
    h = _rms_modulate(x_ref[...], g_ref[...], mod_ref[0, 0:1, :], mod_ref[0, 1:2, :]).astype(BF16)
    if rope:
        cos = cos_ref[...]
        sin = sin_ref[...]

    def seg(name):
        start = PROJ_OFFSETS[name]
        return _dot(h, w_ref[0, :, start:start + PROJ_WIDTHS[name]])

    def maybe_rope(v):
        return _rope_cols(v, cos, sin) if rope else v

    oa_ref[:, 0:512] = (maybe_rope(seg("qa")) * ATTN_SCALE).astype(BF16)
    ka = maybe_rope(seg("ka"))
    va = seg("va")
    oa_ref[:, 512:768] = _dup_kv_cols(ka).astype(BF16)
    oa_ref[:, 768:1024] = _dup_kv_cols(va).astype(BF16)
    ob_ref[:, 0:512] = (seg("qb") * ATTN_SCALE).astype(BF16)
    kb = seg("kb")
    vb = seg("vb")
    ob_ref[:, 512:1024] = kb.astype(BF16)
    ob_ref[:, 1024:1536] = vb.astype(BF16)
    oc_ref[:, 0:512] = (maybe_rope(seg("qc")) * C_QSCALE).astype(BF16)
    kc = maybe_rope(seg("kc"))
    vc = seg("vc")
    oc_ref[:, 512:1024] = kc.astype(BF16)
    oc_ref[:, 1024:1536] = vc.astype(BF16)
    for j, name in enumerate(("ga", "gb", "gc")):
        for c in range(D_MODEL // 512):
            start = PROJ_OFFSETS[name] + c * 512
            gv = _dot(h, w_ref[0, :, start:start + 512])
            og_ref[:, j * D_MODEL + c * 512:j * D_MODEL + (c + 1) * 512] = (1.0 / (1.0 + jnp.exp(-gv))).astype(BF16)
    if emit_kv:
        for ref, val in zip(out_refs[4:], (ka, va, kb, vb, kc, vc)):
            ref[...] = val


def _project(x, mod, mod_row0, tiles_per_row, g, w_bf, layer, cos, sin, *, rope, emit_kv):
    n = x.shape[0]
    nt = n // TM
    pos_tiles = cos.shape[0] // TM
    widths = [A_COLS, B_COLS, C_COLS, G_COLS]
    dtypes = [BF16] * 4
    if emit_kv:
        widths += [PROJ_WIDTHS[k] for k in ("ka", "va", "kb", "vb", "kc", "vc")]
        dtypes += [F32] * 6
    return pl.pallas_call(
        functools.partial(_proj_kernel, rope=rope, emit_kv=emit_kv),
        grid=(nt,),
        in_specs=[
            pl.BlockSpec((TM, D_MODEL), lambda i: (i, 0)),
            pl.BlockSpec((1, N_MOD, D_MODEL), lambda i: (mod_row0 + i // tiles_per_row, 0, 0)),
            pl.BlockSpec((1, D_MODEL), lambda i: (0, 0)),
            pl.BlockSpec((1, D_MODEL, D_PROJ), lambda i: (layer, 0, 0)),
            pl.BlockSpec((TM, LANES), lambda i: (i % pos_tiles, 0)),
            pl.BlockSpec((TM, LANES), lambda i: (i % pos_tiles, 0)),
        ],
        out_specs=[pl.BlockSpec((TM, w), lambda i: (i, 0)) for w in widths],
        out_shape=[jax.ShapeDtypeStruct((n, w), d) for w, d in zip(widths, dtypes)],
        compiler_params=_cparams(("arbitrary",)),
        name="project_rope" if rope else "project",
    )(x, mod, g, w_bf, cos, sin)


def _softmax_pv(scores, values, sink=None):
    m = scores[0].max(axis=-1, keepdims=True)
    for s in scores[1:]:
        m = jnp.maximum(m, s.max(axis=-1, keepdims=True))
    if sink is not None:
        m = jnp.maximum(m, sink)
    l = None
    acc = None
    for s, v in zip(scores, values):
        e = jnp.exp(s - m)
        ls = e.sum(axis=-1, keepdims=True)
        pv = _dot(e.astype(BF16), v)
        l = ls if l is None else l + ls
        acc = pv if acc is None else acc + pv
    if sink is not None:
        l = l + jnp.exp(sink - m)
    return acc * (1.0 / l)


def _sink_column(sink_ref, first_head, rows_per_head, n_heads):
    row = lax.broadcasted_iota(jnp.int32, (rows_per_head * n_heads, 1), 0)
    col = jnp.full((rows_per_head * n_heads, 1), sink_ref[first_head], F32)
    for j in range(1, n_heads):
        col = jnp.where(row >= j * rows_per_head, sink_ref[first_head + j], col)
    return col


def _diff_lambda(lq_ref, lambda_init):
    lf = lq_ref[...]
    a = jnp.sum(lf[0:1] * lf[1:2], axis=-1, keepdims=True)
    b = jnp.sum(lf[2:3] * lf[3:4], axis=-1, keepdims=True)
    return jnp.exp(a) - jnp.exp(b) + lambda_init


def _subln(o, g, lambda_init):
    ms = jnp.mean(o * o, axis=-1, keepdims=True)
    return (o * lax.rsqrt(ms + EPS) * g) * (1.0 - lambda_init)


def _ctx_attn_kernel(sink_ref, lq_ref, sg_ref, a_ref, b_ref, c_ref, oa_ref, ob_ref, oc_ref, *, lambda_init):
    t = a_ref.shape[0]
    lam = _diff_lambda(lq_ref, lambda_init)
    chains = []

    for g in range(A_KV_HEADS):
        def scores_a(g=g):
            q = a_ref[:, g * 256:(g + 1) * 256]
            lhs = jnp.concatenate([_split_heads(q[:, :LANES]), _split_heads(q[:, LANES:])], axis=0)
            return _dot_nt(lhs, a_ref[:, 512 + g * LANES:512 + (g + 1) * LANES])

        def finish_a(s, g=g):
            v = a_ref[:, 768 + g * LANES:768 + (g + 1) * LANES]
            o = _softmax_pv([s], [v], _sink_column(sink_ref, 4 * g, t, 4))
            oa_ref[:, g * 256:g * 256 + LANES] = _merge_heads(o[:2 * t], t).astype(BF16)
            oa_ref[:, g * 256 + LANES:(g + 1) * 256] = _merge_heads(o[2 * t:], t).astype(BF16)

        chains.append((scores_a, finish_a))

    for c in range(B_HEADS // 2):
        def scores_b(c=c):
            return _dot_nt(_split_heads(b_ref[:, c * LANES:(c + 1) * LANES]),
                           b_ref[:, 512 + c * LANES:512 + (c + 1) * LANES])

        def finish_b(s, c=c):
            o = _softmax_pv([s], [b_ref[:, 1024 + c * LANES:1024 + (c + 1) * LANES]])
            ob_ref[:, c * LANES:(c + 1) * LANES] = _merge_heads(o, t).astype(BF16)

        chains.append((scores_b, finish_b))

    for h in range(C_HEADS):
        def scores_c(h=h):
            return _dot_nt(_split_heads(c_ref[:, h * LANES:(h + 1) * LANES]),
                           c_ref[:, 512 + h * LANES:512 + (h + 1) * LANES])

        def finish_c(s, h=h):
            m = s.max(axis=-1, keepdims=True)
            e = jnp.exp2(s - m)
            p = e * (1.0 / e.sum(axis=-1, keepdims=True))
            a = (p[:t] - lam * p[t:]).astype(BF16)
            o = _dot(a, c_ref[:, 1024 + h * LANES:1024 + (h + 1) * LANES])
            oc_ref[:, h * LANES:(h + 1) * LANES] = _subln(o, sg_ref[...], lambda_init).astype(BF16)

        chains.append((scores_c, finish_c))

    s = chains[0][0]()
    for i, (_, finish) in enumerate(chains):
        s_next = chains[i + 1][0]() if i + 1 < len(chains) else None
        finish(s)
        s = s_next


def _ctx_attention(pa, pb, pc, sink, lq, sg, seq, lambda_init):
    n = pa.shape[0]
    smem = pl.BlockSpec(memory_space=pltpu.SMEM)
    return pl.pallas_call(
        functools.partial(_ctx_attn_kernel, lambda_init=lambda_init),
        grid=(n // seq,),
        in_specs=[
            smem,
            pl.BlockSpec((4, HEAD_DIM), lambda b: (0, 0)),
            pl.BlockSpec((1, C_VDIM), lambda b: (0, 0)),
            pl.BlockSpec((seq, A_COLS), lambda b: (b, 0)),
            pl.BlockSpec((seq, B_COLS), lambda b: (b, 0)),
            pl.BlockSpec((seq, C_COLS), lambda b: (b, 0)),
        ],
        out_specs=[pl.BlockSpec((seq, BRANCH_W), lambda b: (b, 0))] * 3,
        out_shape=[jax.ShapeDtypeStruct((n, BRANCH_W), BF16)] * 3,
        compiler_params=_cparams(("arbitrary",)),
        name="context_attention",
    )(sink, lq, sg, pa, pb, pc)


def _win_attn_kernel(sink_ref, q_ref, k_ref, v_ref, kc_ref, vc_ref, o_ref, *, n_lat):
    qi = pl.program_id(1)
    ws = pl.multiple_of(jnp.clip(qi * TQ - WINDOW, 0, n_lat - WIN_KEYS), WINDOW)
    qpos = qi * TQ + lax.broadcasted_iota(jnp.int32, (TQ, WIN_KEYS), 0)
    kpos = ws + lax.broadcasted_iota(jnp.int32, (TQ, WIN_KEYS), 1)
    band = jnp.where(jnp.abs(kpos - qpos) <= WINDOW, 0.0, NEG_INF)
    band = jnp.concatenate([band] * 4, axis=0)
    def scores(g):
        q = q_ref[:, g * 256:(g + 1) * 256]
        lhs = jnp.concatenate([_split_heads(q[:, :LANES]), _split_heads(q[:, LANES:])], axis=0)
        cols = slice(g * LANES, (g + 1) * LANES)
        return [_dot_nt(lhs, k_ref[pl.ds(ws, WIN_KEYS), cols]) + band, _dot_nt(lhs, kc_ref[0, :, cols])]

    s = scores(0)
    for g in range(A_KV_HEADS):
        s_next = scores(g + 1) if g + 1 < A_KV_HEADS else None
        cols = slice(g * LANES, (g + 1) * LANES)
        sink = _sink_column(sink_ref, 4 * g, TQ, 4)
        o = _softmax_pv(s, [v_ref[pl.ds(ws, WIN_KEYS), cols], vc_ref[0, :, cols]], sink)
        o_ref[:, g * 256:g * 256 + LANES] = _merge_heads(o[:2 * TQ], TQ).astype(BF16)
        o_ref[:, g * 256 + LANES:(g + 1) * 256] = _merge_heads(o[2 * TQ:], TQ).astype(BF16)
        s = s_next


def _win_attention(pa, kctx, vctx, sink, n_batch, n_lat):
    nq = n_lat // TQ
    past = kctx.shape[1]
    return pl.pallas_call(
        functools.partial(_win_attn_kernel, n_lat=n_lat),
        grid=(n_batch, nq),
        in_specs=[
            pl.BlockSpec(memory_space=pltpu.SMEM),
            pl.BlockSpec((TQ, 512), lambda b, i: (b * nq + i, 0)),
            pl.BlockSpec((n_lat, 256), lambda b, i: (b, 2)),
            pl.BlockSpec((n_lat, 256), lambda b, i: (b, 3)),
            pl.BlockSpec((1, past, 256), lambda b, i: (b, 0, 0)),
            pl.BlockSpec((1, past, 256), lambda b, i: (b, 0, 0)),
        ],
        out_specs=pl.BlockSpec((TQ, BRANCH_W), lambda b, i: (b * nq + i, 0)),
        out_shape=jax.ShapeDtypeStruct((n_batch * n_lat, BRANCH_W), BF16),
        compiler_params=_cparams(("arbitrary", "arbitrary")),
        name="window_attention",
    )(sink, pa, pa, pa, kctx, vctx)


def _na_window_start(qi, rows):
    return jnp.clip(qi * NA_ROWS_Q - NA_KH // 2, 0, rows - NA_ROWS_K)


def _na_attn_kernel(q_ref, k_ref, v_ref, kc_ref, vc_ref, bias_ref, o_ref, *, rows):
    qi = pl.program_id(2)
    ws = pl.multiple_of(_na_window_start(qi, rows) * GRID_W, GRID_W)
    def scores(c):
        cols = slice(c * LANES, (c + 1) * LANES)
        lhs = _split_heads(q_ref[:, cols])
        s_loc = _dot_nt(lhs, k_ref[pl.ds(ws, NA_KEYS), cols]) + bias_ref[0, 2 * c:2 * c + 2].reshape(2 * TQ, NA_KEYS)
        return [s_loc, _dot_nt(lhs, kc_ref[0, :, cols])]

    s = scores(0)
    for c in range(NA_COLS_PER_STEP):
        s_next = scores(c + 1) if c + 1 < NA_COLS_PER_STEP else None
        cols = slice(c * LANES, (c + 1) * LANES)
        o = _softmax_pv(s, [v_ref[pl.ds(ws, NA_KEYS), cols], vc_ref[0, :, cols]])
        o_ref[:, cols] = _merge_heads(o, TQ).astype(BF16)
        s = s_next


def _na_bias_kernel(rpb_ref, o_ref, *, layer, rows):
    h = pl.program_id(0)
    n_dr = 2 * NA_KH - 1
    n_dc = 2 * NA_KW - 1
    base = (layer * B_HEADS + h) * n_dr * n_dc
    lane = lax.broadcasted_iota(jnp.int32, (GRID_W, LANES), 1)
    qc = lax.broadcasted_iota(jnp.int32, (GRID_W, LANES), 0)
    kc = lane & (GRID_W - 1)
    second = lane >= GRID_W
    cs = jnp.clip(qc - NA_KW // 2, 0, GRID_W - NA_KW)
    col_ok = (kc >= cs) & (kc < cs + NA_KW)
    dc_map = kc - qc + (NA_KW - 1)
    cache = {}

    def pair_tile(dr0, ok0, ok1):
        key = (dr0 if ok0 else None, dr0 + 1 if ok1 else None)
        if key not in cache:
            if not (ok0 or ok1):
                cache[key] = jnp.full((GRID_W, LANES), NEG_INF, F32)
            else:
                acc = jnp.zeros((GRID_W, LANES), F32)
                for dc in range(n_dc):
                    v0 = rpb_ref[base + dr0 * n_dc + dc] if ok0 else 0.0
                    v1 = rpb_ref[base + (dr0 + 1) * n_dc + dc] if ok1 else 0.0
                    acc = jnp.where(dc_map == dc, jnp.where(second, v1, v0), acc)
                if ok0 and ok1:
                    ok = col_ok
                elif ok0:
                    ok = col_ok & jnp.logical_not(second)
                else:
                    ok = col_ok & second
                cache[key] = jnp.where(ok, acc, NEG_INF)
        return cache[key]

    nq = rows // NA_ROWS_Q
    for p, qi in enumerate((0, 1, nq - 1)):
        r0 = qi * NA_ROWS_Q
        ws = min(max(r0 - NA_KH // 2, 0), rows - NA_ROWS_K)
        for i in range(NA_ROWS_Q):
            r = r0 + i
            lo = min(max(r - NA_KH // 2, 0), rows - NA_KH)
            for jp in range(NA_ROWS_K // 2):
                kr = ws + 2 * jp
                ok0 = lo <= kr < lo + NA_KH
                ok1 = lo <= kr + 1 < lo + NA_KH
                o_ref[p, 0, i * GRID_W:(i + 1) * GRID_W, jp * LANES:(jp + 1) * LANES] = pair_tile(
                    kr - r + NA_KH - 1, ok0, ok1)


def _na_bias_table(rpb_flat, layer, rows):
    return pl.pallas_call(
        functools.partial(_na_bias_kernel, layer=layer, rows=rows),
        grid=(B_HEADS,),
        in_specs=[pl.BlockSpec(memory_space=pltpu.SMEM)],
        out_specs=pl.BlockSpec((3, 1, TQ, NA_KEYS), lambda h: (0, h, 0, 0)),
        out_shape=jax.ShapeDtypeStruct((3, B_HEADS, TQ, NA_KEYS), F32),
        compiler_params=_cparams(("arbitrary",)),
        name="neighborhood_bias",
    )(rpb_flat)


def _na_attention(pb, kctx, vctx, bias, n_batch, n_lat):
    nq = n_lat // TQ
    rows = n_lat // GRID_W
    past = kctx.shape[1]
    w = NA_COLS_PER_STEP * LANES
    n_cb = BRANCH_W // w

    def bias_idx(b, c, i):
        return (jnp.where(i == 0, 0, jnp.where(i == nq - 1, 2, 1)), c, 0, 0)

    return pl.pallas_call(
        functools.partial(_na_attn_kernel, rows=rows),
        grid=(n_batch, n_cb, nq),
        in_specs=[
            pl.BlockSpec((TQ, w), lambda b, c, i: (b * nq + i, c)),
            pl.BlockSpec((n_lat, w), lambda b, c, i: (b, n_cb + c)),
            pl.BlockSpec((n_lat, w), lambda b, c, i: (b, 2 * n_cb + c)),
            pl.BlockSpec((1, past, w), lambda b, c, i: (b, 0, c)),
            pl.BlockSpec((1, past, w), lambda b, c, i: (b, 0, c)),
            pl.BlockSpec((1, 2 * NA_COLS_PER_STEP, TQ, NA_KEYS), bias_idx),
        ],
        out_specs=pl.BlockSpec((TQ, w), lambda b, c, i: (b * nq + i, c)),
        out_shape=jax.ShapeDtypeStruct((n_batch * n_lat, BRANCH_W), BF16),
        compiler_params=_cparams(("arbitrary", "arbitrary", "arbitrary")),
        name="neighborhood_attention",
    )(pb, pb, pb, kctx, vctx, bias)


def _diff_attn_kernel(lq_ref, sg_ref, q_ref, k_ref, v_ref, kc_ref, vc_ref, o_ref, vt_s, vtc_s, acc_s, *,
                      n_lat, lambda_init):
    n_chunks = n_lat // C_CHUNK

    @pl.when(pl.program_id(2) == 0)
    def _():
        for j in range(n_chunks):
            vt_s[j, :C_VDIM] = v_ref[j * C_CHUNK:(j + 1) * C_CHUNK, :].astype(F32).T.astype(BF16)
            vt_s[j, C_VDIM:] = jnp.ones((C_VT_ROWS - C_VDIM, C_CHUNK), BF16)
        vtc_s[:C_VDIM] = vc_ref[0].astype(F32).T.astype(BF16)
        vtc_s[C_VDIM:] = jnp.ones((C_VT_ROWS - C_VDIM, vtc_s.shape[1]), BF16)

    lhs = _split_heads(q_ref[...])
    acc_s[...] = jnp.zeros_like(acc_s)

    def scores(j):
        k = k_ref[j * C_CHUNK:(j + 1) * C_CHUNK, :] if j < n_chunks else kc_ref[0]
        st = _dot_nt(k, lhs)
        return st, st.max(axis=0, keepdims=True)

    tq = q_ref.shape[0]
    m = jnp.full((1, 2 * tq), NEG_INF, F32)
    st, st_max = scores(0)
    for j in range(n_chunks + 1):
        st_next, st_max_next = scores(j + 1) if j < n_chunks else (None, None)
        m_new = jnp.maximum(m, st_max)
        e = jnp.exp2((st - m_new).astype(BF16))
        vt = vt_s[j] if j < n_chunks else vtc_s[...]
        acc_s[...] = jnp.exp2(m - m_new) * acc_s[...] + _dot(vt, e)
        m, st, st_max = m_new, st_next, st_max_next
    o = acc_s[:C_VDIM, :] * (1.0 / acc_s[C_VDIM:C_VDIM + 1, :])
    d = o[:, :tq] - _diff_lambda(lq_ref, lambda_init) * o[:, tq:]
    ms = jnp.mean(d * d, axis=0, keepdims=True)
    y = (d * lax.rsqrt(ms + EPS)).T
    o_ref[...] = (y * sg_ref[...] * (1.0 - lambda_init)).astype(BF16)


def _diff_attention(pc, kctx, vctx, lq, sg, n_batch, n_lat, lambda_init):
    nq = n_lat // C_TQ
    past = kctx.shape[1]
    return pl.pallas_call(
        functools.partial(_diff_attn_kernel, n_lat=n_lat, lambda_init=lambda_init),
        grid=(n_batch, C_HEADS, nq),
        in_specs=[
            pl.BlockSpec((4, HEAD_DIM), lambda b, h, i: (0, 0)),
            pl.BlockSpec((1, C_VDIM), lambda b, h, i: (0, 0)),
            pl.BlockSpec((C_TQ, LANES), lambda b, h, i: (b * nq + i, h)),
            pl.BlockSpec((n_lat, LANES), lambda b, h, i: (b, 4 + h)),
            pl.BlockSpec((n_lat, LANES), lambda b, h, i: (b, 8 + h)),
            pl.BlockSpec((1, past, LANES), lambda b, h, i: (b, 0, h)),
            pl.BlockSpec((1, past, LANES), lambda b, h, i: (b, 0, h)),
        ],
        out_specs=pl.BlockSpec((C_TQ, LANES), lambda b, h, i: (b * nq + i, h)),
        out_shape=jax.ShapeDtypeStruct((n_batch * n_lat, BRANCH_W), BF16),
        scratch_shapes=[pltpu.VMEM((n_lat // C_CHUNK, C_VT_ROWS, C_CHUNK), BF16), pltpu.VMEM((C_VT_ROWS, past), BF16),
                        pltpu.VMEM((C_VT_ROWS, 2 * C_TQ), F32)],
        compiler_params=_cparams(("arbitrary", "arbitrary", "arbitrary")),
        name="differential_attention",
    )(lq, sg, pc, pc, pc, kctx, vctx)


def _merge_kernel(x_ref, mod_ref, g_ref, oa_ref, ob_ref, oc_ref, sg_ref, wb_ref, wo_ref, wr_ref, br_ref,
                  xo_ref, h_ref, idx_ref, gate_ref):
    wr = wr_ref[...]
    w_hi = wr.astype(BF16)
    w_lo = (wr - w_hi.astype(F32)).astype(BF16)

    def gated_branches(rows):
        y = None
        for j, o_ref in enumerate((oa_ref, ob_ref, oc_ref)):
            t = sg_ref[rows, j * D_MODEL:(j + 1) * D_MODEL].astype(F32) * _dot(o_ref[rows, :], wb_ref[j])
            y = t if y is None else y + t
        return y.astype(BF16)

    def finish(rows, y):
        x = x_ref[rows, :] + mod_ref[0, 2:3, :] * _dot(y, wo_ref[...])
        xo_ref[rows, :] = x
        h = _rms_modulate(x, g_ref[...], mod_ref[0, 3:4, :], mod_ref[0, 4:5, :])
        h_ref[rows, :] = _pack_rows(h)
        h_hi = h.astype(BF16)
        h_lo = (h - h_hi.astype(F32)).astype(BF16)
        logits = _dot(h_hi, w_hi) + (_dot(h_hi, w_lo) + _dot(h_lo, w_hi)) + br_ref[...]
        tm = logits.shape[0]
        lane_e = lax.broadcasted_iota(jnp.int32, (tm, N_EXPERTS), 1)
        lane_o = lax.broadcasted_iota(jnp.int32, (tm, LANES), 1)
        idx_out = jnp.zeros((tm, LANES), jnp.int32)
        val_out = jnp.zeros((tm, LANES), F32)
        top = None
        denom = None
        for k in range(TOP_K):
            mx = logits.max(axis=-1, keepdims=True)
            sel = jnp.min(jnp.where(logits == mx, lane_e, N_EXPERTS), axis=-1, keepdims=True)
            logits = jnp.where(lane_e == sel, -jnp.inf, logits)
            if top is None:
                top = mx
            e = jnp.exp(mx - top)
            denom = e if denom is None else denom + e
            idx_out = jnp.where(lane_o == k, sel, idx_out)
            val_out = jnp.where(lane_o == k, e, val_out)
        idx_ref[rows, :] = idx_out
        gate_ref[rows, :] = val_out * (1.0 / denom)

    n_groups = x_ref.shape[0] // MERGE_ROWS
    groups = [slice(i * MERGE_ROWS, (i + 1) * MERGE_ROWS) for i in range(n_groups)]
    y = gated_branches(groups[0])
    for i in range(n_groups):
        y_next = gated_branches(groups[i + 1]) if i + 1 < n_groups else None
        finish(groups[i], y)
        y = y_next


def _merge_route(x, mod, mod_row0, tiles_per_row, g_ffn, oa, ob, oc, sg, wb, wo, wr, br):
    n = x.shape[0]
    row = lambda i: (i, 0)
    fixed2 = lambda i: (0, 0)
    return pl.pallas_call(
        _merge_kernel,
        grid=(n // MERGE_TM,),
        in_specs=[
            pl.BlockSpec((MERGE_TM, D_MODEL), row),
            pl.BlockSpec((1, N_MOD, D_MODEL), lambda i: (mod_row0 + i // tiles_per_row, 0, 0)),
            pl.BlockSpec((1, D_MODEL), fixed2),
            pl.BlockSpec((MERGE_TM, BRANCH_W), row),
            pl.BlockSpec((MERGE_TM, BRANCH_W), row),
            pl.BlockSpec((MERGE_TM, BRANCH_W), row),
            pl.BlockSpec((MERGE_TM, G_COLS), row),
            pl.BlockSpec((3, BRANCH_W, D_MODEL), lambda i: (0, 0, 0)),
            pl.BlockSpec((D_MODEL, D_MODEL), fixed2),
            pl.BlockSpec((D_MODEL, N_EXPERTS), fixed2),
            pl.BlockSpec((1, N_EXPERTS), fixed2),
        ],
        out_specs=[pl.BlockSpec((MERGE_TM, D_MODEL), row), pl.BlockSpec((MERGE_TM, D_PACK), row),
                   pl.BlockSpec((MERGE_TM, LANES), row), pl.BlockSpec((MERGE_TM, LANES), row)],
        out_shape=[jax.ShapeDtypeStruct((n, D_MODEL), F32), jax.ShapeDtypeStruct((n, D_PACK), jnp.uint32),
                   jax.ShapeDtypeStruct((n, LANES), jnp.int32), jax.ShapeDtypeStruct((n, LANES), F32)],
        compiler_params=_cparams(("arbitrary",)),
        name="merge_route",
    )(x, mod, g_ffn, oa, ob, oc, sg, wb, wo, wr, br)


def _moe_kernel(be_ref, first_ref, valid_ref, next_ref, x_ref, wgu_hbm, bgu_ref, wd_hbm, bd_ref, o_ref,
                wgu_f, wd_f, wgu_s, wd_s, sem, *, layer):
    i = pl.program_id(0)

    def fetch(e):
        return (pltpu.make_async_copy(wgu_hbm.at[layer, e], wgu_f, sem.at[0]),
                pltpu.make_async_copy(wd_hbm.at[layer, e], wd_f, sem.at[1]))

    @pl.when(i == 0)
    def _():
        for cp in fetch(be_ref[0]):
            cp.start()

    @pl.when(first_ref[i] == 1)
    def _():
        for cp in fetch(be_ref[i]):
            cp.wait()
        wgu_s[...] = wgu_f[...].astype(BF16)
        wd_s[...] = wd_f[...].astype(BF16)

        @pl.when(next_ref[i] >= 0)
        def _():
            for cp in fetch(next_ref[i]):
                cp.start()

    @pl.when(valid_ref[i] == 1)
    def _():
        x_lo, x_hi = _unpack_rows(x_ref[...])
        gu = (_dot(x_lo.astype(BF16), wgu_s[:D_PACK, :]) + _dot(x_hi.astype(BF16), wgu_s[D_PACK:, :])
              + bgu_ref[0, 0])
        gate = jnp.minimum(gu[:, :D_FF], SWIGLU_LIMIT)
        lin = jnp.clip(gu[:, D_FF:], -SWIGLU_LIMIT, SWIGLU_LIMIT)
        act = gate * (1.0 / (1.0 + jnp.exp(-SWIGLU_ALPHA * gate))) * (lin + 1.0)
        o_ref[...] = _pack_rows(_dot(act.astype(BF16), wd_s[...]) + bd_ref[0, 0])

    @pl.when(valid_ref[i] == 0)
    def _():
        o_ref[...] = jnp.zeros_like(o_ref)


def _moe_experts(xb, block_e, block_first, block_valid, block_next, layer, w_gu, b_gu, w_down, b_down):
    m = xb.shape[0]
    nb = m // MOE_TM
    idx = lambda i, be, bf, bv, bn: (layer, be[i], 0, 0)
    grid_spec = pltpu.PrefetchScalarGridSpec(
        num_scalar_prefetch=4,
        grid=(nb,),
        in_specs=[
            pl.BlockSpec((MOE_TM, D_PACK), lambda i, be, bf, bv, bn: (i, 0)),
            pl.BlockSpec(memory_space=pl.ANY),
            pl.BlockSpec((1, 1, 1, 2 * D_FF), idx),
            pl.BlockSpec(memory_space=pl.ANY),
            pl.BlockSpec((1, 1, 1, D_MODEL), idx),
        ],
        out_specs=pl.BlockSpec((MOE_TM, D_PACK), lambda i, be, bf, bv, bn: (i, 0)),
        scratch_shapes=[pltpu.VMEM((D_MODEL, 2 * D_FF), F32), pltpu.VMEM((D_FF, D_MODEL), F32),
                        pltpu.VMEM((D_MODEL, 2 * D_FF), BF16), pltpu.VMEM((D_FF, D_MODEL), BF16),
                        pltpu.SemaphoreType.DMA((2,))],
    )
    return pl.pallas_call(
        functools.partial(_moe_kernel, layer=layer),
        grid_spec=grid_spec,
        out_shape=jax.ShapeDtypeStruct((m, D_PACK), jnp.uint32),
        compiler_params=_cparams(("arbitrary",)),
        name="moe_experts",
    )(block_e, block_first, block_valid, block_next, xb, w_gu, b_gu.reshape(DEPTH, N_EXPERTS, 1, 2 * D_FF),
      w_down, b_down.reshape(DEPTH, N_EXPERTS, 1, D_MODEL))


def _moe_block_tables(counts, nb):
    padded = (counts + MOE_TM - 1) // MOE_TM * MOE_TM
    pad_end = jnp.cumsum(padded).astype(jnp.int32)
    pos = jnp.arange(nb, dtype=jnp.int32)
    blk = pos * MOE_TM
    block_e = jnp.minimum(jnp.sum((blk[:, None] >= pad_end[None, :]).astype(jnp.int32), axis=1), N_EXPERTS - 1)
    block_valid = (blk < pad_end[-1]).astype(jnp.int32)
    prev = jnp.concatenate([jnp.full((1,), -1, jnp.int32), block_e[:-1]])
    block_first = (block_e != prev).astype(jnp.int32)
    first_at_or_after = lax.cummin(jnp.where(block_first == 1, pos, nb), axis=0, reverse=True)
    next_first = jnp.concatenate([first_at_or_after[1:], jnp.full((1,), nb, jnp.int32)])
    block_next = jnp.where(next_first < nb, block_e[jnp.minimum(next_first, nb - 1)], -1).astype(jnp.int32)
    return pad_end, block_e, block_first, block_valid, block_next


def _route_kernel(idx_ref, dest_ref, cnt_ref, run_s, start_s):
    p = pl.program_id(0)
    i = pl.program_id(1)
    tm = idx_ref.shape[0]
    lane = lax.broadcasted_iota(jnp.int32, (tm, LANES), 1)
    idx = idx_ref[...]
    sel = [idx[:, k:k + 1] for k in range(TOP_K)]
    onehot = jnp.zeros((tm, LANES), F32)
    for k in range(TOP_K):
        onehot = onehot + jnp.where(lane == sel[k], 1.0, 0.0)
    tile_cnt = jnp.sum(onehot, axis=0, keepdims=True)

    @pl.when((p == 0) & (i == 0))
    def _():
        run_s[...] = jnp.zeros_like(run_s)
        cnt_ref[...] = jnp.zeros_like(cnt_ref)

    @pl.when(p == 0)
    def _():
        run_s[...] += tile_cnt

    @pl.when((p == 1) & (i == 0))
    def _():
        cnt = run_s[...]
        padded = ((cnt.astype(jnp.int32) + (MOE_TM - 1)) // MOE_TM * MOE_TM).astype(F32)
        r = lax.broadcasted_iota(jnp.int32, (LANES, LANES), 0)
        c = lax.broadcasted_iota(jnp.int32, (LANES, LANES), 1)
        before = jnp.where(r < c, 1.0, 0.0)
        start = jnp.dot(jnp.broadcast_to(padded, (8, LANES)), before, preferred_element_type=F32,
                        precision=lax.Precision.HIGHEST)
        start_s[...] = start[0:1]
        cnt_ref[...] = jnp.broadcast_to(cnt, cnt_ref.shape)
        run_s[...] = jnp.zeros_like(run_s)

    @pl.when(p == 1)
    def _():
        r = lax.broadcasted_iota(jnp.int32, (tm, tm), 0)
        c = lax.broadcasted_iota(jnp.int32, (tm, tm), 1)
        earlier = jnp.where(c < r, 1.0, 0.0).astype(BF16)
        pos = _dot(earlier, onehot.astype(BF16)) + run_s[...] + start_s[...]
        out = jnp.zeros((tm, LANES), F32)
        for k in range(TOP_K):
            d = jnp.sum(jnp.where(lane == sel[k], pos, 0.0), axis=-1, keepdims=True)
            out = jnp.where(lane == k, d, out)
        dest_ref[...] = out.astype(jnp.int32)
        run_s[...] += tile_cnt


def _route(idx_slab):
    n = idx_slab.shape[0]
    return pl.pallas_call(
        _route_kernel,
        grid=(2, n // ROUTE_TM),
        in_specs=[pl.BlockSpec((ROUTE_TM, LANES), lambda p, i: (i, 0))],
        out_specs=[pl.BlockSpec((ROUTE_TM, LANES), lambda p, i: (i * p, 0)),
                   pl.BlockSpec((8, LANES), lambda p, i: (0, 0))],
        out_shape=[jax.ShapeDtypeStruct((n, LANES), jnp.int32), jax.ShapeDtypeStruct((8, LANES), F32)],
        scratch_shapes=[pltpu.VMEM((1, LANES), F32), pltpu.VMEM((1, LANES), F32)],
        compiler_params=_cparams(("arbitrary", "arbitrary")),
        name="route",
    )(idx_slab)


def _row_copy(src_ref, src_row, dst_ref, dst_row, sem):
    return pltpu.make_async_copy(src_ref.at[pl.ds(src_row, 1)], dst_ref.at[pl.ds(dst_row, 1)], sem)


def _dispatch_kernel(pe_ref, cnt_ref, dest_ref, hp_ref, hs_ref, xb_ref, zero_s, sem, *, tiles_p, first_tail_block):
    i = pl.program_id(0)
    tm = hp_ref.shape[0]
    nb = xb_ref.shape[0] // MOE_TM

    def zero_block(row0):
        return pltpu.make_async_copy(zero_s, xb_ref.at[pl.ds(pl.multiple_of(row0, MOE_TM), MOE_TM)], sem)

    @pl.when(i == 0)
    def _():
        zero_s[...] = jnp.zeros_like(zero_s)
        for start in (True, False):
            for e in range(N_EXPERTS):
                @pl.when(cnt_ref[e] > 0)
                def _():
                    cp = zero_block(pe_ref[e] - MOE_TM)
                    cp.start() if start else cp.wait()
            for b in range(first_tail_block, nb):
                @pl.when(b * MOE_TM >= pe_ref[N_EXPERTS - 1])
                def _():
                    cp = zero_block(b * MOE_TM)
                    cp.start() if start else cp.wait()

    def scatter_rows(h_ref):
        def issue(t, carry):
            for k in range(TOP_K):
                _row_copy(h_ref, t, xb_ref, dest_ref[0, 0, t * TOP_K + k], sem).start(priority=k % 2)
            return carry

        def drain(t, carry):
            for k in range(TOP_K):
                _row_copy(h_ref, 0, xb_ref, 0, sem).wait()
            return carry

        lax.fori_loop(0, tm, issue, 0)
        lax.fori_loop(0, tm, drain, 0)

    @pl.when(i < tiles_p)
    def _():
        scatter_rows(hp_ref)

    @pl.when(i >= tiles_p)
    def _():
        scatter_rows(hs_ref)


def _dispatch(pad_end, counts, dest, hp, hs, m):
    tiles_p = hp.shape[0] // TM
    tiles_s = hs.shape[0] // TM
    grid_spec = pltpu.PrefetchScalarGridSpec(
        num_scalar_prefetch=2,
        grid=(tiles_p + tiles_s,),
        in_specs=[
            pl.BlockSpec((1, 1, TM * TOP_K), lambda i, pe, cn: (i, 0, 0), memory_space=pltpu.SMEM),
            pl.BlockSpec((TM, D_PACK), lambda i, pe, cn: (jnp.minimum(i, tiles_p - 1), 0)),
            pl.BlockSpec((TM, D_PACK), lambda i, pe, cn: (jnp.maximum(i - tiles_p, 0), 0)),
        ],
        out_specs=pl.BlockSpec(memory_space=pl.ANY),
        scratch_shapes=[pltpu.VMEM((MOE_TM, D_PACK), jnp.uint32), pltpu.SemaphoreType.DMA(())],
    )
    return pl.pallas_call(
        functools.partial(_dispatch_kernel, tiles_p=tiles_p, first_tail_block=m // MOE_TM - N_EXPERTS),
        grid_spec=grid_spec,
        out_shape=jax.ShapeDtypeStruct((m, D_PACK), jnp.uint32),
        compiler_params=_cparams(("arbitrary",)),
        name="dispatch",
    )(pad_end, counts, dest, hp, hs)


def _combine_kernel(dest_ref, x_ref, mod_ref, gate_ref, gfin_ref, yb_ref, o_ref, buf, sem, *, final):
    tm = x_ref.shape[0]

    def issue(t, carry):
        for k in range(TOP_K):
            _row_copy(yb_ref, dest_ref[0, 0, t * TOP_K + k], buf.at[k], t, sem).start(priority=k % 2)
        return carry

    def drain(t, carry):
        for k in range(TOP_K):
            _row_copy(yb_ref, 0, buf.at[k], 0, sem).wait()
        return carry

    lax.fori_loop(0, tm, issue, 0)
    lax.fori_loop(0, tm, drain, 0)
    g = gate_ref[...]
    acc_lo = acc_hi = None
    for k in range(TOP_K):
        y_lo, y_hi = _unpack_rows(buf[k])
        acc_lo = g[:, k:k + 1] * y_lo if acc_lo is None else acc_lo + g[:, k:k + 1] * y_lo
        acc_hi = g[:, k:k + 1] * y_hi if acc_hi is None else acc_hi + g[:, k:k + 1] * y_hi
    x = x_ref[...] + mod_ref[0, 5:6, :] * jnp.concatenate([acc_lo, acc_hi], axis=1)
    if final:
        ms = jnp.mean(x * x, axis=-1, keepdims=True)
        x = x * lax.rsqrt(ms + EPS) * gfin_ref[...]
    o_ref[...] = x


def _combine(dest, x, mod, mod_row0, tiles_per_row, gates, g_final, yb, *, final):
    n = x.shape[0]
    return pl.pallas_call(
        functools.partial(_combine_kernel, final=final),
        grid=(n // TM,),
        in_specs=[
            pl.BlockSpec((1, 1, TM * TOP_K), lambda i: (i, 0, 0), memory_space=pltpu.SMEM),
            pl.BlockSpec((TM, D_MODEL), lambda i: (i, 0)),
            pl.BlockSpec((1, N_MOD, D_MODEL), lambda i: (mod_row0 + i // tiles_per_row, 0, 0)),
            pl.BlockSpec((TM, LANES), lambda i: (i, 0)),
            pl.BlockSpec((1, D_MODEL), lambda i: (0, 0)),
            pl.BlockSpec(memory_space=pl.ANY),
        ],
        out_specs=pl.BlockSpec((TM, D_MODEL), lambda i: (i, 0)),
        out_shape=jax.ShapeDtypeStruct((n, D_MODEL), F32),
        scratch_shapes=[pltpu.VMEM((TOP_K, TM, D_PACK), jnp.uint32), pltpu.SemaphoreType.DMA(())],
        compiler_params=_cparams(("arbitrary",)),
        name="combine_final" if final else "combine",
    )(dest, x, mod, gates, g_final, yb)


def _rope_tables(n_tokens):
    t = np.arange(n_tokens)
    row = (t // GRID_W).astype(np.float32)
    col = (t % GRID_W).astype(np.float32)
    inv = jnp.asarray(ROPE_THETA, F32) ** (-jnp.arange(ROPE_AXIS_PAIRS, dtype=F32) / ROPE_AXIS_PAIRS)
    ang = jnp.concatenate([jnp.asarray(row)[:, None] * inv, jnp.asarray(col)[:, None] * inv], axis=-1)
    cos, sin = jnp.cos(ang), jnp.sin(ang)
    cos = jnp.tile(cos, (1, LANES // ROPE_HALF))
    sin = jnp.tile(jnp.concatenate([-sin, sin], axis=-1), (1, LANES // HEAD_DIM))
    return cos, sin


def _dup_kv_heads(t):
    b, p = t.shape[:2]
    t = jnp.broadcast_to(t[:, :, :, None, :], (b, p, A_KV_HEADS, 2, HEAD_DIM))
    return t.reshape(b, p, 4 * HEAD_DIM).astype(BF16)


def kernel(x_prompt, x_sample, cache_k_a, cache_v_a, cache_k_b, cache_v_b, cache_k_c, cache_v_c, c, c_ctx,
           w_ada, b_ada, norm_attn, norm_ffn, w_in, sinks, rpb, lambda_qk, subln, w_branch, w_out,
           w_router, b_router, w_gu, b_gu, w_down, b_down, norm_final):
    bsz, seq = x_prompt.shape[:2]
    bsz_d, n_lat = x_sample.shape[:2]
    past = cache_k_a.shape[2]
    n_p = bsz * seq
    n_s = bsz_d * n_lat
    assert seq % TM == 0 and n_lat % TQ == 0 and n_lat % C_CHUNK == 0 and bsz_d + 1 <= 8
    assert (n_p + n_s) % ROUTE_TM == 0 and n_lat % C_TQ == 0 and n_p % MERGE_TM == 0 and n_lat % MERGE_TM == 0

    cond = jnp.zeros((8, D_MODEL), F32).at[0].set(c_ctx).at[1:1 + bsz_d].set(c)
    mod_all = _modulation(cond, w_ada, b_ada).reshape(DEPTH, 8, N_MOD, D_MODEL)
    cos, sin = _rope_tables(n_lat)
    g_final = norm_final[None, :]
    w_bf = w_in.astype(BF16)
    n_assign = (n_p + n_s) * TOP_K
    m_rows = (n_assign + N_EXPERTS * (MOE_TM - 1) + MOE_TM - 1) // MOE_TM * MOE_TM

    xp = x_prompt.reshape(n_p, D_MODEL)
    xs = x_sample.reshape(n_s, D_MODEL)
    kv_out = []
    for l in range(DEPTH):
        lambda_init = 0.8 - 0.6 * math.exp(-0.3 * l)
        mod = mod_all[l]
        g_attn = norm_attn[l][None, :]
        g_ffn = norm_ffn[l][None, :]
        sg = subln[l][None, :]
        wb = w_branch[l].astype(BF16)
        wo = w_out[l].astype(BF16)
        br = b_router[l][None, :]

        pa, pb, pc, pg, *kv = _project(xp, mod, 0, n_p // TM, g_attn, w_bf, l, cos, sin, rope=False, emit_kv=True)
        oa, ob, oc = _ctx_attention(pa, pb, pc, sinks[l], lambda_qk[l], sg, seq, lambda_init)
        xp_mid, hp, idx_p, gate_p = _merge_route(xp, mod, 0, n_p // MERGE_TM, g_ffn, oa, ob, oc, pg, wb, wo,
                                                 w_router[l], br)
        kv_out.append(kv)

        pa, pb, pc, pg = _project(xs, mod, 1, n_lat // TM, g_attn, w_bf, l, cos, sin, rope=True, emit_kv=False)
        oa = _win_attention(pa, _dup_kv_heads(cache_k_a[:, l]), _dup_kv_heads(cache_v_a[:, l]), sinks[l],
                            bsz_d, n_lat)
        ob = _na_attention(pb, cache_k_b[:, l].reshape(bsz_d, past, -1).astype(BF16),
                           cache_v_b[:, l].reshape(bsz_d, past, -1).astype(BF16),
                           _na_bias_table(rpb.reshape(-1), l, n_lat // GRID_W), bsz_d, n_lat)
        oc = _diff_attention(pc, cache_k_c[:, l].reshape(bsz_d, past, -1).astype(BF16),
                             cache_v_c[:, l].reshape(bsz_d, past, -1).astype(BF16),
                             lambda_qk[l], sg, bsz_d, n_lat, lambda_init)
        xs_mid, hs, idx_s, gate_s = _merge_route(xs, mod, 1, n_lat // MERGE_TM, g_ffn, oa, ob, oc, pg, wb, wo,
                                                 w_router[l], br)

        last = l == DEPTH - 1
        dest_slab, cnt = _route(jnp.concatenate([idx_p, idx_s], axis=0))
        counts = cnt[0, :N_EXPERTS].astype(jnp.int32)
        pad_end, block_e, block_first, block_valid, block_next = _moe_block_tables(counts, m_rows // MOE_TM)
        dest = dest_slab[:, :TOP_K].reshape(-1, 1, TM * TOP_K)
        dest_p, dest_s = dest[:n_p // TM], dest[n_p // TM:]
        xb = _dispatch(pad_end, counts, dest, hp, hs, m_rows)
        yb = _moe_experts(xb, block_e, block_first, block_valid, block_next, l, w_gu, b_gu, w_down, b_down)
        xp = _combine(dest_p, xp_mid, mod, 0, n_p // TM, gate_p, g_final, yb, final=last)
        xs = _combine(dest_s, xs_mid, mod, 1, n_lat // TM, gate_s, g_final, yb, final=last)

    y_prompt = xp.reshape(bsz, seq, D_MODEL)
    y_sample = xs.reshape(bsz_d, n_lat, D_MODEL)

    def stacked(j, heads, width):
        return jnp.stack([kv[j].reshape(bsz, seq, heads, width) for kv in kv_out], axis=1)

    new_k_a, new_v_a = stacked(0, A_KV_HEADS, HEAD_DIM), stacked(1, A_KV_HEADS, HEAD_DIM)
    new_k_b, new_v_b = stacked(2, B_HEADS, HEAD_DIM), stacked(3, B_HEADS, HEAD_DIM)
    new_k_c, new_v_c = stacked(4, C_HEADS, 2 * HEAD_DIM), stacked(5, C_HEADS, C_VDIM)
    return (y_prompt, y_sample, new_k_a, new_v_a, new_k_b, new_v_b, new_k_c, new_v_c)
```

```python
import functools
import math

import jax
import jax.numpy as jnp
import numpy as np
from jax import lax
from jax.experimental import pallas as pl
from jax.experimental.pallas import tpu as pltpu

F32 = jnp.float32
BF16 = jnp.bfloat16

D_MODEL = 1024
DEPTH = 2
GRID_W = 64
HEAD_DIM = 64
ROPE_HALF = HEAD_DIM // 2
ROPE_AXIS_PAIRS = HEAD_DIM // 4
ROPE_THETA = 10000.0
A_HEADS = 8
A_KV_HEADS = 2
WINDOW = 128
B_HEADS = 8
NA_KH = 8
NA_KW = 16
C_HEADS = 4
C_VDIM = 2 * HEAD_DIM
BRANCH_W = 512
N_EXPERTS = 32
TOP_K = 4
D_FF = D_MODEL
SWIGLU_LIMIT = 7.0
SWIGLU_ALPHA = 1.702
N_MOD = 6
EPS = 1e-6
NEG_INF = -1e30
ATTN_SCALE = HEAD_DIM ** -0.5
LOG2E = 1.4426950408889634
C_QSCALE = ATTN_SCALE * LOG2E

LANES = 128
VMEM_LIMIT = 56 * 1024 * 1024

PROJ_WIDTHS = dict(qa=512, ka=128, va=128, qb=512, kb=512, vb=512, qc=512, kc=512, vc=512,
                   ga=D_MODEL, gb=D_MODEL, gc=D_MODEL)
PROJ_OFFSETS = dict(zip(PROJ_WIDTHS, np.cumsum([0] + list(PROJ_WIDTHS.values())[:-1]).tolist()))
D_PROJ = sum(PROJ_WIDTHS.values())
A_COLS = 1024
B_COLS = 1536
C_COLS = 1536
G_COLS = 3 * D_MODEL

TM = 256
TQ = 256
NA_ROWS_Q = TQ // GRID_W
NA_ROWS_K = NA_ROWS_Q + NA_KH
NA_KEYS = NA_ROWS_K * GRID_W
WIN_KEYS = TQ + 2 * WINDOW
NA_COLS_PER_STEP = 4
C_TQ = 512
C_CHUNK = 256
C_VT_ROWS = C_VDIM + 16
MOE_TM = 512
ROUTE_TM = 1024
MERGE_TM = 512
MERGE_ROWS = 256
D_PACK = D_MODEL // 2


def _cparams(sem):
    return pltpu.CompilerParams(dimension_semantics=sem, vmem_limit_bytes=VMEM_LIMIT)


def _head_masks():
    lane = lax.broadcasted_iota(jnp.int32, (1, LANES), 1)
    lo = jnp.where(lane < HEAD_DIM, 1.0, 0.0).astype(BF16)
    hi = jnp.where(lane >= HEAD_DIM, 1.0, 0.0).astype(BF16)
    return lo, hi


def _split_heads(q):
    lo, hi = _head_masks()
    return jnp.concatenate([q * lo, q * hi], axis=0)


def _merge_heads(pv, t):
    lane = lax.broadcasted_iota(jnp.int32, (t, LANES), 1)
    return jnp.where(lane < HEAD_DIM, pv[:t], pv[t:])


def _pack_rows(x):
    half = x.shape[1] // 2
    lo = lax.bitcast_convert_type(x[:, :half].astype(BF16).astype(F32), jnp.uint32)
    hi = lax.bitcast_convert_type(x[:, half:].astype(BF16).astype(F32), jnp.uint32)
    return (lo >> 16) | (hi & jnp.uint32(0xFFFF0000))


def _unpack_rows(w):
    lo = lax.bitcast_convert_type(w << 16, F32)
    hi = lax.bitcast_convert_type(w & jnp.uint32(0xFFFF0000), F32)
    return lo, hi


def _dot_nt(a, b):
    return lax.dot_general(a, b, (((1,), (1,)), ((), ())), preferred_element_type=F32)


def _dot(a, b):
    return jnp.dot(a, b, preferred_element_type=F32)


def _ada_kernel(c_ref, w_ref, b_ref, o_ref):
    c = c_ref[...]
    s = c * (1.0 / (1.0 + jnp.exp(-c)))
    o_ref[0] = jnp.dot(s, w_ref[0], preferred_element_type=F32,
                       precision=lax.Precision.HIGHEST) + b_ref[0]


def _modulation(cond, w_ada, b_ada):
    tn = 1536
    n = N_MOD * D_MODEL
    return pl.pallas_call(
        _ada_kernel,
        grid=(DEPTH, n // tn),
        in_specs=[
            pl.BlockSpec((8, D_MODEL), lambda l, j: (0, 0)),
            pl.BlockSpec((1, D_MODEL, tn), lambda l, j: (l, 0, j)),
            pl.BlockSpec((1, 1, tn), lambda l, j: (l, 0, j)),
        ],
        out_specs=pl.BlockSpec((1, 8, tn), lambda l, j: (l, 0, j)),
        out_shape=jax.ShapeDtypeStruct((DEPTH, 8, n), F32),
        compiler_params=_cparams(("arbitrary", "arbitrary")),
        name="modulation",
    )(cond, w_ada, b_ada.reshape(DEPTH, 1, n))


def _rms_modulate(x, g, shift, scale):
    ms = jnp.mean(x * x, axis=-1, keepdims=True)
    return (x * lax.rsqrt(ms + EPS) * g) * (1.0 + scale) + shift


def _rope_cols(v, cos, sin_signed):
    t, w = v.shape
    lane = lax.broadcasted_iota(jnp.int32, (t, LANES), 1)
    first_half = (lane & (HEAD_DIM - 1)) < ROPE_HALF
    cols = []
    for c in range(w // LANES):
        xc = v[:, c * LANES:(c + 1) * LANES]
        partner = jnp.where(first_half,
                            pltpu.roll(xc, LANES - ROPE_HALF, axis=1),
                            pltpu.roll(xc, ROPE_HALF, axis=1))
        cols.append(xc * cos + partner * sin_signed)
    return jnp.concatenate(cols, axis=1) if len(cols) > 1 else cols[0]


def _dup_kv_cols(v):
    lane = lax.broadcasted_iota(jnp.int32, v.shape, 1)
    swapped = pltpu.roll(v, HEAD_DIM, axis=1)
    return jnp.concatenate([jnp.where(lane < HEAD_DIM, v, swapped), jnp.where(lane < HEAD_DIM, swapped, v)], axis=1)


def _proj_kernel(x_ref, mod_ref, g_ref, w_ref, cos_ref, sin_ref, *out_refs, rope, emit_kv):
    oa_ref, ob_ref, oc_ref, og_ref = out_refs[:4]
    h = _rms_modulate(x_ref[...], g_ref[...], mod_ref[0, 0:1, :], mod_ref[0, 1:2, :]).astype(BF16)
    if rope:
        cos = cos_ref[...]
        sin = sin_ref[...]

    def seg(name):
        start = PROJ_OFFSETS[name]
        return _dot(h, w_ref[0, :, start:start + PROJ_WIDTHS[name]])

    def maybe_rope(v):
        return _rope_cols(v, cos, sin) if rope else v

    oa_ref[:, 0:512] = (maybe_rope(seg("qa")) * ATTN_SCALE).astype(BF16)
    ka = maybe_rope(seg("ka"))
    va = seg("va")
    oa_ref[:, 512:768] = _dup_kv_cols(ka).astype(BF16)
    oa_ref[:, 768:1024] = _dup_kv_cols(va).astype(BF16)
    ob_ref[:, 0:512] = (seg("qb") * ATTN_SCALE).astype(BF16)
    kb = seg("kb")
    vb = seg("vb")
    ob_ref[:, 512:1024] = kb.astype(BF16)
    ob_ref[:, 1024:1536] = vb.astype(BF16)
    oc_ref[:, 0:512] = (maybe_rope(seg("qc")) * C_QSCALE).astype(BF16)
    kc = maybe_rope(seg("kc"))
    vc = seg("vc")
    oc_ref[:, 512:1024] = kc.astype(BF16)
    oc_ref[:, 1024:1536] = vc.astype(BF16)
    for j, name in enumerate(("ga", "gb", "gc")):
        for c in range(D_MODEL // 512):
            start = PROJ_OFFSETS[name] + c * 512
            gv = _dot(h, w_ref[0, :, start:start + 512])
            og_ref[:, j * D_MODEL + c * 512:j * D_MODEL + (c + 1) * 512] = (1.0 / (1.0 + jnp.exp(-gv))).astype(BF16)
    if emit_kv:
        for ref, val in zip(out_refs[4:], (ka, va, kb, vb, kc, vc)):
            ref[...] = val


def _project(x, mod, mod_row0, tiles_per_row, g, w_bf, layer, cos, sin, *, rope, emit_kv):
    n = x.shape[0]
    nt = n // TM
    pos_tiles = cos.shape[0] // TM
    widths = [A_COLS, B_COLS, C_COLS, G_COLS]
    dtypes = [BF16] * 4
    if emit_kv:
        widths += [PROJ_WIDTHS[k] for k in ("ka", "va", "kb", "vb", "kc", "vc")]
        dtypes += [F32] * 6
    return pl.pallas_call(
        functools.partial(_proj_kernel, rope=rope, emit_kv=emit_kv),
        grid=(nt,),
        in_specs=[
            pl.BlockSpec((TM, D_MODEL), lambda i: (i, 0)),
            pl.BlockSpec((1, N_MOD, D_MODEL), lambda i: (mod_row0 + i // tiles_per_row, 0, 0)),
            pl.BlockSpec((1, D_MODEL), lambda i: (0, 0)),
            pl.BlockSpec((1, D_MODEL, D_PROJ), lambda i: (layer, 0, 0)),
            pl.BlockSpec((TM, LANES), lambda i: (i % pos_tiles, 0)),
            pl.BlockSpec((TM, LANES), lambda i: (i % pos_tiles, 0)),
        ],
        out_specs=[pl.BlockSpec((TM, w), lambda i: (i, 0)) for w in widths],
        out_shape=[jax.ShapeDtypeStruct((n, w), d) for w, d in zip(widths, dtypes)],
        compiler_params=_cparams(("arbitrary",)),
        name="project_rope" if rope else "project",
    )(x, mod, g, w_bf, cos, sin)


def _softmax_pv(scores, values, sink=None):
    m = scores[0].max(axis=-1, keepdims=True)
    for s in scores[1:]:
        m = jnp.maximum(m, s.max(axis=-1, keepdims=True))
    if sink is not None:
        m = jnp.maximum(m, sink)
    l = None
    acc = None
    for s, v in zip(scores, values):
        e = jnp.exp(s - m)
        ls = e.sum(axis=-1, keepdims=True)
        pv = _dot(e.astype(BF16), v)
        l = ls if l is None else l + ls
        acc = pv if acc is None else acc + pv
    if sink is not None:
        l = l + jnp.exp(sink - m)
    return acc * (1.0 / l)


def _sink_column(sink_ref, first_head, rows_per_head, n_heads):
    row = lax.broadcasted_iota(jnp.int32, (rows_per_head * n_heads, 1), 0)
    col = jnp.full((rows_per_head * n_heads, 1), sink_ref[first_head], F32)
    for j in range(1, n_heads):
        col = jnp.where(row >= j * rows_per_head, sink_ref[first_head + j], col)
    return col


def _diff_lambda(lq_ref, lambda_init):
    lf = lq_ref[...]
    a = jnp.sum(lf[0:1] * lf[1:2], axis=-1, keepdims=True)
    b = jnp.sum(lf[2:3] * lf[3:4], axis=-1, keepdims=True)
    return jnp.exp(a) - jnp.exp(b) + lambda_init


def _subln(o, g, lambda_init):
    ms = jnp.mean(o * o, axis=-1, keepdims=True)
    return (o * lax.rsqrt(ms + EPS) * g) * (1.0 - lambda_init)


def _ctx_attn_kernel(sink_ref, lq_ref, sg_ref, a_ref, b_ref, c_ref, oa_ref, ob_ref, oc_ref, *, lambda_init):
    t = a_ref.shape[0]
    lam = _diff_lambda(lq_ref, lambda_init)
    chains = []

    for g in range(A_KV_HEADS):
        def scores_a(g=g):
            q = a_ref[:, g * 256:(g + 1) * 256]
            lhs = jnp.concatenate([_split_heads(q[:, :LANES]), _split_heads(q[:, LANES:])], axis=0)
            return _dot_nt(lhs, a_ref[:, 512 + g * LANES:512 + (g + 1) * LANES])

        def finish_a(s, g=g):
            v = a_ref[:, 768 + g * LANES:768 + (g + 1) * LANES]
            o = _softmax_pv([s], [v], _sink_column(sink_ref, 4 * g, t, 4))
            oa_ref[:, g * 256:g * 256 + LANES] = _merge_heads(o[:2 * t], t).astype(BF16)
            oa_ref[:, g * 256 + LANES:(g + 1) * 256] = _merge_heads(o[2 * t:], t).astype(BF16)

        chains.append((scores_a, finish_a))

    for c in range(B_HEADS // 2):
        def scores_b(c=c):
            return _dot_nt(_split_heads(b_ref[:, c * LANES:(c + 1) * LANES]),
                           b_ref[:, 512 + c * LANES:512 + (c + 1) * LANES])

        def finish_b(s, c=c):
            o = _softmax_pv([s], [b_ref[:, 1024 + c * LANES:1024 + (c + 1) * LANES]])
            ob_ref[:, c * LANES:(c + 1) * LANES] = _merge_heads(o, t).astype(BF16)

        chains.append((scores_b, finish_b))

    for h in range(C_HEADS):
        def scores_c(h=h):
            return _dot_nt(_split_heads(c_ref[:, h * LANES:(h + 1) * LANES]),
                           c_ref[:, 512 + h * LANES:512 + (h + 1) * LANES])

        def finish_c(s, h=h):
            m = s.max(axis=-1, keepdims=True)
            e = jnp.exp2(s - m)
            p = e * (1.0 / e.sum(axis=-1, keepdims=True))
            a = (p[:t] - lam * p[t:]).astype(BF16)
            o = _dot(a, c_ref[:, 1024 + h * LANES:1024 + (h + 1) * LANES])
            oc_ref[:, h * LANES:(h + 1) * LANES] = _subln(o, sg_ref[...], lambda_init).astype(BF16)

        chains.append((scores_c, finish_c))

    s = chains[0][0]()
    for i, (_, finish) in enumerate(chains):
        s_next = chains[i + 1][0]() if i + 1 < len(chains) else None
        finish(s)
        s = s_next


def _ctx_attention(pa, pb, pc, sink, lq, sg, seq, lambda_init):
    n = pa.shape[0]
    smem = pl.BlockSpec(memory_space=pltpu.SMEM)
    return pl.pallas_call(
        functools.partial(_ctx_attn_kernel, lambda_init=lambda_init),
        grid=(n // seq,),
        in_specs=[
            smem,
            pl.BlockSpec((4, HEAD_DIM), lambda b: (0, 0)),
            pl.BlockSpec((1, C_VDIM), lambda b: (0, 0)),
            pl.BlockSpec((seq, A_COLS), lambda b: (b, 0)),
            pl.BlockSpec((seq, B_COLS), lambda b: (b, 0)),
            pl.BlockSpec((seq, C_COLS), lambda b: (b, 0)),
        ],
        out_specs=[pl.BlockSpec((seq, BRANCH_W), lambda b: (b, 0))] * 3,
        out_shape=[jax.ShapeDtypeStruct((n, BRANCH_W), BF16)] * 3,
        compiler_params=_cparams(("arbitrary",)),
        name="context_attention",
    )(sink, lq, sg, pa, pb, pc)


def _win_attn_kernel(sink_ref, q_ref, k_ref, v_ref, kc_ref, vc_ref, o_ref, *, n_lat):
    qi = pl.program_id(1)
    ws = pl.multiple_of(jnp.clip(qi * TQ - WINDOW, 0, n_lat - WIN_KEYS), WINDOW)
    qpos = qi * TQ + lax.broadcasted_iota(jnp.int32, (TQ, WIN_KEYS), 0)
    kpos = ws + lax.broadcasted_iota(jnp.int32, (TQ, WIN_KEYS), 1)
    band = jnp.where(jnp.abs(kpos - qpos) <= WINDOW, 0.0, NEG_INF)
    band = jnp.concatenate([band] * 4, axis=0)
    def scores(g):
        q = q_ref[:, g * 256:(g + 1) * 256]
        lhs = jnp.concatenate([_split_heads(q[:, :LANES]), _split_heads(q[:, LANES:])], axis=0)
        cols = slice(g * LANES, (g + 1) * LANES)
        return [_dot_nt(lhs, k_ref[pl.ds(ws, WIN_KEYS), cols]) + band, _dot_nt(lhs, kc_ref[0, :, cols])]

    s = scores(0)
    for g in range(A_KV_HEADS):
        s_next = scores(g + 1) if g + 1 < A_KV_HEADS else None
        cols = slice(g * LANES, (g + 1) * LANES)
        sink = _sink_column(sink_ref, 4 * g, TQ, 4)
        o = _softmax_pv(s, [v_ref[pl.ds(ws, WIN_KEYS), cols], vc_ref[0, :, cols]], sink)
        o_ref[:, g * 256:g * 256 + LANES] = _merge_heads(o[:2 * TQ], TQ).astype(BF16)
        o_ref[:, g * 256 + LANES:(g + 1) * 256] = _merge_heads(o[2 * TQ:], TQ).astype(BF16)
        s = s_next


def _win_attention(pa, kctx, vctx, sink, n_batch, n_lat):
    nq = n_lat // TQ
    past = kctx.shape[1]
    return pl.pallas_call(
        functools.partial(_win_attn_kernel, n_lat=n_lat),
        grid=(n_batch, nq),
        in_specs=[
            pl.BlockSpec(memory_space=pltpu.SMEM),
            pl.BlockSpec((TQ, 512), lambda b, i: (b * nq + i, 0)),
            pl.BlockSpec((n_lat, 256), lambda b, i: (b, 2)),
            pl.BlockSpec((n_lat, 256), lambda b, i: (b, 3)),
            pl.BlockSpec((1, past, 256), lambda b, i: (b, 0, 0)),
            pl.BlockSpec((1, past, 256), lambda b, i: (b, 0, 0)),
        ],
        out_specs=pl.BlockSpec((TQ, BRANCH_W), lambda b, i: (b * nq + i, 0)),
        out_shape=jax.ShapeDtypeStruct((n_batch * n_lat, BRANCH_W), BF16),
        compiler_params=_cparams(("arbitrary", "arbitrary")),
        name="window_attention",
    )(sink, pa, pa, pa, kctx, vctx)


def _na_window_start(qi, rows):
    return jnp.clip(qi * NA_ROWS_Q - NA_KH // 2, 0, rows - NA_ROWS_K)


def _na_attn_kernel(q_ref, k_ref, v_ref, kc_ref, vc_ref, bias_ref, o_ref, *, rows):
    qi = pl.program_id(2)
    ws = pl.multiple_of(_na_window_start(qi, rows) * GRID_W, GRID_W)
    def scores(c):
        cols = slice(c * LANES, (c + 1) * LANES)
        lhs = _split_heads(q_ref[:, cols])
        s_loc = _dot_nt(lhs, k_ref[pl.ds(ws, NA_KEYS), cols]) + bias_ref[0, 2 * c:2 * c + 2].reshape(2 * TQ, NA_KEYS)
        return [s_loc, _dot_nt(lhs, kc_ref[0, :, cols])]

    s = scores(0)
    for c in range(NA_COLS_PER_STEP):
        s_next = scores(c + 1) if c + 1 < NA_COLS_PER_STEP else None
        cols = slice(c * LANES, (c + 1) * LANES)
        o = _softmax_pv(s, [v_ref[pl.ds(ws, NA_KEYS), cols], vc_ref[0, :, cols]])
        o_ref[:, cols] = _merge_heads(o, TQ).astype(BF16)
        s = s_next


def _na_bias_kernel(rpb_ref, o_ref, *, layer, rows):
    h = pl.program_id(0)
    n_dr = 2 * NA_KH - 1
    n_dc = 2 * NA_KW - 1
    base = (layer * B_HEADS + h) * n_dr * n_dc
    lane = lax.broadcasted_iota(jnp.int32, (GRID_W, LANES), 1)
    qc = lax.broadcasted_iota(jnp.int32, (GRID_W, LANES), 0)
    kc = lane & (GRID_W - 1)
    second = lane >= GRID_W
    cs = jnp.clip(qc - NA_KW // 2, 0, GRID_W - NA_KW)
    col_ok = (kc >= cs) & (kc < cs + NA_KW)
    dc_map = kc - qc + (NA_KW - 1)
    cache = {}

    def pair_tile(dr0, ok0, ok1):
        key = (dr0 if ok0 else None, dr0 + 1 if ok1 else None)
        if key not in cache:
            if not (ok0 or ok1):
                cache[key] = jnp.full((GRID_W, LANES), NEG_INF, F32)
            else:
                acc = jnp.zeros((GRID_W, LANES), F32)
                for dc in range(n_dc):
                    v0 = rpb_ref[base + dr0 * n_dc + dc] if ok0 else 0.0
                    v1 = rpb_ref[base + (dr0 + 1) * n_dc + dc] if ok1 else 0.0
                    acc = jnp.where(dc_map == dc, jnp.where(second, v1, v0), acc)
                if ok0 and ok1:
                    ok = col_ok
                elif ok0:
                    ok = col_ok & jnp.logical_not(second)
                else:
                    ok = col_ok & second
                cache[key] = jnp.where(ok, acc, NEG_INF)
        return cache[key]

    nq = rows // NA_ROWS_Q
    for p, qi in enumerate((0, 1, nq - 1)):
        r0 = qi * NA_ROWS_Q
        ws = min(max(r0 - NA_KH // 2, 0), rows - NA_ROWS_K)
        for i in range(NA_ROWS_Q):
            r = r0 + i
            lo = min(max(r - NA_KH // 2, 0), rows - NA_KH)
            for jp in range(NA_ROWS_K // 2):
                kr = ws + 2 * jp
                ok0 = lo <= kr < lo + NA_KH
                ok1 = lo <= kr + 1 < lo + NA_KH
                o_ref[p, 0, i * GRID_W:(i + 1) * GRID_W, jp * LANES:(jp + 1) * LANES] = pair_tile(
                    kr - r + NA_KH - 1, ok0, ok1)


def _na_bias_table(rpb_flat, layer, rows):
    return pl.pallas_call(
        functools.partial(_na_bias_kernel, layer=layer, rows=rows),
        grid=(B_HEADS,),
        in_specs=[pl.BlockSpec(memory_space=pltpu.SMEM)],
        out_specs=pl.BlockSpec((3, 1, TQ, NA_KEYS), lambda h: (0, h, 0, 0)),
        out_shape=jax.ShapeDtypeStruct((3, B_HEADS, TQ, NA_KEYS), F32),
        compiler_params=_cparams(("arbitrary",)),
        name="neighborhood_bias",
    )(rpb_flat)


def _na_attention(pb, kctx, vctx, bias, n_batch, n_lat):
    nq = n_lat // TQ
    rows = n_lat // GRID_W
    past = kctx.shape[1]
    w = NA_COLS_PER_STEP * LANES
    n_cb = BRANCH_W // w

    def bias_idx(b, c, i):
        return (jnp.where(i == 0, 0, jnp.where(i == nq - 1, 2, 1)), c, 0, 0)

    return pl.pallas_call(
        functools.partial(_na_attn_kernel, rows=rows),
        grid=(n_batch, n_cb, nq),
        in_specs=[
            pl.BlockSpec((TQ, w), lambda b, c, i: (b * nq + i, c)),
            pl.BlockSpec((n_lat, w), lambda b, c, i: (b, n_cb + c)),
            pl.BlockSpec((n_lat, w), lambda b, c, i: (b, 2 * n_cb + c)),
            pl.BlockSpec((1, past, w), lambda b, c, i: (b, 0, c)),
            pl.BlockSpec((1, past, w), lambda b, c, i: (b, 0, c)),
            pl.BlockSpec((1, 2 * NA_COLS_PER_STEP, TQ, NA_KEYS), bias_idx),
        ],
        out_specs=pl.BlockSpec((TQ, w), lambda b, c, i: (b * nq + i, c)),
        out_shape=jax.ShapeDtypeStruct((n_batch * n_lat, BRANCH_W), BF16),
        compiler_params=_cparams(("arbitrary", "arbitrary", "arbitrary")),
        name="neighborhood_attention",
    )(pb, pb, pb, kctx, vctx, bias)


def _diff_attn_kernel(lq_ref, sg_ref, q_ref, k_ref, v_ref, kc_ref, vc_ref, o_ref, vt_s, vtc_s, acc_s, *,
                      n_lat, lambda_init):
    n_chunks = n_lat // C_CHUNK

    @pl.when(pl.program_id(2) == 0)
    def _():
        for j in range(n_chunks):
            vt_s[j, :C_VDIM] = v_ref[j * C_CHUNK:(j + 1) * C_CHUNK, :].astype(F32).T.astype(BF16)
            vt_s[j, C_VDIM:] = jnp.ones((C_VT_ROWS - C_VDIM, C_CHUNK), BF16)
        vtc_s[:C_VDIM] = vc_ref[0].astype(F32).T.astype(BF16)
        vtc_s[C_VDIM:] = jnp.ones((C_VT_ROWS - C_VDIM, vtc_s.shape[1]), BF16)

    lhs = _split_heads(q_ref[...])
    acc_s[...] = jnp.zeros_like(acc_s)

    def scores(j):
        k = k_ref[j * C_CHUNK:(j + 1) * C_CHUNK, :] if j < n_chunks else kc_ref[0]
        st = _dot_nt(k, lhs)
        return st, st.max(axis=0, keepdims=True)

    tq = q_ref.shape[0]
    m = jnp.full((1, 2 * tq), NEG_INF, F32)
    st, st_max = scores(0)
    for j in range(n_chunks + 1):
        st_next, st_max_next = scores(j + 1) if j < n_chunks else (None, None)
        m_new = jnp.maximum(m, st_max)
        e = jnp.exp2((st - m_new).astype(BF16))
        vt = vt_s[j] if j < n_chunks else vtc_s[...]
        acc_s[...] = jnp.exp2(m - m_new) * acc_s[...] + _dot(vt, e)
        m, st, st_max = m_new, st_next, st_max_next
    o = acc_s[:C_VDIM, :] * (1.0 / acc_s[C_VDIM:C_VDIM + 1, :])
    d = o[:, :tq] - _diff_lambda(lq_ref, lambda_init) * o[:, tq:]
    ms = jnp.mean(d * d, axis=0, keepdims=True)
    y = (d * lax.rsqrt(ms + EPS)).T
    o_ref[...] = (y * sg_ref[...] * (1.0 - lambda_init)).astype(BF16)


def _diff_attention(pc, kctx, vctx, lq, sg, n_batch, n_lat, lambda_init):
    nq = n_lat // C_TQ
    past = kctx.shape[1]
    return pl.pallas_call(
        functools.partial(_diff_attn_kernel, n_lat=n_lat, lambda_init=lambda_init),
        grid=(n_batch, C_HEADS, nq),
        in_specs=[
            pl.BlockSpec((4, HEAD_DIM), lambda b, h, i: (0, 0)),
            pl.BlockSpec((1, C_VDIM), lambda b, h, i: (0, 0)),
            pl.BlockSpec((C_TQ, LANES), lambda b, h, i: (b * nq + i, h)),
            pl.BlockSpec((n_lat, LANES), lambda b, h, i: (b, 4 + h)),
            pl.BlockSpec((n_lat, LANES), lambda b, h, i: (b, 8 + h)),
            pl.BlockSpec((1, past, LANES), lambda b, h, i: (b, 0, h)),
            pl.BlockSpec((1, past, LANES), lambda b, h, i: (b, 0, h)),
        ],
        out_specs=pl.BlockSpec((C_TQ, LANES), lambda b, h, i: (b * nq + i, h)),
        out_shape=jax.ShapeDtypeStruct((n_batch * n_lat, BRANCH_W), BF16),
        scratch_shapes=[pltpu.VMEM((n_lat // C_CHUNK, C_VT_ROWS, C_CHUNK), BF16), pltpu.VMEM((C_VT_ROWS, past), BF16),
                        pltpu.VMEM((C_VT_ROWS, 2 * C_TQ), F32)],
        compiler_params=_cparams(("arbitrary", "arbitrary", "arbitrary")),
        name="differential_attention",
    )(lq, sg, pc, pc, pc, kctx, vctx)


def _merge_kernel(x_ref, mod_ref, g_ref, oa_ref, ob_ref, oc_ref, sg_ref, wb_ref, wo_ref, wr_ref, br_ref,
                  xo_ref, h_ref, idx_ref, gate_ref):
    wr = wr_ref[...]
    w_hi = wr.astype(BF16)
    w_lo = (wr - w_hi.astype(F32)).astype(BF16)

    def gated_branches(rows):
        y = None
        for j, o_ref in enumerate((oa_ref, ob_ref, oc_ref)):
            t = sg_ref[rows, j * D_MODEL:(j + 1) * D_MODEL].astype(F32) * _dot(o_ref[rows, :], wb_ref[j])
            y = t if y is None else y + t
        return y.astype(BF16)

    def finish(rows, y):
        x = x_ref[rows, :] + mod_ref[0, 2:3, :] * _dot(y, wo_ref[...])
        xo_ref[rows, :] = x
        h = _rms_modulate(x, g_ref[...], mod_ref[0, 3:4, :], mod_ref[0, 4:5, :])
        h_ref[rows, :] = _pack_rows(h)
        h_hi = h.astype(BF16)
        h_lo = (h - h_hi.astype(F32)).astype(BF16)
        logits = _dot(h_hi, w_hi) + (_dot(h_hi, w_lo) + _dot(h_lo, w_hi)) + br_ref[...]
        tm = logits.shape[0]
        lane_e = lax.broadcasted_iota(jnp.int32, (tm, N_EXPERTS), 1)
        lane_o = lax.broadcasted_iota(jnp.int32, (tm, LANES), 1)
        idx_out = jnp.zeros((tm, LANES), jnp.int32)
        val_out = jnp.zeros((tm, LANES), F32)
        top = None
        denom = None
        for k in range(TOP_K):
            mx = logits.max(axis=-1, keepdims=True)
            sel = jnp.min(jnp.where(logits == mx, lane_e, N_EXPERTS), axis=-1, keepdims=True)
            logits = jnp.where(lane_e == sel, -jnp.inf, logits)
            if top is None:
                top = mx
            e = jnp.exp(mx - top)
            denom = e if denom is None else denom + e
            idx_out = jnp.where(lane_o == k, sel, idx_out)
            val_out = jnp.where(lane_o == k, e, val_out)
        idx_ref[rows, :] = idx_out
        gate_ref[rows, :] = val_out * (1.0 / denom)

    n_groups = x_ref.shape[0] // MERGE_ROWS
    groups = [slice(i * MERGE_ROWS, (i + 1) * MERGE_ROWS) for i in range(n_groups)]
    y = gated_branches(groups[0])
    for i in range(n_groups):
        y_next = gated_branches(groups[i + 1]) if i + 1 < n_groups else None
        finish(groups[i], y)
        y = y_next


def _merge_route(x, mod, mod_row0, tiles_per_row, g_ffn, oa, ob, oc, sg, wb, wo, wr, br):
    n = x.shape[0]
    row = lambda i: (i, 0)
    fixed2 = lambda i: (0, 0)
    return pl.pallas_call(
        _merge_kernel,
        grid=(n // MERGE_TM,),
        in_specs=[
            pl.BlockSpec((MERGE_TM, D_MODEL), row),
            pl.BlockSpec((1, N_MOD, D_MODEL), lambda i: (mod_row0 + i // tiles_per_row, 0, 0)),
            pl.BlockSpec((1, D_MODEL), fixed2),
            pl.BlockSpec((MERGE_TM, BRANCH_W), row),
            pl.BlockSpec((MERGE_TM, BRANCH_W), row),
            pl.BlockSpec((MERGE_TM, BRANCH_W), row),
            pl.BlockSpec((MERGE_TM, G_COLS), row),
            pl.BlockSpec((3, BRANCH_W, D_MODEL), lambda i: (0, 0, 0)),
            pl.BlockSpec((D_MODEL, D_MODEL), fixed2),
            pl.BlockSpec((D_MODEL, N_EXPERTS), fixed2),
            pl.BlockSpec((1, N_EXPERTS), fixed2),
        ],
        out_specs=[pl.BlockSpec((MERGE_TM, D_MODEL), row), pl.BlockSpec((MERGE_TM, D_PACK), row),
                   pl.BlockSpec((MERGE_TM, LANES), row), pl.BlockSpec((MERGE_TM, LANES), row)],
        out_shape=[jax.ShapeDtypeStruct((n, D_MODEL), F32), jax.ShapeDtypeStruct((n, D_PACK), jnp.uint32),
                   jax.ShapeDtypeStruct((n, LANES), jnp.int32), jax.ShapeDtypeStruct((n, LANES), F32)],
        compiler_params=_cparams(("arbitrary",)),
        name="merge_route",
    )(x, mod, g_ffn, oa, ob, oc, sg, wb, wo, wr, br)


def _moe_kernel(be_ref, first_ref, valid_ref, next_ref, x_ref, wgu_hbm, bgu_ref, wd_hbm, bd_ref, o_ref,
                wgu_f, wd_f, wgu_s, wd_s, sem, *, layer):
    i = pl.program_id(0)

    def fetch(e):
        return (pltpu.make_async_copy(wgu_hbm.at[layer, e], wgu_f, sem.at[0]),
                pltpu.make_async_copy(wd_hbm.at[layer, e], wd_f, sem.at[1]))

    @pl.when(i == 0)
    def _():
        for cp in fetch(be_ref[0]):
            cp.start()

    @pl.when(first_ref[i] == 1)
    def _():
        for cp in fetch(be_ref[i]):
            cp.wait()
        wgu_s[...] = wgu_f[...].astype(BF16)
        wd_s[...] = wd_f[...].astype(BF16)

        @pl.when(next_ref[i] >= 0)
        def _():
            for cp in fetch(next_ref[i]):
                cp.start()

    @pl.when(valid_ref[i] == 1)
    def _():
        def gate_up(rows):
            x_lo, x_hi = _unpack_rows(x_ref[rows, :])
            return (_dot(x_lo.astype(BF16), wgu_s[:D_PACK, :]) + _dot(x_hi.astype(BF16), wgu_s[D_PACK:, :])
                    + bgu_ref[0, 0])

        def down(rows, gu):
            gate = jnp.minimum(gu[:, :D_FF], SWIGLU_LIMIT)
            lin = jnp.clip(gu[:, D_FF:], -SWIGLU_LIMIT, SWIGLU_LIMIT)
            act = gate * (1.0 / (1.0 + jnp.exp(-SWIGLU_ALPHA * gate))) * (lin + 1.0)
            o_ref[rows, :] = _pack_rows(_dot(act.astype(BF16), wd_s[...]) + bd_ref[0, 0])

        groups = [slice(g * (MOE_TM // 2), (g + 1) * (MOE_TM // 2)) for g in range(2)]
        gu = gate_up(groups[0])
        for g in range(2):
            gu_next = gate_up(groups[g + 1]) if g + 1 < 2 else None
            down(groups[g], gu)
            gu = gu_next

    @pl.when(valid_ref[i] == 0)
    def _():
        o_ref[...] = jnp.zeros_like(o_ref)


def _moe_experts(xb, block_e, block_first, block_valid, block_next, layer, w_gu, b_gu, w_down, b_down):
    m = xb.shape[0]
    nb = m // MOE_TM
    idx = lambda i, be, bf, bv, bn: (layer, be[i], 0, 0)
    grid_spec = pltpu.PrefetchScalarGridSpec(
        num_scalar_prefetch=4,
        grid=(nb,),
        in_specs=[
            pl.BlockSpec((MOE_TM, D_PACK), lambda i, be, bf, bv, bn: (i, 0)),
            pl.BlockSpec(memory_space=pl.ANY),
            pl.BlockSpec((1, 1, 1, 2 * D_FF), idx),
            pl.BlockSpec(memory_space=pl.ANY),
            pl.BlockSpec((1, 1, 1, D_MODEL), idx),
        ],
        out_specs=pl.BlockSpec((MOE_TM, D_PACK), lambda i, be, bf, bv, bn: (i, 0)),
        scratch_shapes=[pltpu.VMEM((D_MODEL, 2 * D_FF), F32), pltpu.VMEM((D_FF, D_MODEL), F32),
                        pltpu.VMEM((D_MODEL, 2 * D_FF), BF16), pltpu.VMEM((D_FF, D_MODEL), BF16),
                        pltpu.SemaphoreType.DMA((2,))],
    )
    return pl.pallas_call(
        functools.partial(_moe_kernel, layer=layer),
        grid_spec=grid_spec,
        out_shape=jax.ShapeDtypeStruct((m, D_PACK), jnp.uint32),
        compiler_params=_cparams(("arbitrary",)),
        name="moe_experts",
    )(block_e, block_first, block_valid, block_next, xb, w_gu, b_gu.reshape(DEPTH, N_EXPERTS, 1, 2 * D_FF),
      w_down, b_down.reshape(DEPTH, N_EXPERTS, 1, D_MODEL))


def _moe_block_tables(counts, nb):
    padded = (counts + MOE_TM - 1) // MOE_TM * MOE_TM
    pad_end = jnp.cumsum(padded).astype(jnp.int32)
    pos = jnp.arange(nb, dtype=jnp.int32)
    blk = pos * MOE_TM
    block_e = jnp.minimum(jnp.sum((blk[:, None] >= pad_end[None, :]).astype(jnp.int32), axis=1), N_EXPERTS - 1)
    block_valid = (blk < pad_end[-1]).astype(jnp.int32)
    prev = jnp.concatenate([jnp.full((1,), -1, jnp.int32), block_e[:-1]])
    block_first = (block_e != prev).astype(jnp.int32)
    first_at_or_after = lax.cummin(jnp.where(block_first == 1, pos, nb), axis=0, reverse=True)
    next_first = jnp.concatenate([first_at_or_after[1:], jnp.full((1,), nb, jnp.int32)])
    block_next = jnp.where(next_first < nb, block_e[jnp.minimum(next_first, nb - 1)], -1).astype(jnp.int32)
    return pad_end, block_e, block_first, block_valid, block_next


def _route_kernel(idx_ref, dest_ref, cnt_ref, run_s, start_s):
    p = pl.program_id(0)
    i = pl.program_id(1)
    tm = idx_ref.shape[0]
    lane = lax.broadcasted_iota(jnp.int32, (tm, LANES), 1)
    idx = idx_ref[...]
    sel = [idx[:, k:k + 1] for k in range(TOP_K)]
    onehot = jnp.zeros((tm, LANES), F32)
    for k in range(TOP_K):
        onehot = onehot + jnp.where(lane == sel[k], 1.0, 0.0)
    tile_cnt = jnp.sum(onehot, axis=0, keepdims=True)

    @pl.when((p == 0) & (i == 0))
    def _():
        run_s[...] = jnp.zeros_like(run_s)
        cnt_ref[...] = jnp.zeros_like(cnt_ref)

    @pl.when(p == 0)
    def _():
        run_s[...] += tile_cnt

    @pl.when((p == 1) & (i == 0))
    def _():
        cnt = run_s[...]
        padded = ((cnt.astype(jnp.int32) + (MOE_TM - 1)) // MOE_TM * MOE_TM).astype(F32)
        r = lax.broadcasted_iota(jnp.int32, (LANES, LANES), 0)
        c = lax.broadcasted_iota(jnp.int32, (LANES, LANES), 1)
        before = jnp.where(r < c, 1.0, 0.0)
        start = jnp.dot(jnp.broadcast_to(padded, (8, LANES)), before, preferred_element_type=F32,
                        precision=lax.Precision.HIGHEST)
        start_s[...] = start[0:1]
        cnt_ref[...] = jnp.broadcast_to(cnt, cnt_ref.shape)
        run_s[...] = jnp.zeros_like(run_s)

    @pl.when(p == 1)
    def _():
        r = lax.broadcasted_iota(jnp.int32, (tm, tm), 0)
        c = lax.broadcasted_iota(jnp.int32, (tm, tm), 1)
        earlier = jnp.where(c < r, 1.0, 0.0).astype(BF16)
        pos = _dot(earlier, onehot.astype(BF16)) + run_s[...] + start_s[...]
        out = jnp.zeros((tm, LANES), F32)
        for k in range(TOP_K):
            d = jnp.sum(jnp.where(lane == sel[k], pos, 0.0), axis=-1, keepdims=True)
            out = jnp.where(lane == k, d, out)
        dest_ref[...] = out.astype(jnp.int32)
        run_s[...] += tile_cnt


def _route(idx_slab):
    n = idx_slab.shape[0]
    return pl.pallas_call(
        _route_kernel,
        grid=(2, n // ROUTE_TM),
        in_specs=[pl.BlockSpec((ROUTE_TM, LANES), lambda p, i: (i, 0))],
        out_specs=[pl.BlockSpec((ROUTE_TM, LANES), lambda p, i: (i * p, 0)),
                   pl.BlockSpec((8, LANES), lambda p, i: (0, 0))],
        out_shape=[jax.ShapeDtypeStruct((n, LANES), jnp.int32), jax.ShapeDtypeStruct((8, LANES), F32)],
        scratch_shapes=[pltpu.VMEM((1, LANES), F32), pltpu.VMEM((1, LANES), F32)],
        compiler_params=_cparams(("arbitrary", "arbitrary")),
        name="route",
    )(idx_slab)


def _row_copy(src_ref, src_row, dst_ref, dst_row, sem):
    return pltpu.make_async_copy(src_ref.at[pl.ds(src_row, 1)], dst_ref.at[pl.ds(dst_row, 1)], sem)


def _dispatch_kernel(pe_ref, cnt_ref, dest_ref, hp_ref, hs_ref, xb_ref, zero_s, sem, *, tiles_p, first_tail_block):
    i = pl.program_id(0)
    tm = hp_ref.shape[0]
    nb = xb_ref.shape[0] // MOE_TM

    def zero_block(row0):
        return pltpu.make_async_copy(zero_s, xb_ref.at[pl.ds(pl.multiple_of(row0, MOE_TM), MOE_TM)], sem)

    @pl.when(i == 0)
    def _():
        zero_s[...] = jnp.zeros_like(zero_s)
        for start in (True, False):
            for e in range(N_EXPERTS):
                @pl.when(cnt_ref[e] > 0)
                def _():
                    cp = zero_block(pe_ref[e] - MOE_TM)
                    cp.start() if start else cp.wait()
            for b in range(first_tail_block, nb):
                @pl.when(b * MOE_TM >= pe_ref[N_EXPERTS - 1])
                def _():
                    cp = zero_block(b * MOE_TM)
                    cp.start() if start else cp.wait()

    def scatter_rows(h_ref):
        def issue(t, carry):
            for k in range(TOP_K):
                _row_copy(h_ref, t, xb_ref, dest_ref[0, 0, t * TOP_K + k], sem).start(priority=k % 2)
            return carry

        def drain(t, carry):
            for k in range(TOP_K):
                _row_copy(h_ref, 0, xb_ref, 0, sem).wait()
            return carry

        lax.fori_loop(0, tm, issue, 0)
        lax.fori_loop(0, tm, drain, 0)

    @pl.when(i < tiles_p)
    def _():
        scatter_rows(hp_ref)

    @pl.when(i >= tiles_p)
    def _():
        scatter_rows(hs_ref)


def _dispatch(pad_end, counts, dest, hp, hs, m):
    tiles_p = hp.shape[0] // TM
    tiles_s = hs.shape[0] // TM
    grid_spec = pltpu.PrefetchScalarGridSpec(
        num_scalar_prefetch=2,
        grid=(tiles_p + tiles_s,),
        in_specs=[
            pl.BlockSpec((1, 1, TM * TOP_K), lambda i, pe, cn: (i, 0, 0), memory_space=pltpu.SMEM),
            pl.BlockSpec((TM, D_PACK), lambda i, pe, cn: (jnp.minimum(i, tiles_p - 1), 0)),
            pl.BlockSpec((TM, D_PACK), lambda i, pe, cn: (jnp.maximum(i - tiles_p, 0), 0)),
        ],
        out_specs=pl.BlockSpec(memory_space=pl.ANY),
        scratch_shapes=[pltpu.VMEM((MOE_TM, D_PACK), jnp.uint32), pltpu.SemaphoreType.DMA(())],
    )
    return pl.pallas_call(
        functools.partial(_dispatch_kernel, tiles_p=tiles_p, first_tail_block=m // MOE_TM - N_EXPERTS),
        grid_spec=grid_spec,
        out_shape=jax.ShapeDtypeStruct((m, D_PACK), jnp.uint32),
        compiler_params=_cparams(("arbitrary",)),
        name="dispatch",
    )(pad_end, counts, dest, hp, hs)


def _combine_kernel(dest_ref, x_ref, mod_ref, gate_ref, gfin_ref, yb_ref, o_ref, buf, sem, *, final):
    tm = x_ref.shape[0]

    def issue(t, carry):
        for k in range(TOP_K):
            _row_copy(yb_ref, dest_ref[0, 0, t * TOP_K + k], buf.at[k], t, sem).start(priority=k % 2)
        return carry

    def drain(t, carry):
        for k in range(TOP_K):
            _row_copy(yb_ref, 0, buf.at[k], 0, sem).wait()
        return carry

    lax.fori_loop(0, tm, issue, 0)
    lax.fori_loop(0, tm, drain, 0)
    g = gate_ref[...]
    acc_lo = acc_hi = None
    for k in range(TOP_K):
        y_lo, y_hi = _unpack_rows(buf[k])
        acc_lo = g[:, k:k + 1] * y_lo if acc_lo is None else acc_lo + g[:, k:k + 1] * y_lo
        acc_hi = g[:, k:k + 1] * y_hi if acc_hi is None else acc_hi + g[:, k:k + 1] * y_hi
    x = x_ref[...] + mod_ref[0, 5:6, :] * jnp.concatenate([acc_lo, acc_hi], axis=1)
    if final:
        ms = jnp.mean(x * x, axis=-1, keepdims=True)
        x = x * lax.rsqrt(ms + EPS) * gfin_ref[...]
    o_ref[...] = x


def _combine(dest, x, mod, mod_row0, tiles_per_row, gates, g_final, yb, *, final):
    n = x.shape[0]
    return pl.pallas_call(
        functools.partial(_combine_kernel, final=final),
        grid=(n // TM,),
        in_specs=[
            pl.BlockSpec((1, 1, TM * TOP_K), lambda i: (i, 0, 0), memory_space=pltpu.SMEM),
            pl.BlockSpec((TM, D_MODEL), lambda i: (i, 0)),
            pl.BlockSpec((1, N_MOD, D_MODEL), lambda i: (mod_row0 + i // tiles_per_row, 0, 0)),
            pl.BlockSpec((TM, LANES), lambda i: (i, 0)),
            pl.BlockSpec((1, D_MODEL), lambda i: (0, 0)),
            pl.BlockSpec(memory_space=pl.ANY),
        ],
        out_specs=pl.BlockSpec((TM, D_MODEL), lambda i: (i, 0)),
        out_shape=jax.ShapeDtypeStruct((n, D_MODEL), F32),
        scratch_shapes=[pltpu.VMEM((TOP_K, TM, D_PACK), jnp.uint32), pltpu.SemaphoreType.DMA(())],
        compiler_params=_cparams(("arbitrary",)),
        name="combine_final" if final else "combine",
    )(dest, x, mod, gates, g_final, yb)


def _rope_tables(n_tokens):
    t = np.arange(n_tokens)
    row = (t // GRID_W).astype(np.float32)
    col = (t % GRID_W).astype(np.float32)
    inv = jnp.asarray(ROPE_THETA, F32) ** (-jnp.arange(ROPE_AXIS_PAIRS, dtype=F32) / ROPE_AXIS_PAIRS)
    ang = jnp.concatenate([jnp.asarray(row)[:, None] * inv, jnp.asarray(col)[:, None] * inv], axis=-1)
    cos, sin = jnp.cos(ang), jnp.sin(ang)
    cos = jnp.tile(cos, (1, LANES // ROPE_HALF))
    sin = jnp.tile(jnp.concatenate([-sin, sin], axis=-1), (1, LANES // HEAD_DIM))
    return cos, sin


def _dup_kv_heads(t):
    b, p = t.shape[:2]
    t = jnp.broadcast_to(t[:, :, :, None, :], (b, p, A_KV_HEADS, 2, HEAD_DIM))
    return t.reshape(b, p, 4 * HEAD_DIM).astype(BF16)


def kernel(x_prompt, x_sample, cache_k_a, cache_v_a, cache_k_b, cache_v_b, cache_k_c, cache_v_c, c, c_ctx,
           w_ada, b_ada, norm_attn, norm_ffn, w_in, sinks, rpb, lambda_qk, subln, w_branch, w_out,
           w_router, b_router, w_gu, b_gu, w_down, b_down, norm_final):
    bsz, seq = x_prompt.shape[:2]
    bsz_d, n_lat = x_sample.shape[:2]
    past = cache_k_a.shape[2]
    n_p = bsz * seq
    n_s = bsz_d * n_lat
    assert seq % TM == 0 and n_lat % TQ == 0 and n_lat % C_CHUNK == 0 and bsz_d + 1 <= 8
    assert (n_p + n_s) % ROUTE_TM == 0 and n_lat % C_TQ == 0 and n_p % MERGE_TM == 0 and n_lat % MERGE_TM == 0

    cond = jnp.zeros((8, D_MODEL), F32).at[0].set(c_ctx).at[1:1 + bsz_d].set(c)
    mod_all = _modulation(cond, w_ada, b_ada).reshape(DEPTH, 8, N_MOD, D_MODEL)
    cos, sin = _rope_tables(n_lat)
    g_final = norm_final[None, :]
    w_bf = w_in.astype(BF16)
    n_assign = (n_p + n_s) * TOP_K
    m_rows = (n_assign + N_EXPERTS * (MOE_TM - 1) + MOE_TM - 1) // MOE_TM * MOE_TM

    xp = x_prompt.reshape(n_p, D_MODEL)
    xs = x_sample.reshape(n_s, D_MODEL)
    kv_out = []
    for l in range(DEPTH):
        lambda_init = 0.8 - 0.6 * math.exp(-0.3 * l)
        mod = mod_all[l]
        g_attn = norm_attn[l][None, :]
        g_ffn = norm_ffn[l][None, :]
        sg = subln[l][None, :]
        wb = w_branch[l].astype(BF16)
        wo = w_out[l].astype(BF16)
        br = b_router[l][None, :]

        pa, pb, pc, pg, *kv = _project(xp, mod, 0, n_p // TM, g_attn, w_bf, l, cos, sin, rope=False, emit_kv=True)
        oa, ob, oc = _ctx_attention(pa, pb, pc, sinks[l], lambda_qk[l], sg, seq, lambda_init)
        xp_mid, hp, idx_p, gate_p = _merge_route(xp, mod, 0, n_p // MERGE_TM, g_ffn, oa, ob, oc, pg, wb, wo,
                                                 w_router[l], br)
        kv_out.append(kv)

        pa, pb, pc, pg = _project(xs, mod, 1, n_lat // TM, g_attn, w_bf, l, cos, sin, rope=True, emit_kv=False)
        oa = _win_attention(pa, _dup_kv_heads(cache_k_a[:, l]), _dup_kv_heads(cache_v_a[:, l]), sinks[l],
                            bsz_d, n_lat)
        ob = _na_attention(pb, cache_k_b[:, l].reshape(bsz_d, past, -1).astype(BF16),
                           cache_v_b[:, l].reshape(bsz_d, past, -1).astype(BF16),
                           _na_bias_table(rpb.reshape(-1), l, n_lat // GRID_W), bsz_d, n_lat)
        oc = _diff_attention(pc, cache_k_c[:, l].reshape(bsz_d, past, -1).astype(BF16),
                             cache_v_c[:, l].reshape(bsz_d, past, -1).astype(BF16),
                             lambda_qk[l], sg, bsz_d, n_lat, lambda_init)
        xs_mid, hs, idx_s, gate_s = _merge_route(xs, mod, 1, n_lat // MERGE_TM, g_ffn, oa, ob, oc, pg, wb, wo,
                                                 w_router[l], br)

        last = l == DEPTH - 1
        dest_slab, cnt = _route(jnp.concatenate([idx_p, idx_s], axis=0))
        counts = cnt[0, :N_EXPERTS].astype(jnp.int32)
        pad_end, block_e, block_first, block_valid, block_next = _moe_block_tables(counts, m_rows // MOE_TM)
        dest = dest_slab[:, :TOP_K].reshape(-1, 1, TM * TOP_K)
        dest_p, dest_s = dest[:n_p // TM], dest[n_p // TM:]
        xb = _dispatch(pad_end, counts, dest, hp, hs, m_rows)
        yb = _moe_experts(xb, block_e, block_first, block_valid, block_next, l, w_gu, b_gu, w_down, b_down)
        xp = _combine(dest_p, xp_mid, mod, 0, n_p // TM, gate_p, g_final, yb, final=last)
        xs = _combine(dest_s, xs_mid, mod, 1, n_lat // TM, gate_s, g_final, yb, final=last)

    y_prompt = xp.reshape(bsz, seq, D_MODEL)
    y_sample = xs.reshape(bsz_d, n_lat, D_MODEL)

    def stacked(j, heads, width):
        return jnp.stack([kv[j].reshape(bsz, seq, heads, width) for kv in kv_out], axis=1)

    new_k_a, new_v_a = stacked(0, A_KV_HEADS, HEAD_DIM), stacked(1, A_KV_HEADS, HEAD_DIM)
    new_k_b, new_v_b = stacked(2, B_HEADS, HEAD_DIM), stacked(3, B_HEADS, HEAD_DIM)
    new_k_c, new_v_c = stacked(4, C_HEADS, 2 * HEAD_DIM), stacked(5, C_HEADS, C_VDIM)
    return (y_prompt, y_sample, new_k_a, new_v_a, new_k_b, new_v_b, new_k_c, new_v_c)
```

```python
import functools
import math

import jax
import jax.numpy as jnp
import numpy as np
from jax import lax
from jax.experimental import pallas as pl
from jax.experimental.pallas import tpu as pltpu

F32 = jnp.float32
BF16 = jnp.bfloat16

D_MODEL = 1024
DEPTH = 2
GRID_W = 64
HEAD_DIM = 64
ROPE_HALF = HEAD_DIM // 2
ROPE_AXIS_PAIRS = HEAD_DIM // 4
ROPE_THETA = 10000.0
A_HEADS = 8
A_KV_HEADS = 2
WINDOW = 128
B_HEADS = 8
NA_KH = 8
NA_KW = 16
C_HEADS = 4
C_VDIM = 2 * HEAD_DIM
BRANCH_W = 512
N_EXPERTS = 32
TOP_K = 4
D_FF = D_MODEL
SWIGLU_LIMIT = 7.0
SWIGLU_ALPHA = 1.702
N_MOD = 6
EPS = 1e-6
NEG_INF = -1e30
ATTN_SCALE = HEAD_DIM ** -0.5
LOG2E = 1.4426950408889634
C_QSCALE = ATTN_SCALE * LOG2E

LANES = 128
VMEM_LIMIT = 56 * 1024 * 1024

PROJ_WIDTHS = dict(qa=512, ka=128, va=128, qb=512, kb=512, vb=512, qc=512, kc=512, vc=512,
                   ga=D_MODEL, gb=D_MODEL, gc=D_MODEL)
PROJ_OFFSETS = dict(zip(PROJ_WIDTHS, np.cumsum([0] + list(PROJ_WIDTHS.values())[:-1]).tolist()))
D_PROJ = sum(PROJ_WIDTHS.values())
A_COLS = 1024
B_COLS = 1536
C_COLS = 1536
G_COLS = 3 * D_MODEL

TM = 256
TQ = 256
NA_ROWS_Q = TQ // GRID_W
NA_ROWS_K = NA_ROWS_Q + NA_KH
NA_KEYS = NA_ROWS_K * GRID_W
WIN_KEYS = TQ + 2 * WINDOW
NA_COLS_PER_STEP = 4
C_TQ = 512
C_CHUNK = 256
C_VT_ROWS = C_VDIM + 16
MOE_TM = 512
ROUTE_TM = 1024
MERGE_TM = 512
MERGE_ROWS = 256
D_PACK = D_MODEL // 2


def _cparams(sem):
    return pltpu.CompilerParams(dimension_semantics=sem, vmem_limit_bytes=VMEM_LIMIT)


def _head_masks():
    lane = lax.broadcasted_iota(jnp.int32, (1, LANES), 1)
    lo = jnp.where(lane < HEAD_DIM, 1.0, 0.0).astype(BF16)
    hi = jnp.where(lane >= HEAD_DIM, 1.0, 0.0).astype(BF16)
    return lo, hi


def _split_heads(q):
    lo, hi = _head_masks()
    return jnp.concatenate([q * lo, q * hi], axis=0)


def _merge_heads(pv, t):
    lane = lax.broadcasted_iota(jnp.int32, (t, LANES), 1)
    return jnp.where(lane < HEAD_DIM, pv[:t], pv[t:])


def _pack_rows(x):
    half = x.shape[1] // 2
    lo = lax.bitcast_convert_type(x[:, :half].astype(BF16).astype(F32), jnp.uint32)
    hi = lax.bitcast_convert_type(x[:, half:].astype(BF16).astype(F32), jnp.uint32)
    return (lo >> 16) | (hi & jnp.uint32(0xFFFF0000))


def _unpack_rows(w):
    lo = lax.bitcast_convert_type(w << 16, F32)
    hi = lax.bitcast_convert_type(w & jnp.uint32(0xFFFF0000), F32)
    return lo, hi


def _dot_nt(a, b):
    return lax.dot_general(a, b, (((1,), (1,)), ((), ())), preferred_element_type=F32)


def _dot(a, b):
    return jnp.dot(a, b, preferred_element_type=F32)


def _ada_kernel(c_ref, w_ref, b_ref, o_ref):
    c = c_ref[...]
    s = c * (1.0 / (1.0 + jnp.exp(-c)))
    o_ref[0] = jnp.dot(s, w_ref[0], preferred_element_type=F32,
                       precision=lax.Precision.HIGHEST) + b_ref[0]


def _modulation(cond, w_ada, b_ada):
    tn = 1536
    n = N_MOD * D_MODEL
    return pl.pallas_call(
        _ada_kernel,
        grid=(DEPTH, n // tn),
        in_specs=[
            pl.BlockSpec((8, D_MODEL), lambda l, j: (0, 0)),
            pl.BlockSpec((1, D_MODEL, tn), lambda l, j: (l, 0, j)),
            pl.BlockSpec((1, 1, tn), lambda l, j: (l, 0, j)),
        ],
        out_specs=pl.BlockSpec((1, 8, tn), lambda l, j: (l, 0, j)),
        out_shape=jax.ShapeDtypeStruct((DEPTH, 8, n), F32),
        compiler_params=_cparams(("arbitrary", "arbitrary")),
        name="modulation",
    )(cond, w_ada, b_ada.reshape(DEPTH, 1, n))


def _rms_modulate(x, g, shift, scale):
    ms = jnp.mean(x * x, axis=-1, keepdims=True)
    return (x * lax.rsqrt(ms + EPS) * g) * (1.0 + scale) + shift


def _rope_cols(v, cos, sin_signed):
    t, w = v.shape
    lane = lax.broadcasted_iota(jnp.int32, (t, LANES), 1)
    first_half = (lane & (HEAD_DIM - 1)) < ROPE_HALF
    cols = []
    for c in range(w // LANES):
        xc = v[:, c * LANES:(c + 1) * LANES]
        partner = jnp.where(first_half,
                            pltpu.roll(xc, LANES - ROPE_HALF, axis=1),
                            pltpu.roll(xc, ROPE_HALF, axis=1))
        cols.append(xc * cos + partner * sin_signed)
    return jnp.concatenate(cols, axis=1) if len(cols) > 1 else cols[0]


def _dup_kv_cols(v):
    lane = lax.broadcasted_iota(jnp.int32, v.shape, 1)
    swapped = pltpu.roll(v, HEAD_DIM, axis=1)
    return jnp.concatenate([jnp.where(lane < HEAD_DIM, v, swapped), jnp.where(lane < HEAD_DIM, swapped, v)], axis=1)


def _proj_kernel(x_ref, mod_ref, g_ref, w_ref, cos_ref, sin_ref, *out_refs, rope, emit_kv):
    oa_ref, ob_ref, oc_ref, og_ref = out_refs[:4]
    h = _rms_modulate(x_ref[...], g_ref[...], mod_ref[0, 0:1, :], mod_ref[0, 1:2, :]).astype(BF16)
    if rope:
        cos = cos_ref[...]
        sin = sin_ref[...]

    def seg(name):
        start = PROJ_OFFSETS[name]
        return _dot(h, w_ref[0, :, start:start + PROJ_WIDTHS[name]])

    def maybe_rope(v):
        return _rope_cols(v, cos, sin) if rope else v

    oa_ref[:, 0:512] = (maybe_rope(seg("qa")) * ATTN_SCALE).astype(BF16)
    ka = maybe_rope(seg("ka"))
    va = seg("va")
    oa_ref[:, 512:768] = _dup_kv_cols(ka).astype(BF16)
    oa_ref[:, 768:1024] = _dup_kv_cols(va).astype(BF16)
    ob_ref[:, 0:512] = (seg("qb") * ATTN_SCALE).astype(BF16)
    kb = seg("kb")
    vb = seg("vb")
    ob_ref[:, 512:1024] = kb.astype(BF16)
    ob_ref[:, 1024:1536] = vb.astype(BF16)
    oc_ref[:, 0:512] = (maybe_rope(seg("qc")) * C_QSCALE).astype(BF16)
    kc = maybe_rope(seg("kc"))
    vc = seg("vc")
    oc_ref[:, 512:1024] = kc.astype(BF16)
    oc_ref[:, 1024:1536] = vc.astype(BF16)
    for j, name in enumerate(("ga", "gb", "gc")):
        for c in range(D_MODEL // 512):
            start = PROJ_OFFSETS[name] + c * 512
            gv = _dot(h, w_ref[0, :, start:start + 512])
            og_ref[:, j * D_MODEL + c * 512:j * D_MODEL + (c + 1) * 512] = (1.0 / (1.0 + jnp.exp(-gv))).astype(BF16)
    if emit_kv:
        for ref, val in zip(out_refs[4:], (ka, va, kb, vb, kc, vc)):
            ref[...] = val


def _project(x, mod, mod_row0, tiles_per_row, g, w_bf, layer, cos, sin, *, rope, emit_kv):
    n = x.shape[0]
    nt = n // TM
    pos_tiles = cos.shape[0] // TM
    widths = [A_COLS, B_COLS, C_COLS, G_COLS]
    dtypes = [BF16] * 4
    if emit_kv:
        widths += [PROJ_WIDTHS[k] for k in ("ka", "va", "kb", "vb", "kc", "vc")]
        dtypes += [F32] * 6
    return pl.pallas_call(
        functools.partial(_proj_kernel, rope=rope, emit_kv=emit_kv),
        grid=(nt,),
        in_specs=[
            pl.BlockSpec((TM, D_MODEL), lambda i: (i, 0)),
            pl.BlockSpec((1, N_MOD, D_MODEL), lambda i: (mod_row0 + i // tiles_per_row, 0, 0)),
            pl.BlockSpec((1, D_MODEL), lambda i: (0, 0)),
            pl.BlockSpec((1, D_MODEL, D_PROJ), lambda i: (layer, 0, 0)),
            pl.BlockSpec((TM, LANES), lambda i: (i % pos_tiles, 0)),
            pl.BlockSpec((TM, LANES), lambda i: (i % pos_tiles, 0)),
        ],
        out_specs=[pl.BlockSpec((TM, w), lambda i: (i, 0)) for w in widths],
        out_shape=[jax.ShapeDtypeStruct((n, w), d) for w, d in zip(widths, dtypes)],
        compiler_params=_cparams(("arbitrary",)),
        name="project_rope" if rope else "project",
    )(x, mod, g, w_bf, cos, sin)


def _softmax_pv(scores, values, sink=None):
    m = scores[0].max(axis=-1, keepdims=True)
    for s in scores[1:]:
        m = jnp.maximum(m, s.max(axis=-1, keepdims=True))
    if sink is not None:
        m = jnp.maximum(m, sink)
    l = None
    acc = None
    for s, v in zip(scores, values):
        e = jnp.exp(s - m)
        ls = e.sum(axis=-1, keepdims=True)
        pv = _dot(e.astype(BF16), v)
        l = ls if l is None else l + ls
        acc = pv if acc is None else acc + pv
    if sink is not None:
        l = l + jnp.exp(sink - m)
    return acc * (1.0 / l)


def _sink_column(sink_ref, first_head, rows_per_head, n_heads):
    row = lax.broadcasted_iota(jnp.int32, (rows_per_head * n_heads, 1), 0)
    col = jnp.full((rows_per_head * n_heads, 1), sink_ref[first_head], F32)
    for j in range(1, n_heads):
        col = jnp.where(row >= j * rows_per_head, sink_ref[first_head + j], col)
    return col


def _diff_lambda(lq_ref, lambda_init):
    lf = lq_ref[...]
    a = jnp.sum(lf[0:1] * lf[1:2], axis=-1, keepdims=True)
    b = jnp.sum(lf[2:3] * lf[3:4], axis=-1, keepdims=True)
    return jnp.exp(a) - jnp.exp(b) + lambda_init


def _subln(o, g, lambda_init):
    ms = jnp.mean(o * o, axis=-1, keepdims=True)
    return (o * lax.rsqrt(ms + EPS) * g) * (1.0 - lambda_init)


def _ctx_attn_kernel(sink_ref, lq_ref, sg_ref, a_ref, b_ref, c_ref, oa_ref, ob_ref, oc_ref, *, lambda_init):
    t = a_ref.shape[0]
    lam = _diff_lambda(lq_ref, lambda_init)
    chains = []

    for g in range(A_KV_HEADS):
        def scores_a(g=g):
            q = a_ref[:, g * 256:(g + 1) * 256]
            lhs = jnp.concatenate([_split_heads(q[:, :LANES]), _split_heads(q[:, LANES:])], axis=0)
            return _dot_nt(lhs, a_ref[:, 512 + g * LANES:512 + (g + 1) * LANES])

        def finish_a(s, g=g):
            v = a_ref[:, 768 + g * LANES:768 + (g + 1) * LANES]
            o = _softmax_pv([s], [v], _sink_column(sink_ref, 4 * g, t, 4))
            oa_ref[:, g * 256:g * 256 + LANES] = _merge_heads(o[:2 * t], t).astype(BF16)
            oa_ref[:, g * 256 + LANES:(g + 1) * 256] = _merge_heads(o[2 * t:], t).astype(BF16)

        chains.append((scores_a, finish_a))

    for c in range(B_HEADS // 2):
        def scores_b(c=c):
            return _dot_nt(_split_heads(b_ref[:, c * LANES:(c + 1) * LANES]),
                           b_ref[:, 512 + c * LANES:512 + (c + 1) * LANES])

        def finish_b(s, c=c):
            o = _softmax_pv([s], [b_ref[:, 1024 + c * LANES:1024 + (c + 1) * LANES]])
            ob_ref[:, c * LANES:(c + 1) * LANES] = _merge_heads(o, t).astype(BF16)

        chains.append((scores_b, finish_b))

    for h in range(C_HEADS):
        def scores_c(h=h):
            return _dot_nt(_split_heads(c_ref[:, h * LANES:(h + 1) * LANES]),
                           c_ref[:, 512 + h * LANES:512 + (h + 1) * LANES])

        def finish_c(s, h=h):
            m = s.max(axis=-1, keepdims=True)
            e = jnp.exp2(s - m)
            p = e * (1.0 / e.sum(axis=-1, keepdims=True))
            a = (p[:t] - lam * p[t:]).astype(BF16)
            o = _dot(a, c_ref[:, 1024 + h * LANES:1024 + (h + 1) * LANES])
            oc_ref[:, h * LANES:(h + 1) * LANES] = _subln(o, sg_ref[...], lambda_init).astype(BF16)

        chains.append((scores_c, finish_c))

    s = chains[0][0]()
    for i, (_, finish) in enumerate(chains):
        s_next = chains[i + 1][0]() if i + 1 < len(chains) else None
        finish(s)
        s = s_next


def _ctx_attention(pa, pb, pc, sink, lq, sg, seq, lambda_init):
    n = pa.shape[0]
    smem = pl.BlockSpec(memory_space=pltpu.SMEM)
    return pl.pallas_call(
        functools.partial(_ctx_attn_kernel, lambda_init=lambda_init),
        grid=(n // seq,),
        in_specs=[
            smem,
            pl.BlockSpec((4, HEAD_DIM), lambda b: (0, 0)),
            pl.BlockSpec((1, C_VDIM), lambda b: (0, 0)),
            pl.BlockSpec((seq, A_COLS), lambda b: (b, 0)),
            pl.BlockSpec((seq, B_COLS), lambda b: (b, 0)),
            pl.BlockSpec((seq, C_COLS), lambda b: (b, 0)),
        ],
        out_specs=[pl.BlockSpec((seq, BRANCH_W), lambda b: (b, 0))] * 3,
        out_shape=[jax.ShapeDtypeStruct((n, BRANCH_W), BF16)] * 3,
        compiler_params=_cparams(("arbitrary",)),
        name="context_attention",
    )(sink, lq, sg, pa, pb, pc)


def _win_attn_kernel(sink_ref, q_ref, k_ref, v_ref, kc_ref, vc_ref, o_ref, *, n_lat):
    qi = pl.program_id(1)
    ws = pl.multiple_of(jnp.clip(qi * TQ - WINDOW, 0, n_lat - WIN_KEYS), WINDOW)
    qpos = qi * TQ + lax.broadcasted_iota(jnp.int32, (TQ, WIN_KEYS), 0)
    kpos = ws + lax.broadcasted_iota(jnp.int32, (TQ, WIN_KEYS), 1)
    band = jnp.where(jnp.abs(kpos - qpos) <= WINDOW, 0.0, NEG_INF)
    band = jnp.concatenate([band] * 4, axis=0)
    def scores(g):
        q = q_ref[:, g * 256:(g + 1) * 256]
        lhs = jnp.concatenate([_split_heads(q[:, :LANES]), _split_heads(q[:, LANES:])], axis=0)
        cols = slice(g * LANES, (g + 1) * LANES)
        return [_dot_nt(lhs, k_ref[pl.ds(ws, WIN_KEYS), cols]) + band, _dot_nt(lhs, kc_ref[0, :, cols])]

    s = scores(0)
    for g in range(A_KV_HEADS):
        s_next = scores(g + 1) if g + 1 < A_KV_HEADS else None
        cols = slice(g * LANES, (g + 1) * LANES)
        sink = _sink_column(sink_ref, 4 * g, TQ, 4)
        o = _softmax_pv(s, [v_ref[pl.ds(ws, WIN_KEYS), cols], vc_ref[0, :, cols]], sink)
        o_ref[:, g * 256:g * 256 + LANES] = _merge_heads(o[:2 * TQ], TQ).astype(BF16)
        o_ref[:, g * 256 + LANES:(g + 1) * 256] = _merge_heads(o[2 * TQ:], TQ).astype(BF16)
        s = s_next


def _win_attention(pa, kctx, vctx, sink, n_batch, n_lat):
    nq = n_lat // TQ
    past = kctx.shape[1]
    return pl.pallas_call(
        functools.partial(_win_attn_kernel, n_lat=n_lat),
        grid=(n_batch, nq),
        in_specs=[
            pl.BlockSpec(memory_space=pltpu.SMEM),
            pl.BlockSpec((TQ, 512), lambda b, i: (b * nq + i, 0)),
            pl.BlockSpec((n_lat, 256), lambda b, i: (b, 2)),
            pl.BlockSpec((n_lat, 256), lambda b, i: (b, 3)),
            pl.BlockSpec((1, past, 256), lambda b, i: (b, 0, 0)),
            pl.BlockSpec((1, past, 256), lambda b, i: (b, 0, 0)),
        ],
        out_specs=pl.BlockSpec((TQ, BRANCH_W), lambda b, i: (b * nq + i, 0)),
        out_shape=jax.ShapeDtypeStruct((n_batch * n_lat, BRANCH_W), BF16),
        compiler_params=_cparams(("arbitrary", "arbitrary")),
        name="window_attention",
    )(sink, pa, pa, pa, kctx, vctx)


def _na_window_start(qi, rows):
    return jnp.clip(qi * NA_ROWS_Q - NA_KH // 2, 0, rows - NA_ROWS_K)


def _na_attn_kernel(q_ref, k_ref, v_ref, kc_ref, vc_ref, bias_ref, o_ref, *, rows):
    qi = pl.program_id(2)
    ws = pl.multiple_of(_na_window_start(qi, rows) * GRID_W, GRID_W)
    def scores(c):
        cols = slice(c * LANES, (c + 1) * LANES)
        lhs = _split_heads(q_ref[:, cols])
        s_loc = _dot_nt(lhs, k_ref[pl.ds(ws, NA_KEYS), cols]) + bias_ref[0, 2 * c:2 * c + 2].reshape(2 * TQ, NA_KEYS)
        return [s_loc, _dot_nt(lhs, kc_ref[0, :, cols])]

    s = scores(0)
    for c in range(NA_COLS_PER_STEP):
        s_next = scores(c + 1) if c + 1 < NA_COLS_PER_STEP else None
        cols = slice(c * LANES, (c + 1) * LANES)
        o = _softmax_pv(s, [v_ref[pl.ds(ws, NA_KEYS), cols], vc_ref[0, :, cols]])
        o_ref[:, cols] = _merge_heads(o, TQ).astype(BF16)
        s = s_next


def _na_bias_kernel(rpb_ref, o_ref, *, layer, rows):
    h = pl.program_id(0)
    n_dr = 2 * NA_KH - 1
    n_dc = 2 * NA_KW - 1
    base = (layer * B_HEADS + h) * n_dr * n_dc
    lane = lax.broadcasted_iota(jnp.int32, (GRID_W, LANES), 1)
    qc = lax.broadcasted_iota(jnp.int32, (GRID_W, LANES), 0)
    kc = lane & (GRID_W - 1)
    second = lane >= GRID_W
    cs = jnp.clip(qc - NA_KW // 2, 0, GRID_W - NA_KW)
    col_ok = (kc >= cs) & (kc < cs + NA_KW)
    dc_map = kc - qc + (NA_KW - 1)
    cache = {}

    def pair_tile(dr0, ok0, ok1):
        key = (dr0 if ok0 else None, dr0 + 1 if ok1 else None)
        if key not in cache:
            if not (ok0 or ok1):
                cache[key] = jnp.full((GRID_W, LANES), NEG_INF, F32)
            else:
                acc = jnp.zeros((GRID_W, LANES), F32)
                for dc in range(n_dc):
                    v0 = rpb_ref[base + dr0 * n_dc + dc] if ok0 else 0.0
                    v1 = rpb_ref[base + (dr0 + 1) * n_dc + dc] if ok1 else 0.0
                    acc = jnp.where(dc_map == dc, jnp.where(second, v1, v0), acc)
                if ok0 and ok1:
                    ok = col_ok
                elif ok0:
                    ok = col_ok & jnp.logical_not(second)
                else:
                    ok = col_ok & second
                cache[key] = jnp.where(ok, acc, NEG_INF)
        return cache[key]

    nq = rows // NA_ROWS_Q
    for p, qi in enumerate((0, 1, nq - 1)):
        r0 = qi * NA_ROWS_Q
        ws = min(max(r0 - NA_KH // 2, 0), rows - NA_ROWS_K)
        for i in range(NA_ROWS_Q):
            r = r0 + i
            lo = min(max(r - NA_KH // 2, 0), rows - NA_KH)
            for jp in range(NA_ROWS_K // 2):
                kr = ws + 2 * jp
                ok0 = lo <= kr < lo + NA_KH
                ok1 = lo <= kr + 1 < lo + NA_KH
                o_ref[p, 0, i * GRID_W:(i + 1) * GRID_W, jp * LANES:(jp + 1) * LANES] = pair_tile(
                    kr - r + NA_KH - 1, ok0, ok1)


def _na_bias_table(rpb_flat, layer, rows):
    return pl.pallas_call(
        functools.partial(_na_bias_kernel, layer=layer, rows=rows),
        grid=(B_HEADS,),
        in_specs=[pl.BlockSpec(memory_space=pltpu.SMEM)],
        out_specs=pl.BlockSpec((3, 1, TQ, NA_KEYS), lambda h: (0, h, 0, 0)),
        out_shape=jax.ShapeDtypeStruct((3, B_HEADS, TQ, NA_KEYS), F32),
        compiler_params=_cparams(("arbitrary",)),
        name="neighborhood_bias",
    )(rpb_flat)


def _na_attention(pb, kctx, vctx, bias, n_batch, n_lat):
    nq = n_lat // TQ
    rows = n_lat // GRID_W
    past = kctx.shape[1]
    w = NA_COLS_PER_STEP * LANES
    n_cb = BRANCH_W // w

    def bias_idx(b, c, i):
        return (jnp.where(i == 0, 0, jnp.where(i == nq - 1, 2, 1)), c, 0, 0)

    return pl.pallas_call(
        functools.partial(_na_attn_kernel, rows=rows),
        grid=(n_batch, n_cb, nq),
        in_specs=[
            pl.BlockSpec((TQ, w), lambda b, c, i: (b * nq + i, c)),
            pl.BlockSpec((n_lat, w), lambda b, c, i: (b, n_cb + c)),
            pl.BlockSpec((n_lat, w), lambda b, c, i: (b, 2 * n_cb + c)),
            pl.BlockSpec((1, past, w), lambda b, c, i: (b, 0, c)),
            pl.BlockSpec((1, past, w), lambda b, c, i: (b, 0, c)),
            pl.BlockSpec((1, 2 * NA_COLS_PER_STEP, TQ, NA_KEYS), bias_idx),
        ],
        out_specs=pl.BlockSpec((TQ, w), lambda b, c, i: (b * nq + i, c)),
        out_shape=jax.ShapeDtypeStruct((n_batch * n_lat, BRANCH_W), BF16),
        compiler_params=_cparams(("arbitrary", "arbitrary", "arbitrary")),
        name="neighborhood_attention",
    )(pb, pb, pb, kctx, vctx, bias)


def _diff_attn_kernel(lq_ref, sg_ref, q_ref, k_ref, v_ref, kc_ref, vc_ref, o_ref, vt_s, vtc_s, acc_s, *,
                      n_lat, lambda_init):
    n_chunks = n_lat // C_CHUNK

    @pl.when(pl.program_id(2) == 0)
    def _():
        for j in range(n_chunks):
            vt_s[j, :C_VDIM] = v_ref[j * C_CHUNK:(j + 1) * C_CHUNK, :].astype(F32).T.astype(BF16)
            vt_s[j, C_VDIM:] = jnp.ones((C_VT_ROWS - C_VDIM, C_CHUNK), BF16)
        vtc_s[:C_VDIM] = vc_ref[0].astype(F32).T.astype(BF16)
        vtc_s[C_VDIM:] = jnp.ones((C_VT_ROWS - C_VDIM, vtc_s.shape[1]), BF16)

    lhs = _split_heads(q_ref[...])
    acc_s[...] = jnp.zeros_like(acc_s)

    def scores(j):
        k = k_ref[j * C_CHUNK:(j + 1) * C_CHUNK, :] if j < n_chunks else kc_ref[0]
        st = _dot_nt(k, lhs)
        return st, st.max(axis=0, keepdims=True)

    tq = q_ref.shape[0]
    m = jnp.full((1, 2 * tq), NEG_INF, F32)
    st, st_max = scores(0)
    for j in range(n_chunks + 1):
        st_next, st_max_next = scores(j + 1) if j < n_chunks else (None, None)
        m_new = jnp.maximum(m, st_max)
        e = jnp.exp2((st - m_new).astype(BF16))
        vt = vt_s[j] if j < n_chunks else vtc_s[...]
        acc_s[...] = jnp.exp2(m - m_new) * acc_s[...] + _dot(vt, e)
        m, st, st_max = m_new, st_next, st_max_next
    o = acc_s[:C_VDIM, :] * (1.0 / acc_s[C_VDIM:C_VDIM + 1, :])
    d = o[:, :tq] - _diff_lambda(lq_ref, lambda_init) * o[:, tq:]
    ms = jnp.mean(d * d, axis=0, keepdims=True)
    y = (d * lax.rsqrt(ms + EPS)).T
    o_ref[...] = (y * sg_ref[...] * (1.0 - lambda_init)).astype(BF16)


def _diff_attention(pc, kctx, vctx, lq, sg, n_batch, n_lat, lambda_init):
    nq = n_lat // C_TQ
    past = kctx.shape[1]
    return pl.pallas_call(
        functools.partial(_diff_attn_kernel, n_lat=n_lat, lambda_init=lambda_init),
        grid=(n_batch, C_HEADS, nq),
        in_specs=[
            pl.BlockSpec((4, HEAD_DIM), lambda b, h, i: (0, 0)),
            pl.BlockSpec((1, C_VDIM), lambda b, h, i: (0, 0)),
            pl.BlockSpec((C_TQ, LANES), lambda b, h, i: (b * nq + i, h)),
            pl.BlockSpec((n_lat, LANES), lambda b, h, i: (b, 4 + h)),
            pl.BlockSpec((n_lat, LANES), lambda b, h, i: (b, 8 + h)),
            pl.BlockSpec((1, past, LANES), lambda b, h, i: (b, 0, h)),
            pl.BlockSpec((1, past, LANES), lambda b, h, i: (b, 0, h)),
        ],
        out_specs=pl.BlockSpec((C_TQ, LANES), lambda b, h, i: (b * nq + i, h)),
        out_shape=jax.ShapeDtypeStruct((n_batch * n_lat, BRANCH_W), BF16),
        scratch_shapes=[pltpu.VMEM((n_lat // C_CHUNK, C_VT_ROWS, C_CHUNK), BF16), pltpu.VMEM((C_VT_ROWS, past), BF16),
                        pltpu.VMEM((C_VT_ROWS, 2 * C_TQ), F32)],
        compiler_params=_cparams(("arbitrary", "arbitrary", "arbitrary")),
        name="differential_attention",
    )(lq, sg, pc, pc, pc, kctx, vctx)


def _merge_kernel(x_ref, mod_ref, g_ref, oa_ref, ob_ref, oc_ref, sg_ref, wb_ref, wo_ref, wr_ref, br_ref,
                  xo_ref, h_ref, idx_ref, gate_ref):
    wr = wr_ref[...]
    w_hi = wr.astype(BF16)
    w_lo = (wr - w_hi.astype(F32)).astype(BF16)

    def gated_branches(rows):
        y = None
        for j, o_ref in enumerate((oa_ref, ob_ref, oc_ref)):
            t = sg_ref[rows, j * D_MODEL:(j + 1) * D_MODEL].astype(F32) * _dot(o_ref[rows, :], wb_ref[j])
            y = t if y is None else y + t
        return y.astype(BF16)

    def finish(rows, y):
        x = x_ref[rows, :] + mod_ref[0, 2:3, :] * _dot(y, wo_ref[...])
        xo_ref[rows, :] = x
        h = _rms_modulate(x, g_ref[...], mod_ref[0, 3:4, :], mod_ref[0, 4:5, :])
        h_ref[rows, :] = _pack_rows(h)
        h_hi = h.astype(BF16)
        h_lo = (h - h_hi.astype(F32)).astype(BF16)
        logits = _dot(h_hi, w_hi) + (_dot(h_hi, w_lo) + _dot(h_lo, w_hi)) + br_ref[...]
        tm = logits.shape[0]
        lane_e = lax.broadcasted_iota(jnp.int32, (tm, N_EXPERTS), 1)
        lane_o = lax.broadcasted_iota(jnp.int32, (tm, LANES), 1)
        idx_out = jnp.zeros((tm, LANES), jnp.int32)
        val_out = jnp.zeros((tm, LANES), F32)
        top = None
        denom = None
        for k in range(TOP_K):
            mx = logits.max(axis=-1, keepdims=True)
            sel = jnp.min(jnp.where(logits == mx, lane_e, N_EXPERTS), axis=-1, keepdims=True)
            logits = jnp.where(lane_e == sel, -jnp.inf, logits)
            if top is None:
                top = mx
            e = jnp.exp(mx - top)
            denom = e if denom is None else denom + e
            idx_out = jnp.where(lane_o == k, sel, idx_out)
            val_out = jnp.where(lane_o == k, e, val_out)
        idx_ref[rows, :] = idx_out
        gate_ref[rows, :] = val_out * (1.0 / denom)

    n_groups = x_ref.shape[0] // MERGE_ROWS
    groups = [slice(i * MERGE_ROWS, (i + 1) * MERGE_ROWS) for i in range(n_groups)]
    y = gated_branches(groups[0])
    for i in range(n_groups):
        y_next = gated_branches(groups[i + 1]) if i + 1 < n_groups else None
        finish(groups[i], y)
        y = y_next


def _merge_route(x, mod, mod_row0, tiles_per_row, g_ffn, oa, ob, oc, sg, wb, wo, wr, br):
    n = x.shape[0]
    row = lambda i: (i, 0)
    fixed2 = lambda i: (0, 0)
    return pl.pallas_call(
        _merge_kernel,
        grid=(n // MERGE_TM,),
        in_specs=[
            pl.BlockSpec((MERGE_TM, D_MODEL), row),
            pl.BlockSpec((1, N_MOD, D_MODEL), lambda i: (mod_row0 + i // tiles_per_row, 0, 0)),
            pl.BlockSpec((1, D_MODEL), fixed2),
            pl.BlockSpec((MERGE_TM, BRANCH_W), row),
            pl.BlockSpec((MERGE_TM, BRANCH_W), row),
            pl.BlockSpec((MERGE_TM, BRANCH_W), row),
            pl.BlockSpec((MERGE_TM, G_COLS), row),
            pl.BlockSpec((3, BRANCH_W, D_MODEL), lambda i: (0, 0, 0)),
            pl.BlockSpec((D_MODEL, D_MODEL), fixed2),
            pl.BlockSpec((D_MODEL, N_EXPERTS), fixed2),
            pl.BlockSpec((1, N_EXPERTS), fixed2),
        ],
        out_specs=[pl.BlockSpec((MERGE_TM, D_MODEL), row), pl.BlockSpec((MERGE_TM, D_PACK), row),
                   pl.BlockSpec((MERGE_TM, LANES), row), pl.BlockSpec((MERGE_TM, LANES), row)],
        out_shape=[jax.ShapeDtypeStruct((n, D_MODEL), F32), jax.ShapeDtypeStruct((n, D_PACK), jnp.uint32),
                   jax.ShapeDtypeStruct((n, LANES), jnp.int32), jax.ShapeDtypeStruct((n, LANES), F32)],
        compiler_params=_cparams(("arbitrary",)),
        name="merge_route",
    )(x, mod, g_ffn, oa, ob, oc, sg, wb, wo, wr, br)


def _moe_kernel(be_ref, first_ref, valid_ref, next_ref, x_ref, wgu_hbm, bgu_ref, wd_hbm, bd_ref, o_ref,
                wgu_f, wd_f, wgu_s, wd_s, sem, *, layer):
    i = pl.program_id(0)

    def fetch(e):
        return (pltpu.make_async_copy(wgu_hbm.at[layer, e], wgu_f, sem.at[0]),
                pltpu.make_async_copy(wd_hbm.at[layer, e], wd_f, sem.at[1]))

    @pl.when(i == 0)
    def _():
        for cp in fetch(be_ref[0]):
            cp.start()

    @pl.when(first_ref[i] == 1)
    def _():
        for cp in fetch(be_ref[i]):
            cp.wait()
        wgu_s[...] = wgu_f[...].astype(BF16)
        wd_s[...] = wd_f[...].astype(BF16)

        @pl.when(next_ref[i] >= 0)
        def _():
            for cp in fetch(next_ref[i]):
                cp.start()

    @pl.when(valid_ref[i] == 1)
    def _():
        def gate_up(rows):
            x_lo, x_hi = _unpack_rows(x_ref[rows, :])
            return (_dot(x_lo.astype(BF16), wgu_s[:D_PACK, :]) + _dot(x_hi.astype(BF16), wgu_s[D_PACK:, :])
                    + bgu_ref[0, 0])

        def down(rows, gu):
            gate = jnp.minimum(gu[:, :D_FF], SWIGLU_LIMIT)
            lin = jnp.clip(gu[:, D_FF:], -SWIGLU_LIMIT, SWIGLU_LIMIT)
            act = gate * (1.0 / (1.0 + jnp.exp(-SWIGLU_ALPHA * gate))) * (lin + 1.0)
            o_ref[rows, :] = _pack_rows(_dot(act.astype(BF16), wd_s[...]) + bd_ref[0, 0])

        groups = [slice(g * (MOE_TM // 2), (g + 1) * (MOE_TM // 2)) for g in range(2)]
        gu = gate_up(groups[0])
        for g in range(2):
            gu_next = gate_up(groups[g + 1]) if g + 1 < 2 else None
            down(groups[g], gu)
            gu = gu_next

    @pl.when(valid_ref[i] == 0)
    def _():
        o_ref[...] = jnp.zeros_like(o_ref)


def _moe_experts(xb, block_e, block_first, block_valid, block_next, layer, w_gu, b_gu, w_down, b_down):
    m = xb.shape[0]
    nb = m // MOE_TM
    idx = lambda i, be, bf, bv, bn: (layer, be[i], 0, 0)
    grid_spec = pltpu.PrefetchScalarGridSpec(
        num_scalar_prefetch=4,
        grid=(nb,),
        in_specs=[
            pl.BlockSpec((MOE_TM, D_PACK), lambda i, be, bf, bv, bn: (i, 0)),
            pl.BlockSpec(memory_space=pl.ANY),
            pl.BlockSpec((1, 1, 1, 2 * D_FF), idx),
            pl.BlockSpec(memory_space=pl.ANY),
            pl.BlockSpec((1, 1, 1, D_MODEL), idx),
        ],
        out_specs=pl.BlockSpec((MOE_TM, D_PACK), lambda i, be, bf, bv, bn: (i, 0)),
        scratch_shapes=[pltpu.VMEM((D_MODEL, 2 * D_FF), F32), pltpu.VMEM((D_FF, D_MODEL), F32),
                        pltpu.VMEM((D_MODEL, 2 * D_FF), BF16), pltpu.VMEM((D_FF, D_MODEL), BF16),
                        pltpu.SemaphoreType.DMA((2,))],
    )
    return pl.pallas_call(
        functools.partial(_moe_kernel, layer=layer),
        grid_spec=grid_spec,
        out_shape=jax.ShapeDtypeStruct((m, D_PACK), jnp.uint32),
        compiler_params=_cparams(("arbitrary",)),
        name="moe_experts",
    )(block_e, block_first, block_valid, block_next, xb, w_gu, b_gu.reshape(DEPTH, N_EXPERTS, 1, 2 * D_FF),
      w_down, b_down.reshape(DEPTH, N_EXPERTS, 1, D_MODEL))


def _moe_block_tables(counts, nb):
    padded = (counts + MOE_TM - 1) // MOE_TM * MOE_TM
    pad_end = jnp.cumsum(padded).astype(jnp.int32)
    pos = jnp.arange(nb, dtype=jnp.int32)
    blk = pos * MOE_TM
    block_e = jnp.minimum(jnp.sum((blk[:, None] >= pad_end[None, :]).astype(jnp.int32), axis=1), N_EXPERTS - 1)
    block_valid = (blk < pad_end[-1]).astype(jnp.int32)
    prev = jnp.concatenate([jnp.full((1,), -1, jnp.int32), block_e[:-1]])
    block_first = (block_e != prev).astype(jnp.int32)
    first_at_or_after = lax.cummin(jnp.where(block_first == 1, pos, nb), axis=0, reverse=True)
    next_first = jnp.concatenate([first_at_or_after[1:], jnp.full((1,), nb, jnp.int32)])
    block_next = jnp.where(next_first < nb, block_e[jnp.minimum(next_first, nb - 1)], -1).astype(jnp.int32)
    return pad_end, block_e, block_first, block_valid, block_next


def _route_kernel(idx_ref, dest_ref, cnt_ref, run_s, start_s):
    p = pl.program_id(0)
    i = pl.program_id(1)
    tm = idx_ref.shape[0]
    lane = lax.broadcasted_iota(jnp.int32, (tm, LANES), 1)
    idx = idx_ref[...]
    sel = [idx[:, k:k + 1] for k in range(TOP_K)]
    onehot = jnp.zeros((tm, LANES), F32)
    for k in range(TOP_K):
        onehot = onehot + jnp.where(lane == sel[k], 1.0, 0.0)
    tile_cnt = jnp.sum(onehot, axis=0, keepdims=True)

    @pl.when((p == 0) & (i == 0))
    def _():
        run_s[...] = jnp.zeros_like(run_s)
        cnt_ref[...] = jnp.zeros_like(cnt_ref)

    @pl.when(p == 0)
    def _():
        run_s[...] += tile_cnt

    @pl.when((p == 1) & (i == 0))
    def _():
        cnt = run_s[...]
        padded = ((cnt.astype(jnp.int32) + (MOE_TM - 1)) // MOE_TM * MOE_TM).astype(F32)
        r = lax.broadcasted_iota(jnp.int32, (LANES, LANES), 0)
        c = lax.broadcasted_iota(jnp.int32, (LANES, LANES), 1)
        before = jnp.where(r < c, 1.0, 0.0)
        start = jnp.dot(jnp.broadcast_to(padded, (8, LANES)), before, preferred_element_type=F32,
                        precision=lax.Precision.HIGHEST)
        start_s[...] = start[0:1]
        cnt_ref[...] = jnp.broadcast_to(cnt, cnt_ref.shape)
        run_s[...] = jnp.zeros_like(run_s)

    @pl.when(p == 1)
    def _():
        r = lax.broadcasted_iota(jnp.int32, (tm, tm), 0)
        c = lax.broadcasted_iota(jnp.int32, (tm, tm), 1)
        earlier = jnp.where(c < r, 1.0, 0.0).astype(BF16)
        pos = _dot(earlier, onehot.astype(BF16)) + run_s[...] + start_s[...]
        out = jnp.zeros((tm, LANES), F32)
        for k in range(TOP_K):
            d = jnp.sum(jnp.where(lane == sel[k], pos, 0.0), axis=-1, keepdims=True)
            out = jnp.where(lane == k, d, out)
        dest_ref[...] = out.astype(jnp.int32)
        run_s[...] += tile_cnt


def _route(idx_slab):
    n = idx_slab.shape[0]
    return pl.pallas_call(
        _route_kernel,
        grid=(2, n // ROUTE_TM),
        in_specs=[pl.BlockSpec((ROUTE_TM, LANES), lambda p, i: (i, 0))],
        out_specs=[pl.BlockSpec((ROUTE_TM, LANES), lambda p, i: (i * p, 0)),
                   pl.BlockSpec((8, LANES), lambda p, i: (0, 0))],
        out_shape=[jax.ShapeDtypeStruct((n, LANES), jnp.int32), jax.ShapeDtypeStruct((8, LANES), F32)],
        scratch_shapes=[pltpu.VMEM((1, LANES), F32), pltpu.VMEM((1, LANES), F32)],
        compiler_params=_cparams(("arbitrary", "arbitrary")),
        name="route",
    )(idx_slab)


def _row_copy(src_ref, src_row, dst_ref, dst_row, sem):
    return pltpu.make_async_copy(src_ref.at[pl.ds(src_row, 1)], dst_ref.at[pl.ds(dst_row, 1)], sem)


def _dispatch_kernel(pe_ref, cnt_ref, dest_ref, hp_ref, hs_ref, xb_ref, zero_s, sem, *, tiles_p, first_tail_block):
    i = pl.program_id(0)
    tm = TM
    nb = xb_ref.shape[0] // MOE_TM

    def zero_block(row0):
        return pltpu.make_async_copy(zero_s, xb_ref.at[pl.ds(pl.multiple_of(row0, MOE_TM), MOE_TM)], sem)

    @pl.when(i == 0)
    def _():
        zero_s[...] = jnp.zeros_like(zero_s)
        for start in (True, False):
            for e in range(N_EXPERTS):
                @pl.when(cnt_ref[e] > 0)
                def _():
                    cp = zero_block(pe_ref[e] - MOE_TM)
                    cp.start() if start else cp.wait()
            for b in range(first_tail_block, nb):
                @pl.when(b * MOE_TM >= pe_ref[N_EXPERTS - 1])
                def _():
                    cp = zero_block(b * MOE_TM)
                    cp.start() if start else cp.wait()

    def scatter_rows(h_ref, row0):
        def issue(t, carry):
            for k in range(TOP_K):
                _row_copy(h_ref, row0 + t, xb_ref, dest_ref[0, 0, t * TOP_K + k], sem).start(priority=k % 2)
            return carry

        def drain(t, carry):
            for k in range(TOP_K):
                _row_copy(h_ref, 0, xb_ref, 0, sem).wait()
            return carry

        lax.fori_loop(0, tm, issue, 0)
        lax.fori_loop(0, tm, drain, 0)

    @pl.when(i < tiles_p)
    def _():
        scatter_rows(hp_ref, i * tm)

    @pl.when(i >= tiles_p)
    def _():
        scatter_rows(hs_ref, (i - tiles_p) * tm)


def _dispatch(pad_end, counts, dest, hp, hs, m):
    tiles_p = hp.shape[0] // TM
    tiles_s = hs.shape[0] // TM
    grid_spec = pltpu.PrefetchScalarGridSpec(
        num_scalar_prefetch=2,
        grid=(tiles_p + tiles_s,),
        in_specs=[
            pl.BlockSpec((1, 1, TM * TOP_K), lambda i, pe, cn: (i, 0, 0), memory_space=pltpu.SMEM),
            pl.BlockSpec(memory_space=pl.ANY),
            pl.BlockSpec(memory_space=pl.ANY),
        ],
        out_specs=pl.BlockSpec(memory_space=pl.ANY),
        scratch_shapes=[pltpu.VMEM((MOE_TM, D_PACK), jnp.uint32), pltpu.SemaphoreType.DMA(())],
    )
    return pl.pallas_call(
        functools.partial(_dispatch_kernel, tiles_p=tiles_p, first_tail_block=m // MOE_TM - N_EXPERTS),
        grid_spec=grid_spec,
        out_shape=jax.ShapeDtypeStruct((m, D_PACK), jnp.uint32),
        compiler_params=_cparams(("arbitrary",)),
        name="dispatch",
    )(pad_end, counts, dest, hp, hs)


def _combine_kernel(dest_ref, x_ref, mod_ref, gate_ref, gfin_ref, yb_ref, o_ref, buf, sem, *, final):
    tm = x_ref.shape[0]

    def issue(t, carry):
        for k in range(TOP_K):
            _row_copy(yb_ref, dest_ref[0, 0, t * TOP_K + k], buf.at[k], t, sem).start(priority=k % 2)
        return carry

    def drain(t, carry):
        for k in range(TOP_K):
            _row_copy(yb_ref, 0, buf.at[k], 0, sem).wait()
        return carry

    lax.fori_loop(0, tm, issue, 0)
    lax.fori_loop(0, tm, drain, 0)
    g = gate_ref[...]
    acc_lo = acc_hi = None
    for k in range(TOP_K):
        y_lo, y_hi = _unpack_rows(buf[k])
        acc_lo = g[:, k:k + 1] * y_lo if acc_lo is None else acc_lo + g[:, k:k + 1] * y_lo
        acc_hi = g[:, k:k + 1] * y_hi if acc_hi is None else acc_hi + g[:, k:k + 1] * y_hi
    x = x_ref[...] + mod_ref[0, 5:6, :] * jnp.concatenate([acc_lo, acc_hi], axis=1)
    if final:
        ms = jnp.mean(x * x, axis=-1, keepdims=True)
        x = x * lax.rsqrt(ms + EPS) * gfin_ref[...]
    o_ref[...] = x


def _combine(dest, x, mod, mod_row0, tiles_per_row, gates, g_final, yb, *, final):
    n = x.shape[0]
    return pl.pallas_call(
        functools.partial(_combine_kernel, final=final),
        grid=(n // TM,),
        in_specs=[
            pl.BlockSpec((1, 1, TM * TOP_K), lambda i: (i, 0, 0), memory_space=pltpu.SMEM),
            pl.BlockSpec((TM, D_MODEL), lambda i: (i, 0)),
            pl.BlockSpec((1, N_MOD, D_MODEL), lambda i: (mod_row0 + i // tiles_per_row, 0, 0)),
            pl.BlockSpec((TM, LANES), lambda i: (i, 0)),
            pl.BlockSpec((1, D_MODEL), lambda i: (0, 0)),
            pl.BlockSpec(memory_space=pl.ANY),
        ],
        out_specs=pl.BlockSpec((TM, D_MODEL), lambda i: (i, 0)),
        out_shape=jax.ShapeDtypeStruct((n, D_MODEL), F32),
        scratch_shapes=[pltpu.VMEM((TOP_K, TM, D_PACK), jnp.uint32), pltpu.SemaphoreType.DMA(())],
        compiler_params=_cparams(("arbitrary",)),
        name="combine_final" if final else "combine",
    )(dest, x, mod, gates, g_final, yb)


def _rope_tables(n_tokens):
    t = np.arange(n_tokens)
    row = (t // GRID_W).astype(np.float32)
    col = (t % GRID_W).astype(np.float32)
    inv = jnp.asarray(ROPE_THETA, F32) ** (-jnp.arange(ROPE_AXIS_PAIRS, dtype=F32) / ROPE_AXIS_PAIRS)
    ang = jnp.concatenate([jnp.asarray(row)[:, None] * inv, jnp.asarray(col)[:, None] * inv], axis=-1)
    cos, sin = jnp.cos(ang), jnp.sin(ang)
    cos = jnp.tile(cos, (1, LANES // ROPE_HALF))
    sin = jnp.tile(jnp.concatenate([-sin, sin], axis=-1), (1, LANES // HEAD_DIM))
    return cos, sin


def _dup_kv_heads(t):
    b, p = t.shape[:2]
    t = jnp.broadcast_to(t[:, :, :, None, :], (b, p, A_KV_HEADS, 2, HEAD_DIM))
    return t.reshape(b, p, 4 * HEAD_DIM).astype(BF16)


def kernel(x_prompt, x_sample, cache_k_a, cache_v_a, cache_k_b, cache_v_b, cache_k_c, cache_v_c, c, c_ctx,
           w_ada, b_ada, norm_attn, norm_ffn, w_in, sinks, rpb, lambda_qk, subln, w_branch, w_out,
           w_router, b_router, w_gu, b_gu, w_down, b_down, norm_final):
    bsz, seq = x_prompt.shape[:2]
    bsz_d, n_lat = x_sample.shape[:2]
    past = cache_k_a.shape[2]
    n_p = bsz * seq
    n_s = bsz_d * n_lat
    assert seq % TM == 0 and n_lat % TQ == 0 and n_lat % C_CHUNK == 0 and bsz_d + 1 <= 8
    assert (n_p + n_s) % ROUTE_TM == 0 and n_lat % C_TQ == 0 and n_p % MERGE_TM == 0 and n_lat % MERGE_TM == 0

    cond = jnp.zeros((8, D_MODEL), F32).at[0].set(c_ctx).at[1:1 + bsz_d].set(c)
    mod_all = _modulation(cond, w_ada, b_ada).reshape(DEPTH, 8, N_MOD, D_MODEL)
    cos, sin = _rope_tables(n_lat)
    g_final = norm_final[None, :]
    w_bf = w_in.astype(BF16)
    n_assign = (n_p + n_s) * TOP_K
    m_rows = (n_assign + N_EXPERTS * (MOE_TM - 1) + MOE_TM - 1) // MOE_TM * MOE_TM

    xp = x_prompt.reshape(n_p, D_MODEL)
    xs = x_sample.reshape(n_s, D_MODEL)
    kv_out = []
    for l in range(DEPTH):
        lambda_init = 0.8 - 0.6 * math.exp(-0.3 * l)
        mod = mod_all[l]
        g_attn = norm_attn[l][None, :]
        g_ffn = norm_ffn[l][None, :]
        sg = subln[l][None, :]
        wb = w_branch[l].astype(BF16)
        wo = w_out[l].astype(BF16)
        br = b_router[l][None, :]

        pa, pb, pc, pg, *kv = _project(xp, mod, 0, n_p // TM, g_attn, w_bf, l, cos, sin, rope=False, emit_kv=True)
        oa, ob, oc = _ctx_attention(pa, pb, pc, sinks[l], lambda_qk[l], sg, seq, lambda_init)
        xp_mid, hp, idx_p, gate_p = _merge_route(xp, mod, 0, n_p // MERGE_TM, g_ffn, oa, ob, oc, pg, wb, wo,
                                                 w_router[l], br)
        kv_out.append(kv)

        pa, pb, pc, pg = _project(xs, mod, 1, n_lat // TM, g_attn, w_bf, l, cos, sin, rope=True, emit_kv=False)
        oa = _win_attention(pa, _dup_kv_heads(cache_k_a[:, l]), _dup_kv_heads(cache_v_a[:, l]), sinks[l],
                            bsz_d, n_lat)
        ob = _na_attention(pb, cache_k_b[:, l].reshape(bsz_d, past, -1).astype(BF16),
                           cache_v_b[:, l].reshape(bsz_d, past, -1).astype(BF16),
                           _na_bias_table(rpb.reshape(-1), l, n_lat // GRID_W), bsz_d, n_lat)
        oc = _diff_attention(pc, cache_k_c[:, l].reshape(bsz_d, past, -1).astype(BF16),
                             cache_v_c[:, l].reshape(bsz_d, past, -1).astype(BF16),
                             lambda_qk[l], sg, bsz_d, n_lat, lambda_init)
        xs_mid, hs, idx_s, gate_s = _merge_route(xs, mod, 1, n_lat // MERGE_TM, g_ffn, oa, ob, oc, pg, wb, wo,
                                                 w_router[l], br)

        last = l == DEPTH - 1
        dest_slab, cnt = _route(jnp.concatenate([idx_p, idx_s], axis=0))
        counts = cnt[0, :N_EXPERTS].astype(jnp.int32)
        pad_end, block_e, block_first, block_valid, block_next = _moe_block_tables(counts, m_rows // MOE_TM)
        dest = dest_slab[:, :TOP_K].reshape(-1, 1, TM * TOP_K)
        dest_p, dest_s = dest[:n_p // TM], dest[n_p // TM:]
        xb = _dispatch(pad_end, counts, dest, hp, hs, m_rows)
        yb = _moe_experts(xb, block_e, block_first, block_valid, block_next, l, w_gu, b_gu, w_down, b_down)
        xp = _combine(dest_p, xp_mid, mod, 0, n_p // TM, gate_p, g_final, yb, final=last)
        xs = _combine(dest_s, xs_mid, mod, 1, n_lat // TM, gate_s, g_final, yb, final=last)

    y_prompt = xp.reshape(bsz, seq, D_MODEL)
    y_sample = xs.reshape(bsz_d, n_lat, D_MODEL)

    def stacked(j, heads, width):
        return jnp.stack([kv[j].reshape(bsz, seq, heads, width) for kv in kv_out], axis=1)

    new_k_a, new_v_a = stacked(0, A_KV_HEADS, HEAD_DIM), stacked(1, A_KV_HEADS, HEAD_DIM)
    new_k_b, new_v_b = stacked(2, B_HEADS, HEAD_DIM), stacked(3, B_HEADS, HEAD_DIM)
    new_k_c, new_v_c = stacked(4, C_HEADS, 2 * HEAD_DIM), stacked(5, C_HEADS, C_VDIM)
    return (y_prompt, y_sample, new_k_a, new_v_a, new_k_b, new_v_b, new_k_c, new_v_c)
```
